```python
import math
import jax
import jax.numpy as jnp
from jax import lax
import numpy as np

D_MODEL = 2048
BATCH = 1
SEQ = 8192
DEPTH = 2
DEC_BATCH = 128
DEC_SEQ = 1
PAST_LEN = 2048
PAGE_SIZE = 128

N_MIXERS = 4
GROUP_W = D_MODEL // N_MIXERS
H_A = 4
DK_A = GROUP_W // (2 * H_A)
DV_A = GROUP_W // H_A
GK_RANK = 16
GLA_TAU = 16.0
CHUNK_A = 16
H_B = 8
P_B = GROUP_W // H_B
N_B = 64
G_B = 2
CONV_W = 4
CHUNK_B = 64
XBC_W = GROUP_W + 2 * G_B * N_B
H_C = 4
DK_C = GROUP_W // (2 * H_C)
DV_C = GROUP_W // H_C
ROPE_THETA = 10000.0
Q_BLOCK = 128
H_D = 8
N_D = GROUP_W // H_D
W_LORA = 64
A_LORA = 64
G_LORA = 128
LNX_EPS = 64e-5
N_MEM = 256
H_M = 4
D_MEM = D_MODEL // 4
DH_M = D_MEM // H_M
N_KEYS = 128
N_EXPERTS = N_KEYS * N_KEYS
H_P = 8
TOPK_P = 16
D_PK = 256
PEER_BLOCK = 128
POOL_NUM = 5
POOL_DEN = 4

A_SIZES = (H_A * DK_A, H_A * DK_A, H_A * DV_A, H_A * DV_A, GK_RANK)
B_SIZES = (GROUP_W, XBC_W, H_B)
C_SIZES = (2 * H_C * DK_C, 2 * H_C * DK_C, H_C * DV_C)
D_SIZES = (GROUP_W, GROUP_W, GROUP_W, W_LORA, A_LORA, G_LORA)
A_COLS = 2 * H_A * DK_A + 2 * H_A * DV_A + GK_RANK
B_COLS = GROUP_W + XBC_W + H_B
C_COLS = 4 * H_C * DK_C + H_C * DV_C
D_COLS = 3 * GROUP_W + W_LORA + A_LORA + G_LORA
MIX_SIZES = (A_COLS, B_COLS, C_COLS, D_COLS)
IN_COLS = A_COLS + B_COLS + C_COLS + D_COLS

kernel_name = 'hymba_gla_ssd_diffattn_rwkv7_peer_step'


def _split(t, sizes):
    return jnp.split(t, np.cumsum(sizes)[:-1].tolist(), axis=-1)


def _rms(x, g, eps=1e-6):
    xf = x.astype(jnp.float32)
    return (xf * lax.rsqrt(jnp.mean(xf * xf, axis=-1, keepdims=True) + eps) * g).astype(x.dtype)


def _rope(x, pos):
    half = x.shape[-1] // 2
    inv = ROPE_THETA ** (-jnp.arange(half, dtype=jnp.float32) / half)
    ang = pos.astype(jnp.float32)[:, None] * inv[None, :]
    ang = ang.reshape(ang.shape[0], *([1] * (x.ndim - 3)), half)
    cos, sin = jnp.cos(ang), jnp.sin(ang)
    xf = x.astype(jnp.float32)
    x1, x2 = xf[..., :half], xf[..., half:]
    return jnp.concatenate([x1 * cos - x2 * sin, x2 * cos + x1 * sin], axis=-1).astype(x.dtype)


def _pad_len(t, pad):
    return jnp.pad(t, [(0, 0), (0, pad)] + [(0, 0)] * (t.ndim - 2))


def _chunk(t, n_chunks, c):
    return t.reshape(t.shape[0], n_chunks, c, *t.shape[2:])


def _causal_mask(c):
    return jnp.arange(c)[:, None] >= jnp.arange(c)[None, :]


def _scan_chunks(s0, decay, contrib):
    def step(s, dc):
        d, c = dc
        return s * d + c, s
    s_fin, s_start = lax.scan(step, s0, (jnp.moveaxis(decay, 1, 0), jnp.moveaxis(contrib, 1, 0)))
    return jnp.moveaxis(s_start, 0, 1), s_fin


def _gla(q, k, v, log_a, s0):
    bsz, L = q.shape[:2]
    c = min(CHUNK_A, L)
    nc = -(-L // c)
    pad = nc * c - L
    q, k, v, log_a = (_chunk(_pad_len(t.astype(jnp.float32), pad), nc, c) for t in (q, k, v, log_a))
    b = jnp.cumsum(log_a, axis=2)
    causal = _causal_mask(c)[None, None, :, :, None, None]
    rel = jnp.exp(jnp.where(causal, b[:, :, :, None] - b[:, :, None, :], -jnp.inf))
    att = jnp.einsum('bcthd,bcshd,bctshd->bchts', q, k, rel)
    o_intra = jnp.einsum('bchts,bcshv->bcthv', att, v)
    b_last = b[:, :, -1]
    contrib = jnp.einsum('bcshd,bcshv->bchdv', k * jnp.exp(b_last[:, :, None] - b), v)
    s_start, s_fin = _scan_chunks(s0.astype(jnp.float32), jnp.exp(b_last)[..., None], contrib)
    o_inter = jnp.einsum('bcthd,bchdv->bcthv', q * jnp.exp(b), s_start)
    o = (o_intra + o_inter).reshape(bsz, nc * c, *v.shape[3:])[:, :L]
    return o, s_fin


def _ssd(x, dt, a, bm, cm, s0):
    bsz, L = x.shape[:2]
    rep = H_B // G_B
    bm = jnp.repeat(bm, rep, axis=2)
    cm = jnp.repeat(cm, rep, axis=2)
    la = dt * a
    xd = x.astype(jnp.float32) * dt[..., None]
    c = min(CHUNK_B, L)
    nc = -(-L // c)
    pad = nc * c - L
    la, xd, bm, cm = (_chunk(_pad_len(t.astype(jnp.float32), pad), nc, c) for t in (la, xd, bm, cm))
    b = jnp.cumsum(la, axis=2)
    causal = _causal_mask(c)[None, None, :, :, None]
    decay_ts = jnp.exp(jnp.where(causal, b[:, :, :, None] - b[:, :, None, :], -jnp.inf))
    scores = jnp.einsum('bcthn,bcshn->bctsh', cm, bm) * decay_ts
    y_intra = jnp.einsum('bctsh,bcshp->bcthp', scores, xd)
    b_last = b[:, :, -1]
    contrib = jnp.einsum('bcsh,bcshn,bcshp->bchpn', jnp.exp(b_last[:, :, None] - b), bm, xd)
    s_start, s_fin = _scan_chunks(s0.astype(jnp.float32), jnp.exp(b_last)[..., None, None], contrib)
    y_inter = jnp.einsum('bcthn,bchpn->bcthp', cm, s_start) * jnp.exp(b)[..., None]
    y = (y_intra + y_inter).reshape(bsz, nc * c, H_B, P_B)[:, :L]
    return y, s_fin


def _rwkv7(r, w, k, v, a_vec, b_vec, s0):
    decay = jnp.exp(-jnp.exp(w))

    def step(s, inp):
        r_t, d_t, k_t, v_t, a_t, b_t = inp
        sa = jnp.einsum('bhvk,bhk->bhv', s, a_t)
        s = s * d_t[:, :, None, :] + sa[..., None] * b_t[:, :, None, :] + v_t[..., None] * k_t[:, :, None, :]
        return s, jnp.einsum('bhvk,bhk->bhv', s, r_t)

    xs = tuple(jnp.moveaxis(t.astype(jnp.float32), 1, 0) for t in (r, decay, k, v, a_vec, b_vec))
    s_fin, o = lax.scan(step, s0.astype(jnp.float32), xs)
    return jnp.moveaxis(o, 0, 1), s_fin


def _diff_attn(q, k, v, q_pos, k_pos, lam):
    s = jnp.einsum('bqhmd,bkhmd->bhmqk', q, k).astype(jnp.float32) * (DK_C ** -0.5)
    mask = k_pos[None, :] <= q_pos[:, None]
    p = jax.nn.softmax(jnp.where(mask, s, -jnp.inf), axis=-1)
    att = p[:, :, 0] - lam * p[:, :, 1]
    return jnp.einsum('bhqk,bkhv->bqhv', att, v.astype(jnp.float32))


def _diff_attn_blocked(q, k, v, q_pos, k_pos, lam):
    bsz, lq = q.shape[:2]
    if lq <= Q_BLOCK:
        return _diff_attn(q, k, v, q_pos, k_pos, lam)
    nb = -(-lq // Q_BLOCK)
    pad = nb * Q_BLOCK - lq
    qp = _pad_len(q, pad)
    pp = jnp.concatenate([q_pos, jnp.full((pad,), q_pos[-1], q_pos.dtype)])
    qb = jnp.moveaxis(qp.reshape(bsz, nb, Q_BLOCK, H_C, 2, DK_C), 1, 0)
    pb = pp.reshape(nb, Q_BLOCK)
    ob = lax.map(lambda a: _diff_attn(a[0], k, v, a[1], k_pos, lam), (qb, pb))
    return jnp.moveaxis(ob, 0, 1).reshape(bsz, nb * Q_BLOCK, H_C, DV_C)[:, :lq]


def _causal_conv(u, prev, w, b):
    full = jnp.concatenate([prev.astype(u.dtype), u], axis=1)
    L = u.shape[1]
    out = b
    for i in range(CONV_W):
        out = out + full[:, i:i + L] * w[i]
    return out, full[:, L:]


def _mem_kv(mem, g_src, w_mk, w_mv, g_k):
    m = _rms(mem, g_src)
    bsz = mem.shape[0]
    k = _rms((m @ w_mk).reshape(bsz, -1, H_M, DH_M), g_k)
    v = (m @ w_mv).reshape(bsz, -1, H_M, DH_M)
    return k, v


def _mem_attn(xn, mk, mv, w_mq, g_q, w_mo):
    bsz, L = xn.shape[:2]
    q = _rms((xn @ w_mq).reshape(bsz, L, H_M, DH_M), g_q)
    s = jnp.einsum('blhd,bmhd->bhlm', q, mk).astype(jnp.float32) * (DH_M ** -0.5)
    p = jax.nn.softmax(s, axis=-1)
    o = jnp.einsum('bhlm,bmhd->blhd', p, mv.astype(jnp.float32)).reshape(bsz, L, D_MEM)
    return o.astype(xn.dtype) @ w_mo


def _peer(xn, w_pq, sub_keys, u_tab, v_tab):
    bsz, L, D = xn.shape
    t = xn.reshape(-1, D)
    T = t.shape[0]
    nb = -(-T // PEER_BLOCK)
    t = jnp.pad(t, ((0, nb * PEER_BLOCK - T), (0, 0))).reshape(nb, PEER_BLOCK, D)

    def block(tb):
        q = (tb @ w_pq).reshape(PEER_BLOCK, H_P, 2, D_PK // 2)
        s = jnp.einsum('thxd,hxnd->thxn', q, sub_keys).astype(jnp.float32)
        sc, idx = lax.top_k(s, TOPK_P)
        cand = (sc[:, :, 0, :, None] + sc[:, :, 1, None, :]).reshape(PEER_BLOCK, H_P, TOPK_P * TOPK_P)
        cidx = (idx[:, :, 0, :, None] * N_KEYS + idx[:, :, 1, None, :]).reshape(PEER_BLOCK, H_P, TOPK_P * TOPK_P)
        top, sel = lax.top_k(cand, TOPK_P)
        eidx = jnp.take_along_axis(cidx, sel, axis=-1)
        gate = jax.nn.softmax(top, axis=-1)
        u = u_tab[eidx]
        hid = jax.nn.gelu(jnp.einsum('thkd,td->thk', u, tb).astype(jnp.float32))
        return jnp.einsum('thk,thkd->td', (gate * hid).astype(tb.dtype), v_tab[eidx])

    out = lax.map(block, t).reshape(-1, D)[:T]
    return out.reshape(bsz, L, D)


def _hybrid_mixer(p, lam_init, xn, q_pos, past, st):
    bsz, L, _ = xn.shape
    odt = xn.dtype
    f32 = jnp.float32
    cols_a, cols_b, cols_c, cols_d = _split(xn @ p['w_in'], MIX_SIZES)

    qa, ka, va, ga, gka = _split(cols_a, A_SIZES)
    log_a = jax.nn.log_sigmoid((gka @ p['gla_wg2'] + p['gla_bg']).astype(f32)) / GLA_TAU
    oa, gla_new = _gla(qa.reshape(bsz, L, H_A, DK_A) * (DK_A ** -0.5), ka.reshape(bsz, L, H_A, DK_A),
                       va.reshape(bsz, L, H_A, DV_A), log_a.reshape(bsz, L, H_A, DK_A), st['gla'])
    oa = _rms(oa, p['gla_gn']) * jax.nn.silu(ga.reshape(bsz, L, H_A, DV_A).astype(f32))
    oa = oa.reshape(bsz, L, GROUP_W).astype(odt)

    z, xbc, dtr = _split(cols_b, B_SIZES)
    xbc, conv_new = _causal_conv(xbc, st['conv'], p['conv_w'], p['conv_b'])
    xs, bm, cm = _split(jax.nn.silu(xbc), (GROUP_W, G_B * N_B, G_B * N_B))
    dtv = jax.nn.softplus((dtr + p['dt_bias']).astype(f32))
    xs = xs.reshape(bsz, L, H_B, P_B)
    y, ssm_new = _ssd(xs, dtv, -jnp.exp(p['a_log'].astype(f32)), bm.reshape(bsz, L, G_B, N_B),
                      cm.reshape(bsz, L, G_B, N_B), st['ssm'])
    y = (y + xs.astype(f32) * p['d_skip'][:, None]).reshape(bsz, L, GROUP_W) * jax.nn.silu(z.astype(f32))
    ob = _rms(y.reshape(bsz, L, G_B, GROUP_W // G_B), p['ssm_gn'].reshape(G_B, GROUP_W // G_B))
    ob = ob.reshape(bsz, L, GROUP_W).astype(odt)

    qc, kc, vc = _split(cols_c, C_SIZES)
    qc = _rope(_rms(qc.reshape(bsz, L, H_C, 2, DK_C), p['dq_norm']), q_pos)
    kc = _rope(_rms(kc.reshape(bsz, L, H_C, 2, DK_C), p['dk_norm']), q_pos)
    vc = vc.reshape(bsz, L, H_C, DV_C)
    if past is None:
        k_all, v_all, k_pos = kc, vc, q_pos
    else:
        k_all = jnp.concatenate([past[0].astype(kc.dtype), kc], axis=1)
        v_all = jnp.concatenate([past[1].astype(vc.dtype), vc], axis=1)
        k_pos = jnp.concatenate([jnp.arange(past[0].shape[1], dtype=q_pos.dtype), q_pos])
    lq, lk = p['lam_q'].astype(f32), p['lam_k'].astype(f32)
    lam = jnp.exp(jnp.sum(lq[0] * lk[0])) - jnp.exp(jnp.sum(lq[1] * lk[1])) + lam_init
    oc = _diff_attn_blocked(qc, k_all, v_all, q_pos, k_pos, lam)
    oc = (_rms(oc, p['diff_gn']) * (1.0 - lam_init)).reshape(bsz, L, GROUP_W).astype(odt)

    full = jnp.concatenate([st['shift'].astype(cols_d.dtype), cols_d], axis=1)
    shift_new = full[:, -1:]
    mixed = (cols_d + (full[:, :-1] - cols_d) * p['shift_mu']).astype(f32)
    rd, kd, vd, wl, al, gl = _split(mixed, D_SIZES)
    w = -jax.nn.softplus(-(p['w0'] + jnp.tanh(wl) @ p['w2'])) - 0.5
    a = jax.nn.sigmoid(p['a0'] + al @ p['a2'])
    g = jax.nn.sigmoid(gl) @ p['g2']
    heads = lambda t: t.reshape(bsz, L, H_D, N_D)
    kk = heads(kd * p['k_k'])
    kk = kk * lax.rsqrt(jnp.maximum(jnp.sum(kk * kk, axis=-1, keepdims=True), 1e-24))
    kd = kd * (1.0 + (a - 1.0) * p['k_a'])
    rh, kh, vh, ah = heads(rd), heads(kd), heads(vd), heads(a)
    od, rwkv_new = _rwkv7(rh, heads(w), kh, vh, -kk, kk * ah, st['rwkv'])
    mu = jnp.mean(od, axis=-1, keepdims=True)
    var = jnp.mean(jnp.square(od - mu), axis=-1, keepdims=True)
    od = (od - mu) * lax.rsqrt(var + LNX_EPS) * p['lnx_g'].reshape(H_D, N_D) + p['lnx_b'].reshape(H_D, N_D)
    od = od + jnp.sum(rh * kh * p['r_k'], axis=-1, keepdims=True) * vh
    od = (od.reshape(bsz, L, GROUP_W) * g).astype(odt)

    out = jnp.concatenate([oa, ob, oc, od], axis=-1) @ p['w_out']
    new_st = dict(gla=gla_new, ssm=ssm_new, conv=conv_new, rwkv=rwkv_new, shift=shift_new)
    return out, new_st, kc.reshape(bsz, L, H_C, 2 * DK_C), vc


def _trunk(x, q_pos, st0, W, mem=None, mem_cache=None, paged=None):
    names = ('k', 'v', 'gla', 'ssm', 'conv', 'rwkv', 'shift')
    outs = {n: [] for n in names}
    mks, mvs = [], []
    for l in range(DEPTH):
        p = {n: a[l] for n, a in W.items()}
        st = {n: a[l] for n, a in st0.items()}
        lam_init = 0.8 - 0.6 * math.exp(-0.3 * l)
        past = None
        if paged is not None:
            ck, cv, pt = paged
            nbs = pt.shape[0]
            past = (ck[l][pt].reshape(nbs, -1, H_C, 2, DK_C), cv[l][pt].reshape(nbs, -1, H_C, DV_C))
        h, st_new, kc, vc = _hybrid_mixer(p, lam_init, _rms(x, p['norm_mix']), q_pos, past, st)
        x = x + h.astype(x.dtype)
        if mem is not None:
            mk, mv = _mem_kv(mem, p['norm_memsrc'], p['w_mk'], p['w_mv'], p['mk_norm'])
            mks.append(mk)
            mvs.append(mv)
        else:
            mk, mv = mem_cache[0][l], mem_cache[1][l]
        x = x + _mem_attn(_rms(x, p['norm_mem']), mk, mv, p['w_mq'], p['mq_norm'], p['w_mo']).astype(x.dtype)
        x = x + _peer(_rms(x, p['norm_ffn']), p['peer_wq'], p['peer_keys'], p['peer_u'], p['peer_v']).astype(x.dtype)
        outs['k'].append(kc)
        outs['v'].append(vc)
        for n in ('gla', 'ssm', 'conv', 'rwkv', 'shift'):
            outs[n].append(st_new[n])
    stacked = {n: jnp.stack(outs[n]) for n in names}
    mem_out = (jnp.stack(mks), jnp.stack(mvs)) if mks else None
    return x, stacked, mem_out


def setup_inputs(seed: int = 0) -> dict:
    key = jax.random.key(seed)
    ks = iter(jax.random.split(key, 80))

    def nrm(shape, scale):
        return jax.random.normal(next(ks), shape, jnp.float32) * scale

    def gain(shape):
        return 1.0 + nrm(shape, 0.02)

    def unif(shape, lo, hi):
        return jax.random.uniform(next(ks), shape, jnp.float32, lo, hi)

    D = D_MODEL
    n_pages = PAST_LEN // PAGE_SIZE
    n_pool = (POOL_NUM * DEC_BATCH * n_pages + POOL_DEN - 1) // POOL_DEN
    perm = jax.random.permutation(next(ks), n_pool)
    page_table = perm[:DEC_BATCH * n_pages].reshape(DEC_BATCH, n_pages).astype(jnp.int32)
    dt0 = jnp.exp(unif((DEPTH, H_B), math.log(1e-3), math.log(1e-1)))
    return {
        'x_prompt': nrm((BATCH, SEQ, D), 1.0),
        'x_sample': nrm((DEC_BATCH, DEC_SEQ, D), 1.0),
        'cache_diff_k': nrm((DEPTH, n_pool, PAGE_SIZE, H_C, 2 * DK_C), 1.0),
        'cache_diff_v': nrm((DEPTH, n_pool, PAGE_SIZE, H_C, DV_C), 1.0),
        'cache_mem_k': nrm((DEPTH, DEC_BATCH, N_MEM, H_M, DH_M), 1.0),
        'cache_mem_v': nrm((DEPTH, DEC_BATCH, N_MEM, H_M, DH_M), 1.0),
        'state_gla': nrm((DEPTH, DEC_BATCH, H_A, DK_A, DV_A), 0.5),
        'state_ssm': nrm((DEPTH, DEC_BATCH, H_B, P_B, N_B), 0.1),
        'state_conv': nrm((DEPTH, DEC_BATCH, CONV_W - 1, XBC_W), 1.0),
        'state_rwkv': nrm((DEPTH, DEC_BATCH, H_D, N_D, N_D), 0.5),
        'state_shift': nrm((DEPTH, DEC_BATCH, 1, D_COLS), 1.0),
        'page_table': page_table,
        'mem_prompt': nrm((BATCH, N_MEM, D), 1.0),
        'norm_mix': gain((DEPTH, D)),
        'w_in': nrm((DEPTH, D, IN_COLS), D ** -0.5),
        'w_out': nrm((DEPTH, D, D), D ** -0.5),
        'gla_wg2': nrm((DEPTH, GK_RANK, H_A * DK_A), GK_RANK ** -0.5),
        'gla_bg': nrm((DEPTH, H_A * DK_A), 0.1),
        'gla_gn': gain((DEPTH, DV_A)),
        'conv_w': nrm((DEPTH, CONV_W, XBC_W), CONV_W ** -0.5),
        'conv_b': nrm((DEPTH, XBC_W), 0.02),
        'dt_bias': dt0 + jnp.log(-jnp.expm1(-dt0)),
        'a_log': jnp.log(unif((DEPTH, H_B), 1.0, 16.0)),
        'd_skip': gain((DEPTH, H_B)),
        'ssm_gn': gain((DEPTH, GROUP_W)),
        'dq_norm': gain((DEPTH, 2, DK_C)),
        'dk_norm': gain((DEPTH, 2, DK_C)),
        'lam_q': nrm((DEPTH, 2, DK_C), 0.1),
        'lam_k': nrm((DEPTH, 2, DK_C), 0.1),
        'diff_gn': gain((DEPTH, DV_C)),
        'shift_mu': unif((DEPTH, D_COLS), 0.0, 1.0),
        'w0': unif((DEPTH, GROUP_W), -6.0, -1.0),
        'w2': nrm((DEPTH, W_LORA, GROUP_W), 0.1),
        'a0': nrm((DEPTH, GROUP_W), 0.1),
        'a2': nrm((DEPTH, A_LORA, GROUP_W), 0.1),
        'g2': nrm((DEPTH, G_LORA, GROUP_W), G_LORA ** -0.5),
        'k_k': 0.85 + nrm((DEPTH, GROUP_W), 0.02),
        'k_a': gain((DEPTH, GROUP_W)),
        'r_k': nrm((DEPTH, H_D, N_D), 0.1),
        'lnx_g': gain((DEPTH, GROUP_W)),
        'lnx_b': nrm((DEPTH, GROUP_W), 0.02),
        'norm_mem': gain((DEPTH, D)),
        'norm_memsrc': gain((DEPTH, D)),
        'w_mq': nrm((DEPTH, D, D_MEM), D ** -0.5),
        'w_mk': nrm((DEPTH, D, D_MEM), D ** -0.5),
        'w_mv': nrm((DEPTH, D, D_MEM), D ** -0.5),
        'w_mo': nrm((DEPTH, D_MEM, D), D_MEM ** -0.5),
        'mq_norm': gain((DEPTH, DH_M)),
        'mk_norm': gain((DEPTH, DH_M)),
        'norm_ffn': gain((DEPTH, D)),
        'peer_wq': nrm((DEPTH, D, H_P * D_PK), D ** -0.5),
        'peer_keys': nrm((DEPTH, H_P, 2, N_KEYS, D_PK // 2), (D_PK // 2) ** -0.5),
        'peer_u': nrm((DEPTH, N_EXPERTS, D), D ** -0.5),
        'peer_v': nrm((DEPTH, N_EXPERTS, D), 0.5 * H_P ** -0.5),
    }


def reference(x_prompt, x_sample, cache_diff_k, cache_diff_v, cache_mem_k, cache_mem_v, state_gla, state_ssm,
              state_conv, state_rwkv, state_shift, page_table, mem_prompt, norm_mix, w_in, w_out, gla_wg2, gla_bg,
              gla_gn, conv_w, conv_b, dt_bias, a_log, d_skip, ssm_gn, dq_norm, dk_norm, lam_q, lam_k, diff_gn,
              shift_mu, w0, w2, a0, a2, g2, k_k, k_a, r_k, lnx_g, lnx_b, norm_mem, norm_memsrc, w_mq, w_mk, w_mv,
              w_mo, mq_norm, mk_norm, norm_ffn, peer_wq, peer_keys, peer_u, peer_v):
    W = dict(norm_mix=norm_mix, w_in=w_in, w_out=w_out, gla_wg2=gla_wg2, gla_bg=gla_bg, gla_gn=gla_gn,
             conv_w=conv_w, conv_b=conv_b, dt_bias=dt_bias, a_log=a_log, d_skip=d_skip, ssm_gn=ssm_gn,
             dq_norm=dq_norm, dk_norm=dk_norm, lam_q=lam_q, lam_k=lam_k, diff_gn=diff_gn, shift_mu=shift_mu,
             w0=w0, w2=w2, a0=a0, a2=a2, g2=g2, k_k=k_k, k_a=k_a, r_k=r_k, lnx_g=lnx_g, lnx_b=lnx_b,
             norm_mem=norm_mem, norm_memsrc=norm_memsrc, w_mq=w_mq, w_mk=w_mk, w_mv=w_mv, w_mo=w_mo,
             mq_norm=mq_norm, mk_norm=mk_norm, norm_ffn=norm_ffn, peer_wq=peer_wq, peer_keys=peer_keys,
             peer_u=peer_u, peer_v=peer_v)
    f32 = jnp.float32
    bp, lp = x_prompt.shape[:2]
    st_p = dict(gla=jnp.zeros((DEPTH, bp, H_A, DK_A, DV_A), f32),
                ssm=jnp.zeros((DEPTH, bp, H_B, P_B, N_B), f32),
                conv=jnp.zeros((DEPTH, bp, CONV_W - 1, XBC_W), x_prompt.dtype),
                rwkv=jnp.zeros((DEPTH, bp, H_D, N_D, N_D), f32),
                shift=jnp.zeros((DEPTH, bp, 1, D_COLS), x_prompt.dtype))
    st_s = dict(gla=state_gla, ssm=state_ssm, conv=state_conv, rwkv=state_rwkv, shift=state_shift)
    pos_p = jnp.arange(lp, dtype=jnp.int32)
    pos_s = PAST_LEN + jnp.arange(x_sample.shape[1], dtype=jnp.int32)
    y_prompt, sp, mem_p = _trunk(x_prompt, pos_p, st_p, W, mem=mem_prompt)
    y_sample, ss, _ = _trunk(x_sample, pos_s, st_s, W, mem_cache=(cache_mem_k, cache_mem_v),
                             paged=(cache_diff_k, cache_diff_v, page_table))
    return (y_prompt, y_sample, sp['k'], sp['v'], ss['k'], ss['v'], mem_p[0], mem_p[1],
            sp['gla'], ss['gla'], sp['ssm'], ss['ssm'], sp['conv'], ss['conv'],
            sp['rwkv'], ss['rwkv'], sp['shift'], ss['shift'])
```

```python
import functools
import math

import numpy as np
import jax
import jax.numpy as jnp
from jax import lax
from jax.experimental import pallas as pl
from jax.experimental.pallas import tpu as pltpu

f32 = jnp.float32
bf16 = jnp.bfloat16
HI = lax.Precision.HIGHEST

D_MODEL = 2048
SEQ = 8192
DEPTH = 2
DEC_BATCH = 128
PAST_LEN = 2048
PAGE_SIZE = 128
T_ALL = SEQ + DEC_BATCH

GROUP_W = 512
H_A, DK_A, DV_A, GK_RANK, GLA_TAU = 4, 64, 128, 16, 16.0
H_B, P_B, N_B, G_B, CONV_W, XBC_W = 8, 64, 64, 2, 4, 768
H_C, DK_C, DV_C, ROPE_THETA = 4, 64, 128, 10000.0
H_D, N_D, LNX_EPS = 8, 64, 64e-5
N_MEM, H_M, D_MEM, DH_M = 256, 4, 512, 128
N_KEYS, H_P, TOPK_P, D_PK = 128, 8, 16, 256
N_EXPERTS = N_KEYS * N_KEYS

C_D = 0
C_BC = 1792
C_QKA = 2048
C_VA = 2560
C_GA = 3072
C_Z = 3584
C_XS = 4096
C_QC = 4608
C_KC = 5120
C_VC = 5632
C_TAIL = 6144
IN_PAD = 6272

LANES = 128
VMEM_LIMIT = 56 * 1024 * 1024

NN = ((1,), (0,))
NT = ((1,), (1,))
TN = ((0,), (0,))


def _dot(a, b, dims=NN, hi=False):
    if hi:
        return lax.dot_general(a, b, (dims, ((), ())), precision=HI, preferred_element_type=f32)
    return lax.dot_general(a.astype(bf16), b.astype(bf16), (dims, ((), ())), preferred_element_type=f32)


def _softplus(x):
    return jnp.maximum(x, 0.0) + jnp.log(1.0 + jnp.exp(-jnp.abs(x)))


def _sigmoid(x):
    return 1.0 / (1.0 + jnp.exp(-x))


def _silu(x):
    return x * _sigmoid(x)


def _iota(shape, axis):
    return lax.broadcasted_iota(jnp.int32, shape, axis)


def _params(*sem):
    return pltpu.CompilerParams(dimension_semantics=sem, vmem_limit_bytes=VMEM_LIMIT)


def _mm_kernel(*refs, norm, residual, emit_xn):
    x_ref, g_ref, w_ref = refs[:3]
    res_ref = refs[3] if residual else None
    xn_ref = refs[-1]
    o_ref = refs[-3] if emit_xn else refs[-2]

    @pl.when(pl.program_id(1) == 0)
    def _():
        x = x_ref[...]
        if norm:
            x = x * lax.rsqrt(jnp.mean(x * x, axis=-1, keepdims=True) + 1e-6) * g_ref[...]
        xn_ref[...] = x.astype(bf16)
        if emit_xn:
            refs[-2][...] = xn_ref[...]

    acc = jnp.dot(xn_ref[...], w_ref[...], preferred_element_type=f32)
    if residual:
        acc = acc + res_ref[...]
    o_ref[...] = acc


def _matmul(x, w, gain=None, res=None, emit_xn=False, tm=None, tn=None):
    m, k = x.shape
    n = w.shape[1]
    tm = tm or _pick(m, (1040, 1024, 512, 256, 128))
    tn = tn or _pick(n, (1024, 896, 512, 256, 128))
    norm = gain is not None
    g = (gain if norm else jnp.ones((k,), f32)).reshape(1, k)
    args = [x, g, w]
    in_specs = [pl.BlockSpec((tm, k), lambda i, j: (i, 0)),
                pl.BlockSpec((1, k), lambda i, j: (0, 0)),
                pl.BlockSpec((k, tn), lambda i, j: (0, j))]
    if res is not None:
        args.append(res)
        in_specs.append(pl.BlockSpec((tm, tn), lambda i, j: (i, j)))
    out_specs = [pl.BlockSpec((tm, tn), lambda i, j: (i, j))]
    out_shape = [jax.ShapeDtypeStruct((m, n), f32)]
    if emit_xn:
        out_specs.append(pl.BlockSpec((tm, k), lambda i, j: (i, 0)))
        out_shape.append(jax.ShapeDtypeStruct((m, k), bf16))
    out = pl.pallas_call(
        functools.partial(_mm_kernel, norm=norm, residual=res is not None, emit_xn=emit_xn),
        grid=(m // tm, n // tn),
        in_specs=in_specs,
        out_specs=out_specs,
        out_shape=out_shape,
        scratch_shapes=[pltpu.VMEM((tm, k), bf16)],
        compiler_params=_params("parallel", "arbitrary"),
        name="mm",
    )(*args)
    return out if emit_xn else out[0]


def _pick(n, cands):
    for c in cands:
        if n % c == 0:
            return c
    return n


def _const(shape):
    nd = len(shape)
    return pl.BlockSpec(shape, lambda i, _n=nd: (0,) * _n)


def _cols(width, start, tb):
    assert start % width == 0
    return pl.BlockSpec((tb, width), lambda i, _c=start // width: (i, _c))


def _blockdiag_tri(tb, c):
    r = _iota((tb, tb), 0)
    s = _iota((tb, tb), 1)
    return jnp.where((r // c == s // c) & (s <= r), 1.0, 0.0).astype(f32)


def _segment_ones(n, seg):
    r = _iota((n, n), 0)
    s = _iota((n, n), 1)
    return jnp.where(r // seg == s // seg, 1.0, 0.0).astype(f32)


GLA_TB = 128
GLA_C = 16


def _gla_gate_log(tail, wg2p, bg):
    z = _dot(tail, wg2p, hi=True) + bg
    return -_softplus(-z) * (1.0 / GLA_TAU)


def _gla_out(o, g, gn):
    outs = []
    for h in range(H_A):
        oh = o[:, h * DV_A:(h + 1) * DV_A]
        oh = oh * lax.rsqrt(jnp.mean(oh * oh, axis=-1, keepdims=True) + 1e-6) * gn
        outs.append(oh * _silu(g[:, h * DV_A:(h + 1) * DV_A]))
    return jnp.concatenate(outs, axis=-1)


def _gla_kernel(qk_ref, v_ref, g_ref, tail_ref, wg2_ref, bg_ref, gn_ref, o_ref, sfin_ref, st_ref, w_ref):
    i = pl.program_id(0)

    @pl.when(i == 0)
    def _():
        st_ref[...] = jnp.zeros_like(st_ref)

    tb, c = GLA_TB, GLA_C
    qk = qk_ref[...]
    q = qk[:, :256] * (DK_A ** -0.5)
    k = qk[:, 256:]
    v = v_ref[...]
    la = _gla_gate_log(tail_ref[...], wg2_ref[...], bg_ref[...])
    b = _dot(_blockdiag_tri(tb, c), la, hi=True)
    e_r = _iota((256, 512), 0) // DK_A
    e_c = _iota((256, 512), 1) // DV_A
    expand = jnp.where(e_r == e_c, 1.0, 0.0).astype(bf16)
    s_idx = _iota((c, 256), 0)
    for j in range(tb // c):
        r0 = j * c
        qj, kj, bj, vj = q[r0:r0 + c], k[r0:r0 + c], b[r0:r0 + c], v[r0:r0 + c]
        for t in range(c):
            wt = qj[t:t + 1] * kj * jnp.exp(bj[t:t + 1] - bj)
            w_ref[t * c:(t + 1) * c, :] = jnp.where(s_idx <= t, wt, 0.0)
        att = _dot(w_ref[...], expand)
        o = jnp.sum(att.reshape(c, c, 512) * vj[None], axis=1)
        qe = qj * jnp.exp(bj)
        bl = bj[c - 1:c]
        ke = kj * jnp.exp(bl - bj)
        dl = jnp.exp(bl)
        inter = []
        for h in range(H_A):
            ks = slice(h * DK_A, (h + 1) * DK_A)
            st = st_ref[h]
            inter.append(_dot(qe[:, ks], st, NT))
            st_ref[h] = st * dl[:, ks] + _dot(vj[:, h * DV_A:(h + 1) * DV_A], ke[:, ks], TN)
        o = o + jnp.concatenate(inter, axis=-1)
        o_ref[r0:r0 + c, :] = _gla_out(o, g_ref[r0:r0 + c, :], gn_ref[...])

    @pl.when(i == pl.num_programs(0) - 1)
    def _():
        for h in range(H_A):
            sfin_ref[h] = st_ref[h].T


def _gla_prompt(cols, wg2p, bg, gn, seq=SEQ):
    tb = GLA_TB
    return pl.pallas_call(
        _gla_kernel,
        grid=(seq // tb,),
        in_specs=[_cols(512, C_QKA, tb), _cols(512, C_VA, tb), _cols(512, C_GA, tb), _cols(128, C_TAIL, tb),
                  _const((128, 256)), _const((1, 256)), _const((1, DV_A))],
        out_specs=[pl.BlockSpec((tb, 512), lambda i: (i, 0)), _const((H_A, DK_A, DV_A))],
        out_shape=[jax.ShapeDtypeStruct((seq, 512), f32), jax.ShapeDtypeStruct((H_A, DK_A, DV_A), f32)],
        scratch_shapes=[pltpu.VMEM((H_A, DV_A, DK_A), f32), pltpu.VMEM((GLA_C * GLA_C, 256), f32)],
        compiler_params=_params("arbitrary"),
        name="gla_prompt",
    )(cols, cols, cols, cols, wg2p, bg, gn)


SSD_TB = 128
SSD_C = 64


def _ssd_conv(ext, conv_w, conv_b, rows):
    out = conv_b
    for j in range(CONV_W):
        shifted = pltpu.roll(ext, j, 0) if j else ext
        out = out + shifted[8:8 + rows] * conv_w[CONV_W - 1 - j:CONV_W - j]
    return out


def _ssd_dt(tail, ex, dtb_x):
    return _softplus(_dot(tail, ex, hi=True) + dtb_x)


def _ssd_out(y, z, gn):
    y = y * _silu(z)
    w = GROUP_W // G_B
    outs = []
    for g in range(G_B):
        yg = y[:, g * w:(g + 1) * w]
        outs.append(yg * lax.rsqrt(jnp.mean(yg * yg, axis=-1, keepdims=True) + 1e-6) * gn[:, g * w:(g + 1) * w])
    return jnp.concatenate(outs, axis=-1)


def _ssd_kernel(z_ref, xs_ref, bc_ref, tail_ref, cw_ref, cb_ref, ex_ref, dtb_ref, alog_ref, dskip_ref, gn_ref,
                sel_ref, o_ref, hfin_ref, carry_ref, h_ref, y_ref):
    i = pl.program_id(0)

    @pl.when(i == 0)
    def _():
        carry_ref[...] = jnp.zeros_like(carry_ref)
        h_ref[...] = jnp.zeros_like(h_ref)

    tb, c = SSD_TB, SSD_C
    u = jnp.concatenate([xs_ref[...], bc_ref[...]], axis=-1)
    ext = jnp.concatenate([carry_ref[...], u], axis=0)
    carry_ref[...] = u[tb - 8:tb]
    act = _silu(_ssd_conv(ext, cw_ref[...], cb_ref[...], tb))
    xs = act[:, :GROUP_W]
    dtx = _ssd_dt(tail_ref[...], ex_ref[...], dtb_ref[...])
    la = dtx * (-jnp.exp(alog_ref[...]))
    b = _dot(_blockdiag_tri(tb, c), la, hi=True)
    brow = _dot(sel_ref[...], b, NT, hi=True)
    xd = xs * dtx
    tri = _iota((c, c), 1) <= _iota((c, c), 0)
    for ch in range(tb // c):
        r0 = ch * c
        rows = slice(r0, r0 + c)
        scores = []
        for g in range(G_B):
            bm = act[rows, GROUP_W + g * N_B:GROUP_W + (g + 1) * N_B]
            cm = act[rows, GROUP_W + G_B * N_B + g * N_B:GROUP_W + G_B * N_B + (g + 1) * N_B]
            scores.append((_dot(cm, bm, NT), bm, cm))
        for h in range(H_B):
            hs = slice(h * P_B, (h + 1) * P_B)
            sc, bm, cm = scores[h // (H_B // G_B)]
            bh = b[rows, hs]
            dec = jnp.where(tri, jnp.exp(bh - brow[h:h + 1, r0:r0 + c]), 0.0)
            xdh = xd[rows, hs]
            hst = h_ref[h]
            y = _dot(sc * dec, xdh) + _dot(cm, hst, NT) * jnp.exp(bh)
            bl = bh[c - 1:c]
            h_ref[h] = hst * jnp.exp(bl) + _dot(xdh * jnp.exp(bl - bh), bm, TN)
            y_ref[rows, hs] = y + xs[rows, hs] * dskip_ref[:, hs]
    o_ref[...] = _ssd_out(y_ref[...], z_ref[...], gn_ref[...])

    @pl.when(i == pl.num_programs(0) - 1)
    def _():
        hfin_ref[...] = h_ref[...]


def _ssd_prompt(cols, cw, cb, ex, dtb_x, alog_x, dskip_x, gn, sel, seq=SEQ):
    tb = SSD_TB
    return pl.pallas_call(
        _ssd_kernel,
        grid=(seq // tb,),
        in_specs=[_cols(512, C_Z, tb), _cols(512, C_XS, tb), _cols(256, C_BC, tb), _cols(128, C_TAIL, tb),
                  _const((CONV_W, XBC_W)), _const((1, XBC_W)), _const((128, 512)), _const((1, 512)), _const((1, 512)),
                  _const((1, 512)), _const((1, 512)), _const((8, 512))],
        out_specs=[pl.BlockSpec((tb, 512), lambda i: (i, 0)), _const((H_B, P_B, N_B))],
        out_shape=[jax.ShapeDtypeStruct((seq, 512), f32), jax.ShapeDtypeStruct((H_B, P_B, N_B), f32)],
        scratch_shapes=[pltpu.VMEM((8, XBC_W), f32), pltpu.VMEM((H_B, P_B, N_B), f32), pltpu.VMEM((tb, 512), f32)],
        compiler_params=_params("arbitrary"),
        name="ssd_prompt",
    )(cols, cols, cols, cols, cw, cb, ex, dtb_x, alog_x, dskip_x, gn, sel)


RWKV_TB = 128
RWKV_C = 64


def _rwkv_pre(x, xprev, mu, w0, a0, k_k, k_a, w2a2, g2):
    mixed = x + (xprev - x) * mu
    r = mixed[:, :512]
    kd = mixed[:, 512:1024]
    v = mixed[:, 1024:1536]
    lw = mixed[:, 1536:1664]
    lin = jnp.where(_iota(lw.shape, 1) < 64, jnp.tanh(lw), lw)
    wa = _dot(lin, w2a2, hi=True)
    w = -_softplus(-(w0 + wa[:, :512])) - 0.5
    a = _sigmoid(a0 + wa[:, 512:])
    g = _dot(_sigmoid(mixed[:, 1664:1792]), g2)
    kk = kd * k_k
    ss = _dot(kk * kk, _segment_ones(512, N_D), hi=True)
    kk = kk * lax.rsqrt(jnp.maximum(ss, 1e-24))
    kd = kd * (1.0 + (a - 1.0) * k_a)
    return r, w, kd, v, -kk, kk * a, g


def _rwkv_post(o, r, kd, v, g, r_k, lnx_g, lnx_b):
    seg = _segment_ones(512, N_D)
    mu = _dot(o, seg, hi=True) * (1.0 / N_D)
    d = o - mu
    var = _dot(d * d, seg, hi=True) * (1.0 / N_D)
    o = d * lax.rsqrt(var + LNX_EPS) * lnx_g + lnx_b
    o = o + _dot(r * kd * r_k, seg, hi=True) * v
    return o * g


def _rwkv_kernel(d_ref, mu_ref, w0_ref, a0_ref, kk_ref, ka_ref, rk_ref, lg_ref, lb_ref, w2a2_ref, g2_ref,
                 o_ref, sfin_ref, prev_ref, s_ref, oacc_ref):
    i = pl.program_id(0)

    @pl.when(i == 0)
    def _():
        prev_ref[...] = jnp.zeros_like(prev_ref)
        s_ref[...] = jnp.zeros_like(s_ref)

    tb, c = RWKV_TB, RWKV_C
    x = d_ref[...]
    xprev = jnp.where(_iota(x.shape, 0) == 0, prev_ref[0:1, :], pltpu.roll(x, 1, 0))
    prev_ref[0:1, :] = x[tb - 1:tb]
    r, w, kd, v, alpha, beta, g = _rwkv_pre(x, xprev, mu_ref[...], w0_ref[...], a0_ref[...], kk_ref[...], ka_ref[...],
                                            w2a2_ref[...], g2_ref[...])
    ld = -jnp.exp(w)
    cum = _dot(_blockdiag_tri(tb, c), ld, hi=True)
    at = alpha * jnp.exp(cum - ld)
    rt = r * jnp.exp(cum)
    einv = jnp.exp(-cum)
    kt = kd * einv
    bt = beta * einv
    ri = _iota((c, c), 0)
    ci = _iota((c, c), 1)
    strict, incl = ci < ri, ci <= ri
    eye = jnp.where(ri == ci, 1.0, 0.0).astype(f32)
    for ch in range(tb // c):
        rows = slice(ch * c, (ch + 1) * c)
        cl = cum[ch * c + c - 1:ch * c + c]
        efin = jnp.exp(cl - cum[rows])
        kfin = kd[rows] * efin
        bfin = beta[rows] * efin
        dfin = jnp.exp(cl)
        for h in range(H_D):
            hs = slice(h * N_D, (h + 1) * N_D)
            a_, r_, k_, b_, v_ = at[rows, hs], rt[rows, hs], kt[rows, hs], bt[rows, hs], v[rows, hs]
            lb = jnp.where(strict, _dot(a_, b_, NT, hi=True), 0.0)
            lk = jnp.where(strict, _dot(a_, k_, NT, hi=True), 0.0)
            mk = jnp.where(incl, _dot(r_, k_, NT, hi=True), 0.0)
            mb = jnp.where(incl, _dot(r_, b_, NT, hi=True), 0.0)
            tinv = eye + lb
            p = lb
            for _ in range(5):
                p = _dot(p, p, hi=True)
                tinv = tinv + _dot(tinv, p, hi=True)
            s0 = s_ref[h]
            u = _dot(tinv, _dot(a_, s0, NT, hi=True) + _dot(lk, v_, hi=True), hi=True)
            oacc_ref[rows, hs] = _dot(r_, s0, NT, hi=True) + _dot(mk, v_, hi=True) + _dot(mb, u, hi=True)
            s_ref[h] = s0 * dfin[:, hs] + _dot(v_, kfin[:, hs], TN, hi=True) + _dot(u, bfin[:, hs], TN, hi=True)
    o_ref[...] = _rwkv_post(oacc_ref[...], r, kd, v, g, rk_ref[...], lg_ref[...], lb_ref[...])

    @pl.when(i == pl.num_programs(0) - 1)
    def _():
        sfin_ref[...] = s_ref[...]


def _rwkv_prompt(cols, mu, w0, a0, k_k, k_a, r_k, lnx_g, lnx_b, w2a2, g2, seq=SEQ):
    tb = RWKV_TB
    vec = _const((1, 512))
    return pl.pallas_call(
        _rwkv_kernel,
        grid=(seq // tb,),
        in_specs=[_cols(1792, C_D, tb), _const((1, 1792)), vec, vec, vec, vec, vec, vec, vec,
                  _const((128, 1024)), _const((128, 512))],
        out_specs=[pl.BlockSpec((tb, 512), lambda i: (i, 0)), _const((H_D, N_D, N_D))],
        out_shape=[jax.ShapeDtypeStruct((seq, 512), f32), jax.ShapeDtypeStruct((H_D, N_D, N_D), f32)],
        scratch_shapes=[pltpu.VMEM((8, 1792), f32), pltpu.VMEM((H_D, N_D, N_D), f32), pltpu.VMEM((tb, 512), f32)],
        compiler_params=_params("arbitrary"),
        name="rwkv_prompt",
    )(cols, mu, w0, a0, k_k, k_a, r_k, lnx_g, lnx_b, w2a2, g2)


def _rope_tables(pos):
    half = DK_C // 2
    inv = ROPE_THETA ** (-jnp.arange(half, dtype=f32) / half)
    ang = pos.astype(f32)[:, None] * inv[None, :]
    cos, sin = jnp.cos(ang), jnp.sin(ang)
    return jnp.tile(jnp.concatenate([cos, cos], axis=-1), (1, 2)), jnp.tile(jnp.concatenate([-sin, sin], axis=-1), (1, 2))


def _qk_norm_rope(x, gain, cos, sin):
    ms = _dot(x * x, _segment_ones(512, DK_C), hi=True) * (1.0 / DK_C)
    x = x * lax.rsqrt(ms + 1e-6) * gain
    first = (_iota(x.shape, 1) % DK_C) < (DK_C // 2)
    partner = jnp.where(first, pltpu.roll(x, 512 - DK_C // 2, 1), pltpu.roll(x, DK_C // 2, 1))
    cos = jnp.concatenate([cos] * 4, axis=-1)
    sin = jnp.concatenate([sin] * 4, axis=-1)
    return x * cos + partner * sin


def _diff_prep_kernel(q_ref, k_ref, cos_ref, sin_ref, gq_ref, gk_ref, qb_ref, kf_ref, kb_ref):
    cos, sin = cos_ref[...], sin_ref[...]
    q = _qk_norm_rope(q_ref[...], gq_ref[...], cos, sin)
    k = _qk_norm_rope(k_ref[...], gk_ref[...], cos, sin)
    qb_ref[...] = (q * (DK_C ** -0.5)).astype(bf16)
    kf_ref[...] = k
    kb_ref[...] = k.astype(bf16)


def _diff_prep(cols, cos, sin, gq, gk, row0, rows, tb):
    assert row0 % tb == 0
    r0 = row0 // tb
    colspec = lambda start: pl.BlockSpec((tb, 512), lambda i, _c=start // 512: (i + r0, _c))
    out = pl.BlockSpec((tb, 512), lambda i: (i, 0))
    tab = pl.BlockSpec((tb, 128), lambda i: (i, 0))
    return pl.pallas_call(
        _diff_prep_kernel,
        grid=(rows // tb,),
        in_specs=[colspec(C_QC), colspec(C_KC), tab, tab, _const((1, 512)), _const((1, 512))],
        out_specs=[out, out, out],
        out_shape=[jax.ShapeDtypeStruct((rows, 512), bf16), jax.ShapeDtypeStruct((rows, 512), f32),
                   jax.ShapeDtypeStruct((rows, 512), bf16)],
        compiler_params=_params("parallel"),
        name="diff_prep",
    )(cols, cols, cos, sin, gq, gk)


def _diff_finish(o1, o2, lam, gn, lam_init):
    o = o1 - lam * o2
    return o * lax.rsqrt(jnp.mean(o * o, axis=-1, keepdims=True) + 1e-6) * gn * (1.0 - lam_init)


def _flash_kernel(q_ref, k_ref, v_ref, lam_ref, gn_ref, o_ref, m_ref, l_ref, acc_ref, *, tq, tk, lam_init):
    qi, kj = pl.program_id(1), pl.program_id(2)

    @pl.when(kj == 0)
    def _():
        m_ref[...] = jnp.full_like(m_ref, -jnp.inf)
        l_ref[...] = jnp.zeros_like(l_ref)
        acc_ref[...] = jnp.zeros_like(acc_ref)

    @pl.when(kj * tk <= qi * tq + tq - 1)
    def _():
        q = q_ref[...]
        k = k_ref[...]
        v = v_ref[...].astype(bf16)
        lane = _iota(q.shape, 1)
        keep = (kj * tk + _iota((tq, tk), 1)) <= (qi * tq + _iota((tq, tk), 0))
        for m in range(2):
            qm = jnp.where((lane < DK_C) if m == 0 else (lane >= DK_C), q, jnp.zeros_like(q))
            s = jnp.where(keep, _dot(qm, k, NT), -jnp.inf)
            m_old = m_ref[m]
            m_new = jnp.maximum(m_old, jnp.max(s, axis=-1, keepdims=True))
            p = jnp.exp(s - m_new)
            corr = jnp.exp(m_old - m_new)
            l_ref[m] = corr * l_ref[m] + jnp.sum(p, axis=-1, keepdims=True)
            acc_ref[m] = corr * acc_ref[m] + _dot(p, v)
            m_ref[m] = m_new

    @pl.when(kj == pl.num_programs(2) - 1)
    def _():
        o_ref[...] = _diff_finish(acc_ref[0] / l_ref[0], acc_ref[1] / l_ref[1], lam_ref[...], gn_ref[...], lam_init)


def _diff_attn_prompt(qb, kb, cols, lam, gn, lam_init, seq=SEQ, tq=512, tk=512):
    nq, nk = seq // tq, seq // tk
    last = lambda i: (i * tq + tq - 1) // tk
    kmap = lambda h, i, j: (jnp.minimum(j, last(i)), h)
    vmap_ = lambda h, i, j: (jnp.minimum(j, last(i)), C_VC // DV_C + h)
    return pl.pallas_call(
        functools.partial(_flash_kernel, tq=tq, tk=tk, lam_init=lam_init),
        grid=(H_C, nq, nk),
        in_specs=[pl.BlockSpec((tq, 128), lambda h, i, j: (i, h)), pl.BlockSpec((tk, 128), kmap),
                  pl.BlockSpec((tk, DV_C), vmap_), pl.BlockSpec((1, 128), lambda h, i, j: (0, 0)),
                  pl.BlockSpec((1, DV_C), lambda h, i, j: (0, 0))],
        out_specs=pl.BlockSpec((tq, DV_C), lambda h, i, j: (i, h)),
        out_shape=jax.ShapeDtypeStruct((seq, 512), f32),
        scratch_shapes=[pltpu.VMEM((2, tq, 1), f32), pltpu.VMEM((2, tq, 1), f32), pltpu.VMEM((2, tq, DV_C), f32)],
        compiler_params=_params("parallel", "parallel", "arbitrary"),
        name="diff_attn_prompt",
    )(qb, kb, cols, lam, gn)


PEER_SEL_TB = 128
PEER_TB = 640
PEER_EB = 512
_PAIRS = [(a, b) for a in range(TOPK_P) for b in range(TOPK_P) if (a + 1) * (b + 1) <= TOPK_P]
_NPAIR = -(-len(_PAIRS) // 8) * 8


def _top16_ranks(s):
    n_idx = _iota(s.shape, 0)
    rank = jnp.full(s.shape, float(TOPK_P), f32)
    tops = []
    work = s
    for k in range(TOPK_P):
        m = jnp.max(work, axis=0, keepdims=True)
        idx = jnp.min(jnp.where(work == m, n_idx, N_KEYS), axis=0, keepdims=True)
        hit = n_idx == idx
        rank = jnp.where(hit, float(k), rank)
        work = jnp.where(hit, -jnp.inf, work)
        tops.append(m)
    return rank, jnp.concatenate(tops, axis=0)


def _peer_select_kernel(q_ref, keys_ref, pk1_ref, flat_ref, ci_ref, cnt_ref, e2_ref, r2_ref):
    tb = q_ref.shape[0]
    flat = flat_ref[...]
    for h in range(H_P):
        ranks, tops, scores = [], [], []
        for x in range(2):
            hx = 2 * h + x
            s = _dot(keys_ref[hx], q_ref[:, hx * 128:(hx + 1) * 128], NT, hi=True)
            rk, tp = _top16_ranks(s)
            ranks.append(rk), tops.append(tp), scores.append(s)
        t1, t2 = tops
        cand = jnp.concatenate([t1[a:a + 1] + t2[b:b + 1] for a, b in _PAIRS]
                               + [jnp.full((_NPAIR - len(_PAIRS), tb), -jnp.inf, f32)], axis=0)
        work = cand
        sel = jnp.zeros(cand.shape, f32)
        for _ in range(TOPK_P):
            m = jnp.max(work, axis=0, keepdims=True)
            idx = jnp.min(jnp.where(work == m, flat, 4096.0), axis=0, keepdims=True)
            hit = flat == idx
            sel = jnp.where(hit, 1.0, sel)
            work = jnp.where(hit, -jnp.inf, work)
        top = t1[0:1] + t2[0:1]
        z = jnp.sum(sel * jnp.exp(jnp.where(sel > 0, cand - top, 0.0)), axis=0, keepdims=True)
        cnt = _dot(pk1_ref[...], sel)
        cnt_i = jnp.zeros((N_KEYS, tb), f32)
        for k1 in range(TOPK_P):
            cnt_i = cnt_i + jnp.where(ranks[0] == float(k1), cnt[k1:k1 + 1], 0.0)
        ci_ref[0, h] = jnp.exp(scores[0] - t1[0:1]) / z
        cnt_ref[0, h] = cnt_i
        e2_ref[0, h] = jnp.exp(scores[1] - t2[0:1])
        r2_ref[0, h] = ranks[1]


def _peer_select(q, keys):
    t = q.shape[0]
    tb, per = PEER_SEL_TB, PEER_TB // PEER_SEL_TB
    pk1 = np.zeros((TOPK_P, _NPAIR), np.float32)
    flat = np.full((_NPAIR, 1), 8192.0, np.float32)
    for r, (a, b) in enumerate(_PAIRS):
        pk1[a, r] = 1.0
        flat[r, 0] = a * TOPK_P + b
    out = pl.BlockSpec((1, H_P, N_KEYS, tb), lambda i: (i // per, 0, 0, i % per))
    shp = jax.ShapeDtypeStruct((t // PEER_TB, H_P, N_KEYS, PEER_TB), f32)
    return pl.pallas_call(
        _peer_select_kernel,
        grid=(t // tb,),
        in_specs=[pl.BlockSpec((tb, 2048), lambda i: (i, 0)), _const((2 * H_P, N_KEYS, 128)),
                  _const((TOPK_P, _NPAIR)), _const((_NPAIR, 1))],
        out_specs=[out, out, out, out],
        out_shape=[shp, shp, shp, shp],
        compiler_params=_params("parallel"),
        name="peer_select",
    )(q, keys, jnp.asarray(pk1), jnp.asarray(flat))


def _sample_pre_kernel(qk_ref, tail_ref, xs_ref, bc_ref, d_ref, conv_ref, shift_ref,
                       wg2_ref, bg_ref, cw_ref, cb_ref, ex_ref, dtb_ref, alog_ref,
                       mu_ref, w0_ref, a0_ref, kk_ref, ka_ref, w2a2_ref, g2_ref,
                       gq_ref, gk_ref, ga_ref, act_ref, sdec_ref, sxd_ref, rw_ref):
    qk = qk_ref[...]
    la = _gla_gate_log(tail_ref[...], wg2_ref[...], bg_ref[...])
    gq_ref[...] = qk[:, :256] * (DK_A ** -0.5)
    gk_ref[...] = qk[:, 256:]
    ga_ref[...] = jnp.exp(la)
    u = jnp.concatenate([xs_ref[...], bc_ref[...]], axis=-1)
    cw = cw_ref[...]
    conv = cb_ref[...] + u * cw[CONV_W - 1:CONV_W]
    for j in range(CONV_W - 1):
        conv = conv + conv_ref[j] * cw[j:j + 1]
    act = _silu(conv)
    act_ref[...] = act
    dtx = _ssd_dt(tail_ref[...], ex_ref[...], dtb_ref[...])
    sdec_ref[...] = jnp.exp(dtx * (-jnp.exp(alog_ref[...])))
    sxd_ref[...] = act[:, :GROUP_W] * dtx
    r, w, kd, v, alpha, beta, g = _rwkv_pre(d_ref[...], shift_ref[...], mu_ref[...], w0_ref[...], a0_ref[...],
                                            kk_ref[...], ka_ref[...], w2a2_ref[...], g2_ref[...])
    for n, t in enumerate((r, jnp.exp(-jnp.exp(w)), kd, v, alpha, beta, g)):
        rw_ref[n] = t


def _sample_pre(cols, conv_st, shift_st, wts, row0, b):
    tb = b
    assert row0 % tb == 0
    r0 = row0 // tb
    cs = lambda w, start: pl.BlockSpec((tb, w), lambda i, _c=start // w: (r0, _c))
    full = lambda *s: jax.ShapeDtypeStruct(s, f32)
    return pl.pallas_call(
        _sample_pre_kernel,
        grid=(1,),
        in_specs=[cs(512, C_QKA), cs(128, C_TAIL), cs(512, C_XS), cs(256, C_BC), cs(1792, C_D),
                  _const((CONV_W - 1, b, XBC_W)), _const((b, 1792))] + [_const(w.shape) for w in wts],
        out_specs=[_const((b, 256))] * 3 + [_const((b, XBC_W)), _const((b, 512)), _const((b, 512)), _const((7, b, 512))],
        out_shape=[full(b, 256)] * 3 + [full(b, XBC_W), full(b, 512), full(b, 512), full(7, b, 512)],
        compiler_params=_params("arbitrary"),
        name="sample_pre",
    )(cols, cols, cols, cols, cols, conv_st, shift_st, *wts)


def _rows_to_tile(row, heads, width, reps):
    return jnp.concatenate([jnp.broadcast_to(row[:, h * width:(h + 1) * width], (reps, width)) for h in range(heads)], axis=0)


def _gla_step_kernel(s_ref, a_ref, k_ref, q_ref, v_ref, sn_ref, o_ref):
    s = a_ref[0] * s_ref[0] + k_ref[0] * _rows_to_tile(v_ref[0], H_A, DV_A, DK_A)
    sn_ref[0] = s
    qs = q_ref[0] * s
    o_ref[0] = jnp.concatenate([jnp.sum(qs[h * DK_A:(h + 1) * DK_A], axis=0, keepdims=True) for h in range(H_A)], axis=-1)


def _ssd_step_kernel(h_ref, dec_ref, xd_ref, bc_ref, hn_ref, y_ref):
    bc = bc_ref[0]
    reps = (H_B // G_B) * P_B
    hn = dec_ref[0] * h_ref[0] + xd_ref[0] * _rows_to_tile(bc[:, :G_B * N_B], G_B, N_B, reps)
    hn_ref[0] = hn
    y_ref[0] = jnp.sum(hn * _rows_to_tile(bc[:, G_B * N_B:], G_B, N_B, reps), axis=-1, keepdims=True)


def _rwkv_step_kernel(s_ref, rows_ref, v_ref, sn_ref, o_ref):
    tile = lambda n: _rows_to_tile(rows_ref[0, n:n + 1, :], H_D, N_D, N_D)
    s = s_ref[0]
    sa = jnp.sum(s * tile(4), axis=-1, keepdims=True)
    s = s * tile(1) + sa * tile(5) + v_ref[0] * tile(2)
    sn_ref[0] = s
    o_ref[0] = jnp.sum(s * tile(0), axis=-1, keepdims=True)


def _state_step(kernel, name, state, ins, outs):
    b = state.shape[0]
    spec = lambda shp: pl.BlockSpec((1,) + tuple(shp[1:]), lambda i: (i, 0, 0))
    return pl.pallas_call(
        kernel,
        grid=(b,),
        in_specs=[spec(state.shape)] + [spec(a.shape) for a in ins],
        out_specs=[spec(state.shape)] + [spec(s) for s in outs],
        out_shape=[jax.ShapeDtypeStruct(state.shape, f32)] + [jax.ShapeDtypeStruct(s, f32) for s in outs],
        compiler_params=_params("parallel"),
        name=name,
    )(state, *ins)


def _diff_decode_kernel(pt_ref, q_ref, ks_ref, vs_ref, kp_ref, vp_ref, ind_ref, lam_ref, gn_ref, o_ref,
                        m_ref, l_ref, acc_ref, *, lam_init, n_pages):
    del pt_ref
    p_i = pl.program_id(1)
    q = q_ref[0].astype(f32)

    @pl.when(p_i == 0)
    def _():
        m_ref[...] = jnp.full_like(m_ref, -jnp.inf)
        l_ref[...] = jnp.zeros_like(l_ref)
        acc_ref[...] = jnp.zeros_like(acc_ref)

    def absorb(k, v):
        kq = k * q
        if kq.shape[0] == 1:
            s = _dot(jnp.broadcast_to(kq, (8, 512)), ind_ref[...], hi=True)[0:1]
        else:
            s = _dot(kq, ind_ref[...], hi=True)
        m_old = m_ref[...]
        m_new = jnp.maximum(m_old, jnp.max(s, axis=0, keepdims=True))
        p = jnp.exp(s - m_new)
        corr = jnp.exp(m_old - m_new)
        l_ref[...] = corr * l_ref[...] + jnp.sum(p, axis=0, keepdims=True)
        m_ref[...] = m_new
        for h in range(H_C):
            vh = v[:, h * DV_C:(h + 1) * DV_C]
            for m in range(2):
                c = 2 * h + m
                acc_ref[c:c + 1, :] = corr[:, c:c + 1] * acc_ref[c:c + 1, :] + jnp.sum(p[:, c:c + 1] * vh, axis=0, keepdims=True)

    absorb(kp_ref[0], vp_ref[0])

    @pl.when(p_i == n_pages - 1)
    def _():
        absorb(ks_ref[0], vs_ref[0])
        outs = []
        for h in range(H_C):
            o = []
            for m in range(2):
                c = 2 * h + m
                o.append(acc_ref[c:c + 1, :] / l_ref[:, c:c + 1])
            outs.append(_diff_finish(o[0], o[1], lam_ref[...], gn_ref[...], lam_init))
        o_ref[0] = jnp.concatenate(outs, axis=-1)

def _diff_decode(pt_flat, qb, ks, vs, ck, cv, ind, lam, gn, lam_init, n_pages):
    b = qb.shape[0]
    row = pl.BlockSpec((1, 1, 512), lambda i, p, pt: (i, 0, 0))
    page = pl.BlockSpec((1, PAGE_SIZE, 512), lambda i, p, pt: (pt[i * n_pages + p], 0, 0))
    cst = lambda shp: pl.BlockSpec(shp, lambda i, p, pt: (0, 0))
    return pl.pallas_call(
        functools.partial(_diff_decode_kernel, lam_init=lam_init, n_pages=n_pages),
        grid_spec=pltpu.PrefetchScalarGridSpec(
            num_scalar_prefetch=1, grid=(b, n_pages),
            in_specs=[row, row, row, page, page, cst((512, 128)), cst((1, 128)), cst((1, DV_C))],
            out_specs=row,
            scratch_shapes=[pltpu.VMEM((1, 128), f32), pltpu.VMEM((1, 128), f32), pltpu.VMEM((8, DV_C), f32)]),
        out_shape=jax.ShapeDtypeStruct((b, 1, 512), f32),
        compiler_params=_params("parallel", "arbitrary"),
        name="diff_decode",
    )(pt_flat, qb, ks, vs, ck, cv, ind, lam, gn)


def _sample_post_kernel(oa_ref, ga_ref, gn_a_ref, y_ref, act_ref, z_ref, dskip_ref, gn_b_ref, oc_ref,
                        od_ref, rw_ref, rk_ref, lg_ref, lb_ref, o_ref):
    oa = _gla_out(oa_ref[...], ga_ref[...], gn_a_ref[...])
    ob = _ssd_out(y_ref[...] + act_ref[:, :GROUP_W] * dskip_ref[...], z_ref[...], gn_b_ref[...])
    od = _rwkv_post(od_ref[...], rw_ref[0], rw_ref[2], rw_ref[3], rw_ref[6], rk_ref[...], lg_ref[...], lb_ref[...])
    o_ref[...] = jnp.concatenate([oa, ob, oc_ref[...], od], axis=-1)


def _sample_post(cols, oa, y, act, oc, od, rw, gn_a, dskip_x, gn_b, r_k, lnx_g, lnx_b, row0, b):
    r0 = row0 // b
    cs = lambda w, start: pl.BlockSpec((b, w), lambda i, _c=start // w: (r0, _c))
    c512 = _const((b, 512))
    v512 = _const((1, 512))
    return pl.pallas_call(
        _sample_post_kernel,
        grid=(1,),
        in_specs=[c512, cs(512, C_GA), _const((1, DV_A)), c512, _const((b, XBC_W)), cs(512, C_Z), v512, v512, c512,
                  c512, _const((7, b, 512)), v512, v512, v512],
        out_specs=_const((b, D_MODEL)),
        out_shape=jax.ShapeDtypeStruct((b, D_MODEL), f32),
        compiler_params=_params("arbitrary"),
        name="sample_post",
    )(oa, cols, gn_a, y, act, cols, dskip_x, gn_b, oc, od, rw, r_k, lnx_g, lnx_b)


def _head_rms(x, gain, width):
    outs = []
    for h in range(x.shape[1] // width):
        xh = x[:, h * width:(h + 1) * width]
        outs.append(xh * lax.rsqrt(jnp.mean(xh * xh, axis=-1, keepdims=True) + 1e-6) * gain)
    return jnp.concatenate(outs, axis=-1)


def _mem_kv_kernel(m_ref, g_ref, w_ref, gk_ref, k_ref, v_ref):
    m = m_ref[...]
    m = m * lax.rsqrt(jnp.mean(m * m, axis=-1, keepdims=True) + 1e-6) * g_ref[...]
    kv = _dot(m, w_ref[...])
    k_ref[...] = _head_rms(kv[:, :D_MEM], gk_ref[...], DH_M)
    v_ref[...] = kv[:, D_MEM:]


def _mem_kv(mem, g_src, w_kv, g_k):
    shp = jax.ShapeDtypeStruct((N_MEM, D_MEM), f32)
    return pl.pallas_call(
        _mem_kv_kernel, out_shape=[shp, shp],
        compiler_params=pltpu.CompilerParams(vmem_limit_bytes=VMEM_LIMIT), name="mem_kv",
    )(mem, g_src, w_kv, g_k)


def _mem_attn_prompt_kernel(q_ref, gq_ref, k_ref, v_ref, o_ref):
    q = _head_rms(q_ref[...], gq_ref[...], DH_M) * (DH_M ** -0.5)
    k, v = k_ref[...], v_ref[...]
    outs = []
    for h in range(H_M):
        hs = slice(h * DH_M, (h + 1) * DH_M)
        s = _dot(q[:, hs], k[:, hs], NT)
        p = jnp.exp(s - jnp.max(s, axis=-1, keepdims=True))
        outs.append(_dot(p, v[:, hs]) / jnp.sum(p, axis=-1, keepdims=True))
    o_ref[...] = jnp.concatenate(outs, axis=-1)


def _mem_attn_prompt(q, gq, k, v, seq=SEQ, tb=512):
    return pl.pallas_call(
        _mem_attn_prompt_kernel,
        grid=(seq // tb,),
        in_specs=[pl.BlockSpec((tb, D_MEM), lambda i: (i, 0)), _const((1, DH_M)), _const((N_MEM, D_MEM)),
                  _const((N_MEM, D_MEM))],
        out_specs=pl.BlockSpec((tb, D_MEM), lambda i: (i, 0)),
        out_shape=jax.ShapeDtypeStruct((seq, D_MEM), f32),
        compiler_params=_params("parallel"),
        name="mem_attn_prompt",
    )(q, gq, k, v)


def _mem_attn_sample_kernel(q_ref, gq_ref, k_ref, v_ref, o_ref):
    q = _head_rms(q_ref[0], gq_ref[...], DH_M) * (DH_M ** -0.5)
    qk = k_ref[0] * q
    outs = []
    for h in range(H_M):
        hs = slice(h * DH_M, (h + 1) * DH_M)
        s = jnp.sum(qk[:, hs], axis=-1, keepdims=True)
        p = jnp.exp(s - jnp.max(s, axis=0, keepdims=True))
        outs.append(jnp.sum(p * v_ref[0, :, hs], axis=0, keepdims=True) / jnp.sum(p, axis=0, keepdims=True))
    o_ref[0] = jnp.concatenate(outs, axis=-1)


def _mem_attn_sample(q, gq, ck, cv):
    b = q.shape[0]
    row = pl.BlockSpec((1, 1, D_MEM), lambda i: (i, 0, 0))
    kv = pl.BlockSpec((1, N_MEM, D_MEM), lambda i: (i, 0, 0))
    return pl.pallas_call(
        _mem_attn_sample_kernel,
        grid=(b,),
        in_specs=[row, _const((1, DH_M)), kv, kv],
        out_specs=row,
        out_shape=jax.ShapeDtypeStruct((b, 1, D_MEM), f32),
        compiler_params=_params("parallel"),
        name="mem_attn_sample",
    )(q, gq, ck, cv)


def _gelu(x):
    return 0.5 * x * (1.0 + jnp.tanh(0.7978845608028654 * (x + 0.044715 * x * x * x)))


def _peer_dense_kernel(xn_ref, u_ref, v_ref, ci_ref, cnt_ref, e2_ref, r2_ref, res_ref, o_ref, w_ref):
    e = pl.program_id(1)
    tb = xn_ref.shape[0]
    n_i = PEER_EB // N_KEYS

    @pl.when(e == 0)
    def _():
        o_ref[...] = res_ref[...]

    for ii in range(n_i):
        w = jnp.zeros((N_KEYS, tb), f32)
        for h in range(H_P):
            row = pl.ds(e * n_i + ii, 1)
            ci = ci_ref[0, h, row, :]
            cnt = cnt_ref[0, h, row, :]
            w = w + jnp.where(r2_ref[0, h] < cnt, e2_ref[0, h] * ci, 0.0)
        w_ref[ii * N_KEYS:(ii + 1) * N_KEYS, :] = w
    hid = _gelu(_dot(u_ref[...], xn_ref[...], NT))
    o_ref[...] += _dot(hid * w_ref[...], v_ref[...], TN)


def _peer_dense(xn, u, v, sel, res):
    t = xn.shape[0]
    tb, eb = PEER_TB, PEER_EB
    once = pl.Buffered(1)
    selspec = pl.BlockSpec((1, H_P, N_KEYS, tb), lambda i, e: (i, 0, 0, 0), pipeline_mode=once)
    return pl.pallas_call(
        _peer_dense_kernel,
        grid=(t // tb, N_EXPERTS // eb),
        in_specs=[pl.BlockSpec((tb, D_MODEL), lambda i, e: (i, 0), pipeline_mode=once),
                  pl.BlockSpec((eb, D_MODEL), lambda i, e: (e, 0)),
                  pl.BlockSpec((eb, D_MODEL), lambda i, e: (e, 0)), selspec, selspec, selspec, selspec,
                  pl.BlockSpec((tb, D_MODEL), lambda i, e: (i, 0), pipeline_mode=once)],
        out_specs=pl.BlockSpec((tb, D_MODEL), lambda i, e: (i, 0)),
        out_shape=jax.ShapeDtypeStruct((t, D_MODEL), f32),
        scratch_shapes=[pltpu.VMEM((eb, tb), f32)],
        compiler_params=_params("parallel", "arbitrary"),
        name="peer_dense",
    )(xn, u, v, *sel, res)


def _pad_w_in(w):
    a0, b0, c0, d0 = 0, 1552, 2840, 4376
    seg = lambda s, n: w[:, s:s + n]
    parts = [seg(d0, 1792), seg(b0 + 1024, 256), seg(a0, 512), seg(a0 + 512, 512), seg(a0 + 1024, 512),
             seg(b0, 512), seg(b0 + 512, 512), seg(c0, 512), seg(c0 + 512, 512), seg(c0 + 1024, 512),
             seg(a0 + 1536, 16), seg(b0 + 1280, 8), jnp.zeros((w.shape[0], IN_PAD - 6168), w.dtype)]
    return jnp.concatenate(parts, axis=1).astype(bf16)


def _layer_consts():
    ex = np.zeros((128, 512), np.float32)
    sel = np.zeros((8, 512), np.float32)
    ind = np.zeros((512, 128), np.float32)
    for h in range(8):
        ex[GK_RANK + h, h * 64:(h + 1) * 64] = 1.0
        sel[h, h * 64] = 1.0
        ind[h * 64:(h + 1) * 64, h] = 1.0
    return jnp.asarray(ex), jnp.asarray(sel), jnp.asarray(ind)


def kernel(x_prompt, x_sample, cache_diff_k, cache_diff_v, cache_mem_k, cache_mem_v, state_gla, state_ssm, state_conv, state_rwkv, state_shift, page_table, mem_prompt, norm_mix, w_in, w_out, gla_wg2, gla_bg, gla_gn, conv_w, conv_b, dt_bias, a_log, d_skip, ssm_gn, dq_norm, dk_norm, lam_q, lam_k, diff_gn, shift_mu, w0, w2, a0, a2, g2, k_k, k_a, r_k, lnx_g, lnx_b, norm_mem, norm_memsrc, w_mq, w_mk, w_mv, w_mo, mq_norm, mk_norm, norm_ffn, peer_wq, peer_keys, peer_u, peer_v):
    nb = DEC_BATCH
    n_pages = page_table.shape[1]
    n_pool = cache_diff_k.shape[1]
    x = jnp.concatenate([x_prompt[0], x_sample[:, 0]], axis=0)
    pt_flat = page_table.reshape(-1)
    cos_p, sin_p = _rope_tables(jnp.arange(SEQ, dtype=jnp.int32))
    cos_s, sin_s = _rope_tables(jnp.full((nb,), PAST_LEN, jnp.int32))
    ex, sel8, ind = _layer_consts()
    row = lambda a: a.reshape(1, -1)
    rep64 = lambda a: jnp.repeat(a, 64).reshape(1, 512)
    outs = {n: [] for n in ('kp', 'vp', 'ks', 'vs', 'mk', 'mv', 'gla_p', 'gla_s', 'ssm_p', 'ssm_s', 'conv_p',
                            'conv_s', 'rwkv_p', 'rwkv_s', 'shift_p', 'shift_s')}
    for l in range(DEPTH):
        lam_init = 0.8 - 0.6 * math.exp(-0.3 * l)
        lq, lk = lam_q[l], lam_k[l]
        lam = jnp.exp(jnp.sum(lq[0] * lk[0])) - jnp.exp(jnp.sum(lq[1] * lk[1])) + lam_init
        lam = jnp.full((1, 128), lam, f32)
        wg2p = jnp.zeros((128, 256), f32).at[:GK_RANK].set(gla_wg2[l])
        w2a2 = jnp.zeros((128, 1024), f32).at[:64, :512].set(w2[l]).at[64:, 512:].set(a2[l])
        gq = jnp.tile(dq_norm[l].reshape(128), 4).reshape(1, 512)
        gk = jnp.tile(dk_norm[l].reshape(128), 4).reshape(1, 512)
        dtb_x, alog_x, dskip_x = rep64(dt_bias[l]), rep64(a_log[l]), rep64(d_skip[l])

        cols = _matmul(x, _pad_w_in(w_in[l]), gain=norm_mix[l])

        oa, gla_p = _gla_prompt(cols, wg2p, row(gla_bg[l]), row(gla_gn[l]), seq=SEQ)
        ob, ssm_p = _ssd_prompt(cols, conv_w[l], row(conv_b[l]), ex, dtb_x, alog_x, dskip_x, row(ssm_gn[l]), sel8,
                                seq=SEQ)
        qb, kf, kb = _diff_prep(cols, cos_p, sin_p, gq, gk, 0, SEQ, 512)
        oc = _diff_attn_prompt(qb, kb, cols, lam, row(diff_gn[l]), lam_init, seq=SEQ)
        od, rwkv_p = _rwkv_prompt(cols, row(shift_mu[l]), row(w0[l]), row(a0[l]), row(k_k[l]), row(k_a[l]),
                                  row(r_k[l]), row(lnx_g[l]), row(lnx_b[l]), w2a2, g2[l], seq=SEQ)
        mix_p = jnp.concatenate([oa, ob, oc, od], axis=1)

        pre_w = [wg2p, row(gla_bg[l]), conv_w[l], row(conv_b[l]), ex, dtb_x, alog_x,
                 row(shift_mu[l]), row(w0[l]), row(a0[l]), row(k_k[l]), row(k_a[l]), w2a2, g2[l]]
        s_gq, s_gk, s_ga, s_act, s_dec, s_xd, s_rw = _sample_pre(cols, jnp.transpose(state_conv[l], (1, 0, 2)),
                                                                 state_shift[l][:, 0], pre_w, SEQ, nb)
        col = lambda a: a.reshape(nb, -1, 1)
        tail = cols[SEQ:]
        gla_s, oa_s = _state_step(_gla_step_kernel, "gla_step", state_gla[l].reshape(nb, H_A * DK_A, DV_A),
                                  [col(s_ga), col(s_gk), col(s_gq), tail[:, C_VA:C_VA + 512].reshape(nb, 1, 512)],
                                  [(nb, 1, 512)])
        ssm_s, y_s = _state_step(_ssd_step_kernel, "ssd_step", state_ssm[l].reshape(nb, H_B * P_B, N_B),
                                 [col(s_dec), col(s_xd), s_act[:, GROUP_W:].reshape(nb, 1, 256)], [(nb, H_B * P_B, 1)])
        rwkv_s, od_s = _state_step(_rwkv_step_kernel, "rwkv_step", state_rwkv[l].reshape(nb, H_D * N_D, N_D),
                                   [jnp.transpose(s_rw, (1, 0, 2)), col(s_rw[3])], [(nb, H_D * N_D, 1)])
        qb_s, kf_s, _ = _diff_prep(cols, cos_s, sin_s, gq, gk, SEQ, nb, nb)
        vc_s = tail[:, C_VC:C_VC + 512]
        oc_s = _diff_decode(pt_flat, qb_s.reshape(nb, 1, 512), kf_s.reshape(nb, 1, 512), vc_s.reshape(nb, 1, 512),
                            cache_diff_k[l].reshape(n_pool, PAGE_SIZE, 512), cache_diff_v[l].reshape(n_pool, PAGE_SIZE, 512),
                            ind, lam, row(diff_gn[l]), lam_init, n_pages)
        mix_s = _sample_post(cols, oa_s.reshape(nb, 512), y_s.reshape(nb, 512), s_act, oc_s.reshape(nb, 512),
                             od_s.reshape(nb, 512), s_rw, row(gla_gn[l]), dskip_x, row(ssm_gn[l]), row(r_k[l]),
                             row(lnx_g[l]), row(lnx_b[l]), SEQ, nb)

        x = _matmul(jnp.concatenate([mix_p, mix_s], axis=0), w_out[l].astype(bf16), res=x)

        mk_p, mv_p = _mem_kv(mem_prompt[0], row(norm_memsrc[l]),
                             jnp.concatenate([w_mk[l], w_mv[l]], axis=1).astype(bf16), row(mk_norm[l]))
        qm = _matmul(x, w_mq[l].astype(bf16), gain=norm_mem[l])
        om_p = _mem_attn_prompt(qm, row(mq_norm[l]), mk_p, mv_p, seq=SEQ)
        om_s = _mem_attn_sample(qm[SEQ:].reshape(nb, 1, D_MEM), row(mq_norm[l]),
                                cache_mem_k[l].reshape(nb, N_MEM, D_MEM), cache_mem_v[l].reshape(nb, N_MEM, D_MEM))
        x = _matmul(jnp.concatenate([om_p, om_s.reshape(nb, D_MEM)], axis=0), w_mo[l].astype(bf16), res=x)

        qp, xn = _matmul(x, peer_wq[l].astype(bf16), gain=norm_ffn[l], emit_xn=True, tn=512)
        picks = _peer_select(qp, peer_keys[l].reshape(2 * H_P, N_KEYS, D_PK // 2))
        x = _peer_dense(xn, peer_u[l].astype(bf16), peer_v[l].astype(bf16), picks, x)

        outs['kp'].append(kf.reshape(1, SEQ, H_C, 2 * DK_C))
        outs['vp'].append(cols[:SEQ, C_VC:C_VC + 512].reshape(1, SEQ, H_C, DV_C))
        outs['ks'].append(kf_s.reshape(nb, 1, H_C, 2 * DK_C))
        outs['vs'].append(vc_s.reshape(nb, 1, H_C, DV_C))
        outs['mk'].append(mk_p.reshape(1, N_MEM, H_M, DH_M))
        outs['mv'].append(mv_p.reshape(1, N_MEM, H_M, DH_M))
        outs['gla_p'].append(gla_p[None])
        outs['gla_s'].append(gla_s.reshape(nb, H_A, DK_A, DV_A))
        outs['ssm_p'].append(ssm_p[None])
        outs['ssm_s'].append(ssm_s.reshape(nb, H_B, P_B, N_B))
        u_p = jnp.concatenate([cols[SEQ - 3:SEQ, C_XS:C_XS + 512], cols[SEQ - 3:SEQ, C_BC:C_BC + 256]], axis=1)
        u_s = jnp.concatenate([tail[:, C_XS:C_XS + 512], tail[:, C_BC:C_BC + 256]], axis=1)
        outs['conv_p'].append(u_p[None])
        outs['conv_s'].append(jnp.concatenate([state_conv[l][:, 1:], u_s[:, None]], axis=1))
        outs['rwkv_p'].append(rwkv_p[None])
        outs['rwkv_s'].append(rwkv_s.reshape(nb, H_D, N_D, N_D))
        outs['shift_p'].append(cols[SEQ - 1:SEQ, C_D:C_D + 1792][None])
        outs['shift_s'].append(tail[:, C_D:C_D + 1792][:, None])
    st = {n: jnp.stack(v) for n, v in outs.items()}
    return (x[:SEQ][None], x[SEQ:][:, None], st['kp'], st['vp'], st['ks'], st['vs'], st['mk'], st['mv'],
            st['gla_p'], st['gla_s'], st['ssm_p'], st['ssm_s'], st['conv_p'], st['conv_s'],
            st['rwkv_p'], st['rwkv_s'], st['shift_p'], st['shift_s'])
```

```python
import functools
import math

import numpy as np
import jax
import jax.numpy as jnp
from jax import lax
from jax.experimental import pallas as pl
from jax.experimental.pallas import tpu as pltpu

f32 = jnp.float32
bf16 = jnp.bfloat16
HI = lax.Precision.HIGHEST

D_MODEL = 2048
SEQ = 8192
DEPTH = 2
DEC_BATCH = 128
PAST_LEN = 2048
PAGE_SIZE = 128
T_ALL = SEQ + DEC_BATCH

GROUP_W = 512
H_A, DK_A, DV_A, GK_RANK, GLA_TAU = 4, 64, 128, 16, 16.0
H_B, P_B, N_B, G_B, CONV_W, XBC_W = 8, 64, 64, 2, 4, 768
H_C, DK_C, DV_C, ROPE_THETA = 4, 64, 128, 10000.0
H_D, N_D, LNX_EPS = 8, 64, 64e-5
N_MEM, H_M, D_MEM, DH_M = 256, 4, 512, 128
N_KEYS, H_P, TOPK_P, D_PK = 128, 8, 16, 256
N_EXPERTS = N_KEYS * N_KEYS

C_D = 0
C_BC = 1792
C_QKA = 2048
C_VA = 2560
C_GA = 3072
C_Z = 3584
C_XS = 4096
C_QC = 4608
C_KC = 5120
C_VC = 5632
C_TAIL = 6144
IN_PAD = 6272

LANES = 128
VMEM_LIMIT = 56 * 1024 * 1024

NN = ((1,), (0,))
NT = ((1,), (1,))
TN = ((0,), (0,))


def _dot(a, b, dims=NN, hi=False):
    if hi:
        return lax.dot_general(a, b, (dims, ((), ())), precision=HI, preferred_element_type=f32)
    return lax.dot_general(a.astype(bf16), b.astype(bf16), (dims, ((), ())), preferred_element_type=f32)


def _softplus(x):
    return jnp.maximum(x, 0.0) + jnp.log(1.0 + jnp.exp(-jnp.abs(x)))


def _sigmoid(x):
    return 1.0 / (1.0 + jnp.exp(-x))


def _silu(x):
    return x * _sigmoid(x)


def _iota(shape, axis):
    return lax.broadcasted_iota(jnp.int32, shape, axis)


def _params(*sem):
    return pltpu.CompilerParams(dimension_semantics=sem, vmem_limit_bytes=VMEM_LIMIT)


def _mm_kernel(*refs, norm, residual, emit_xn):
    x_ref, g_ref, w_ref = refs[:3]
    res_ref = refs[3] if residual else None
    xn_ref = refs[-1]
    o_ref = refs[-3] if emit_xn else refs[-2]

    @pl.when(pl.program_id(1) == 0)
    def _():
        x = x_ref[...]
        if norm:
            x = x * lax.rsqrt(jnp.mean(x * x, axis=-1, keepdims=True) + 1e-6) * g_ref[...]
        xn_ref[...] = x.astype(bf16)
        if emit_xn:
            refs[-2][...] = xn_ref[...]

    acc = jnp.dot(xn_ref[...], w_ref[...], preferred_element_type=f32)
    if residual:
        acc = acc + res_ref[...]
    o_ref[...] = acc


def _matmul(x, w, gain=None, res=None, emit_xn=False, tm=None, tn=None):
    m, k = x.shape
    n = w.shape[1]
    tm = tm or _pick(m, (1040, 1024, 512, 256, 128))
    tn = tn or _pick(n, (1024, 896, 512, 256, 128))
    norm = gain is not None
    g = (gain if norm else jnp.ones((k,), f32)).reshape(1, k)
    args = [x, g, w]
    in_specs = [pl.BlockSpec((tm, k), lambda i, j: (i, 0)),
                pl.BlockSpec((1, k), lambda i, j: (0, 0)),
                pl.BlockSpec((k, tn), lambda i, j: (0, j))]
    if res is not None:
        args.append(res)
        in_specs.append(pl.BlockSpec((tm, tn), lambda i, j: (i, j)))
    out_specs = [pl.BlockSpec((tm, tn), lambda i, j: (i, j))]
    out_shape = [jax.ShapeDtypeStruct((m, n), f32)]
    if emit_xn:
        out_specs.append(pl.BlockSpec((tm, k), lambda i, j: (i, 0)))
        out_shape.append(jax.ShapeDtypeStruct((m, k), bf16))
    out = pl.pallas_call(
        functools.partial(_mm_kernel, norm=norm, residual=res is not None, emit_xn=emit_xn),
        grid=(m // tm, n // tn),
        in_specs=in_specs,
        out_specs=out_specs,
        out_shape=out_shape,
        scratch_shapes=[pltpu.VMEM((tm, k), bf16)],
        compiler_params=_params("parallel", "arbitrary"),
        name="mm",
    )(*args)
    return out if emit_xn else out[0]


def _pick(n, cands):
    for c in cands:
        if n % c == 0:
            return c
    return n


def _const(shape):
    nd = len(shape)
    return pl.BlockSpec(shape, lambda i, _n=nd: (0,) * _n)


def _cols(width, start, tb):
    assert start % width == 0
    return pl.BlockSpec((tb, width), lambda i, _c=start // width: (i, _c))


def _blockdiag_tri(tb, c):
    r = _iota((tb, tb), 0)
    s = _iota((tb, tb), 1)
    return jnp.where((r // c == s // c) & (s <= r), 1.0, 0.0).astype(f32)


def _segment_ones(n, seg):
    r = _iota((n, n), 0)
    s = _iota((n, n), 1)
    return jnp.where(r // seg == s // seg, 1.0, 0.0).astype(f32)


GLA_TB = 128
GLA_C = 16


def _gla_gate_log(tail, wg2p, bg):
    z = _dot(tail, wg2p, hi=True) + bg
    return -_softplus(-z) * (1.0 / GLA_TAU)


def _gla_out(o, g, gn):
    outs = []
    for h in range(H_A):
        oh = o[:, h * DV_A:(h + 1) * DV_A]
        oh = oh * lax.rsqrt(jnp.mean(oh * oh, axis=-1, keepdims=True) + 1e-6) * gn
        outs.append(oh * _silu(g[:, h * DV_A:(h + 1) * DV_A]))
    return jnp.concatenate(outs, axis=-1)


def _gla_kernel(qk_ref, v_ref, g_ref, tail_ref, wg2_ref, bg_ref, gn_ref, o_ref, sfin_ref, st_ref, w_ref):
    i = pl.program_id(0)

    @pl.when(i == 0)
    def _():
        st_ref[...] = jnp.zeros_like(st_ref)

    tb, c = GLA_TB, GLA_C
    qk = qk_ref[...]
    q = qk[:, :256] * (DK_A ** -0.5)
    k = qk[:, 256:]
    v = v_ref[...]
    la = _gla_gate_log(tail_ref[...], wg2_ref[...], bg_ref[...])
    b = _dot(_blockdiag_tri(tb, c), la, hi=True)
    e_r = _iota((256, 512), 0) // DK_A
    e_c = _iota((256, 512), 1) // DV_A
    expand = jnp.where(e_r == e_c, 1.0, 0.0).astype(bf16)
    s_idx = _iota((c, 256), 0)
    for j in range(tb // c):
        r0 = j * c
        qj, kj, bj, vj = q[r0:r0 + c], k[r0:r0 + c], b[r0:r0 + c], v[r0:r0 + c]
        for t in range(c):
            wt = qj[t:t + 1] * kj * jnp.exp(bj[t:t + 1] - bj)
            w_ref[t * c:(t + 1) * c, :] = jnp.where(s_idx <= t, wt, 0.0)
        att = _dot(w_ref[...], expand)
        o = jnp.sum(att.reshape(c, c, 512) * vj[None], axis=1)
        qe = qj * jnp.exp(bj)
        bl = bj[c - 1:c]
        ke = kj * jnp.exp(bl - bj)
        dl = jnp.exp(bl)
        inter = []
        for h in range(H_A):
            ks = slice(h * DK_A, (h + 1) * DK_A)
            st = st_ref[h]
            inter.append(_dot(qe[:, ks], st, NT))
            st_ref[h] = st * dl[:, ks] + _dot(vj[:, h * DV_A:(h + 1) * DV_A], ke[:, ks], TN)
        o = o + jnp.concatenate(inter, axis=-1)
        o_ref[r0:r0 + c, :] = _gla_out(o, g_ref[r0:r0 + c, :], gn_ref[...])

    @pl.when(i == pl.num_programs(0) - 1)
    def _():
        for h in range(H_A):
            sfin_ref[h] = st_ref[h].T


def _gla_prompt(cols, wg2p, bg, gn, seq=SEQ):
    tb = GLA_TB
    return pl.pallas_call(
        _gla_kernel,
        grid=(seq // tb,),
        in_specs=[_cols(512, C_QKA, tb), _cols(512, C_VA, tb), _cols(512, C_GA, tb), _cols(128, C_TAIL, tb),
                  _const((128, 256)), _const((1, 256)), _const((1, DV_A))],
        out_specs=[pl.BlockSpec((tb, 512), lambda i: (i, 0)), _const((H_A, DK_A, DV_A))],
        out_shape=[jax.ShapeDtypeStruct((seq, 512), f32), jax.ShapeDtypeStruct((H_A, DK_A, DV_A), f32)],
        scratch_shapes=[pltpu.VMEM((H_A, DV_A, DK_A), f32), pltpu.VMEM((GLA_C * GLA_C, 256), f32)],
        compiler_params=_params("arbitrary"),
        name="gla_prompt",
    )(cols, cols, cols, cols, wg2p, bg, gn)


SSD_TB = 128
SSD_C = 64


def _ssd_conv(ext, conv_w, conv_b, rows):
    out = conv_b
    for j in range(CONV_W):
        shifted = pltpu.roll(ext, j, 0) if j else ext
        out = out + shifted[8:8 + rows] * conv_w[CONV_W - 1 - j:CONV_W - j]
    return out


def _ssd_dt(tail, ex, dtb_x):
    return _softplus(_dot(tail, ex, hi=True) + dtb_x)


def _ssd_out(y, z, gn):
    y = y * _silu(z)
    w = GROUP_W // G_B
    outs = []
    for g in range(G_B):
        yg = y[:, g * w:(g + 1) * w]
        outs.append(yg * lax.rsqrt(jnp.mean(yg * yg, axis=-1, keepdims=True) + 1e-6) * gn[:, g * w:(g + 1) * w])
    return jnp.concatenate(outs, axis=-1)


def _ssd_kernel(z_ref, xs_ref, bc_ref, tail_ref, cw_ref, cb_ref, ex_ref, dtb_ref, alog_ref, dskip_ref, gn_ref,
                sel_ref, o_ref, hfin_ref, carry_ref, h_ref, y_ref):
    i = pl.program_id(0)

    @pl.when(i == 0)
    def _():
        carry_ref[...] = jnp.zeros_like(carry_ref)
        h_ref[...] = jnp.zeros_like(h_ref)

    tb, c = SSD_TB, SSD_C
    u = jnp.concatenate([xs_ref[...], bc_ref[...]], axis=-1)
    ext = jnp.concatenate([carry_ref[...], u], axis=0)
    carry_ref[...] = u[tb - 8:tb]
    act = _silu(_ssd_conv(ext, cw_ref[...], cb_ref[...], tb))
    xs = act[:, :GROUP_W]
    dtx = _ssd_dt(tail_ref[...], ex_ref[...], dtb_ref[...])
    la = dtx * (-jnp.exp(alog_ref[...]))
    b = _dot(_blockdiag_tri(tb, c), la, hi=True)
    brow = _dot(sel_ref[...], b, NT, hi=True)
    xd = xs * dtx
    tri = _iota((c, c), 1) <= _iota((c, c), 0)
    for ch in range(tb // c):
        r0 = ch * c
        rows = slice(r0, r0 + c)
        scores = []
        for g in range(G_B):
            bm = act[rows, GROUP_W + g * N_B:GROUP_W + (g + 1) * N_B]
            cm = act[rows, GROUP_W + G_B * N_B + g * N_B:GROUP_W + G_B * N_B + (g + 1) * N_B]
            scores.append((_dot(cm, bm, NT), bm, cm))
        for h in range(H_B):
            hs = slice(h * P_B, (h + 1) * P_B)
            sc, bm, cm = scores[h // (H_B // G_B)]
            bh = b[rows, hs]
            dec = jnp.where(tri, jnp.exp(bh - brow[h:h + 1, r0:r0 + c]), 0.0)
            xdh = xd[rows, hs]
            hst = h_ref[h]
            y = _dot(sc * dec, xdh) + _dot(cm, hst, NT) * jnp.exp(bh)
            bl = bh[c - 1:c]
            h_ref[h] = hst * jnp.exp(bl) + _dot(xdh * jnp.exp(bl - bh), bm, TN)
            y_ref[rows, hs] = y + xs[rows, hs] * dskip_ref[:, hs]
    o_ref[...] = _ssd_out(y_ref[...], z_ref[...], gn_ref[...])

    @pl.when(i == pl.num_programs(0) - 1)
    def _():
        hfin_ref[...] = h_ref[...]


def _ssd_prompt(cols, cw, cb, ex, dtb_x, alog_x, dskip_x, gn, sel, seq=SEQ):
    tb = SSD_TB
    return pl.pallas_call(
        _ssd_kernel,
        grid=(seq // tb,),
        in_specs=[_cols(512, C_Z, tb), _cols(512, C_XS, tb), _cols(256, C_BC, tb), _cols(128, C_TAIL, tb),
                  _const((CONV_W, XBC_W)), _const((1, XBC_W)), _const((128, 512)), _const((1, 512)), _const((1, 512)),
                  _const((1, 512)), _const((1, 512)), _const((8, 512))],
        out_specs=[pl.BlockSpec((tb, 512), lambda i: (i, 0)), _const((H_B, P_B, N_B))],
        out_shape=[jax.ShapeDtypeStruct((seq, 512), f32), jax.ShapeDtypeStruct((H_B, P_B, N_B), f32)],
        scratch_shapes=[pltpu.VMEM((8, XBC_W), f32), pltpu.VMEM((H_B, P_B, N_B), f32), pltpu.VMEM((tb, 512), f32)],
        compiler_params=_params("arbitrary"),
        name="ssd_prompt",
    )(cols, cols, cols, cols, cw, cb, ex, dtb_x, alog_x, dskip_x, gn, sel)


RWKV_TB = 128
RWKV_C = 64


def _rwkv_pre(x, xprev, mu, w0, a0, k_k, k_a, w2a2, g2):
    mixed = x + (xprev - x) * mu
    r = mixed[:, :512]
    kd = mixed[:, 512:1024]
    v = mixed[:, 1024:1536]
    lw = mixed[:, 1536:1664]
    lin = jnp.where(_iota(lw.shape, 1) < 64, jnp.tanh(lw), lw)
    wa = _dot(lin, w2a2, hi=True)
    w = -_softplus(-(w0 + wa[:, :512])) - 0.5
    a = _sigmoid(a0 + wa[:, 512:])
    g = _dot(_sigmoid(mixed[:, 1664:1792]), g2)
    kk = kd * k_k
    ss = _dot(kk * kk, _segment_ones(512, N_D), hi=True)
    kk = kk * lax.rsqrt(jnp.maximum(ss, 1e-24))
    kd = kd * (1.0 + (a - 1.0) * k_a)
    return r, w, kd, v, -kk, kk * a, g


def _rwkv_post(o, r, kd, v, g, r_k, lnx_g, lnx_b):
    seg = _segment_ones(512, N_D)
    mu = _dot(o, seg, hi=True) * (1.0 / N_D)
    d = o - mu
    var = _dot(d * d, seg, hi=True) * (1.0 / N_D)
    o = d * lax.rsqrt(var + LNX_EPS) * lnx_g + lnx_b
    o = o + _dot(r * kd * r_k, seg, hi=True) * v
    return o * g


def _rwkv_kernel(d_ref, mu_ref, w0_ref, a0_ref, kk_ref, ka_ref, rk_ref, lg_ref, lb_ref, w2a2_ref, g2_ref,
                 o_ref, sfin_ref, prev_ref, s_ref, oacc_ref):
    i = pl.program_id(0)

    @pl.when(i == 0)
    def _():
        prev_ref[...] = jnp.zeros_like(prev_ref)
        s_ref[...] = jnp.zeros_like(s_ref)

    tb, c = RWKV_TB, RWKV_C
    x = d_ref[...]
    xprev = jnp.where(_iota(x.shape, 0) == 0, prev_ref[0:1, :], pltpu.roll(x, 1, 0))
    prev_ref[0:1, :] = x[tb - 1:tb]
    r, w, kd, v, alpha, beta, g = _rwkv_pre(x, xprev, mu_ref[...], w0_ref[...], a0_ref[...], kk_ref[...], ka_ref[...],
                                            w2a2_ref[...], g2_ref[...])
    ld = -jnp.exp(w)
    cum = _dot(_blockdiag_tri(tb, c), ld, hi=True)
    at = alpha * jnp.exp(cum - ld)
    rt = r * jnp.exp(cum)
    einv = jnp.exp(-cum)
    kt = kd * einv
    bt = beta * einv
    ri = _iota((c, c), 0)
    ci = _iota((c, c), 1)
    strict, incl = ci < ri, ci <= ri
    eye = jnp.where(ri == ci, 1.0, 0.0).astype(f32)
    for ch in range(tb // c):
        rows = slice(ch * c, (ch + 1) * c)
        cl = cum[ch * c + c - 1:ch * c + c]
        efin = jnp.exp(cl - cum[rows])
        kfin = kd[rows] * efin
        bfin = beta[rows] * efin
        dfin = jnp.exp(cl)
        for h in range(H_D):
            hs = slice(h * N_D, (h + 1) * N_D)
            v_ = v[rows, hs]
            ar = jnp.concatenate([at[rows, hs], rt[rows, hs]], axis=0)
            kb = jnp.concatenate([kt[rows, hs], bt[rows, hs]], axis=0)
            gram = _dot(ar, kb, NT)
            lk = jnp.where(strict, gram[:c, :c], 0.0)
            lb = jnp.where(strict, gram[:c, c:], 0.0)
            mkb = jnp.concatenate([jnp.where(incl, gram[c:, :c], 0.0), jnp.where(incl, gram[c:, c:], 0.0)], axis=1)
            tinv = eye + lb
            p = lb
            for _ in range(5):
                p = _dot(p, p)
                tinv = tinv + _dot(tinv, p)
            s0 = s_ref[h]
            ars = _dot(ar, s0, NT)
            u = _dot(tinv, ars[:c] + _dot(lk, v_))
            vu = jnp.concatenate([v_, u], axis=0)
            oacc_ref[rows, hs] = ars[c:] + _dot(mkb, vu)
            kbfin = jnp.concatenate([kfin[:, hs], bfin[:, hs]], axis=0)
            s_ref[h] = s0 * dfin[:, hs] + _dot(vu, kbfin, TN)
    o_ref[...] = _rwkv_post(oacc_ref[...], r, kd, v, g, rk_ref[...], lg_ref[...], lb_ref[...])

    @pl.when(i == pl.num_programs(0) - 1)
    def _():
        sfin_ref[...] = s_ref[...]


def _rwkv_prompt(cols, mu, w0, a0, k_k, k_a, r_k, lnx_g, lnx_b, w2a2, g2, seq=SEQ):
    tb = RWKV_TB
    vec = _const((1, 512))
    return pl.pallas_call(
        _rwkv_kernel,
        grid=(seq // tb,),
        in_specs=[_cols(1792, C_D, tb), _const((1, 1792)), vec, vec, vec, vec, vec, vec, vec,
                  _const((128, 1024)), _const((128, 512))],
        out_specs=[pl.BlockSpec((tb, 512), lambda i: (i, 0)), _const((H_D, N_D, N_D))],
        out_shape=[jax.ShapeDtypeStruct((seq, 512), f32), jax.ShapeDtypeStruct((H_D, N_D, N_D), f32)],
        scratch_shapes=[pltpu.VMEM((8, 1792), f32), pltpu.VMEM((H_D, N_D, N_D), f32), pltpu.VMEM((tb, 512), f32)],
        compiler_params=_params("arbitrary"),
        name="rwkv_prompt",
    )(cols, mu, w0, a0, k_k, k_a, r_k, lnx_g, lnx_b, w2a2, g2)


def _rope_tables(pos):
    half = DK_C // 2
    inv = ROPE_THETA ** (-jnp.arange(half, dtype=f32) / half)
    ang = pos.astype(f32)[:, None] * inv[None, :]
    cos, sin = jnp.cos(ang), jnp.sin(ang)
    return jnp.tile(jnp.concatenate([cos, cos], axis=-1), (1, 2)), jnp.tile(jnp.concatenate([-sin, sin], axis=-1), (1, 2))


def _qk_norm_rope(x, gain, cos, sin):
    ms = _dot(x * x, _segment_ones(512, DK_C), hi=True) * (1.0 / DK_C)
    x = x * lax.rsqrt(ms + 1e-6) * gain
    first = (_iota(x.shape, 1) % DK_C) < (DK_C // 2)
    partner = jnp.where(first, pltpu.roll(x, 512 - DK_C // 2, 1), pltpu.roll(x, DK_C // 2, 1))
    cos = jnp.concatenate([cos] * 4, axis=-1)
    sin = jnp.concatenate([sin] * 4, axis=-1)
    return x * cos + partner * sin


def _diff_prep_kernel(q_ref, k_ref, cos_ref, sin_ref, gq_ref, gk_ref, qb_ref, kf_ref, kb_ref):
    cos, sin = cos_ref[...], sin_ref[...]
    q = _qk_norm_rope(q_ref[...], gq_ref[...], cos, sin)
    k = _qk_norm_rope(k_ref[...], gk_ref[...], cos, sin)
    qb_ref[...] = (q * (DK_C ** -0.5)).astype(bf16)
    kf_ref[...] = k
    kb_ref[...] = k.astype(bf16)


def _diff_prep(cols, cos, sin, gq, gk, row0, rows, tb):
    assert row0 % tb == 0
    r0 = row0 // tb
    colspec = lambda start: pl.BlockSpec((tb, 512), lambda i, _c=start // 512: (i + r0, _c))
    out = pl.BlockSpec((tb, 512), lambda i: (i, 0))
    tab = pl.BlockSpec((tb, 128), lambda i: (i, 0))
    return pl.pallas_call(
        _diff_prep_kernel,
        grid=(rows // tb,),
        in_specs=[colspec(C_QC), colspec(C_KC), tab, tab, _const((1, 512)), _const((1, 512))],
        out_specs=[out, out, out],
        out_shape=[jax.ShapeDtypeStruct((rows, 512), bf16), jax.ShapeDtypeStruct((rows, 512), f32),
                   jax.ShapeDtypeStruct((rows, 512), bf16)],
        compiler_params=_params("parallel"),
        name="diff_prep",
    )(cols, cols, cos, sin, gq, gk)


def _diff_finish(o1, o2, lam, gn, lam_init):
    o = o1 - lam * o2
    return o * lax.rsqrt(jnp.mean(o * o, axis=-1, keepdims=True) + 1e-6) * gn * (1.0 - lam_init)


FLASH_SUB = 64


def _flash_kernel(q_ref, k_ref, v_ref, lam_ref, gn_ref, o_ref, m_ref, l_ref, acc_ref, *, tq, tk, lam_init):
    qi, kj = pl.program_id(1), pl.program_id(2)

    @pl.when(kj == 0)
    def _():
        m_ref[...] = jnp.full_like(m_ref, -jnp.inf)
        l_ref[...] = jnp.zeros_like(l_ref)
        acc_ref[...] = jnp.zeros_like(acc_ref)

    def step(masked):
        k = k_ref[...]
        v = v_ref[...].astype(bf16)
        sub = FLASH_SUB

        def body(r, carry):
            rows = pl.ds(pl.multiple_of(r * sub, sub), sub)
            q = q_ref[rows, :]
            lane = _iota(q.shape, 1)
            if masked:
                keep = (kj * tk + _iota((sub, tk), 1)) <= (qi * tq + r * sub + _iota((sub, tk), 0))
            for m in range(2):
                qm = jnp.where((lane < DK_C) if m == 0 else (lane >= DK_C), q, jnp.zeros_like(q))
                s = _dot(qm, k, NT)
                if masked:
                    s = jnp.where(keep, s, -jnp.inf)
                m_old = m_ref[m, rows, :]
                m_new = jnp.maximum(m_old, jnp.max(s, axis=-1, keepdims=True))
                p = jnp.exp(s - m_new)
                corr = jnp.exp(m_old - m_new)
                l_ref[m, rows, :] = corr * l_ref[m, rows, :] + jnp.sum(p, axis=-1, keepdims=True)
                acc_ref[m, rows, :] = corr * acc_ref[m, rows, :] + _dot(p, v)
                m_ref[m, rows, :] = m_new
            return carry

        lax.fori_loop(0, tq // sub, body, 0, unroll=2)

    first, last = kj * tk, kj * tk + tk - 1

    @pl.when(last <= qi * tq)
    def _():
        step(False)

    @pl.when((first <= qi * tq + tq - 1) & (last > qi * tq))
    def _():
        step(True)

    @pl.when(kj == pl.num_programs(2) - 1)
    def _():
        o_ref[...] = _diff_finish(acc_ref[0] / l_ref[0], acc_ref[1] / l_ref[1], lam_ref[...], gn_ref[...], lam_init)


def _diff_attn_prompt(qb, kb, cols, lam, gn, lam_init, seq=SEQ, tq=512, tk=512):
    nq, nk = seq // tq, seq // tk
    last = lambda i: (i * tq + tq - 1) // tk
    kmap = lambda h, i, j: (jnp.minimum(j, last(i)), h)
    vmap_ = lambda h, i, j: (jnp.minimum(j, last(i)), C_VC // DV_C + h)
    return pl.pallas_call(
        functools.partial(_flash_kernel, tq=tq, tk=tk, lam_init=lam_init),
        grid=(H_C, nq, nk),
        in_specs=[pl.BlockSpec((tq, 128), lambda h, i, j: (i, h)), pl.BlockSpec((tk, 128), kmap),
                  pl.BlockSpec((tk, DV_C), vmap_), pl.BlockSpec((1, 128), lambda h, i, j: (0, 0)),
                  pl.BlockSpec((1, DV_C), lambda h, i, j: (0, 0))],
        out_specs=pl.BlockSpec((tq, DV_C), lambda h, i, j: (i, h)),
        out_shape=jax.ShapeDtypeStruct((seq, 512), f32),
        scratch_shapes=[pltpu.VMEM((2, tq, 1), f32), pltpu.VMEM((2, tq, 1), f32), pltpu.VMEM((2, tq, DV_C), f32)],
        compiler_params=_params("parallel", "parallel", "arbitrary"),
        name="diff_attn_prompt",
    )(qb, kb, cols, lam, gn)


PEER_SEL_TB = 128
PEER_TB = 640
PEER_EB = 512
_PAIRS = [(a, b) for a in range(TOPK_P) for b in range(TOPK_P) if (a + 1) * (b + 1) <= TOPK_P]
_NPAIR = -(-len(_PAIRS) // 8) * 8


def _top16_ranks(s):
    n_idx = _iota(s.shape, 0)
    rank = jnp.full(s.shape, float(TOPK_P), f32)
    tops = []
    work = s
    for k in range(TOPK_P):
        m = jnp.max(work, axis=0, keepdims=True)
        idx = jnp.min(jnp.where(work == m, n_idx, N_KEYS), axis=0, keepdims=True)
        hit = n_idx == idx
        rank = jnp.where(hit, float(k), rank)
        work = jnp.where(hit, -jnp.inf, work)
        tops.append(m)
    return rank, jnp.concatenate(tops, axis=0)


def _peer_select_kernel(q_ref, keys_ref, pk1_ref, flat_ref, ci_ref, cnt_ref, e2_ref, r2_ref):
    tb = q_ref.shape[0]
    flat = flat_ref[...]
    for h in range(H_P):
        ranks, tops, scores = [], [], []
        for x in range(2):
            hx = 2 * h + x
            s = _dot(keys_ref[hx], q_ref[:, hx * 128:(hx + 1) * 128], NT, hi=True)
            rk, tp = _top16_ranks(s)
            ranks.append(rk), tops.append(tp), scores.append(s)
        t1, t2 = tops
        cand = jnp.concatenate([t1[a:a + 1] + t2[b:b + 1] for a, b in _PAIRS]
                               + [jnp.full((_NPAIR - len(_PAIRS), tb), -jnp.inf, f32)], axis=0)
        work = cand
        sel = jnp.zeros(cand.shape, f32)
        for _ in range(TOPK_P):
            m = jnp.max(work, axis=0, keepdims=True)
            idx = jnp.min(jnp.where(work == m, flat, 4096.0), axis=0, keepdims=True)
            hit = flat == idx
            sel = jnp.where(hit, 1.0, sel)
            work = jnp.where(hit, -jnp.inf, work)
        top = t1[0:1] + t2[0:1]
        z = jnp.sum(sel * jnp.exp(jnp.where(sel > 0, cand - top, 0.0)), axis=0, keepdims=True)
        cnt = _dot(pk1_ref[...], sel)
        cnt_i = jnp.zeros((N_KEYS, tb), f32)
        for k1 in range(TOPK_P):
            cnt_i = cnt_i + jnp.where(ranks[0] == float(k1), cnt[k1:k1 + 1], 0.0)
        ci_ref[0, h] = jnp.exp(scores[0] - t1[0:1]) / z
        cnt_ref[0, h] = cnt_i
        e2_ref[0, h] = jnp.exp(scores[1] - t2[0:1])
        r2_ref[0, h] = ranks[1]


def _peer_select(q, keys):
    t = q.shape[0]
    tb, per = PEER_SEL_TB, PEER_TB // PEER_SEL_TB
    pk1 = np.zeros((TOPK_P, _NPAIR), np.float32)
    flat = np.full((_NPAIR, 1), 8192.0, np.float32)
    for r, (a, b) in enumerate(_PAIRS):
        pk1[a, r] = 1.0
        flat[r, 0] = a * TOPK_P + b
    out = pl.BlockSpec((1, H_P, N_KEYS, tb), lambda i: (i // per, 0, 0, i % per))
    shp = jax.ShapeDtypeStruct((t // PEER_TB, H_P, N_KEYS, PEER_TB), f32)
    return pl.pallas_call(
        _peer_select_kernel,
        grid=(t // tb,),
        in_specs=[pl.BlockSpec((tb, 2048), lambda i: (i, 0)), _const((2 * H_P, N_KEYS, 128)),
                  _const((TOPK_P, _NPAIR)), _const((_NPAIR, 1))],
        out_specs=[out, out, out, out],
        out_shape=[shp, shp, shp, shp],
        compiler_params=_params("parallel"),
        name="peer_select",
    )(q, keys, jnp.asarray(pk1), jnp.asarray(flat))


def _sample_pre_kernel(qk_ref, tail_ref, xs_ref, bc_ref, d_ref, conv_ref, shift_ref,
                       wg2_ref, bg_ref, cw_ref, cb_ref, ex_ref, dtb_ref, alog_ref,
                       mu_ref, w0_ref, a0_ref, kk_ref, ka_ref, w2a2_ref, g2_ref,
                       gq_ref, gk_ref, ga_ref, act_ref, sdec_ref, sxd_ref, rw_ref):
    qk = qk_ref[...]
    la = _gla_gate_log(tail_ref[...], wg2_ref[...], bg_ref[...])
    gq_ref[...] = qk[:, :256] * (DK_A ** -0.5)
    gk_ref[...] = qk[:, 256:]
    ga_ref[...] = jnp.exp(la)
    u = jnp.concatenate([xs_ref[...], bc_ref[...]], axis=-1)
    cw = cw_ref[...]
    conv = cb_ref[...] + u * cw[CONV_W - 1:CONV_W]
    for j in range(CONV_W - 1):
        conv = conv + conv_ref[j] * cw[j:j + 1]
    act = _silu(conv)
    act_ref[...] = act
    dtx = _ssd_dt(tail_ref[...], ex_ref[...], dtb_ref[...])
    sdec_ref[...] = jnp.exp(dtx * (-jnp.exp(alog_ref[...])))
    sxd_ref[...] = act[:, :GROUP_W] * dtx
    r, w, kd, v, alpha, beta, g = _rwkv_pre(d_ref[...], shift_ref[...], mu_ref[...], w0_ref[...], a0_ref[...],
                                            kk_ref[...], ka_ref[...], w2a2_ref[...], g2_ref[...])
    for n, t in enumerate((r, jnp.exp(-jnp.exp(w)), kd, v, alpha, beta, g)):
        rw_ref[n] = t


def _sample_pre(cols, conv_st, shift_st, wts, row0, b):
    tb = b
    assert row0 % tb == 0
    r0 = row0 // tb
    cs = lambda w, start: pl.BlockSpec((tb, w), lambda i, _c=start // w: (r0, _c))
    full = lambda *s: jax.ShapeDtypeStruct(s, f32)
    return pl.pallas_call(
        _sample_pre_kernel,
        grid=(1,),
        in_specs=[cs(512, C_QKA), cs(128, C_TAIL), cs(512, C_XS), cs(256, C_BC), cs(1792, C_D),
                  _const((CONV_W - 1, b, XBC_W)), _const((b, 1792))] + [_const(w.shape) for w in wts],
        out_specs=[_const((b, 256))] * 3 + [_const((b, XBC_W)), _const((b, 512)), _const((b, 512)), _const((7, b, 512))],
        out_shape=[full(b, 256)] * 3 + [full(b, XBC_W), full(b, 512), full(b, 512), full(7, b, 512)],
        compiler_params=_params("arbitrary"),
        name="sample_pre",
    )(cols, cols, cols, cols, cols, conv_st, shift_st, *wts)


def _rows_to_tile(row, heads, width, reps):
    return jnp.concatenate([jnp.broadcast_to(row[:, h * width:(h + 1) * width], (reps, width)) for h in range(heads)], axis=0)


def _gla_step_kernel(s_ref, a_ref, k_ref, q_ref, v_ref, sn_ref, o_ref):
    s = a_ref[0] * s_ref[0] + k_ref[0] * _rows_to_tile(v_ref[0], H_A, DV_A, DK_A)
    sn_ref[0] = s
    qs = q_ref[0] * s
    o_ref[0] = jnp.concatenate([jnp.sum(qs[h * DK_A:(h + 1) * DK_A], axis=0, keepdims=True) for h in range(H_A)], axis=-1)


def _ssd_step_kernel(h_ref, dec_ref, xd_ref, bc_ref, hn_ref, y_ref):
    bc = bc_ref[0]
    reps = (H_B // G_B) * P_B
    hn = dec_ref[0] * h_ref[0] + xd_ref[0] * _rows_to_tile(bc[:, :G_B * N_B], G_B, N_B, reps)
    hn_ref[0] = hn
    y_ref[0] = jnp.sum(hn * _rows_to_tile(bc[:, G_B * N_B:], G_B, N_B, reps), axis=-1, keepdims=True)


def _rwkv_step_kernel(s_ref, rows_ref, v_ref, sn_ref, o_ref):
    tile = lambda n: _rows_to_tile(rows_ref[0, n:n + 1, :], H_D, N_D, N_D)
    s = s_ref[0]
    sa = jnp.sum(s * tile(4), axis=-1, keepdims=True)
    s = s * tile(1) + sa * tile(5) + v_ref[0] * tile(2)
    sn_ref[0] = s
    o_ref[0] = jnp.sum(s * tile(0), axis=-1, keepdims=True)


def _state_step(kernel, name, state, ins, outs):
    b = state.shape[0]
    spec = lambda shp: pl.BlockSpec((1,) + tuple(shp[1:]), lambda i: (i, 0, 0))
    return pl.pallas_call(
        kernel,
        grid=(b,),
        in_specs=[spec(state.shape)] + [spec(a.shape) for a in ins],
        out_specs=[spec(state.shape)] + [spec(s) for s in outs],
        out_shape=[jax.ShapeDtypeStruct(state.shape, f32)] + [jax.ShapeDtypeStruct(s, f32) for s in outs],
        compiler_params=_params("parallel"),
        name=name,
    )(state, *ins)


def _split_dot(x, w):
    hi = x.astype(bf16)
    lo = (x - hi.astype(f32)).astype(bf16)
    return _dot(hi, w) + _dot(lo, w)


def _rowhead_attend(kx, vx, q4, ind, n_maps, valid_rows=None):
    r = kx.shape[0]
    g = r // 8
    q8 = jnp.concatenate([q4, q4], axis=0)
    qt = jnp.broadcast_to(q8[None], (g, 8, 128)).reshape(r, 128)
    s = _split_dot(kx * qt, ind)
    if valid_rows is not None:
        s = jnp.where(_iota(s.shape, 0) < valid_rows, s, -jnp.inf)
    s3 = s.reshape(g, 8, 128)
    m8 = jnp.max(s3, axis=0)
    mh = jnp.maximum(m8, pltpu.roll(m8, 4, 0))
    p3 = jnp.exp(s3 - mh[None])
    l8 = jnp.sum(p3, axis=0)
    lh = l8 + pltpu.roll(l8, 4, 0)
    p = p3.reshape(r, 128)
    outs = []
    for m in range(n_maps):
        spread = jnp.where(_iota((128, 128), 0) == m, 1.0, 0.0).astype(f32)
        pv = jnp.sum((_dot(p, spread) * vx).reshape(g, 8, 128), axis=0)
        pv = pv + pltpu.roll(pv, 4, 0)
        outs.append(pv / _split_dot(lh, spread))
    return outs


def _diff_decode_kernel(pt_ref, q_ref, ks_ref, vs_ref, *rest, lam_init, n_pages):
    del pt_ref
    k_refs, v_refs = rest[:n_pages], rest[n_pages:2 * n_pages]
    ind_ref, lam_ref, gn_ref, o_ref = rest[2 * n_pages:]
    own = lambda ref: jnp.concatenate([ref[0], ref[0]], axis=0)
    kx = jnp.concatenate([r[...] for r in k_refs] + [own(ks_ref)], axis=0)
    vx = jnp.concatenate([r[...] for r in v_refs] + [own(vs_ref)], axis=0)
    o1, o2 = _rowhead_attend(kx, vx, q_ref[0].astype(f32), ind_ref[...], 2, valid_rows=kx.shape[0] - 4)
    o_ref[0] = _diff_finish(o1, o2, lam_ref[...], gn_ref[...], lam_init)


def _diff_decode(pt_flat, layer, qb, ks, vs, ck, cv, ind, lam, gn, lam_init, n_pages):
    b = qb.shape[0]
    rows = ck.shape[2]
    row = pl.BlockSpec((1, H_C, 128), lambda i, pt: (i, 0, 0))
    page = lambda j: pl.BlockSpec((None, None, rows, 128), lambda i, pt, _j=j: (layer, pt[i * n_pages + _j], 0, 0))
    cst = lambda shp: pl.BlockSpec(shp, lambda i, pt: (0, 0))
    pages = [page(j) for j in range(n_pages)]
    return pl.pallas_call(
        functools.partial(_diff_decode_kernel, lam_init=lam_init, n_pages=n_pages),
        grid_spec=pltpu.PrefetchScalarGridSpec(
            num_scalar_prefetch=1, grid=(b,),
            in_specs=[row, row, row] + pages + pages + [cst((128, 128)), cst((1, 128)), cst((1, DV_C))],
            out_specs=pl.BlockSpec((1, 8, 128), lambda i, pt: (i, 0, 0))),
        out_shape=jax.ShapeDtypeStruct((b, 8, 128), f32),
        compiler_params=_params("parallel"),
        name="diff_decode",
    )(pt_flat, qb, ks, vs, *([ck] * n_pages), *([cv] * n_pages), ind, lam, gn)


def _sample_post_kernel(oa_ref, ga_ref, gn_a_ref, y_ref, act_ref, z_ref, dskip_ref, gn_b_ref, oc_ref,
                        od_ref, rw_ref, rk_ref, lg_ref, lb_ref, o_ref):
    oa = _gla_out(oa_ref[...], ga_ref[...], gn_a_ref[...])
    ob = _ssd_out(y_ref[...] + act_ref[:, :GROUP_W] * dskip_ref[...], z_ref[...], gn_b_ref[...])
    od = _rwkv_post(od_ref[...], rw_ref[0], rw_ref[2], rw_ref[3], rw_ref[6], rk_ref[...], lg_ref[...], lb_ref[...])
    o_ref[...] = jnp.concatenate([oa, ob, oc_ref[...], od], axis=-1)


def _sample_post(cols, oa, y, act, oc, od, rw, gn_a, dskip_x, gn_b, r_k, lnx_g, lnx_b, row0, b):
    r0 = row0 // b
    cs = lambda w, start: pl.BlockSpec((b, w), lambda i, _c=start // w: (r0, _c))
    c512 = _const((b, 512))
    v512 = _const((1, 512))
    return pl.pallas_call(
        _sample_post_kernel,
        grid=(1,),
        in_specs=[c512, cs(512, C_GA), _const((1, DV_A)), c512, _const((b, XBC_W)), cs(512, C_Z), v512, v512, c512,
                  c512, _const((7, b, 512)), v512, v512, v512],
        out_specs=_const((b, D_MODEL)),
        out_shape=jax.ShapeDtypeStruct((b, D_MODEL), f32),
        compiler_params=_params("arbitrary"),
        name="sample_post",
    )(oa, cols, gn_a, y, act, cols, dskip_x, gn_b, oc, od, rw, r_k, lnx_g, lnx_b)


def _head_rms(x, gain, width):
    outs = []
    for h in range(x.shape[1] // width):
        xh = x[:, h * width:(h + 1) * width]
        outs.append(xh * lax.rsqrt(jnp.mean(xh * xh, axis=-1, keepdims=True) + 1e-6) * gain)
    return jnp.concatenate(outs, axis=-1)


def _mem_kv_kernel(m_ref, g_ref, w_ref, gk_ref, k_ref, v_ref):
    m = m_ref[...]
    m = m * lax.rsqrt(jnp.mean(m * m, axis=-1, keepdims=True) + 1e-6) * g_ref[...]
    kv = _dot(m, w_ref[...])
    k_ref[...] = _head_rms(kv[:, :D_MEM], gk_ref[...], DH_M)
    v_ref[...] = kv[:, D_MEM:]


def _mem_kv(mem, g_src, w_kv, g_k):
    shp = jax.ShapeDtypeStruct((N_MEM, D_MEM), f32)
    return pl.pallas_call(
        _mem_kv_kernel, out_shape=[shp, shp],
        compiler_params=pltpu.CompilerParams(vmem_limit_bytes=VMEM_LIMIT), name="mem_kv",
    )(mem, g_src, w_kv, g_k)


def _mem_attn_prompt_kernel(q_ref, gq_ref, k_ref, v_ref, o_ref):
    q = _head_rms(q_ref[...], gq_ref[...], DH_M) * (DH_M ** -0.5)
    k, v = k_ref[...], v_ref[...]
    outs = []
    for h in range(H_M):
        hs = slice(h * DH_M, (h + 1) * DH_M)
        s = _dot(q[:, hs], k[:, hs], NT)
        p = jnp.exp(s - jnp.max(s, axis=-1, keepdims=True))
        outs.append(_dot(p, v[:, hs]) / jnp.sum(p, axis=-1, keepdims=True))
    o_ref[...] = jnp.concatenate(outs, axis=-1)


def _mem_attn_prompt(q, gq, k, v, seq=SEQ, tb=512):
    return pl.pallas_call(
        _mem_attn_prompt_kernel,
        grid=(seq // tb,),
        in_specs=[pl.BlockSpec((tb, D_MEM), lambda i: (i, 0)), _const((1, DH_M)), _const((N_MEM, D_MEM)),
                  _const((N_MEM, D_MEM))],
        out_specs=pl.BlockSpec((tb, D_MEM), lambda i: (i, 0)),
        out_shape=jax.ShapeDtypeStruct((seq, D_MEM), f32),
        compiler_params=_params("parallel"),
        name="mem_attn_prompt",
    )(q, gq, k, v)


def _mem_attn_sample_kernel(q_ref, gq_ref, k_ref, v_ref, ind_ref, o_ref):
    q = q_ref[0]
    q = q * lax.rsqrt(jnp.mean(q * q, axis=-1, keepdims=True) + 1e-6) * gq_ref[...] * (DH_M ** -0.5)
    o_ref[0] = _rowhead_attend(k_ref[...], v_ref[...], q, ind_ref[...], 1)[0]


def _mem_attn_sample(q, gq, ck, cv, ind, layer):
    b = q.shape[0]
    kv = pl.BlockSpec((None, None, N_MEM * H_M, DH_M), lambda i: (layer, i, 0, 0))
    return pl.pallas_call(
        _mem_attn_sample_kernel,
        grid=(b,),
        in_specs=[pl.BlockSpec((1, H_M, DH_M), lambda i: (i, 0, 0)), _const((1, DH_M)), kv, kv, _const((128, 128))],
        out_specs=pl.BlockSpec((1, 8, DH_M), lambda i: (i, 0, 0)),
        out_shape=jax.ShapeDtypeStruct((b, 8, DH_M), f32),
        compiler_params=_params("parallel"),
        name="mem_attn_sample",
    )(q, gq, ck, cv, ind)


def _cast_kernel(x_ref, o_ref):
    o_ref[...] = x_ref[...].astype(bf16)


def _table_bf16(tab, layer, rows=1024):
    n, d = tab.shape[1:]
    return pl.pallas_call(
        _cast_kernel,
        grid=(n // rows,),
        in_specs=[pl.BlockSpec((None, rows, d), lambda i: (layer, i, 0))],
        out_specs=pl.BlockSpec((rows, d), lambda i: (i, 0)),
        out_shape=jax.ShapeDtypeStruct((n, d), bf16),
        compiler_params=_params("parallel"),
        name="table_bf16",
    )(tab)


def _gelu(x):
    return 0.5 * x * (1.0 + jnp.tanh(0.7978845608028654 * (x + 0.044715 * x * x * x)))


def _peer_dense_kernel(xn_ref, u_ref, v_ref, ci_ref, cnt_ref, e2_ref, r2_ref, res_ref, o_ref, w_ref):
    e = pl.program_id(1)
    tb = xn_ref.shape[0]
    n_i = PEER_EB // N_KEYS

    @pl.when(e == 0)
    def _():
        o_ref[...] = res_ref[...]

    for ii in range(n_i):
        w = jnp.zeros((N_KEYS, tb), f32)
        for h in range(H_P):
            row = pl.ds(e * n_i + ii, 1)
            ci = ci_ref[0, h, row, :]
            cnt = cnt_ref[0, h, row, :]
            w = w + jnp.where(r2_ref[0, h] < cnt, e2_ref[0, h] * ci, 0.0)
        w_ref[ii * N_KEYS:(ii + 1) * N_KEYS, :] = w
    hid = _gelu(_dot(u_ref[...], xn_ref[...], NT))
    o_ref[...] += _dot(hid * w_ref[...], v_ref[...], TN)


def _peer_dense(xn, u, v, sel, res):
    t = xn.shape[0]
    tb, eb = PEER_TB, PEER_EB
    once = pl.Buffered(1)
    selspec = pl.BlockSpec((1, H_P, N_KEYS, tb), lambda i, e: (i, 0, 0, 0), pipeline_mode=once)
    return pl.pallas_call(
        _peer_dense_kernel,
        grid=(t // tb, N_EXPERTS // eb),
        in_specs=[pl.BlockSpec((tb, D_MODEL), lambda i, e: (i, 0), pipeline_mode=once),
                  pl.BlockSpec((eb, D_MODEL), lambda i, e: (e, 0)),
                  pl.BlockSpec((eb, D_MODEL), lambda i, e: (e, 0)), selspec, selspec, selspec, selspec,
                  pl.BlockSpec((tb, D_MODEL), lambda i, e: (i, 0), pipeline_mode=once)],
        out_specs=pl.BlockSpec((tb, D_MODEL), lambda i, e: (i, 0)),
        out_shape=jax.ShapeDtypeStruct((t, D_MODEL), f32),
        scratch_shapes=[pltpu.VMEM((eb, tb), f32)],
        compiler_params=_params("parallel", "arbitrary"),
        name="peer_dense",
    )(xn, u, v, *sel, res)


def _pad_w_in(w):
    a0, b0, c0, d0 = 0, 1552, 2840, 4376
    seg = lambda s, n: w[:, s:s + n]
    parts = [seg(d0, 1792), seg(b0 + 1024, 256), seg(a0, 512), seg(a0 + 512, 512), seg(a0 + 1024, 512),
             seg(b0, 512), seg(b0 + 512, 512), seg(c0, 512), seg(c0 + 512, 512), seg(c0 + 1024, 512),
             seg(a0 + 1536, 16), seg(b0 + 1280, 8), jnp.zeros((w.shape[0], IN_PAD - 6168), w.dtype)]
    return jnp.concatenate(parts, axis=1).astype(bf16)


def _layer_consts():
    ex = np.zeros((128, 512), np.float32)
    sel = np.zeros((8, 512), np.float32)
    for h in range(8):
        ex[GK_RANK + h, h * 64:(h + 1) * 64] = 1.0
        sel[h, h * 64] = 1.0
    ind2 = np.zeros((128, 128), np.float32)
    ind2[:DK_C, 0] = 1.0
    ind2[DK_C:, 1] = 1.0
    ind1 = np.zeros((128, 128), np.float32)
    ind1[:, 0] = 1.0
    return jnp.asarray(ex), jnp.asarray(sel), jnp.asarray(ind2), jnp.asarray(ind1)


def kernel(x_prompt, x_sample, cache_diff_k, cache_diff_v, cache_mem_k, cache_mem_v, state_gla, state_ssm, state_conv, state_rwkv, state_shift, page_table, mem_prompt, norm_mix, w_in, w_out, gla_wg2, gla_bg, gla_gn, conv_w, conv_b, dt_bias, a_log, d_skip, ssm_gn, dq_norm, dk_norm, lam_q, lam_k, diff_gn, shift_mu, w0, w2, a0, a2, g2, k_k, k_a, r_k, lnx_g, lnx_b, norm_mem, norm_memsrc, w_mq, w_mk, w_mv, w_mo, mq_norm, mk_norm, norm_ffn, peer_wq, peer_keys, peer_u, peer_v):
    nb = DEC_BATCH
    n_pages = page_table.shape[1]
    n_pool = cache_diff_k.shape[1]
    x = jnp.concatenate([x_prompt[0], x_sample[:, 0]], axis=0)
    pt_flat = page_table.reshape(-1)
    cos_p, sin_p = _rope_tables(jnp.arange(SEQ, dtype=jnp.int32))
    cos_s, sin_s = _rope_tables(jnp.full((nb,), PAST_LEN, jnp.int32))
    ex, sel8, ind2, ind1 = _layer_consts()
    ck_rows = cache_diff_k.reshape(DEPTH, n_pool, PAGE_SIZE * H_C, 2 * DK_C)
    cv_rows = cache_diff_v.reshape(DEPTH, n_pool, PAGE_SIZE * H_C, DV_C)
    mk_rows = cache_mem_k.reshape(DEPTH, nb, N_MEM * H_M, DH_M)
    mv_rows = cache_mem_v.reshape(DEPTH, nb, N_MEM * H_M, DH_M)
    row = lambda a: a.reshape(1, -1)
    rep64 = lambda a: jnp.repeat(a, 64).reshape(1, 512)
    outs = {n: [] for n in ('kp', 'vp', 'ks', 'vs', 'mk', 'mv', 'gla_p', 'gla_s', 'ssm_p', 'ssm_s', 'conv_p',
                            'conv_s', 'rwkv_p', 'rwkv_s', 'shift_p', 'shift_s')}
    for l in range(DEPTH):
        lam_init = 0.8 - 0.6 * math.exp(-0.3 * l)
        lq, lk = lam_q[l], lam_k[l]
        lam = jnp.exp(jnp.sum(lq[0] * lk[0])) - jnp.exp(jnp.sum(lq[1] * lk[1])) + lam_init
        lam = jnp.full((1, 128), lam, f32)
        wg2p = jnp.zeros((128, 256), f32).at[:GK_RANK].set(gla_wg2[l])
        w2a2 = jnp.zeros((128, 1024), f32).at[:64, :512].set(w2[l]).at[64:, 512:].set(a2[l])
        gq = jnp.tile(dq_norm[l].reshape(128), 4).reshape(1, 512)
        gk = jnp.tile(dk_norm[l].reshape(128), 4).reshape(1, 512)
        dtb_x, alog_x, dskip_x = rep64(dt_bias[l]), rep64(a_log[l]), rep64(d_skip[l])

        cols = _matmul(x, _pad_w_in(w_in[l]), gain=norm_mix[l])

        oa, gla_p = _gla_prompt(cols, wg2p, row(gla_bg[l]), row(gla_gn[l]), seq=SEQ)
        ob, ssm_p = _ssd_prompt(cols, conv_w[l], row(conv_b[l]), ex, dtb_x, alog_x, dskip_x, row(ssm_gn[l]), sel8,
                                seq=SEQ)
        qb, kf, kb = _diff_prep(cols, cos_p, sin_p, gq, gk, 0, SEQ, 512)
        oc = _diff_attn_prompt(qb, kb, cols, lam, row(diff_gn[l]), lam_init, seq=SEQ)
        od, rwkv_p = _rwkv_prompt(cols, row(shift_mu[l]), row(w0[l]), row(a0[l]), row(k_k[l]), row(k_a[l]),
                                  row(r_k[l]), row(lnx_g[l]), row(lnx_b[l]), w2a2, g2[l], seq=SEQ)
        mix_p = jnp.concatenate([oa, ob, oc, od], axis=1)

        pre_w = [wg2p, row(gla_bg[l]), conv_w[l], row(conv_b[l]), ex, dtb_x, alog_x,
                 row(shift_mu[l]), row(w0[l]), row(a0[l]), row(k_k[l]), row(k_a[l]), w2a2, g2[l]]
        s_gq, s_gk, s_ga, s_act, s_dec, s_xd, s_rw = _sample_pre(cols, jnp.transpose(state_conv[l], (1, 0, 2)),
                                                                 state_shift[l][:, 0], pre_w, SEQ, nb)
        col = lambda a: a.reshape(nb, -1, 1)
        tail = cols[SEQ:]
        gla_s, oa_s = _state_step(_gla_step_kernel, "gla_step", state_gla[l].reshape(nb, H_A * DK_A, DV_A),
                                  [col(s_ga), col(s_gk), col(s_gq), tail[:, C_VA:C_VA + 512].reshape(nb, 1, 512)],
                                  [(nb, 1, 512)])
        ssm_s, y_s = _state_step(_ssd_step_kernel, "ssd_step", state_ssm[l].reshape(nb, H_B * P_B, N_B),
                                 [col(s_dec), col(s_xd), s_act[:, GROUP_W:].reshape(nb, 1, 256)], [(nb, H_B * P_B, 1)])
        rwkv_s, od_s = _state_step(_rwkv_step_kernel, "rwkv_step", state_rwkv[l].reshape(nb, H_D * N_D, N_D),
                                   [jnp.transpose(s_rw, (1, 0, 2)), col(s_rw[3])], [(nb, H_D * N_D, 1)])
        qb_s, kf_s, _ = _diff_prep(cols, cos_s, sin_s, gq, gk, SEQ, nb, nb)
        vc_s = tail[:, C_VC:C_VC + 512]
        oc_s = _diff_decode(pt_flat, l, qb_s.reshape(nb, H_C, 128), kf_s.reshape(nb, H_C, 128), vc_s.reshape(nb, H_C, 128),
                            ck_rows, cv_rows, ind2, lam, row(diff_gn[l]), lam_init, n_pages)
        mix_s = _sample_post(cols, oa_s.reshape(nb, 512), y_s.reshape(nb, 512), s_act, oc_s[:, :H_C].reshape(nb, 512),
                             od_s.reshape(nb, 512), s_rw, row(gla_gn[l]), dskip_x, row(ssm_gn[l]), row(r_k[l]),
                             row(lnx_g[l]), row(lnx_b[l]), SEQ, nb)

        x = _matmul(jnp.concatenate([mix_p, mix_s], axis=0), w_out[l].astype(bf16), res=x)

        mk_p, mv_p = _mem_kv(mem_prompt[0], row(norm_memsrc[l]),
                             jnp.concatenate([w_mk[l], w_mv[l]], axis=1).astype(bf16), row(mk_norm[l]))
        qm = _matmul(x, w_mq[l].astype(bf16), gain=norm_mem[l])
        om_p = _mem_attn_prompt(qm, row(mq_norm[l]), mk_p, mv_p, seq=SEQ)
        om_s = _mem_attn_sample(qm[SEQ:].reshape(nb, H_M, DH_M), row(mq_norm[l]), mk_rows, mv_rows, ind1, l)
        x = _matmul(jnp.concatenate([om_p, om_s[:, :H_M].reshape(nb, D_MEM)], axis=0), w_mo[l].astype(bf16), res=x)

        qp, xn = _matmul(x, peer_wq[l].astype(bf16), gain=norm_ffn[l], emit_xn=True, tn=512)
        picks = _peer_select(qp, peer_keys[l].reshape(2 * H_P, N_KEYS, D_PK // 2))
        x = _peer_dense(xn, _table_bf16(peer_u, l), _table_bf16(peer_v, l), picks, x)

        outs['kp'].append(kf.reshape(1, SEQ, H_C, 2 * DK_C))
        outs['vp'].append(cols[:SEQ, C_VC:C_VC + 512].reshape(1, SEQ, H_C, DV_C))
        outs['ks'].append(kf_s.reshape(nb, 1, H_C, 2 * DK_C))
        outs['vs'].append(vc_s.reshape(nb, 1, H_C, DV_C))
        outs['mk'].append(mk_p.reshape(1, N_MEM, H_M, DH_M))
        outs['mv'].append(mv_p.reshape(1, N_MEM, H_M, DH_M))
        outs['gla_p'].append(gla_p[None])
        outs['gla_s'].append(gla_s.reshape(nb, H_A, DK_A, DV_A))
        outs['ssm_p'].append(ssm_p[None])
        outs['ssm_s'].append(ssm_s.reshape(nb, H_B, P_B, N_B))
        u_p = jnp.concatenate([cols[SEQ - 3:SEQ, C_XS:C_XS + 512], cols[SEQ - 3:SEQ, C_BC:C_BC + 256]], axis=1)
        u_s = jnp.concatenate([tail[:, C_XS:C_XS + 512], tail[:, C_BC:C_BC + 256]], axis=1)
        outs['conv_p'].append(u_p[None])
        outs['conv_s'].append(jnp.concatenate([state_conv[l][:, 1:], u_s[:, None]], axis=1))
        outs['rwkv_p'].append(rwkv_p[None])
        outs['rwkv_s'].append(rwkv_s.reshape(nb, H_D, N_D, N_D))
        outs['shift_p'].append(cols[SEQ - 1:SEQ, C_D:C_D + 1792][None])
        outs['shift_s'].append(tail[:, C_D:C_D + 1792][:, None])
    st = {n: jnp.stack(v) for n, v in outs.items()}
    return (x[:SEQ][None], x[SEQ:][:, None], st['kp'], st['vp'], st['ks'], st['vs'], st['mk'], st['mv'],
            st['gla_p'], st['gla_s'], st['ssm_p'], st['ssm_s'], st['conv_p'], st['conv_s'],
            st['rwkv_p'], st['rwkv_s'], st['shift_p'], st['shift_s'])
```

```python
import functools
import math

import numpy as np
import jax
import jax.numpy as jnp
from jax import lax
from jax.experimental import pallas as pl
from jax.experimental.pallas import tpu as pltpu

f32 = jnp.float32
bf16 = jnp.bfloat16
HI = lax.Precision.HIGHEST

D_MODEL = 2048
SEQ = 8192
DEPTH = 2
DEC_BATCH = 128
PAST_LEN = 2048
PAGE_SIZE = 128
T_ALL = SEQ + DEC_BATCH

GROUP_W = 512
H_A, DK_A, DV_A, GK_RANK, GLA_TAU = 4, 64, 128, 16, 16.0
H_B, P_B, N_B, G_B, CONV_W, XBC_W = 8, 64, 64, 2, 4, 768
H_C, DK_C, DV_C, ROPE_THETA = 4, 64, 128, 10000.0
H_D, N_D, LNX_EPS = 8, 64, 64e-5
N_MEM, H_M, D_MEM, DH_M = 256, 4, 512, 128
N_KEYS, H_P, TOPK_P, D_PK = 128, 8, 16, 256
N_EXPERTS = N_KEYS * N_KEYS

C_D = 0
C_BC = 1792
C_QKA = 2048
C_VA = 2560
C_GA = 3072
C_Z = 3584
C_XS = 4096
C_QC = 4608
C_KC = 5120
C_VC = 5632
C_TAIL = 6144
IN_PAD = 6272

LANES = 128
VMEM_LIMIT = 56 * 1024 * 1024

NN = ((1,), (0,))
NT = ((1,), (1,))
TN = ((0,), (0,))


def _dot(a, b, dims=NN, hi=False):
    if hi:
        return lax.dot_general(a, b, (dims, ((), ())), precision=HI, preferred_element_type=f32)
    return lax.dot_general(a.astype(bf16), b.astype(bf16), (dims, ((), ())), preferred_element_type=f32)


def _softplus(x):
    return jnp.maximum(x, 0.0) + jnp.log(1.0 + jnp.exp(-jnp.abs(x)))


def _sigmoid(x):
    return 1.0 / (1.0 + jnp.exp(-x))


def _silu(x):
    return x * _sigmoid(x)


def _iota(shape, axis):
    return lax.broadcasted_iota(jnp.int32, shape, axis)


def _params(*sem):
    return pltpu.CompilerParams(dimension_semantics=sem, vmem_limit_bytes=VMEM_LIMIT)


def _mm_kernel(*refs, norm, residual, emit_xn):
    x_ref, g_ref, w_ref = refs[:3]
    res_ref = refs[3] if residual else None
    xn_ref = refs[-1]
    o_ref = refs[-3] if emit_xn else refs[-2]

    @pl.when(pl.program_id(1) == 0)
    def _():
        x = x_ref[...]
        if norm:
            x = x * lax.rsqrt(jnp.mean(x * x, axis=-1, keepdims=True) + 1e-6) * g_ref[...]
        xn_ref[...] = x.astype(bf16)
        if emit_xn:
            refs[-2][...] = xn_ref[...]

    acc = jnp.dot(xn_ref[...], w_ref[...], preferred_element_type=f32)
    if residual:
        acc = acc + res_ref[...]
    o_ref[...] = acc


def _matmul(x, w, gain=None, res=None, emit_xn=False, tm=None, tn=None):
    m, k = x.shape
    n = w.shape[1]
    tm = tm or _pick(m, (1040, 1024, 512, 256, 128))
    tn = tn or _pick(n, (1024, 896, 512, 256, 128))
    norm = gain is not None
    g = (gain if norm else jnp.ones((k,), f32)).reshape(1, k)
    args = [x, g, w]
    in_specs = [pl.BlockSpec((tm, k), lambda i, j: (i, 0)),
                pl.BlockSpec((1, k), lambda i, j: (0, 0)),
                pl.BlockSpec((k, tn), lambda i, j: (0, j))]
    if res is not None:
        args.append(res)
        in_specs.append(pl.BlockSpec((tm, tn), lambda i, j: (i, j)))
    out_specs = [pl.BlockSpec((tm, tn), lambda i, j: (i, j))]
    out_shape = [jax.ShapeDtypeStruct((m, n), f32)]
    if emit_xn:
        out_specs.append(pl.BlockSpec((tm, k), lambda i, j: (i, 0)))
        out_shape.append(jax.ShapeDtypeStruct((m, k), bf16))
    out = pl.pallas_call(
        functools.partial(_mm_kernel, norm=norm, residual=res is not None, emit_xn=emit_xn),
        grid=(m // tm, n // tn),
        in_specs=in_specs,
        out_specs=out_specs,
        out_shape=out_shape,
        scratch_shapes=[pltpu.VMEM((tm, k), bf16)],
        compiler_params=_params("parallel", "arbitrary"),
        name="mm",
    )(*args)
    return out if emit_xn else out[0]


def _pick(n, cands):
    for c in cands:
        if n % c == 0:
            return c
    return n


def _const(shape):
    nd = len(shape)
    return pl.BlockSpec(shape, lambda i, _n=nd: (0,) * _n)


def _cols(width, start, tb):
    assert start % width == 0
    return pl.BlockSpec((tb, width), lambda i, _c=start // width: (i, _c))


def _blockdiag_tri(tb, c):
    r = _iota((tb, tb), 0)
    s = _iota((tb, tb), 1)
    return jnp.where((r // c == s // c) & (s <= r), 1.0, 0.0).astype(f32)


def _segment_ones(n, seg):
    r = _iota((n, n), 0)
    s = _iota((n, n), 1)
    return jnp.where(r // seg == s // seg, 1.0, 0.0).astype(f32)


GLA_TB = 128
GLA_C = 16


def _gla_gate_log(tail, wg2p, bg):
    z = _dot(tail, wg2p, hi=True) + bg
    return -_softplus(-z) * (1.0 / GLA_TAU)


def _gla_out(o, g, gn):
    outs = []
    for h in range(H_A):
        oh = o[:, h * DV_A:(h + 1) * DV_A]
        oh = oh * lax.rsqrt(jnp.mean(oh * oh, axis=-1, keepdims=True) + 1e-6) * gn
        outs.append(oh * _silu(g[:, h * DV_A:(h + 1) * DV_A]))
    return jnp.concatenate(outs, axis=-1)


def _gla_kernel(qk_ref, v_ref, g_ref, tail_ref, wg2_ref, bg_ref, gn_ref, o_ref, sfin_ref, st_ref, w_ref):
    i = pl.program_id(0)

    @pl.when(i == 0)
    def _():
        st_ref[...] = jnp.zeros_like(st_ref)

    tb, c = GLA_TB, GLA_C
    qk = qk_ref[...]
    q = qk[:, :256] * (DK_A ** -0.5)
    k = qk[:, 256:]
    v = v_ref[...]
    la = _gla_gate_log(tail_ref[...], wg2_ref[...], bg_ref[...])
    b = _dot(_blockdiag_tri(tb, c), la, hi=True)
    e_r = _iota((256, 512), 0) // DK_A
    e_c = _iota((256, 512), 1) // DV_A
    expand = jnp.where(e_r == e_c, 1.0, 0.0).astype(bf16)
    s_idx = _iota((c, 256), 0)
    for j in range(tb // c):
        r0 = j * c
        qj, kj, bj, vj = q[r0:r0 + c], k[r0:r0 + c], b[r0:r0 + c], v[r0:r0 + c]
        for t in range(c):
            wt = qj[t:t + 1] * kj * jnp.exp(bj[t:t + 1] - bj)
            w_ref[t * c:(t + 1) * c, :] = jnp.where(s_idx <= t, wt, 0.0)
        att = _dot(w_ref[...], expand)
        o = jnp.sum(att.reshape(c, c, 512) * vj[None], axis=1)
        qe = qj * jnp.exp(bj)
        bl = bj[c - 1:c]
        ke = kj * jnp.exp(bl - bj)
        dl = jnp.exp(bl)
        inter = []
        for h in range(H_A):
            ks = slice(h * DK_A, (h + 1) * DK_A)
            st = st_ref[h]
            inter.append(_dot(qe[:, ks], st, NT))
            st_ref[h] = st * dl[:, ks] + _dot(vj[:, h * DV_A:(h + 1) * DV_A], ke[:, ks], TN)
        o = o + jnp.concatenate(inter, axis=-1)
        o_ref[r0:r0 + c, :] = _gla_out(o, g_ref[r0:r0 + c, :], gn_ref[...])

    @pl.when(i == pl.num_programs(0) - 1)
    def _():
        for h in range(H_A):
            sfin_ref[h] = st_ref[h].T


def _gla_prompt(cols, wg2p, bg, gn, seq=SEQ):
    tb = GLA_TB
    return pl.pallas_call(
        _gla_kernel,
        grid=(seq // tb,),
        in_specs=[_cols(512, C_QKA, tb), _cols(512, C_VA, tb), _cols(512, C_GA, tb), _cols(128, C_TAIL, tb),
                  _const((128, 256)), _const((1, 256)), _const((1, DV_A))],
        out_specs=[pl.BlockSpec((tb, 512), lambda i: (i, 0)), _const((H_A, DK_A, DV_A))],
        out_shape=[jax.ShapeDtypeStruct((seq, 512), f32), jax.ShapeDtypeStruct((H_A, DK_A, DV_A), f32)],
        scratch_shapes=[pltpu.VMEM((H_A, DV_A, DK_A), f32), pltpu.VMEM((GLA_C * GLA_C, 256), f32)],
        compiler_params=_params("arbitrary"),
        name="gla_prompt",
    )(cols, cols, cols, cols, wg2p, bg, gn)


SSD_TB = 128
SSD_C = 64


def _ssd_conv(ext, conv_w, conv_b, rows):
    out = conv_b
    for j in range(CONV_W):
        shifted = pltpu.roll(ext, j, 0) if j else ext
        out = out + shifted[8:8 + rows] * conv_w[CONV_W - 1 - j:CONV_W - j]
    return out


def _ssd_dt(tail, ex, dtb_x):
    return _softplus(_dot(tail, ex, hi=True) + dtb_x)


def _ssd_out(y, z, gn):
    y = y * _silu(z)
    w = GROUP_W // G_B
    outs = []
    for g in range(G_B):
        yg = y[:, g * w:(g + 1) * w]
        outs.append(yg * lax.rsqrt(jnp.mean(yg * yg, axis=-1, keepdims=True) + 1e-6) * gn[:, g * w:(g + 1) * w])
    return jnp.concatenate(outs, axis=-1)


def _ssd_kernel(z_ref, xs_ref, bc_ref, tail_ref, cw_ref, cb_ref, ex_ref, dtb_ref, alog_ref, dskip_ref, gn_ref,
                sel_ref, o_ref, hfin_ref, carry_ref, h_ref, y_ref):
    i = pl.program_id(0)

    @pl.when(i == 0)
    def _():
        carry_ref[...] = jnp.zeros_like(carry_ref)
        h_ref[...] = jnp.zeros_like(h_ref)

    tb, c = SSD_TB, SSD_C
    u = jnp.concatenate([xs_ref[...], bc_ref[...]], axis=-1)
    ext = jnp.concatenate([carry_ref[...], u], axis=0)
    carry_ref[...] = u[tb - 8:tb]
    act = _silu(_ssd_conv(ext, cw_ref[...], cb_ref[...], tb))
    xs = act[:, :GROUP_W]
    dtx = _ssd_dt(tail_ref[...], ex_ref[...], dtb_ref[...])
    la = dtx * (-jnp.exp(alog_ref[...]))
    b = _dot(_blockdiag_tri(tb, c), la, hi=True)
    brow = _dot(sel_ref[...], b, NT, hi=True)
    xd = xs * dtx
    tri = _iota((c, c), 1) <= _iota((c, c), 0)
    for ch in range(tb // c):
        r0 = ch * c
        rows = slice(r0, r0 + c)
        scores = []
        for g in range(G_B):
            bm = act[rows, GROUP_W + g * N_B:GROUP_W + (g + 1) * N_B]
            cm = act[rows, GROUP_W + G_B * N_B + g * N_B:GROUP_W + G_B * N_B + (g + 1) * N_B]
            scores.append((_dot(cm, bm, NT), bm, cm))
        for h in range(H_B):
            hs = slice(h * P_B, (h + 1) * P_B)
            sc, bm, cm = scores[h // (H_B // G_B)]
            bh = b[rows, hs]
            dec = jnp.where(tri, jnp.exp(bh - brow[h:h + 1, r0:r0 + c]), 0.0)
            xdh = xd[rows, hs]
            hst = h_ref[h]
            y = _dot(sc * dec, xdh) + _dot(cm, hst, NT) * jnp.exp(bh)
            bl = bh[c - 1:c]
            h_ref[h] = hst * jnp.exp(bl) + _dot(xdh * jnp.exp(bl - bh), bm, TN)
            y_ref[rows, hs] = y + xs[rows, hs] * dskip_ref[:, hs]
    o_ref[...] = _ssd_out(y_ref[...], z_ref[...], gn_ref[...])

    @pl.when(i == pl.num_programs(0) - 1)
    def _():
        hfin_ref[...] = h_ref[...]


def _ssd_prompt(cols, cw, cb, ex, dtb_x, alog_x, dskip_x, gn, sel, seq=SEQ):
    tb = SSD_TB
    return pl.pallas_call(
        _ssd_kernel,
        grid=(seq // tb,),
        in_specs=[_cols(512, C_Z, tb), _cols(512, C_XS, tb), _cols(256, C_BC, tb), _cols(128, C_TAIL, tb),
                  _const((CONV_W, XBC_W)), _const((1, XBC_W)), _const((128, 512)), _const((1, 512)), _const((1, 512)),
                  _const((1, 512)), _const((1, 512)), _const((8, 512))],
        out_specs=[pl.BlockSpec((tb, 512), lambda i: (i, 0)), _const((H_B, P_B, N_B))],
        out_shape=[jax.ShapeDtypeStruct((seq, 512), f32), jax.ShapeDtypeStruct((H_B, P_B, N_B), f32)],
        scratch_shapes=[pltpu.VMEM((8, XBC_W), f32), pltpu.VMEM((H_B, P_B, N_B), f32), pltpu.VMEM((tb, 512), f32)],
        compiler_params=_params("arbitrary"),
        name="ssd_prompt",
    )(cols, cols, cols, cols, cw, cb, ex, dtb_x, alog_x, dskip_x, gn, sel)


RWKV_TB = 128
RWKV_C = 64


def _rwkv_pre(x, xprev, mu, w0, a0, k_k, k_a, w2a2, g2):
    mixed = x + (xprev - x) * mu
    r = mixed[:, :512]
    kd = mixed[:, 512:1024]
    v = mixed[:, 1024:1536]
    lw = mixed[:, 1536:1664]
    lin = jnp.where(_iota(lw.shape, 1) < 64, jnp.tanh(lw), lw)
    wa = _dot(lin, w2a2, hi=True)
    w = -_softplus(-(w0 + wa[:, :512])) - 0.5
    a = _sigmoid(a0 + wa[:, 512:])
    g = _dot(_sigmoid(mixed[:, 1664:1792]), g2)
    kk = kd * k_k
    ss = _dot(kk * kk, _segment_ones(512, N_D), hi=True)
    kk = kk * lax.rsqrt(jnp.maximum(ss, 1e-24))
    kd = kd * (1.0 + (a - 1.0) * k_a)
    return r, w, kd, v, -kk, kk * a, g


def _rwkv_post(o, r, kd, v, g, r_k, lnx_g, lnx_b):
    seg = _segment_ones(512, N_D)
    mu = _dot(o, seg, hi=True) * (1.0 / N_D)
    d = o - mu
    var = _dot(d * d, seg, hi=True) * (1.0 / N_D)
    o = d * lax.rsqrt(var + LNX_EPS) * lnx_g + lnx_b
    o = o + _dot(r * kd * r_k, seg, hi=True) * v
    return o * g


def _rwkv_kernel(d_ref, mu_ref, w0_ref, a0_ref, kk_ref, ka_ref, rk_ref, lg_ref, lb_ref, w2a2_ref, g2_ref,
                 o_ref, sfin_ref, prev_ref, s_ref, oacc_ref):
    i = pl.program_id(0)

    @pl.when(i == 0)
    def _():
        prev_ref[...] = jnp.zeros_like(prev_ref)
        s_ref[...] = jnp.zeros_like(s_ref)

    tb, c = RWKV_TB, RWKV_C
    x = d_ref[...]
    xprev = jnp.where(_iota(x.shape, 0) == 0, prev_ref[0:1, :], pltpu.roll(x, 1, 0))
    prev_ref[0:1, :] = x[tb - 1:tb]
    r, w, kd, v, alpha, beta, g = _rwkv_pre(x, xprev, mu_ref[...], w0_ref[...], a0_ref[...], kk_ref[...], ka_ref[...],
                                            w2a2_ref[...], g2_ref[...])
    ld = -jnp.exp(w)
    cum = _dot(_blockdiag_tri(tb, c), ld, hi=True)
    at = alpha * jnp.exp(cum - ld)
    rt = r * jnp.exp(cum)
    einv = jnp.exp(-cum)
    kt = kd * einv
    bt = beta * einv
    ri = _iota((c, c), 0)
    ci = _iota((c, c), 1)
    strict, incl = ci < ri, ci <= ri
    eye = jnp.where(ri == ci, 1.0, 0.0).astype(f32)
    pairs = [(ch, h) for ch in range(tb // c) for h in range(H_D)]
    rows_of = lambda ch: slice(ch * c, (ch + 1) * c)
    lanes_of = lambda h: slice(h * N_D, (h + 1) * N_D)
    ar, kb, lk, mkb, tinv, p = {}, {}, {}, {}, {}, {}
    for ch, h in pairs:
        rows, hs = rows_of(ch), lanes_of(h)
        ar[ch, h] = jnp.concatenate([at[rows, hs], rt[rows, hs]], axis=0)
        kb[ch, h] = jnp.concatenate([kt[rows, hs], bt[rows, hs]], axis=0)
    for key in pairs:
        gram = _dot(ar[key], kb[key], NT)
        lk[key] = jnp.where(strict, gram[:c, :c], 0.0)
        p[key] = jnp.where(strict, gram[:c, c:], 0.0)
        mkb[key] = jnp.concatenate([jnp.where(incl, gram[c:, :c], 0.0), jnp.where(incl, gram[c:, c:], 0.0)], axis=1)
        tinv[key] = eye + p[key]
    for _ in range(5):
        for key in pairs:
            p[key] = _dot(p[key], p[key])
        for key in pairs:
            tinv[key] = tinv[key] + _dot(tinv[key], p[key])
    for ch in range(tb // c):
        rows = rows_of(ch)
        cl = cum[ch * c + c - 1:ch * c + c]
        efin = jnp.exp(cl - cum[rows])
        kfin = kd[rows] * efin
        bfin = beta[rows] * efin
        dfin = jnp.exp(cl)
        heads = range(H_D)
        s0 = [s_ref[h] for h in heads]
        ars = [_dot(ar[ch, h], s0[h], NT) for h in heads]
        lkv = [_dot(lk[ch, h], v[rows, lanes_of(h)]) for h in heads]
        u = [_dot(tinv[ch, h], ars[h][:c] + lkv[h]) for h in heads]
        vu = [jnp.concatenate([v[rows, lanes_of(h)], u[h]], axis=0) for h in heads]
        for h in heads:
            hs = lanes_of(h)
            oacc_ref[rows, hs] = ars[h][c:] + _dot(mkb[ch, h], vu[h])
            kbfin = jnp.concatenate([kfin[:, hs], bfin[:, hs]], axis=0)
            s_ref[h] = s0[h] * dfin[:, hs] + _dot(vu[h], kbfin, TN)
    o_ref[...] = _rwkv_post(oacc_ref[...], r, kd, v, g, rk_ref[...], lg_ref[...], lb_ref[...])

    @pl.when(i == pl.num_programs(0) - 1)
    def _():
        sfin_ref[...] = s_ref[...]


def _rwkv_prompt(cols, mu, w0, a0, k_k, k_a, r_k, lnx_g, lnx_b, w2a2, g2, seq=SEQ):
    tb = RWKV_TB
    vec = _const((1, 512))
    return pl.pallas_call(
        _rwkv_kernel,
        grid=(seq // tb,),
        in_specs=[_cols(1792, C_D, tb), _const((1, 1792)), vec, vec, vec, vec, vec, vec, vec,
                  _const((128, 1024)), _const((128, 512))],
        out_specs=[pl.BlockSpec((tb, 512), lambda i: (i, 0)), _const((H_D, N_D, N_D))],
        out_shape=[jax.ShapeDtypeStruct((seq, 512), f32), jax.ShapeDtypeStruct((H_D, N_D, N_D), f32)],
        scratch_shapes=[pltpu.VMEM((8, 1792), f32), pltpu.VMEM((H_D, N_D, N_D), f32), pltpu.VMEM((tb, 512), f32)],
        compiler_params=_params("arbitrary"),
        name="rwkv_prompt",
    )(cols, mu, w0, a0, k_k, k_a, r_k, lnx_g, lnx_b, w2a2, g2)


def _rope_tables(pos):
    half = DK_C // 2
    inv = ROPE_THETA ** (-jnp.arange(half, dtype=f32) / half)
    ang = pos.astype(f32)[:, None] * inv[None, :]
    cos, sin = jnp.cos(ang), jnp.sin(ang)
    return jnp.tile(jnp.concatenate([cos, cos], axis=-1), (1, 2)), jnp.tile(jnp.concatenate([-sin, sin], axis=-1), (1, 2))


def _qk_norm_rope(x, gain, cos, sin):
    ms = _dot(x * x, _segment_ones(512, DK_C), hi=True) * (1.0 / DK_C)
    x = x * lax.rsqrt(ms + 1e-6) * gain
    first = (_iota(x.shape, 1) % DK_C) < (DK_C // 2)
    partner = jnp.where(first, pltpu.roll(x, 512 - DK_C // 2, 1), pltpu.roll(x, DK_C // 2, 1))
    cos = jnp.concatenate([cos] * 4, axis=-1)
    sin = jnp.concatenate([sin] * 4, axis=-1)
    return x * cos + partner * sin


def _diff_prep_kernel(q_ref, k_ref, cos_ref, sin_ref, gq_ref, gk_ref, qb_ref, kf_ref, kb_ref):
    cos, sin = cos_ref[...], sin_ref[...]
    q = _qk_norm_rope(q_ref[...], gq_ref[...], cos, sin)
    k = _qk_norm_rope(k_ref[...], gk_ref[...], cos, sin)
    qb_ref[...] = (q * (DK_C ** -0.5)).astype(bf16)
    kf_ref[...] = k
    kb_ref[...] = k.astype(bf16)


def _diff_prep(cols, cos, sin, gq, gk, row0, rows, tb):
    assert row0 % tb == 0
    r0 = row0 // tb
    colspec = lambda start: pl.BlockSpec((tb, 512), lambda i, _c=start // 512: (i + r0, _c))
    out = pl.BlockSpec((tb, 512), lambda i: (i, 0))
    tab = pl.BlockSpec((tb, 128), lambda i: (i, 0))
    return pl.pallas_call(
        _diff_prep_kernel,
        grid=(rows // tb,),
        in_specs=[colspec(C_QC), colspec(C_KC), tab, tab, _const((1, 512)), _const((1, 512))],
        out_specs=[out, out, out],
        out_shape=[jax.ShapeDtypeStruct((rows, 512), bf16), jax.ShapeDtypeStruct((rows, 512), f32),
                   jax.ShapeDtypeStruct((rows, 512), bf16)],
        compiler_params=_params("parallel"),
        name="diff_prep",
    )(cols, cols, cos, sin, gq, gk)


def _diff_finish(o1, o2, lam, gn, lam_init):
    o = o1 - lam * o2
    return o * lax.rsqrt(jnp.mean(o * o, axis=-1, keepdims=True) + 1e-6) * gn * (1.0 - lam_init)


def _flash_kernel(q_ref, k_ref, v_ref, lam_ref, gn_ref, o_ref, m_ref, l_ref, acc_ref, *, tq, tk, lam_init):
    qi, kj = pl.program_id(1), pl.program_id(2)

    @pl.when(kj == 0)
    def _():
        m_ref[...] = jnp.full_like(m_ref, -jnp.inf)
        l_ref[...] = jnp.zeros_like(l_ref)
        acc_ref[...] = jnp.zeros_like(acc_ref)

    def step(masked):
        q = q_ref[...]
        k = k_ref[...]
        v = v_ref[...].astype(bf16)
        lane = _iota(q.shape, 1)
        if masked:
            keep = (kj * tk + _iota((tk, tq), 0)) <= (qi * tq + _iota((tk, tq), 1))
        for m in range(2):
            qm = jnp.where((lane < DK_C) if m == 0 else (lane >= DK_C), q, jnp.zeros_like(q))
            s = _dot(k, qm, NT)
            if masked:
                s = jnp.where(keep, s, -jnp.inf)
            m_old = m_ref[m]
            m_new = jnp.maximum(m_old, jnp.max(s, axis=0, keepdims=True))
            p = jnp.exp(s - m_new)
            corr = jnp.exp(m_old - m_new)
            l_ref[m] = corr * l_ref[m] + jnp.sum(p, axis=0, keepdims=True)
            acc_ref[m] = corr * acc_ref[m] + _dot(v, p, TN)
            m_ref[m] = m_new

    first, last = kj * tk, kj * tk + tk - 1

    @pl.when(last <= qi * tq)
    def _():
        step(False)

    @pl.when((first <= qi * tq + tq - 1) & (last > qi * tq))
    def _():
        step(True)

    @pl.when(kj == pl.num_programs(2) - 1)
    def _():
        o1 = (acc_ref[0] / l_ref[0]).T
        o2 = (acc_ref[1] / l_ref[1]).T
        o_ref[...] = _diff_finish(o1, o2, lam_ref[...], gn_ref[...], lam_init)


def _diff_attn_prompt(qb, kb, cols, lam, gn, lam_init, seq=SEQ, tq=512, tk=512):
    nq, nk = seq // tq, seq // tk
    last = lambda i: (i * tq + tq - 1) // tk
    kmap = lambda h, i, j: (jnp.minimum(j, last(i)), h)
    vmap_ = lambda h, i, j: (jnp.minimum(j, last(i)), C_VC // DV_C + h)
    return pl.pallas_call(
        functools.partial(_flash_kernel, tq=tq, tk=tk, lam_init=lam_init),
        grid=(H_C, nq, nk),
        in_specs=[pl.BlockSpec((tq, 128), lambda h, i, j: (i, h)), pl.BlockSpec((tk, 128), kmap),
                  pl.BlockSpec((tk, DV_C), vmap_), pl.BlockSpec((1, 128), lambda h, i, j: (0, 0)),
                  pl.BlockSpec((1, DV_C), lambda h, i, j: (0, 0))],
        out_specs=pl.BlockSpec((tq, DV_C), lambda h, i, j: (i, h)),
        out_shape=jax.ShapeDtypeStruct((seq, 512), f32),
        scratch_shapes=[pltpu.VMEM((2, 1, tq), f32), pltpu.VMEM((2, 1, tq), f32), pltpu.VMEM((2, DV_C, tq), f32)],
        compiler_params=_params("parallel", "parallel", "arbitrary"),
        name="diff_attn_prompt",
    )(qb, kb, cols, lam, gn)


PEER_SEL_TB = 128
PEER_TB = 640
PEER_EB = 512
_PAIRS = [(a, b) for a in range(TOPK_P) for b in range(TOPK_P) if (a + 1) * (b + 1) <= TOPK_P]
_NPAIR = -(-len(_PAIRS) // 8) * 8


def _top16_ranks(s):
    n_idx = _iota(s.shape, 0)
    rank = jnp.full(s.shape, float(TOPK_P), f32)
    tops = []
    work = s
    for k in range(TOPK_P):
        m = jnp.max(work, axis=0, keepdims=True)
        idx = jnp.min(jnp.where(work == m, n_idx, N_KEYS), axis=0, keepdims=True)
        hit = n_idx == idx
        rank = jnp.where(hit, float(k), rank)
        work = jnp.where(hit, -jnp.inf, work)
        tops.append(m)
    return rank, jnp.concatenate(tops, axis=0)


def _peer_select_kernel(q_ref, keys_ref, pk1_ref, flat_ref, ci_ref, cnt_ref, e2_ref, r2_ref):
    tb = q_ref.shape[0]
    flat = flat_ref[...]
    for h in range(H_P):
        ranks, tops, scores = [], [], []
        for x in range(2):
            hx = 2 * h + x
            s = _dot(keys_ref[hx], q_ref[:, hx * 128:(hx + 1) * 128], NT, hi=True)
            rk, tp = _top16_ranks(s)
            ranks.append(rk), tops.append(tp), scores.append(s)
        t1, t2 = tops
        cand = jnp.concatenate([t1[a:a + 1] + t2[b:b + 1] for a, b in _PAIRS]
                               + [jnp.full((_NPAIR - len(_PAIRS), tb), -jnp.inf, f32)], axis=0)
        work = cand
        sel = jnp.zeros(cand.shape, f32)
        for _ in range(TOPK_P):
            m = jnp.max(work, axis=0, keepdims=True)
            idx = jnp.min(jnp.where(work == m, flat, 4096.0), axis=0, keepdims=True)
            hit = flat == idx
            sel = jnp.where(hit, 1.0, sel)
            work = jnp.where(hit, -jnp.inf, work)
        top = t1[0:1] + t2[0:1]
        z = jnp.sum(sel * jnp.exp(jnp.where(sel > 0, cand - top, 0.0)), axis=0, keepdims=True)
        cnt = _dot(pk1_ref[...], sel)
        cnt_i = jnp.zeros((N_KEYS, tb), f32)
        for k1 in range(TOPK_P):
            cnt_i = cnt_i + jnp.where(ranks[0] == float(k1), cnt[k1:k1 + 1], 0.0)
        ci_ref[0, h] = jnp.exp(scores[0] - t1[0:1]) / z
        cnt_ref[0, h] = cnt_i
        e2_ref[0, h] = jnp.exp(scores[1] - t2[0:1]).astype(bf16)
        r2_ref[0, h] = ranks[1].astype(bf16)


def _peer_select(q, keys):
    t = q.shape[0]
    tb, per = PEER_SEL_TB, PEER_TB // PEER_SEL_TB
    pk1 = np.zeros((TOPK_P, _NPAIR), np.float32)
    flat = np.full((_NPAIR, 1), 8192.0, np.float32)
    for r, (a, b) in enumerate(_PAIRS):
        pk1[a, r] = 1.0
        flat[r, 0] = a * TOPK_P + b
    out = pl.BlockSpec((1, H_P, N_KEYS, tb), lambda i: (i // per, 0, 0, i % per))
    shp = lambda dt: jax.ShapeDtypeStruct((t // PEER_TB, H_P, N_KEYS, PEER_TB), dt)
    return pl.pallas_call(
        _peer_select_kernel,
        grid=(t // tb,),
        in_specs=[pl.BlockSpec((tb, 2048), lambda i: (i, 0)), _const((2 * H_P, N_KEYS, 128)),
                  _const((TOPK_P, _NPAIR)), _const((_NPAIR, 1))],
        out_specs=[out, out, out, out],
        out_shape=[shp(f32), shp(f32), shp(bf16), shp(bf16)],
        compiler_params=_params("parallel"),
        name="peer_select",
    )(q, keys, jnp.asarray(pk1), jnp.asarray(flat))


def _sample_pre_kernel(qk_ref, tail_ref, xs_ref, bc_ref, d_ref, conv_ref, shift_ref,
                       wg2_ref, bg_ref, cw_ref, cb_ref, ex_ref, dtb_ref, alog_ref,
                       mu_ref, w0_ref, a0_ref, kk_ref, ka_ref, w2a2_ref, g2_ref,
                       gq_ref, gk_ref, ga_ref, act_ref, sdec_ref, sxd_ref, rw_ref):
    qk = qk_ref[...]
    la = _gla_gate_log(tail_ref[...], wg2_ref[...], bg_ref[...])
    gq_ref[...] = qk[:, :256] * (DK_A ** -0.5)
    gk_ref[...] = qk[:, 256:]
    ga_ref[...] = jnp.exp(la)
    u = jnp.concatenate([xs_ref[...], bc_ref[...]], axis=-1)
    cw = cw_ref[...]
    conv = cb_ref[...] + u * cw[CONV_W - 1:CONV_W]
    for j in range(CONV_W - 1):
        conv = conv + conv_ref[j] * cw[j:j + 1]
    act = _silu(conv)
    act_ref[...] = act
    dtx = _ssd_dt(tail_ref[...], ex_ref[...], dtb_ref[...])
    sdec_ref[...] = jnp.exp(dtx * (-jnp.exp(alog_ref[...])))
    sxd_ref[...] = act[:, :GROUP_W] * dtx
    r, w, kd, v, alpha, beta, g = _rwkv_pre(d_ref[...], shift_ref[...], mu_ref[...], w0_ref[...], a0_ref[...],
                                            kk_ref[...], ka_ref[...], w2a2_ref[...], g2_ref[...])
    for n, t in enumerate((r, jnp.exp(-jnp.exp(w)), kd, v, alpha, beta, g)):
        rw_ref[n] = t


def _sample_pre(cols, conv_st, shift_st, wts, row0, b):
    tb = b
    assert row0 % tb == 0
    r0 = row0 // tb
    cs = lambda w, start: pl.BlockSpec((tb, w), lambda i, _c=start // w: (r0, _c))
    full = lambda *s: jax.ShapeDtypeStruct(s, f32)
    return pl.pallas_call(
        _sample_pre_kernel,
        grid=(1,),
        in_specs=[cs(512, C_QKA), cs(128, C_TAIL), cs(512, C_XS), cs(256, C_BC), cs(1792, C_D),
                  _const((CONV_W - 1, b, XBC_W)), _const((b, 1792))] + [_const(w.shape) for w in wts],
        out_specs=[_const((b, 256))] * 3 + [_const((b, XBC_W)), _const((b, 512)), _const((b, 512)), _const((7, b, 512))],
        out_shape=[full(b, 256)] * 3 + [full(b, XBC_W), full(b, 512), full(b, 512), full(7, b, 512)],
        compiler_params=_params("arbitrary"),
        name="sample_pre",
    )(cols, cols, cols, cols, cols, conv_st, shift_st, *wts)


def _rows_to_tile(row, heads, width, reps):
    return jnp.concatenate([jnp.broadcast_to(row[:, h * width:(h + 1) * width], (reps, width)) for h in range(heads)], axis=0)


def _gla_step_kernel(s_ref, a_ref, k_ref, q_ref, v_ref, sn_ref, o_ref):
    s = a_ref[0] * s_ref[0] + k_ref[0] * _rows_to_tile(v_ref[0], H_A, DV_A, DK_A)
    sn_ref[0] = s
    qs = q_ref[0] * s
    o_ref[0] = jnp.concatenate([jnp.sum(qs[h * DK_A:(h + 1) * DK_A], axis=0, keepdims=True) for h in range(H_A)], axis=-1)


def _ssd_step_kernel(h_ref, dec_ref, xd_ref, bc_ref, hn_ref, y_ref):
    bc = bc_ref[0]
    reps = (H_B // G_B) * P_B
    hn = dec_ref[0] * h_ref[0] + xd_ref[0] * _rows_to_tile(bc[:, :G_B * N_B], G_B, N_B, reps)
    hn_ref[0] = hn
    y_ref[0] = jnp.sum(hn * _rows_to_tile(bc[:, G_B * N_B:], G_B, N_B, reps), axis=-1, keepdims=True)


def _rwkv_step_kernel(s_ref, rows_ref, v_ref, sn_ref, o_ref):
    tile = lambda n: _rows_to_tile(rows_ref[0, n:n + 1, :], H_D, N_D, N_D)
    s = s_ref[0]
    sa = jnp.sum(s * tile(4), axis=-1, keepdims=True)
    s = s * tile(1) + sa * tile(5) + v_ref[0] * tile(2)
    sn_ref[0] = s
    o_ref[0] = jnp.sum(s * tile(0), axis=-1, keepdims=True)


def _state_step(kernel, name, state, ins, outs):
    b = state.shape[0]
    spec = lambda shp: pl.BlockSpec((1,) + tuple(shp[1:]), lambda i: (i, 0, 0))
    return pl.pallas_call(
        kernel,
        grid=(b,),
        in_specs=[spec(state.shape)] + [spec(a.shape) for a in ins],
        out_specs=[spec(state.shape)] + [spec(s) for s in outs],
        out_shape=[jax.ShapeDtypeStruct(state.shape, f32)] + [jax.ShapeDtypeStruct(s, f32) for s in outs],
        compiler_params=_params("parallel"),
        name=name,
    )(state, *ins)


def _split_dot(x, w):
    hi = x.astype(bf16)
    lo = (x - hi.astype(f32)).astype(bf16)
    return _dot(hi, w) + _dot(lo, w)


def _rowhead_attend(kx, vx, q4, ind, n_maps, valid_rows=None):
    r = kx.shape[0]
    g = r // 8
    q8 = jnp.concatenate([q4, q4], axis=0)
    qt = jnp.broadcast_to(q8[None], (g, 8, 128)).reshape(r, 128)
    s = _split_dot(kx * qt, ind)
    if valid_rows is not None:
        s = jnp.where(_iota(s.shape, 0) < valid_rows, s, -jnp.inf)
    s3 = s.reshape(g, 8, 128 * n_maps)
    m8 = jnp.max(s3, axis=0)
    mh = jnp.maximum(m8, pltpu.roll(m8, 4, 0))
    p3 = jnp.exp(s3 - mh[None])
    l8 = jnp.sum(p3, axis=0)
    lh = l8 + pltpu.roll(l8, 4, 0)
    v3 = vx.reshape(g, 8, 128)
    outs = []
    for m in range(n_maps):
        ms = slice(m * 128, (m + 1) * 128)
        pv = jnp.sum(p3[:, :, ms] * v3, axis=0)
        pv = pv + pltpu.roll(pv, 4, 0)
        outs.append(pv / lh[:, ms])
    return outs


def _diff_decode_kernel(pt_ref, q_ref, ks_ref, vs_ref, *rest, lam_init, n_pages):
    del pt_ref
    k_refs, v_refs = rest[:n_pages], rest[n_pages:2 * n_pages]
    ind_ref, lam_ref, gn_ref, o_ref = rest[2 * n_pages:]
    own = lambda ref: jnp.concatenate([ref[0], ref[0]], axis=0)
    kx = jnp.concatenate([r[...] for r in k_refs] + [own(ks_ref)], axis=0)
    vx = jnp.concatenate([r[...] for r in v_refs] + [own(vs_ref)], axis=0)
    o1, o2 = _rowhead_attend(kx, vx, q_ref[0].astype(f32), ind_ref[...], 2, valid_rows=kx.shape[0] - 4)
    o_ref[0] = _diff_finish(o1, o2, lam_ref[...], gn_ref[...], lam_init)


def _diff_decode(pt_flat, layer, qb, ks, vs, ck, cv, ind, lam, gn, lam_init, n_pages):
    b = qb.shape[0]
    rows = ck.shape[2]
    row = pl.BlockSpec((1, H_C, 128), lambda i, pt: (i, 0, 0))
    page = lambda j: pl.BlockSpec((None, None, rows, 128), lambda i, pt, _j=j: (layer, pt[i * n_pages + _j], 0, 0))
    cst = lambda shp: pl.BlockSpec(shp, lambda i, pt: (0, 0))
    pages = [page(j) for j in range(n_pages)]
    return pl.pallas_call(
        functools.partial(_diff_decode_kernel, lam_init=lam_init, n_pages=n_pages),
        grid_spec=pltpu.PrefetchScalarGridSpec(
            num_scalar_prefetch=1, grid=(b,),
            in_specs=[row, row, row] + pages + pages + [cst((128, 256)), cst((1, 128)), cst((1, DV_C))],
            out_specs=pl.BlockSpec((1, 8, 128), lambda i, pt: (i, 0, 0))),
        out_shape=jax.ShapeDtypeStruct((b, 8, 128), f32),
        compiler_params=_params("parallel"),
        name="diff_decode",
    )(pt_flat, qb, ks, vs, *([ck] * n_pages), *([cv] * n_pages), ind, lam, gn)


def _sample_post_kernel(oa_ref, ga_ref, gn_a_ref, y_ref, act_ref, z_ref, dskip_ref, gn_b_ref, oc_ref,
                        od_ref, rw_ref, rk_ref, lg_ref, lb_ref, o_ref):
    oa = _gla_out(oa_ref[...], ga_ref[...], gn_a_ref[...])
    ob = _ssd_out(y_ref[...] + act_ref[:, :GROUP_W] * dskip_ref[...], z_ref[...], gn_b_ref[...])
    od = _rwkv_post(od_ref[...], rw_ref[0], rw_ref[2], rw_ref[3], rw_ref[6], rk_ref[...], lg_ref[...], lb_ref[...])
    o_ref[...] = jnp.concatenate([oa, ob, oc_ref[...], od], axis=-1)


def _sample_post(cols, oa, y, act, oc, od, rw, gn_a, dskip_x, gn_b, r_k, lnx_g, lnx_b, row0, b):
    r0 = row0 // b
    cs = lambda w, start: pl.BlockSpec((b, w), lambda i, _c=start // w: (r0, _c))
    c512 = _const((b, 512))
    v512 = _const((1, 512))
    return pl.pallas_call(
        _sample_post_kernel,
        grid=(1,),
        in_specs=[c512, cs(512, C_GA), _const((1, DV_A)), c512, _const((b, XBC_W)), cs(512, C_Z), v512, v512, c512,
                  c512, _const((7, b, 512)), v512, v512, v512],
        out_specs=_const((b, D_MODEL)),
        out_shape=jax.ShapeDtypeStruct((b, D_MODEL), f32),
        compiler_params=_params("arbitrary"),
        name="sample_post",
    )(oa, cols, gn_a, y, act, cols, dskip_x, gn_b, oc, od, rw, r_k, lnx_g, lnx_b)


def _head_rms(x, gain, width):
    outs = []
    for h in range(x.shape[1] // width):
        xh = x[:, h * width:(h + 1) * width]
        outs.append(xh * lax.rsqrt(jnp.mean(xh * xh, axis=-1, keepdims=True) + 1e-6) * gain)
    return jnp.concatenate(outs, axis=-1)


def _mem_kv_kernel(m_ref, g_ref, w_ref, gk_ref, k_ref, v_ref):
    m = m_ref[...]
    m = m * lax.rsqrt(jnp.mean(m * m, axis=-1, keepdims=True) + 1e-6) * g_ref[...]
    kv = _dot(m, w_ref[...])
    k_ref[...] = _head_rms(kv[:, :D_MEM], gk_ref[...], DH_M)
    v_ref[...] = kv[:, D_MEM:]


def _mem_kv(mem, g_src, w_kv, g_k):
    shp = jax.ShapeDtypeStruct((N_MEM, D_MEM), f32)
    return pl.pallas_call(
        _mem_kv_kernel, out_shape=[shp, shp],
        compiler_params=pltpu.CompilerParams(vmem_limit_bytes=VMEM_LIMIT), name="mem_kv",
    )(mem, g_src, w_kv, g_k)


def _mem_attn_prompt_kernel(q_ref, gq_ref, k_ref, v_ref, o_ref):
    q = _head_rms(q_ref[...], gq_ref[...], DH_M) * (DH_M ** -0.5)
    k, v = k_ref[...], v_ref[...]
    outs = []
    for h in range(H_M):
        hs = slice(h * DH_M, (h + 1) * DH_M)
        s = _dot(q[:, hs], k[:, hs], NT)
        p = jnp.exp(s - jnp.max(s, axis=-1, keepdims=True))
        outs.append(_dot(p, v[:, hs]) / jnp.sum(p, axis=-1, keepdims=True))
    o_ref[...] = jnp.concatenate(outs, axis=-1)


def _mem_attn_prompt(q, gq, k, v, seq=SEQ, tb=512):
    return pl.pallas_call(
        _mem_attn_prompt_kernel,
        grid=(seq // tb,),
        in_specs=[pl.BlockSpec((tb, D_MEM), lambda i: (i, 0)), _const((1, DH_M)), _const((N_MEM, D_MEM)),
                  _const((N_MEM, D_MEM))],
        out_specs=pl.BlockSpec((tb, D_MEM), lambda i: (i, 0)),
        out_shape=jax.ShapeDtypeStruct((seq, D_MEM), f32),
        compiler_params=_params("parallel"),
        name="mem_attn_prompt",
    )(q, gq, k, v)


def _mem_attn_sample_kernel(q_ref, gq_ref, k_ref, v_ref, ind_ref, o_ref):
    q = q_ref[0]
    q = q * lax.rsqrt(jnp.mean(q * q, axis=-1, keepdims=True) + 1e-6) * gq_ref[...] * (DH_M ** -0.5)
    o_ref[0] = _rowhead_attend(k_ref[...], v_ref[...], q, ind_ref[...], 1)[0]


def _mem_attn_sample(q, gq, ck, cv, ind, layer):
    b = q.shape[0]
    kv = pl.BlockSpec((None, None, N_MEM * H_M, DH_M), lambda i: (layer, i, 0, 0))
    return pl.pallas_call(
        _mem_attn_sample_kernel,
        grid=(b,),
        in_specs=[pl.BlockSpec((1, H_M, DH_M), lambda i: (i, 0, 0)), _const((1, DH_M)), kv, kv, _const((128, 128))],
        out_specs=pl.BlockSpec((1, 8, DH_M), lambda i: (i, 0, 0)),
        out_shape=jax.ShapeDtypeStruct((b, 8, DH_M), f32),
        compiler_params=_params("parallel"),
        name="mem_attn_sample",
    )(q, gq, ck, cv, ind)


def _cast_kernel(x_ref, o_ref):
    o_ref[...] = x_ref[...].astype(bf16)


def _table_bf16(tab, layer, rows=1024):
    n, d = tab.shape[1:]
    return pl.pallas_call(
        _cast_kernel,
        grid=(n // rows,),
        in_specs=[pl.BlockSpec((None, rows, d), lambda i: (layer, i, 0))],
        out_specs=pl.BlockSpec((rows, d), lambda i: (i, 0)),
        out_shape=jax.ShapeDtypeStruct((n, d), bf16),
        compiler_params=_params("parallel"),
        name="table_bf16",
    )(tab)


def _gelu(x):
    return 0.5 * x * (1.0 + jnp.tanh(0.7978845608028654 * (x + 0.044715 * x * x * x)))


def _peer_dense_kernel(xn_ref, u_ref, v_ref, ci_ref, cnt_ref, e2_ref, r2_ref, res_ref, o_ref, w_ref):
    e = pl.program_id(1)
    tb = xn_ref.shape[0]
    n_i = PEER_EB // N_KEYS

    @pl.when(e == 0)
    def _():
        o_ref[...] = res_ref[...]

    n_half = 2
    per = n_i // n_half
    parts = []
    for half in range(n_half):
        for ii in range(half * per, (half + 1) * per):
            w = jnp.zeros((N_KEYS, tb), bf16)
            for h in range(H_P):
                row = pl.ds(e * n_i + ii, 1)
                ci = ci_ref[0, h, row, :].astype(bf16)
                cnt = cnt_ref[0, h, row, :].astype(bf16)
                w = w + jnp.where(r2_ref[0, h] < cnt, e2_ref[0, h] * ci, jnp.zeros((), bf16))
            w_ref[ii * N_KEYS:(ii + 1) * N_KEYS, :] = w
        rows = slice(half * per * N_KEYS, (half + 1) * per * N_KEYS)
        hid = _gelu(_dot(u_ref[rows, :], xn_ref[...], NT))
        parts.append(_dot(hid.astype(bf16) * w_ref[rows, :], v_ref[rows, :], TN))
    o_ref[...] += sum(parts)


def _peer_dense(xn, u, v, sel, res):
    t = xn.shape[0]
    tb, eb = PEER_TB, PEER_EB
    once = pl.Buffered(1)
    selspec = pl.BlockSpec((1, H_P, N_KEYS, tb), lambda i, e: (i, 0, 0, 0), pipeline_mode=once)
    return pl.pallas_call(
        _peer_dense_kernel,
        grid=(t // tb, N_EXPERTS // eb),
        in_specs=[pl.BlockSpec((tb, D_MODEL), lambda i, e: (i, 0), pipeline_mode=once),
                  pl.BlockSpec((eb, D_MODEL), lambda i, e: (e, 0)),
                  pl.BlockSpec((eb, D_MODEL), lambda i, e: (e, 0)), selspec, selspec, selspec, selspec,
                  pl.BlockSpec((tb, D_MODEL), lambda i, e: (i, 0), pipeline_mode=once)],
        out_specs=pl.BlockSpec((tb, D_MODEL), lambda i, e: (i, 0)),
        out_shape=jax.ShapeDtypeStruct((t, D_MODEL), f32),
        scratch_shapes=[pltpu.VMEM((eb, tb), bf16)],
        compiler_params=_params("parallel", "arbitrary"),
        name="peer_dense",
    )(xn, u, v, *sel, res)


def _pad_w_in(w):
    a0, b0, c0, d0 = 0, 1552, 2840, 4376
    seg = lambda s, n: w[:, s:s + n]
    parts = [seg(d0, 1792), seg(b0 + 1024, 256), seg(a0, 512), seg(a0 + 512, 512), seg(a0 + 1024, 512),
             seg(b0, 512), seg(b0 + 512, 512), seg(c0, 512), seg(c0 + 512, 512), seg(c0 + 1024, 512),
             seg(a0 + 1536, 16), seg(b0 + 1280, 8), jnp.zeros((w.shape[0], IN_PAD - 6168), w.dtype)]
    return jnp.concatenate(parts, axis=1).astype(bf16)


def _layer_consts():
    ex = np.zeros((128, 512), np.float32)
    sel = np.zeros((8, 512), np.float32)
    for h in range(8):
        ex[GK_RANK + h, h * 64:(h + 1) * 64] = 1.0
        sel[h, h * 64] = 1.0
    ind2 = np.zeros((128, 256), np.float32)
    ind2[:DK_C, :128] = 1.0
    ind2[DK_C:, 128:] = 1.0
    ind1 = np.ones((128, 128), np.float32)
    return jnp.asarray(ex), jnp.asarray(sel), jnp.asarray(ind2), jnp.asarray(ind1)


def kernel(x_prompt, x_sample, cache_diff_k, cache_diff_v, cache_mem_k, cache_mem_v, state_gla, state_ssm, state_conv, state_rwkv, state_shift, page_table, mem_prompt, norm_mix, w_in, w_out, gla_wg2, gla_bg, gla_gn, conv_w, conv_b, dt_bias, a_log, d_skip, ssm_gn, dq_norm, dk_norm, lam_q, lam_k, diff_gn, shift_mu, w0, w2, a0, a2, g2, k_k, k_a, r_k, lnx_g, lnx_b, norm_mem, norm_memsrc, w_mq, w_mk, w_mv, w_mo, mq_norm, mk_norm, norm_ffn, peer_wq, peer_keys, peer_u, peer_v):
    nb = DEC_BATCH
    n_pages = page_table.shape[1]
    n_pool = cache_diff_k.shape[1]
    x = jnp.concatenate([x_prompt[0], x_sample[:, 0]], axis=0)
    pt_flat = page_table.reshape(-1)
    cos_p, sin_p = _rope_tables(jnp.arange(SEQ, dtype=jnp.int32))
    cos_s, sin_s = _rope_tables(jnp.full((nb,), PAST_LEN, jnp.int32))
    ex, sel8, ind2, ind1 = _layer_consts()
    ck_rows = cache_diff_k.reshape(DEPTH, n_pool, PAGE_SIZE * H_C, 2 * DK_C)
    cv_rows = cache_diff_v.reshape(DEPTH, n_pool, PAGE_SIZE * H_C, DV_C)
    mk_rows = cache_mem_k.reshape(DEPTH, nb, N_MEM * H_M, DH_M)
    mv_rows = cache_mem_v.reshape(DEPTH, nb, N_MEM * H_M, DH_M)
    row = lambda a: a.reshape(1, -1)
    rep64 = lambda a: jnp.repeat(a, 64).reshape(1, 512)
    outs = {n: [] for n in ('kp', 'vp', 'ks', 'vs', 'mk', 'mv', 'gla_p', 'gla_s', 'ssm_p', 'ssm_s', 'conv_p',
                            'conv_s', 'rwkv_p', 'rwkv_s', 'shift_p', 'shift_s')}
    for l in range(DEPTH):
        lam_init = 0.8 - 0.6 * math.exp(-0.3 * l)
        lq, lk = lam_q[l], lam_k[l]
        lam = jnp.exp(jnp.sum(lq[0] * lk[0])) - jnp.exp(jnp.sum(lq[1] * lk[1])) + lam_init
        lam = jnp.full((1, 128), lam, f32)
        wg2p = jnp.zeros((128, 256), f32).at[:GK_RANK].set(gla_wg2[l])
        w2a2 = jnp.zeros((128, 1024), f32).at[:64, :512].set(w2[l]).at[64:, 512:].set(a2[l])
        gq = jnp.tile(dq_norm[l].reshape(128), 4).reshape(1, 512)
        gk = jnp.tile(dk_norm[l].reshape(128), 4).reshape(1, 512)
        dtb_x, alog_x, dskip_x = rep64(dt_bias[l]), rep64(a_log[l]), rep64(d_skip[l])

        cols = _matmul(x, _pad_w_in(w_in[l]), gain=norm_mix[l])

        oa, gla_p = _gla_prompt(cols, wg2p, row(gla_bg[l]), row(gla_gn[l]), seq=SEQ)
        ob, ssm_p = _ssd_prompt(cols, conv_w[l], row(conv_b[l]), ex, dtb_x, alog_x, dskip_x, row(ssm_gn[l]), sel8,
                                seq=SEQ)
        qb, kf, kb = _diff_prep(cols, cos_p, sin_p, gq, gk, 0, SEQ, 512)
        oc = _diff_attn_prompt(qb, kb, cols, lam, row(diff_gn[l]), lam_init, seq=SEQ)
        od, rwkv_p = _rwkv_prompt(cols, row(shift_mu[l]), row(w0[l]), row(a0[l]), row(k_k[l]), row(k_a[l]),
                                  row(r_k[l]), row(lnx_g[l]), row(lnx_b[l]), w2a2, g2[l], seq=SEQ)
        mix_p = jnp.concatenate([oa, ob, oc, od], axis=1)

        pre_w = [wg2p, row(gla_bg[l]), conv_w[l], row(conv_b[l]), ex, dtb_x, alog_x,
                 row(shift_mu[l]), row(w0[l]), row(a0[l]), row(k_k[l]), row(k_a[l]), w2a2, g2[l]]
        s_gq, s_gk, s_ga, s_act, s_dec, s_xd, s_rw = _sample_pre(cols, jnp.transpose(state_conv[l], (1, 0, 2)),
                                                                 state_shift[l][:, 0], pre_w, SEQ, nb)
        col = lambda a: a.reshape(nb, -1, 1)
        tail = cols[SEQ:]
        gla_s, oa_s = _state_step(_gla_step_kernel, "gla_step", state_gla[l].reshape(nb, H_A * DK_A, DV_A),
                                  [col(s_ga), col(s_gk), col(s_gq), tail[:, C_VA:C_VA + 512].reshape(nb, 1, 512)],
                                  [(nb, 1, 512)])
        ssm_s, y_s = _state_step(_ssd_step_kernel, "ssd_step", state_ssm[l].reshape(nb, H_B * P_B, N_B),
                                 [col(s_dec), col(s_xd), s_act[:, GROUP_W:].reshape(nb, 1, 256)], [(nb, H_B * P_B, 1)])
        rwkv_s, od_s = _state_step(_rwkv_step_kernel, "rwkv_step", state_rwkv[l].reshape(nb, H_D * N_D, N_D),
                                   [jnp.transpose(s_rw, (1, 0, 2)), col(s_rw[3])], [(nb, H_D * N_D, 1)])
        qb_s, kf_s, _ = _diff_prep(cols, cos_s, sin_s, gq, gk, SEQ, nb, nb)
        vc_s = tail[:, C_VC:C_VC + 512]
        oc_s = _diff_decode(pt_flat, l, qb_s.reshape(nb, H_C, 128), kf_s.reshape(nb, H_C, 128), vc_s.reshape(nb, H_C, 128),
                            ck_rows, cv_rows, ind2, lam, row(diff_gn[l]), lam_init, n_pages)
        mix_s = _sample_post(cols, oa_s.reshape(nb, 512), y_s.reshape(nb, 512), s_act, oc_s[:, :H_C].reshape(nb, 512),
                             od_s.reshape(nb, 512), s_rw, row(gla_gn[l]), dskip_x, row(ssm_gn[l]), row(r_k[l]),
                             row(lnx_g[l]), row(lnx_b[l]), SEQ, nb)

        x = _matmul(jnp.concatenate([mix_p, mix_s], axis=0), w_out[l].astype(bf16), res=x)

        mk_p, mv_p = _mem_kv(mem_prompt[0], row(norm_memsrc[l]),
                             jnp.concatenate([w_mk[l], w_mv[l]], axis=1).astype(bf16), row(mk_norm[l]))
        qm = _matmul(x, w_mq[l].astype(bf16), gain=norm_mem[l])
        om_p = _mem_attn_prompt(qm, row(mq_norm[l]), mk_p, mv_p, seq=SEQ)
        om_s = _mem_attn_sample(qm[SEQ:].reshape(nb, H_M, DH_M), row(mq_norm[l]), mk_rows, mv_rows, ind1, l)
        x = _matmul(jnp.concatenate([om_p, om_s[:, :H_M].reshape(nb, D_MEM)], axis=0), w_mo[l].astype(bf16), res=x)

        qp, xn = _matmul(x, peer_wq[l].astype(bf16), gain=norm_ffn[l], emit_xn=True, tn=512)
        picks = _peer_select(qp, peer_keys[l].reshape(2 * H_P, N_KEYS, D_PK // 2))
        x = _peer_dense(xn, _table_bf16(peer_u, l), _table_bf16(peer_v, l), picks, x)

        outs['kp'].append(kf.reshape(1, SEQ, H_C, 2 * DK_C))
        outs['vp'].append(cols[:SEQ, C_VC:C_VC + 512].reshape(1, SEQ, H_C, DV_C))
        outs['ks'].append(kf_s.reshape(nb, 1, H_C, 2 * DK_C))
        outs['vs'].append(vc_s.reshape(nb, 1, H_C, DV_C))
        outs['mk'].append(mk_p.reshape(1, N_MEM, H_M, DH_M))
        outs['mv'].append(mv_p.reshape(1, N_MEM, H_M, DH_M))
        outs['gla_p'].append(gla_p[None])
        outs['gla_s'].append(gla_s.reshape(nb, H_A, DK_A, DV_A))
        outs['ssm_p'].append(ssm_p[None])
        outs['ssm_s'].append(ssm_s.reshape(nb, H_B, P_B, N_B))
        u_p = jnp.concatenate([cols[SEQ - 3:SEQ, C_XS:C_XS + 512], cols[SEQ - 3:SEQ, C_BC:C_BC + 256]], axis=1)
        u_s = jnp.concatenate([tail[:, C_XS:C_XS + 512], tail[:, C_BC:C_BC + 256]], axis=1)
        outs['conv_p'].append(u_p[None])
        outs['conv_s'].append(jnp.concatenate([state_conv[l][:, 1:], u_s[:, None]], axis=1))
        outs['rwkv_p'].append(rwkv_p[None])
        outs['rwkv_s'].append(rwkv_s.reshape(nb, H_D, N_D, N_D))
        outs['shift_p'].append(cols[SEQ - 1:SEQ, C_D:C_D + 1792][None])
        outs['shift_s'].append(tail[:, C_D:C_D + 1792][:, None])
    st = {n: jnp.stack(v) for n, v in outs.items()}
    return (x[:SEQ][None], x[SEQ:][:, None], st['kp'], st['vp'], st['ks'], st['vs'], st['mk'], st['mv'],
            st['gla_p'], st['gla_s'], st['ssm_p'], st['ssm_s'], st['conv_p'], st['conv_s'],
            st['rwkv_p'], st['rwkv_s'], st['shift_p'], st['shift_s'])
```

```python
import functools
import math

import numpy as np
import jax
import jax.numpy as jnp
from jax import lax
from jax.experimental import pallas as pl
from jax.experimental.pallas import tpu as pltpu

f32 = jnp.float32
bf16 = jnp.bfloat16
HI = lax.Precision.HIGHEST

D_MODEL = 2048
SEQ = 8192
DEPTH = 2
DEC_BATCH = 128
PAST_LEN = 2048
PAGE_SIZE = 128
T_ALL = SEQ + DEC_BATCH

GROUP_W = 512
H_A, DK_A, DV_A, GK_RANK, GLA_TAU = 4, 64, 128, 16, 16.0
H_B, P_B, N_B, G_B, CONV_W, XBC_W = 8, 64, 64, 2, 4, 768
H_C, DK_C, DV_C, ROPE_THETA = 4, 64, 128, 10000.0
H_D, N_D, LNX_EPS = 8, 64, 64e-5
N_MEM, H_M, D_MEM, DH_M = 256, 4, 512, 128
N_KEYS, H_P, TOPK_P, D_PK = 128, 8, 16, 256
N_EXPERTS = N_KEYS * N_KEYS

C_D = 0
C_BC = 1792
C_QKA = 2048
C_VA = 2560
C_GA = 3072
C_Z = 3584
C_XS = 4096
C_QC = 4608
C_KC = 5120
C_VC = 5632
C_TAIL = 6144
IN_PAD = 6272

LANES = 128
VMEM_LIMIT = 56 * 1024 * 1024

NN = ((1,), (0,))
NT = ((1,), (1,))
TN = ((0,), (0,))


def _dot(a, b, dims=NN, hi=False):
    if hi:
        return lax.dot_general(a, b, (dims, ((), ())), precision=HI, preferred_element_type=f32)
    return lax.dot_general(a.astype(bf16), b.astype(bf16), (dims, ((), ())), preferred_element_type=f32)


def _softplus(x):
    return jnp.maximum(x, 0.0) + jnp.log(1.0 + jnp.exp(-jnp.abs(x)))


def _sigmoid(x):
    return 1.0 / (1.0 + jnp.exp(-x))


def _silu(x):
    return x * _sigmoid(x)


def _iota(shape, axis):
    return lax.broadcasted_iota(jnp.int32, shape, axis)


def _params(*sem):
    return pltpu.CompilerParams(dimension_semantics=sem, vmem_limit_bytes=VMEM_LIMIT)


def _mm_kernel(*refs, norm, residual, emit_xn):
    x_ref, g_ref, w_ref = refs[:3]
    res_ref = refs[3] if residual else None
    xn_ref = refs[-1]
    o_ref = refs[-3] if emit_xn else refs[-2]

    @pl.when(pl.program_id(1) == 0)
    def _():
        x = x_ref[...]
        if norm:
            x = x * lax.rsqrt(jnp.mean(x * x, axis=-1, keepdims=True) + 1e-6) * g_ref[...]
        xn_ref[...] = x.astype(bf16)
        if emit_xn:
            refs[-2][...] = xn_ref[...]

    acc = jnp.dot(xn_ref[...], w_ref[...], preferred_element_type=f32)
    if residual:
        acc = acc + res_ref[...]
    o_ref[...] = acc


def _matmul(x, w, gain=None, res=None, emit_xn=False, tm=None, tn=None):
    m, k = x.shape
    n = w.shape[1]
    tm = tm or _pick(m, (1040, 1024, 512, 256, 128))
    tn = tn or _pick(n, (1024, 896, 512, 256, 128))
    norm = gain is not None
    g = (gain if norm else jnp.ones((k,), f32)).reshape(1, k)
    args = [x, g, w]
    in_specs = [pl.BlockSpec((tm, k), lambda i, j: (i, 0)),
                pl.BlockSpec((1, k), lambda i, j: (0, 0)),
                pl.BlockSpec((k, tn), lambda i, j: (0, j))]
    if res is not None:
        args.append(res)
        in_specs.append(pl.BlockSpec((tm, tn), lambda i, j: (i, j)))
    out_specs = [pl.BlockSpec((tm, tn), lambda i, j: (i, j))]
    out_shape = [jax.ShapeDtypeStruct((m, n), f32)]
    if emit_xn:
        out_specs.append(pl.BlockSpec((tm, k), lambda i, j: (i, 0)))
        out_shape.append(jax.ShapeDtypeStruct((m, k), bf16))
    out = pl.pallas_call(
        functools.partial(_mm_kernel, norm=norm, residual=res is not None, emit_xn=emit_xn),
        grid=(m // tm, n // tn),
        in_specs=in_specs,
        out_specs=out_specs,
        out_shape=out_shape,
        scratch_shapes=[pltpu.VMEM((tm, k), bf16)],
        compiler_params=_params("parallel", "arbitrary"),
        name="mm",
    )(*args)
    return out if emit_xn else out[0]


def _pick(n, cands):
    for c in cands:
        if n % c == 0:
            return c
    return n


def _const(shape):
    nd = len(shape)
    return pl.BlockSpec(shape, lambda i, _n=nd: (0,) * _n)


def _cols(width, start, tb):
    assert start % width == 0
    return pl.BlockSpec((tb, width), lambda i, _c=start // width: (i, _c))


def _blockdiag_tri(tb, c):
    r = _iota((tb, tb), 0)
    s = _iota((tb, tb), 1)
    return jnp.where((r // c == s // c) & (s <= r), 1.0, 0.0).astype(f32)


def _segment_ones(n, seg):
    r = _iota((n, n), 0)
    s = _iota((n, n), 1)
    return jnp.where(r // seg == s // seg, 1.0, 0.0).astype(f32)


GLA_TB = 128
GLA_C = 16


def _gla_gate_log(tail, wg2p, bg):
    z = _dot(tail, wg2p, hi=True) + bg
    return -_softplus(-z) * (1.0 / GLA_TAU)


def _gla_out(o, g, gn):
    outs = []
    for h in range(H_A):
        oh = o[:, h * DV_A:(h + 1) * DV_A]
        oh = oh * lax.rsqrt(jnp.mean(oh * oh, axis=-1, keepdims=True) + 1e-6) * gn
        outs.append(oh * _silu(g[:, h * DV_A:(h + 1) * DV_A]))
    return jnp.concatenate(outs, axis=-1)


def _gla_kernel(qk_ref, v_ref, g_ref, tail_ref, wg2_ref, bg_ref, gn_ref, o_ref, sfin_ref, st_ref, w_ref):
    i = pl.program_id(0)

    @pl.when(i == 0)
    def _():
        st_ref[...] = jnp.zeros_like(st_ref)

    tb, c = GLA_TB, GLA_C
    qk = qk_ref[...]
    q = qk[:, :256] * (DK_A ** -0.5)
    k = qk[:, 256:]
    v = v_ref[...]
    la = _gla_gate_log(tail_ref[...], wg2_ref[...], bg_ref[...])
    b = _dot(_blockdiag_tri(tb, c), la, hi=True)
    e_r = _iota((256, 512), 0) // DK_A
    e_c = _iota((256, 512), 1) // DV_A
    expand = jnp.where(e_r == e_c, 1.0, 0.0).astype(bf16)
    s_idx = _iota((c, 256), 0)
    for j in range(tb // c):
        r0 = j * c
        qj, kj, bj, vj = q[r0:r0 + c], k[r0:r0 + c], b[r0:r0 + c], v[r0:r0 + c]
        for t in range(c):
            wt = qj[t:t + 1] * kj * jnp.exp(bj[t:t + 1] - bj)
            w_ref[t * c:(t + 1) * c, :] = jnp.where(s_idx <= t, wt, 0.0)
        att = _dot(w_ref[...], expand)
        o = jnp.sum(att.reshape(c, c, 512) * vj[None], axis=1)
        qe = qj * jnp.exp(bj)
        bl = bj[c - 1:c]
        ke = kj * jnp.exp(bl - bj)
        dl = jnp.exp(bl)
        inter = []
        for h in range(H_A):
            ks = slice(h * DK_A, (h + 1) * DK_A)
            st = st_ref[h]
            inter.append(_dot(qe[:, ks], st, NT))
            st_ref[h] = st * dl[:, ks] + _dot(vj[:, h * DV_A:(h + 1) * DV_A], ke[:, ks], TN)
        o = o + jnp.concatenate(inter, axis=-1)
        o_ref[r0:r0 + c, :] = _gla_out(o, g_ref[r0:r0 + c, :], gn_ref[...])

    @pl.when(i == pl.num_programs(0) - 1)
    def _():
        for h in range(H_A):
            sfin_ref[h] = st_ref[h].T


def _gla_prompt(cols, wg2p, bg, gn, seq=SEQ):
    tb = GLA_TB
    return pl.pallas_call(
        _gla_kernel,
        grid=(seq // tb,),
        in_specs=[_cols(512, C_QKA, tb), _cols(512, C_VA, tb), _cols(512, C_GA, tb), _cols(128, C_TAIL, tb),
                  _const((128, 256)), _const((1, 256)), _const((1, DV_A))],
        out_specs=[pl.BlockSpec((tb, 512), lambda i: (i, 0)), _const((H_A, DK_A, DV_A))],
        out_shape=[jax.ShapeDtypeStruct((seq, 512), f32), jax.ShapeDtypeStruct((H_A, DK_A, DV_A), f32)],
        scratch_shapes=[pltpu.VMEM((H_A, DV_A, DK_A), f32), pltpu.VMEM((GLA_C * GLA_C, 256), f32)],
        compiler_params=_params("arbitrary"),
        name="gla_prompt",
    )(cols, cols, cols, cols, wg2p, bg, gn)


SSD_TB = 128
SSD_C = 64


def _ssd_conv(ext, conv_w, conv_b, rows):
    out = conv_b
    for j in range(CONV_W):
        shifted = pltpu.roll(ext, j, 0) if j else ext
        out = out + shifted[8:8 + rows] * conv_w[CONV_W - 1 - j:CONV_W - j]
    return out


def _ssd_dt(tail, ex, dtb_x):
    return _softplus(_dot(tail, ex, hi=True) + dtb_x)


def _ssd_out(y, z, gn):
    y = y * _silu(z)
    w = GROUP_W // G_B
    outs = []
    for g in range(G_B):
        yg = y[:, g * w:(g + 1) * w]
        outs.append(yg * lax.rsqrt(jnp.mean(yg * yg, axis=-1, keepdims=True) + 1e-6) * gn[:, g * w:(g + 1) * w])
    return jnp.concatenate(outs, axis=-1)


def _ssd_kernel(z_ref, xs_ref, bc_ref, tail_ref, cw_ref, cb_ref, ex_ref, dtb_ref, alog_ref, dskip_ref, gn_ref,
                sel_ref, o_ref, hfin_ref, carry_ref, h_ref, y_ref):
    i = pl.program_id(0)

    @pl.when(i == 0)
    def _():
        carry_ref[...] = jnp.zeros_like(carry_ref)
        h_ref[...] = jnp.zeros_like(h_ref)

    tb, c = SSD_TB, SSD_C
    u = jnp.concatenate([xs_ref[...], bc_ref[...]], axis=-1)
    ext = jnp.concatenate([carry_ref[...], u], axis=0)
    carry_ref[...] = u[tb - 8:tb]
    act = _silu(_ssd_conv(ext, cw_ref[...], cb_ref[...], tb))
    xs = act[:, :GROUP_W]
    dtx = _ssd_dt(tail_ref[...], ex_ref[...], dtb_ref[...])
    la = dtx * (-jnp.exp(alog_ref[...]))
    b = _dot(_blockdiag_tri(tb, c), la, hi=True)
    brow = _dot(sel_ref[...], b, NT, hi=True)
    xd = xs * dtx
    tri = _iota((c, c), 1) <= _iota((c, c), 0)
    for ch in range(tb // c):
        r0 = ch * c
        rows = slice(r0, r0 + c)
        scores = []
        for g in range(G_B):
            bm = act[rows, GROUP_W + g * N_B:GROUP_W + (g + 1) * N_B]
            cm = act[rows, GROUP_W + G_B * N_B + g * N_B:GROUP_W + G_B * N_B + (g + 1) * N_B]
            scores.append((_dot(cm, bm, NT), bm, cm))
        for h in range(H_B):
            hs = slice(h * P_B, (h + 1) * P_B)
            sc, bm, cm = scores[h // (H_B // G_B)]
            bh = b[rows, hs]
            dec = jnp.where(tri, jnp.exp(bh - brow[h:h + 1, r0:r0 + c]), 0.0)
            xdh = xd[rows, hs]
            hst = h_ref[h]
            y = _dot(sc * dec, xdh) + _dot(cm, hst, NT) * jnp.exp(bh)
            bl = bh[c - 1:c]
            h_ref[h] = hst * jnp.exp(bl) + _dot(xdh * jnp.exp(bl - bh), bm, TN)
            y_ref[rows, hs] = y + xs[rows, hs] * dskip_ref[:, hs]
    o_ref[...] = _ssd_out(y_ref[...], z_ref[...], gn_ref[...])

    @pl.when(i == pl.num_programs(0) - 1)
    def _():
        hfin_ref[...] = h_ref[...]


def _ssd_prompt(cols, cw, cb, ex, dtb_x, alog_x, dskip_x, gn, sel, seq=SEQ):
    tb = SSD_TB
    return pl.pallas_call(
        _ssd_kernel,
        grid=(seq // tb,),
        in_specs=[_cols(512, C_Z, tb), _cols(512, C_XS, tb), _cols(256, C_BC, tb), _cols(128, C_TAIL, tb),
                  _const((CONV_W, XBC_W)), _const((1, XBC_W)), _const((128, 512)), _const((1, 512)), _const((1, 512)),
                  _const((1, 512)), _const((1, 512)), _const((8, 512))],
        out_specs=[pl.BlockSpec((tb, 512), lambda i: (i, 0)), _const((H_B, P_B, N_B))],
        out_shape=[jax.ShapeDtypeStruct((seq, 512), f32), jax.ShapeDtypeStruct((H_B, P_B, N_B), f32)],
        scratch_shapes=[pltpu.VMEM((8, XBC_W), f32), pltpu.VMEM((H_B, P_B, N_B), f32), pltpu.VMEM((tb, 512), f32)],
        compiler_params=_params("arbitrary"),
        name="ssd_prompt",
    )(cols, cols, cols, cols, cw, cb, ex, dtb_x, alog_x, dskip_x, gn, sel)


RWKV_TB = 128
RWKV_C = 64


def _rwkv_pre(x, xprev, mu, w0, a0, k_k, k_a, w2a2, g2):
    mixed = x + (xprev - x) * mu
    r = mixed[:, :512]
    kd = mixed[:, 512:1024]
    v = mixed[:, 1024:1536]
    lw = mixed[:, 1536:1664]
    lin = jnp.where(_iota(lw.shape, 1) < 64, jnp.tanh(lw), lw)
    wa = _dot(lin, w2a2, hi=True)
    w = -_softplus(-(w0 + wa[:, :512])) - 0.5
    a = _sigmoid(a0 + wa[:, 512:])
    g = _dot(_sigmoid(mixed[:, 1664:1792]), g2)
    kk = kd * k_k
    ss = _dot(kk * kk, _segment_ones(512, N_D), hi=True)
    kk = kk * lax.rsqrt(jnp.maximum(ss, 1e-24))
    kd = kd * (1.0 + (a - 1.0) * k_a)
    return r, w, kd, v, -kk, kk * a, g


def _rwkv_post(o, r, kd, v, g, r_k, lnx_g, lnx_b):
    seg = _segment_ones(512, N_D)
    mu = _dot(o, seg, hi=True) * (1.0 / N_D)
    d = o - mu
    var = _dot(d * d, seg, hi=True) * (1.0 / N_D)
    o = d * lax.rsqrt(var + LNX_EPS) * lnx_g + lnx_b
    o = o + _dot(r * kd * r_k, seg, hi=True) * v
    return o * g


def _rwkv_kernel(d_ref, mu_ref, w0_ref, a0_ref, kk_ref, ka_ref, rk_ref, lg_ref, lb_ref, w2a2_ref, g2_ref,
                 o_ref, sfin_ref, prev_ref, s_ref, oacc_ref):
    i = pl.program_id(0)

    @pl.when(i == 0)
    def _():
        prev_ref[...] = jnp.zeros_like(prev_ref)
        s_ref[...] = jnp.zeros_like(s_ref)

    tb, c = RWKV_TB, RWKV_C
    x = d_ref[...]
    xprev = jnp.where(_iota(x.shape, 0) == 0, prev_ref[0:1, :], pltpu.roll(x, 1, 0))
    prev_ref[0:1, :] = x[tb - 1:tb]
    r, w, kd, v, alpha, beta, g = _rwkv_pre(x, xprev, mu_ref[...], w0_ref[...], a0_ref[...], kk_ref[...], ka_ref[...],
                                            w2a2_ref[...], g2_ref[...])
    ld = -jnp.exp(w)
    cum = _dot(_blockdiag_tri(tb, c), ld, hi=True)
    at = alpha * jnp.exp(cum - ld)
    rt = r * jnp.exp(cum)
    einv = jnp.exp(-cum)
    kt = kd * einv
    bt = beta * einv
    ri = _iota((c, c), 0)
    ci = _iota((c, c), 1)
    strict, incl = ci < ri, ci <= ri
    eye = jnp.where(ri == ci, 1.0, 0.0).astype(f32)
    pairs = [(ch, h) for ch in range(tb // c) for h in range(H_D)]
    rows_of = lambda ch: slice(ch * c, (ch + 1) * c)
    lanes_of = lambda h: slice(h * N_D, (h + 1) * N_D)
    ar, kb, lk, mkb, tinv, p = {}, {}, {}, {}, {}, {}
    for ch, h in pairs:
        rows, hs = rows_of(ch), lanes_of(h)
        ar[ch, h] = jnp.concatenate([at[rows, hs], rt[rows, hs]], axis=0)
        kb[ch, h] = jnp.concatenate([kt[rows, hs], bt[rows, hs]], axis=0)
    for key in pairs:
        gram = _dot(ar[key], kb[key], NT)
        lk[key] = jnp.where(strict, gram[:c, :c], 0.0)
        p[key] = jnp.where(strict, gram[:c, c:], 0.0)
        mkb[key] = jnp.concatenate([jnp.where(incl, gram[c:, :c], 0.0), jnp.where(incl, gram[c:, c:], 0.0)], axis=1)
        tinv[key] = eye + p[key]
    for _ in range(5):
        for key in pairs:
            p[key] = _dot(p[key], p[key])
        for key in pairs:
            tinv[key] = tinv[key] + _dot(tinv[key], p[key])
    for ch in range(tb // c):
        rows = rows_of(ch)
        cl = cum[ch * c + c - 1:ch * c + c]
        efin = jnp.exp(cl - cum[rows])
        kfin = kd[rows] * efin
        bfin = beta[rows] * efin
        dfin = jnp.exp(cl)
        heads = range(H_D)
        s0 = [s_ref[h] for h in heads]
        ars = [_dot(ar[ch, h], s0[h], NT) for h in heads]
        lkv = [_dot(lk[ch, h], v[rows, lanes_of(h)]) for h in heads]
        u = [_dot(tinv[ch, h], ars[h][:c] + lkv[h]) for h in heads]
        vu = [jnp.concatenate([v[rows, lanes_of(h)], u[h]], axis=0) for h in heads]
        for h in heads:
            hs = lanes_of(h)
            oacc_ref[rows, hs] = ars[h][c:] + _dot(mkb[ch, h], vu[h])
            kbfin = jnp.concatenate([kfin[:, hs], bfin[:, hs]], axis=0)
            s_ref[h] = s0[h] * dfin[:, hs] + _dot(vu[h], kbfin, TN)
    o_ref[...] = _rwkv_post(oacc_ref[...], r, kd, v, g, rk_ref[...], lg_ref[...], lb_ref[...])

    @pl.when(i == pl.num_programs(0) - 1)
    def _():
        sfin_ref[...] = s_ref[...]


def _rwkv_prompt(cols, mu, w0, a0, k_k, k_a, r_k, lnx_g, lnx_b, w2a2, g2, seq=SEQ):
    tb = RWKV_TB
    vec = _const((1, 512))
    return pl.pallas_call(
        _rwkv_kernel,
        grid=(seq // tb,),
        in_specs=[_cols(1792, C_D, tb), _const((1, 1792)), vec, vec, vec, vec, vec, vec, vec,
                  _const((128, 1024)), _const((128, 512))],
        out_specs=[pl.BlockSpec((tb, 512), lambda i: (i, 0)), _const((H_D, N_D, N_D))],
        out_shape=[jax.ShapeDtypeStruct((seq, 512), f32), jax.ShapeDtypeStruct((H_D, N_D, N_D), f32)],
        scratch_shapes=[pltpu.VMEM((8, 1792), f32), pltpu.VMEM((H_D, N_D, N_D), f32), pltpu.VMEM((tb, 512), f32)],
        compiler_params=_params("arbitrary"),
        name="rwkv_prompt",
    )(cols, mu, w0, a0, k_k, k_a, r_k, lnx_g, lnx_b, w2a2, g2)


def _rope_tables(pos):
    half = DK_C // 2
    inv = ROPE_THETA ** (-jnp.arange(half, dtype=f32) / half)
    ang = pos.astype(f32)[:, None] * inv[None, :]
    cos, sin = jnp.cos(ang), jnp.sin(ang)
    return jnp.tile(jnp.concatenate([cos, cos], axis=-1), (1, 2)), jnp.tile(jnp.concatenate([-sin, sin], axis=-1), (1, 2))


def _qk_norm_rope(x, gain, cos, sin):
    ms = _dot(x * x, _segment_ones(512, DK_C), hi=True) * (1.0 / DK_C)
    x = x * lax.rsqrt(ms + 1e-6) * gain
    first = (_iota(x.shape, 1) % DK_C) < (DK_C // 2)
    partner = jnp.where(first, pltpu.roll(x, 512 - DK_C // 2, 1), pltpu.roll(x, DK_C // 2, 1))
    cos = jnp.concatenate([cos] * 4, axis=-1)
    sin = jnp.concatenate([sin] * 4, axis=-1)
    return x * cos + partner * sin


def _diff_prep_kernel(q_ref, k_ref, cos_ref, sin_ref, gq_ref, gk_ref, qb_ref, kf_ref, kb_ref):
    cos, sin = cos_ref[...], sin_ref[...]
    q = _qk_norm_rope(q_ref[...], gq_ref[...], cos, sin)
    k = _qk_norm_rope(k_ref[...], gk_ref[...], cos, sin)
    qb_ref[...] = (q * (DK_C ** -0.5)).astype(bf16)
    kf_ref[...] = k
    kb_ref[...] = k.astype(bf16)


def _diff_prep(cols, cos, sin, gq, gk, row0, rows, tb):
    assert row0 % tb == 0
    r0 = row0 // tb
    colspec = lambda start: pl.BlockSpec((tb, 512), lambda i, _c=start // 512: (i + r0, _c))
    out = pl.BlockSpec((tb, 512), lambda i: (i, 0))
    tab = pl.BlockSpec((tb, 128), lambda i: (i, 0))
    return pl.pallas_call(
        _diff_prep_kernel,
        grid=(rows // tb,),
        in_specs=[colspec(C_QC), colspec(C_KC), tab, tab, _const((1, 512)), _const((1, 512))],
        out_specs=[out, out, out],
        out_shape=[jax.ShapeDtypeStruct((rows, 512), bf16), jax.ShapeDtypeStruct((rows, 512), f32),
                   jax.ShapeDtypeStruct((rows, 512), bf16)],
        compiler_params=_params("parallel"),
        name="diff_prep",
    )(cols, cols, cos, sin, gq, gk)


def _diff_finish(o1, o2, lam, gn, lam_init):
    o = o1 - lam * o2
    return o * lax.rsqrt(jnp.mean(o * o, axis=-1, keepdims=True) + 1e-6) * gn * (1.0 - lam_init)


def _flash_kernel(q_ref, k_ref, v_ref, lam_ref, gn_ref, o_ref, m_ref, l_ref, acc_ref, *, tq, tk, lam_init):
    qi, kj = pl.program_id(1), pl.program_id(2)

    @pl.when(kj == 0)
    def _():
        m_ref[...] = jnp.full_like(m_ref, -jnp.inf)
        l_ref[...] = jnp.zeros_like(l_ref)
        acc_ref[...] = jnp.zeros_like(acc_ref)

    def step(masked):
        q = q_ref[...]
        k = k_ref[...]
        v = v_ref[...].astype(bf16)
        lane = _iota(q.shape, 1)
        if masked:
            keep = (kj * tk + _iota((tk, tq), 0)) <= (qi * tq + _iota((tk, tq), 1))
        for m in range(2):
            qm = jnp.where((lane < DK_C) if m == 0 else (lane >= DK_C), q, jnp.zeros_like(q))
            s = _dot(k, qm, NT)
            if masked:
                s = jnp.where(keep, s, -jnp.inf)
            m_old = m_ref[m]
            m_new = jnp.maximum(m_old, jnp.max(s, axis=0, keepdims=True))
            p = jnp.exp(s - m_new)
            corr = jnp.exp(m_old - m_new)
            l_ref[m] = corr * l_ref[m] + jnp.sum(p, axis=0, keepdims=True)
            acc_ref[m] = corr * acc_ref[m] + _dot(v, p, TN)
            m_ref[m] = m_new

    first, last = kj * tk, kj * tk + tk - 1

    @pl.when(last <= qi * tq)
    def _():
        step(False)

    @pl.when((first <= qi * tq + tq - 1) & (last > qi * tq))
    def _():
        step(True)

    @pl.when(kj == pl.num_programs(2) - 1)
    def _():
        o1 = (acc_ref[0] / l_ref[0]).T
        o2 = (acc_ref[1] / l_ref[1]).T
        o_ref[...] = _diff_finish(o1, o2, lam_ref[...], gn_ref[...], lam_init)


def _diff_attn_prompt(qb, kb, cols, lam, gn, lam_init, seq=SEQ, tq=512, tk=512):
    nq, nk = seq // tq, seq // tk
    last = lambda i: (i * tq + tq - 1) // tk
    kmap = lambda h, i, j: (jnp.minimum(j, last(i)), h)
    vmap_ = lambda h, i, j: (jnp.minimum(j, last(i)), C_VC // DV_C + h)
    return pl.pallas_call(
        functools.partial(_flash_kernel, tq=tq, tk=tk, lam_init=lam_init),
        grid=(H_C, nq, nk),
        in_specs=[pl.BlockSpec((tq, 128), lambda h, i, j: (i, h)), pl.BlockSpec((tk, 128), kmap),
                  pl.BlockSpec((tk, DV_C), vmap_), pl.BlockSpec((1, 128), lambda h, i, j: (0, 0)),
                  pl.BlockSpec((1, DV_C), lambda h, i, j: (0, 0))],
        out_specs=pl.BlockSpec((tq, DV_C), lambda h, i, j: (i, h)),
        out_shape=jax.ShapeDtypeStruct((seq, 512), f32),
        scratch_shapes=[pltpu.VMEM((2, 1, tq), f32), pltpu.VMEM((2, 1, tq), f32), pltpu.VMEM((2, DV_C, tq), f32)],
        compiler_params=_params("parallel", "parallel", "arbitrary"),
        name="diff_attn_prompt",
    )(qb, kb, cols, lam, gn)


PEER_SEL_TB = 128
PEER_TB = 640
PEER_EB = 512
PEER_SUB = 256
_PAIRS = [(a, b) for a in range(TOPK_P) for b in range(TOPK_P) if (a + 1) * (b + 1) <= TOPK_P]
_NPAIR = -(-len(_PAIRS) // 8) * 8


def _top16_ranks(s):
    n_idx = _iota(s.shape, 0)
    rank = jnp.full(s.shape, float(TOPK_P), f32)
    tops = []
    work = s
    for k in range(TOPK_P):
        m = jnp.max(work, axis=0, keepdims=True)
        idx = jnp.min(jnp.where(work == m, n_idx, N_KEYS), axis=0, keepdims=True)
        hit = n_idx == idx
        rank = jnp.where(hit, float(k), rank)
        work = jnp.where(hit, -jnp.inf, work)
        tops.append(m)
    return rank, jnp.concatenate(tops, axis=0)


def _peer_select_kernel(q_ref, keys_ref, pk1_ref, flat_ref, ci_ref, cnt_ref, e2_ref, r2_ref):
    tb = q_ref.shape[0]
    flat = flat_ref[...]
    for h in range(H_P):
        ranks, tops, scores = [], [], []
        for x in range(2):
            hx = 2 * h + x
            s = _dot(keys_ref[hx], q_ref[:, hx * 128:(hx + 1) * 128], NT, hi=True)
            rk, tp = _top16_ranks(s)
            ranks.append(rk), tops.append(tp), scores.append(s)
        t1, t2 = tops
        cand = jnp.concatenate([t1[a:a + 1] + t2[b:b + 1] for a, b in _PAIRS]
                               + [jnp.full((_NPAIR - len(_PAIRS), tb), -jnp.inf, f32)], axis=0)
        work = cand
        sel = jnp.zeros(cand.shape, f32)
        for _ in range(TOPK_P):
            m = jnp.max(work, axis=0, keepdims=True)
            idx = jnp.min(jnp.where(work == m, flat, 4096.0), axis=0, keepdims=True)
            hit = flat == idx
            sel = jnp.where(hit, 1.0, sel)
            work = jnp.where(hit, -jnp.inf, work)
        top = t1[0:1] + t2[0:1]
        z = jnp.sum(sel * jnp.exp(jnp.where(sel > 0, cand - top, 0.0)), axis=0, keepdims=True)
        cnt = _dot(pk1_ref[...], sel)
        cnt_i = jnp.zeros((N_KEYS, tb), f32)
        for k1 in range(TOPK_P):
            cnt_i = cnt_i + jnp.where(ranks[0] == float(k1), cnt[k1:k1 + 1], 0.0)
        ci_ref[0, h] = jnp.exp(scores[0] - t1[0:1]) / z
        cnt_ref[0, h] = cnt_i
        e2_ref[0, h] = jnp.exp(scores[1] - t2[0:1]).astype(bf16)
        r2_ref[0, h] = ranks[1].astype(bf16)


def _peer_select(q, keys):
    t = q.shape[0]
    tb, per = PEER_SEL_TB, PEER_TB // PEER_SEL_TB
    pk1 = np.zeros((TOPK_P, _NPAIR), np.float32)
    flat = np.full((_NPAIR, 1), 8192.0, np.float32)
    for r, (a, b) in enumerate(_PAIRS):
        pk1[a, r] = 1.0
        flat[r, 0] = a * TOPK_P + b
    out = pl.BlockSpec((1, H_P, N_KEYS, tb), lambda i: (i // per, 0, 0, i % per))
    shp = lambda dt: jax.ShapeDtypeStruct((t // PEER_TB, H_P, N_KEYS, PEER_TB), dt)
    return pl.pallas_call(
        _peer_select_kernel,
        grid=(t // tb,),
        in_specs=[pl.BlockSpec((tb, 2048), lambda i: (i, 0)), _const((2 * H_P, N_KEYS, 128)),
                  _const((TOPK_P, _NPAIR)), _const((_NPAIR, 1))],
        out_specs=[out, out, out, out],
        out_shape=[shp(f32), shp(f32), shp(bf16), shp(bf16)],
        compiler_params=_params("parallel"),
        name="peer_select",
    )(q, keys, jnp.asarray(pk1), jnp.asarray(flat))


def _sample_pre_kernel(qk_ref, tail_ref, xs_ref, bc_ref, d_ref, conv_ref, shift_ref,
                       wg2_ref, bg_ref, cw_ref, cb_ref, ex_ref, dtb_ref, alog_ref,
                       mu_ref, w0_ref, a0_ref, kk_ref, ka_ref, w2a2_ref, g2_ref,
                       gq_ref, gk_ref, ga_ref, act_ref, sdec_ref, sxd_ref, rw_ref):
    qk = qk_ref[...]
    la = _gla_gate_log(tail_ref[...], wg2_ref[...], bg_ref[...])
    gq_ref[...] = qk[:, :256] * (DK_A ** -0.5)
    gk_ref[...] = qk[:, 256:]
    ga_ref[...] = jnp.exp(la)
    u = jnp.concatenate([xs_ref[...], bc_ref[...]], axis=-1)
    cw = cw_ref[...]
    conv = cb_ref[...] + u * cw[CONV_W - 1:CONV_W]
    for j in range(CONV_W - 1):
        conv = conv + conv_ref[j] * cw[j:j + 1]
    act = _silu(conv)
    act_ref[...] = act
    dtx = _ssd_dt(tail_ref[...], ex_ref[...], dtb_ref[...])
    sdec_ref[...] = jnp.exp(dtx * (-jnp.exp(alog_ref[...])))
    sxd_ref[...] = act[:, :GROUP_W] * dtx
    r, w, kd, v, alpha, beta, g = _rwkv_pre(d_ref[...], shift_ref[...], mu_ref[...], w0_ref[...], a0_ref[...],
                                            kk_ref[...], ka_ref[...], w2a2_ref[...], g2_ref[...])
    for n, t in enumerate((r, jnp.exp(-jnp.exp(w)), kd, v, alpha, beta, g)):
        rw_ref[n] = t


def _sample_pre(cols, conv_st, shift_st, wts, row0, b):
    tb = b
    assert row0 % tb == 0
    r0 = row0 // tb
    cs = lambda w, start: pl.BlockSpec((tb, w), lambda i, _c=start // w: (r0, _c))
    full = lambda *s: jax.ShapeDtypeStruct(s, f32)
    return pl.pallas_call(
        _sample_pre_kernel,
        grid=(1,),
        in_specs=[cs(512, C_QKA), cs(128, C_TAIL), cs(512, C_XS), cs(256, C_BC), cs(1792, C_D),
                  _const((CONV_W - 1, b, XBC_W)), _const((b, 1792))] + [_const(w.shape) for w in wts],
        out_specs=[_const((b, 256))] * 3 + [_const((b, XBC_W)), _const((b, 512)), _const((b, 512)), _const((7, b, 512))],
        out_shape=[full(b, 256)] * 3 + [full(b, XBC_W), full(b, 512), full(b, 512), full(7, b, 512)],
        compiler_params=_params("arbitrary"),
        name="sample_pre",
    )(cols, cols, cols, cols, cols, conv_st, shift_st, *wts)


def _rows_to_tile(row, heads, width, reps):
    return jnp.concatenate([jnp.broadcast_to(row[:, h * width:(h + 1) * width], (reps, width)) for h in range(heads)], axis=0)


STEP_BPB = 8


def _gla_step_kernel(s_ref, a_ref, k_ref, q_ref, v_ref, sn_ref, o_ref):
    for j in range(s_ref.shape[0]):
        s = a_ref[j] * s_ref[j] + k_ref[j] * _rows_to_tile(v_ref[j], H_A, DV_A, DK_A)
        sn_ref[j] = s
        qs = q_ref[j] * s
        o_ref[j] = jnp.concatenate([jnp.sum(qs[h * DK_A:(h + 1) * DK_A], axis=0, keepdims=True) for h in range(H_A)],
                                   axis=-1)


def _ssd_step_kernel(h_ref, dec_ref, xd_ref, bc_ref, hn_ref, y_ref):
    reps = (H_B // G_B) * P_B
    for j in range(h_ref.shape[0]):
        bc = bc_ref[j]
        hn = dec_ref[j] * h_ref[j] + xd_ref[j] * _rows_to_tile(bc[:, :G_B * N_B], G_B, N_B, reps)
        hn_ref[j] = hn
        y_ref[j] = jnp.sum(hn * _rows_to_tile(bc[:, G_B * N_B:], G_B, N_B, reps), axis=-1, keepdims=True)


def _rwkv_step_kernel(s_ref, rows_ref, v_ref, sn_ref, o_ref):
    for j in range(s_ref.shape[0]):
        tile = lambda n, _j=j: _rows_to_tile(rows_ref[_j, n:n + 1, :], H_D, N_D, N_D)
        s = s_ref[j]
        sa = jnp.sum(s * tile(4), axis=-1, keepdims=True)
        s = s * tile(1) + sa * tile(5) + v_ref[j] * tile(2)
        sn_ref[j] = s
        o_ref[j] = jnp.sum(s * tile(0), axis=-1, keepdims=True)


def _state_step(kernel, name, state, ins, outs):
    b = state.shape[0]
    spec = lambda shp: pl.BlockSpec((STEP_BPB,) + tuple(shp[1:]), lambda i: (i, 0, 0))
    return pl.pallas_call(
        kernel,
        grid=(b // STEP_BPB,),
        in_specs=[spec(state.shape)] + [spec(a.shape) for a in ins],
        out_specs=[spec(state.shape)] + [spec(s) for s in outs],
        out_shape=[jax.ShapeDtypeStruct(state.shape, f32)] + [jax.ShapeDtypeStruct(s, f32) for s in outs],
        compiler_params=_params("parallel"),
        name=name,
    )(state, *ins)


def _rowhead_attend(kx, vx, q4, ind, n_maps, tail_rows=None):
    r = kx.shape[0]
    g = r // 8
    q8 = jnp.concatenate([q4, q4], axis=0)
    qt = jnp.broadcast_to(q8[None], (g, 8, 128)).reshape(r, 128)
    s3 = _dot(kx * qt, ind).reshape(g, 8, 128 * n_maps)
    if tail_rows is not None:
        last = jnp.where(_iota((1, 8, 128 * n_maps), 1) < tail_rows, s3[g - 1:g], -jnp.inf)
        s3 = jnp.concatenate([s3[:g - 1], last], axis=0)
    m8 = jnp.max(s3, axis=0)
    mh = jnp.maximum(m8, pltpu.roll(m8, 4, 0))
    p3 = jnp.exp(s3 - mh[None])
    l8 = jnp.sum(p3, axis=0)
    lh = l8 + pltpu.roll(l8, 4, 0)
    v3 = vx.reshape(g, 8, 128)
    outs = []
    for m in range(n_maps):
        ms = slice(m * 128, (m + 1) * 128)
        pv = jnp.sum(p3[:, :, ms] * v3, axis=0)
        pv = pv + pltpu.roll(pv, 4, 0)
        outs.append(pv / lh[:, ms])
    return outs


def _diff_decode_kernel(pt_ref, q_ref, ks_ref, vs_ref, *rest, lam_init, n_pages):
    del pt_ref
    k_refs, v_refs = rest[:n_pages], rest[n_pages:2 * n_pages]
    ind_ref, lam_ref, gn_ref, o_ref = rest[2 * n_pages:]
    own = lambda ref: jnp.concatenate([ref[0], ref[0]], axis=0)
    kx = jnp.concatenate([r[...] for r in k_refs] + [own(ks_ref)], axis=0)
    vx = jnp.concatenate([r[...] for r in v_refs] + [own(vs_ref)], axis=0)
    o1, o2 = _rowhead_attend(kx, vx, q_ref[0].astype(f32), ind_ref[...], 2, tail_rows=H_C)
    o_ref[0] = _diff_finish(o1, o2, lam_ref[...], gn_ref[...], lam_init)


def _diff_decode(pt_flat, layer, qb, ks, vs, ck, cv, ind, lam, gn, lam_init, n_pages):
    b = qb.shape[0]
    rows = ck.shape[2]
    row = pl.BlockSpec((1, H_C, 128), lambda i, pt: (i, 0, 0))
    page = lambda j: pl.BlockSpec((None, None, rows, 128), lambda i, pt, _j=j: (layer, pt[i * n_pages + _j], 0, 0))
    cst = lambda shp: pl.BlockSpec(shp, lambda i, pt: (0, 0))
    pages = [page(j) for j in range(n_pages)]
    return pl.pallas_call(
        functools.partial(_diff_decode_kernel, lam_init=lam_init, n_pages=n_pages),
        grid_spec=pltpu.PrefetchScalarGridSpec(
            num_scalar_prefetch=1, grid=(b,),
            in_specs=[row, row, row] + pages + pages + [cst((128, 256)), cst((1, 128)), cst((1, DV_C))],
            out_specs=pl.BlockSpec((1, 8, 128), lambda i, pt: (i, 0, 0))),
        out_shape=jax.ShapeDtypeStruct((b, 8, 128), f32),
        compiler_params=_params("parallel"),
        name="diff_decode",
    )(pt_flat, qb, ks, vs, *([ck] * n_pages), *([cv] * n_pages), ind, lam, gn)


def _sample_post_kernel(oa_ref, ga_ref, gn_a_ref, y_ref, act_ref, z_ref, dskip_ref, gn_b_ref, oc_ref,
                        od_ref, rw_ref, rk_ref, lg_ref, lb_ref, o_ref):
    oa = _gla_out(oa_ref[...], ga_ref[...], gn_a_ref[...])
    ob = _ssd_out(y_ref[...] + act_ref[:, :GROUP_W] * dskip_ref[...], z_ref[...], gn_b_ref[...])
    od = _rwkv_post(od_ref[...], rw_ref[0], rw_ref[2], rw_ref[3], rw_ref[6], rk_ref[...], lg_ref[...], lb_ref[...])
    o_ref[...] = jnp.concatenate([oa, ob, oc_ref[...], od], axis=-1)


def _sample_post(cols, oa, y, act, oc, od, rw, gn_a, dskip_x, gn_b, r_k, lnx_g, lnx_b, row0, b):
    r0 = row0 // b
    cs = lambda w, start: pl.BlockSpec((b, w), lambda i, _c=start // w: (r0, _c))
    c512 = _const((b, 512))
    v512 = _const((1, 512))
    return pl.pallas_call(
        _sample_post_kernel,
        grid=(1,),
        in_specs=[c512, cs(512, C_GA), _const((1, DV_A)), c512, _const((b, XBC_W)), cs(512, C_Z), v512, v512, c512,
                  c512, _const((7, b, 512)), v512, v512, v512],
        out_specs=_const((b, D_MODEL)),
        out_shape=jax.ShapeDtypeStruct((b, D_MODEL), f32),
        compiler_params=_params("arbitrary"),
        name="sample_post",
    )(oa, cols, gn_a, y, act, cols, dskip_x, gn_b, oc, od, rw, r_k, lnx_g, lnx_b)


def _head_rms(x, gain, width):
    outs = []
    for h in range(x.shape[1] // width):
        xh = x[:, h * width:(h + 1) * width]
        outs.append(xh * lax.rsqrt(jnp.mean(xh * xh, axis=-1, keepdims=True) + 1e-6) * gain)
    return jnp.concatenate(outs, axis=-1)


def _mem_kv_kernel(m_ref, g_ref, w_ref, gk_ref, k_ref, v_ref):
    m = m_ref[...]
    m = m * lax.rsqrt(jnp.mean(m * m, axis=-1, keepdims=True) + 1e-6) * g_ref[...]
    kv = _dot(m, w_ref[...])
    k_ref[...] = _head_rms(kv[:, :D_MEM], gk_ref[...], DH_M)
    v_ref[...] = kv[:, D_MEM:]


def _mem_kv(mem, g_src, w_kv, g_k):
    shp = jax.ShapeDtypeStruct((N_MEM, D_MEM), f32)
    return pl.pallas_call(
        _mem_kv_kernel, out_shape=[shp, shp],
        compiler_params=pltpu.CompilerParams(vmem_limit_bytes=VMEM_LIMIT), name="mem_kv",
    )(mem, g_src, w_kv, g_k)


def _mem_attn_prompt_kernel(q_ref, gq_ref, k_ref, v_ref, o_ref):
    q = _head_rms(q_ref[...], gq_ref[...], DH_M) * (DH_M ** -0.5)
    k, v = k_ref[...], v_ref[...]
    outs = []
    for h in range(H_M):
        hs = slice(h * DH_M, (h + 1) * DH_M)
        s = _dot(q[:, hs], k[:, hs], NT)
        p = jnp.exp(s - jnp.max(s, axis=-1, keepdims=True))
        outs.append(_dot(p, v[:, hs]) / jnp.sum(p, axis=-1, keepdims=True))
    o_ref[...] = jnp.concatenate(outs, axis=-1)


def _mem_attn_prompt(q, gq, k, v, seq=SEQ, tb=512):
    return pl.pallas_call(
        _mem_attn_prompt_kernel,
        grid=(seq // tb,),
        in_specs=[pl.BlockSpec((tb, D_MEM), lambda i: (i, 0)), _const((1, DH_M)), _const((N_MEM, D_MEM)),
                  _const((N_MEM, D_MEM))],
        out_specs=pl.BlockSpec((tb, D_MEM), lambda i: (i, 0)),
        out_shape=jax.ShapeDtypeStruct((seq, D_MEM), f32),
        compiler_params=_params("parallel"),
        name="mem_attn_prompt",
    )(q, gq, k, v)


def _mem_attn_sample_kernel(q_ref, gq_ref, k_ref, v_ref, ind_ref, o_ref):
    q = q_ref[0]
    q = q * lax.rsqrt(jnp.mean(q * q, axis=-1, keepdims=True) + 1e-6) * gq_ref[...] * (DH_M ** -0.5)
    o_ref[0] = _rowhead_attend(k_ref[...], v_ref[...], q, ind_ref[...], 1)[0]


def _mem_attn_sample(q, gq, ck, cv, ind, layer):
    b = q.shape[0]
    kv = pl.BlockSpec((None, None, N_MEM * H_M, DH_M), lambda i: (layer, i, 0, 0))
    return pl.pallas_call(
        _mem_attn_sample_kernel,
        grid=(b,),
        in_specs=[pl.BlockSpec((1, H_M, DH_M), lambda i: (i, 0, 0)), _const((1, DH_M)), kv, kv, _const((128, 128))],
        out_specs=pl.BlockSpec((1, 8, DH_M), lambda i: (i, 0, 0)),
        out_shape=jax.ShapeDtypeStruct((b, 8, DH_M), f32),
        compiler_params=_params("parallel"),
        name="mem_attn_sample",
    )(q, gq, ck, cv, ind)


def _cast_kernel(x_ref, o_ref):
    o_ref[...] = x_ref[...].astype(bf16)


def _table_bf16(tab, layer, rows=1024):
    n, d = tab.shape[1:]
    return pl.pallas_call(
        _cast_kernel,
        grid=(n // rows,),
        in_specs=[pl.BlockSpec((None, rows, d), lambda i: (layer, i, 0))],
        out_specs=pl.BlockSpec((rows, d), lambda i: (i, 0)),
        out_shape=jax.ShapeDtypeStruct((n, d), bf16),
        compiler_params=_params("parallel"),
        name="table_bf16",
    )(tab)


def _gelu(x):
    return 0.5 * x * (1.0 + jnp.tanh(0.7978845608028654 * (x + 0.044715 * x * x * x)))


def _peer_dense_kernel(xn_ref, u_ref, v_ref, ci_ref, cnt_ref, e2_ref, r2_ref, res_ref, o_ref, w_ref):
    e = pl.program_id(1)
    tb = xn_ref.shape[0]
    n_i = PEER_EB // N_KEYS

    @pl.when(e == 0)
    def _():
        o_ref[...] = res_ref[...]

    n_half = PEER_EB // PEER_SUB
    per = n_i // n_half
    parts = []
    for half in range(n_half):
        for ii in range(half * per, (half + 1) * per):
            w = jnp.zeros((N_KEYS, tb), bf16)
            for h in range(H_P):
                row = pl.ds(e * n_i + ii, 1)
                ci = ci_ref[0, h, row, :].astype(bf16)
                cnt = cnt_ref[0, h, row, :].astype(bf16)
                w = w + jnp.where(r2_ref[0, h] < cnt, e2_ref[0, h] * ci, jnp.zeros((), bf16))
            w_ref[ii * N_KEYS:(ii + 1) * N_KEYS, :] = w
        rows = slice(half * per * N_KEYS, (half + 1) * per * N_KEYS)
        hid = _gelu(_dot(u_ref[rows, :], xn_ref[...], NT))
        parts.append(_dot(hid.astype(bf16) * w_ref[rows, :], v_ref[rows, :], TN))
    o_ref[...] += sum(parts)


def _peer_dense(xn, u, v, sel, res):
    t = xn.shape[0]
    tb, eb = PEER_TB, PEER_EB
    once = pl.Buffered(1)
    selspec = pl.BlockSpec((1, H_P, N_KEYS, tb), lambda i, e: (i, 0, 0, 0), pipeline_mode=once)
    return pl.pallas_call(
        _peer_dense_kernel,
        grid=(t // tb, N_EXPERTS // eb),
        in_specs=[pl.BlockSpec((tb, D_MODEL), lambda i, e: (i, 0), pipeline_mode=once),
                  pl.BlockSpec((eb, D_MODEL), lambda i, e: (e, 0)),
                  pl.BlockSpec((eb, D_MODEL), lambda i, e: (e, 0)), selspec, selspec, selspec, selspec,
                  pl.BlockSpec((tb, D_MODEL), lambda i, e: (i, 0), pipeline_mode=once)],
        out_specs=pl.BlockSpec((tb, D_MODEL), lambda i, e: (i, 0)),
        out_shape=jax.ShapeDtypeStruct((t, D_MODEL), f32),
        scratch_shapes=[pltpu.VMEM((eb, tb), bf16)],
        compiler_params=_params("parallel", "arbitrary"),
        name="peer_dense",
    )(xn, u, v, *sel, res)


def _pad_w_in(w):
    a0, b0, c0, d0 = 0, 1552, 2840, 4376
    seg = lambda s, n: w[:, s:s + n]
    parts = [seg(d0, 1792), seg(b0 + 1024, 256), seg(a0, 512), seg(a0 + 512, 512), seg(a0 + 1024, 512),
             seg(b0, 512), seg(b0 + 512, 512), seg(c0, 512), seg(c0 + 512, 512), seg(c0 + 1024, 512),
             seg(a0 + 1536, 16), seg(b0 + 1280, 8), jnp.zeros((w.shape[0], IN_PAD - 6168), w.dtype)]
    return jnp.concatenate(parts, axis=1).astype(bf16)


def _layer_consts():
    ex = np.zeros((128, 512), np.float32)
    sel = np.zeros((8, 512), np.float32)
    for h in range(8):
        ex[GK_RANK + h, h * 64:(h + 1) * 64] = 1.0
        sel[h, h * 64] = 1.0
    ind2 = np.zeros((128, 256), np.float32)
    ind2[:DK_C, :128] = 1.0
    ind2[DK_C:, 128:] = 1.0
    ind1 = np.ones((128, 128), np.float32)
    return jnp.asarray(ex), jnp.asarray(sel), jnp.asarray(ind2), jnp.asarray(ind1)


def kernel(x_prompt, x_sample, cache_diff_k, cache_diff_v, cache_mem_k, cache_mem_v, state_gla, state_ssm, state_conv, state_rwkv, state_shift, page_table, mem_prompt, norm_mix, w_in, w_out, gla_wg2, gla_bg, gla_gn, conv_w, conv_b, dt_bias, a_log, d_skip, ssm_gn, dq_norm, dk_norm, lam_q, lam_k, diff_gn, shift_mu, w0, w2, a0, a2, g2, k_k, k_a, r_k, lnx_g, lnx_b, norm_mem, norm_memsrc, w_mq, w_mk, w_mv, w_mo, mq_norm, mk_norm, norm_ffn, peer_wq, peer_keys, peer_u, peer_v):
    nb = DEC_BATCH
    n_pages = page_table.shape[1]
    n_pool = cache_diff_k.shape[1]
    x = jnp.concatenate([x_prompt[0], x_sample[:, 0]], axis=0)
    pt_flat = page_table.reshape(-1)
    cos_p, sin_p = _rope_tables(jnp.arange(SEQ, dtype=jnp.int32))
    cos_s, sin_s = _rope_tables(jnp.full((nb,), PAST_LEN, jnp.int32))
    ex, sel8, ind2, ind1 = _layer_consts()
    ck_rows = cache_diff_k.reshape(DEPTH, n_pool, PAGE_SIZE * H_C, 2 * DK_C)
    cv_rows = cache_diff_v.reshape(DEPTH, n_pool, PAGE_SIZE * H_C, DV_C)
    mk_rows = cache_mem_k.reshape(DEPTH, nb, N_MEM * H_M, DH_M)
    mv_rows = cache_mem_v.reshape(DEPTH, nb, N_MEM * H_M, DH_M)
    row = lambda a: a.reshape(1, -1)
    rep64 = lambda a: jnp.repeat(a, 64).reshape(1, 512)
    outs = {n: [] for n in ('kp', 'vp', 'ks', 'vs', 'mk', 'mv', 'gla_p', 'gla_s', 'ssm_p', 'ssm_s', 'conv_p',
                            'conv_s', 'rwkv_p', 'rwkv_s', 'shift_p', 'shift_s')}
    for l in range(DEPTH):
        lam_init = 0.8 - 0.6 * math.exp(-0.3 * l)
        lq, lk = lam_q[l], lam_k[l]
        lam = jnp.exp(jnp.sum(lq[0] * lk[0])) - jnp.exp(jnp.sum(lq[1] * lk[1])) + lam_init
        lam = jnp.full((1, 128), lam, f32)
        wg2p = jnp.zeros((128, 256), f32).at[:GK_RANK].set(gla_wg2[l])
        w2a2 = jnp.zeros((128, 1024), f32).at[:64, :512].set(w2[l]).at[64:, 512:].set(a2[l])
        gq = jnp.tile(dq_norm[l].reshape(128), 4).reshape(1, 512)
        gk = jnp.tile(dk_norm[l].reshape(128), 4).reshape(1, 512)
        dtb_x, alog_x, dskip_x = rep64(dt_bias[l]), rep64(a_log[l]), rep64(d_skip[l])

        cols = _matmul(x, _pad_w_in(w_in[l]), gain=norm_mix[l])

        oa, gla_p = _gla_prompt(cols, wg2p, row(gla_bg[l]), row(gla_gn[l]), seq=SEQ)
        ob, ssm_p = _ssd_prompt(cols, conv_w[l], row(conv_b[l]), ex, dtb_x, alog_x, dskip_x, row(ssm_gn[l]), sel8,
                                seq=SEQ)
        qb, kf, kb = _diff_prep(cols, cos_p, sin_p, gq, gk, 0, SEQ, 512)
        oc = _diff_attn_prompt(qb, kb, cols, lam, row(diff_gn[l]), lam_init, seq=SEQ)
        od, rwkv_p = _rwkv_prompt(cols, row(shift_mu[l]), row(w0[l]), row(a0[l]), row(k_k[l]), row(k_a[l]),
                                  row(r_k[l]), row(lnx_g[l]), row(lnx_b[l]), w2a2, g2[l], seq=SEQ)
        mix_p = jnp.concatenate([oa, ob, oc, od], axis=1)

        pre_w = [wg2p, row(gla_bg[l]), conv_w[l], row(conv_b[l]), ex, dtb_x, alog_x,
                 row(shift_mu[l]), row(w0[l]), row(a0[l]), row(k_k[l]), row(k_a[l]), w2a2, g2[l]]
        s_gq, s_gk, s_ga, s_act, s_dec, s_xd, s_rw = _sample_pre(cols, jnp.transpose(state_conv[l], (1, 0, 2)),
                                                                 state_shift[l][:, 0], pre_w, SEQ, nb)
        col = lambda a: a.reshape(nb, -1, 1)
        tail = cols[SEQ:]
        gla_s, oa_s = _state_step(_gla_step_kernel, "gla_step", state_gla[l].reshape(nb, H_A * DK_A, DV_A),
                                  [col(s_ga), col(s_gk), col(s_gq), tail[:, C_VA:C_VA + 512].reshape(nb, 1, 512)],
                                  [(nb, 1, 512)])
        ssm_s, y_s = _state_step(_ssd_step_kernel, "ssd_step", state_ssm[l].reshape(nb, H_B * P_B, N_B),
                                 [col(s_dec), col(s_xd), s_act[:, GROUP_W:].reshape(nb, 1, 256)], [(nb, H_B * P_B, 1)])
        rwkv_s, od_s = _state_step(_rwkv_step_kernel, "rwkv_step", state_rwkv[l].reshape(nb, H_D * N_D, N_D),
                                   [jnp.transpose(s_rw, (1, 0, 2)), col(s_rw[3])], [(nb, H_D * N_D, 1)])
        qb_s, kf_s, _ = _diff_prep(cols, cos_s, sin_s, gq, gk, SEQ, nb, nb)
        vc_s = tail[:, C_VC:C_VC + 512]
        oc_s = _diff_decode(pt_flat, l, qb_s.reshape(nb, H_C, 128), kf_s.reshape(nb, H_C, 128), vc_s.reshape(nb, H_C, 128),
                            ck_rows, cv_rows, ind2, lam, row(diff_gn[l]), lam_init, n_pages)
        mix_s = _sample_post(cols, oa_s.reshape(nb, 512), y_s.reshape(nb, 512), s_act, oc_s[:, :H_C].reshape(nb, 512),
                             od_s.reshape(nb, 512), s_rw, row(gla_gn[l]), dskip_x, row(ssm_gn[l]), row(r_k[l]),
                             row(lnx_g[l]), row(lnx_b[l]), SEQ, nb)

        x = _matmul(jnp.concatenate([mix_p, mix_s], axis=0), w_out[l].astype(bf16), res=x)

        mk_p, mv_p = _mem_kv(mem_prompt[0], row(norm_memsrc[l]),
                             jnp.concatenate([w_mk[l], w_mv[l]], axis=1).astype(bf16), row(mk_norm[l]))
        qm = _matmul(x, w_mq[l].astype(bf16), gain=norm_mem[l])
        om_p = _mem_attn_prompt(qm, row(mq_norm[l]), mk_p, mv_p, seq=SEQ)
        om_s = _mem_attn_sample(qm[SEQ:].reshape(nb, H_M, DH_M), row(mq_norm[l]), mk_rows, mv_rows, ind1, l)
        x = _matmul(jnp.concatenate([om_p, om_s[:, :H_M].reshape(nb, D_MEM)], axis=0), w_mo[l].astype(bf16), res=x)

        qp, xn = _matmul(x, peer_wq[l].astype(bf16), gain=norm_ffn[l], emit_xn=True, tn=512)
        picks = _peer_select(qp, peer_keys[l].reshape(2 * H_P, N_KEYS, D_PK // 2))
        x = _peer_dense(xn, _table_bf16(peer_u, l), _table_bf16(peer_v, l), picks, x)

        outs['kp'].append(kf.reshape(1, SEQ, H_C, 2 * DK_C))
        outs['vp'].append(cols[:SEQ, C_VC:C_VC + 512].reshape(1, SEQ, H_C, DV_C))
        outs['ks'].append(kf_s.reshape(nb, 1, H_C, 2 * DK_C))
        outs['vs'].append(vc_s.reshape(nb, 1, H_C, DV_C))
        outs['mk'].append(mk_p.reshape(1, N_MEM, H_M, DH_M))
        outs['mv'].append(mv_p.reshape(1, N_MEM, H_M, DH_M))
        outs['gla_p'].append(gla_p[None])
        outs['gla_s'].append(gla_s.reshape(nb, H_A, DK_A, DV_A))
        outs['ssm_p'].append(ssm_p[None])
        outs['ssm_s'].append(ssm_s.reshape(nb, H_B, P_B, N_B))
        u_p = jnp.concatenate([cols[SEQ - 3:SEQ, C_XS:C_XS + 512], cols[SEQ - 3:SEQ, C_BC:C_BC + 256]], axis=1)
        u_s = jnp.concatenate([tail[:, C_XS:C_XS + 512], tail[:, C_BC:C_BC + 256]], axis=1)
        outs['conv_p'].append(u_p[None])
        outs['conv_s'].append(jnp.concatenate([state_conv[l][:, 1:], u_s[:, None]], axis=1))
        outs['rwkv_p'].append(rwkv_p[None])
        outs['rwkv_s'].append(rwkv_s.reshape(nb, H_D, N_D, N_D))
        outs['shift_p'].append(cols[SEQ - 1:SEQ, C_D:C_D + 1792][None])
        outs['shift_s'].append(tail[:, C_D:C_D + 1792][:, None])
    st = {n: jnp.stack(v) for n, v in outs.items()}
    return (x[:SEQ][None], x[SEQ:][:, None], st['kp'], st['vp'], st['ks'], st['vs'], st['mk'], st['mv'],
            st['gla_p'], st['gla_s'], st['ssm_p'], st['ssm_s'], st['conv_p'], st['conv_s'],
            st['rwkv_p'], st['rwkv_s'], st['shift_p'], st['shift_s'])
```

```python
import functools
import math

import numpy as np
import jax
import jax.numpy as jnp
from jax import lax
from jax.experimental import pallas as pl
from jax.experimental.pallas import tpu as pltpu

f32 = jnp.float32
bf16 = jnp.bfloat16
HI = lax.Precision.HIGHEST

D_MODEL = 2048
SEQ = 8192
DEPTH = 2
DEC_BATCH = 128
PAST_LEN = 2048
PAGE_SIZE = 128
T_ALL = SEQ + DEC_BATCH

GROUP_W = 512
H_A, DK_A, DV_A, GK_RANK, GLA_TAU = 4, 64, 128, 16, 16.0
H_B, P_B, N_B, G_B, CONV_W, XBC_W = 8, 64, 64, 2, 4, 768
H_C, DK_C, DV_C, ROPE_THETA = 4, 64, 128, 10000.0
H_D, N_D, LNX_EPS = 8, 64, 64e-5
N_MEM, H_M, D_MEM, DH_M = 256, 4, 512, 128
N_KEYS, H_P, TOPK_P, D_PK = 128, 8, 16, 256
N_EXPERTS = N_KEYS * N_KEYS

C_D = 0
C_BC = 1792
C_QKA = 2048
C_VA = 2560
C_GA = 3072
C_Z = 3584
C_XS = 4096
C_QC = 4608
C_KC = 5120
C_VC = 5632
C_TAIL = 6144
IN_PAD = 6272

LANES = 128
VMEM_LIMIT = 56 * 1024 * 1024

NN = ((1,), (0,))
NT = ((1,), (1,))
TN = ((0,), (0,))


def _dot(a, b, dims=NN, hi=False):
    if hi:
        return lax.dot_general(a, b, (dims, ((), ())), precision=HI, preferred_element_type=f32)
    return lax.dot_general(a.astype(bf16), b.astype(bf16), (dims, ((), ())), preferred_element_type=f32)


def _softplus(x):
    return jnp.maximum(x, 0.0) + jnp.log(1.0 + jnp.exp(-jnp.abs(x)))


def _sigmoid(x):
    return 1.0 / (1.0 + jnp.exp(-x))


def _silu(x):
    return x * _sigmoid(x)


def _iota(shape, axis):
    return lax.broadcasted_iota(jnp.int32, shape, axis)


def _params(*sem):
    return pltpu.CompilerParams(dimension_semantics=sem, vmem_limit_bytes=VMEM_LIMIT)


def _mm_kernel(*refs, norm, residual, emit_xn):
    x_ref, g_ref, w_ref = refs[:3]
    res_ref = refs[3] if residual else None
    xn_ref = refs[-1]
    o_ref = refs[-3] if emit_xn else refs[-2]

    @pl.when(pl.program_id(1) == 0)
    def _():
        x = x_ref[...]
        if norm:
            x = x * lax.rsqrt(jnp.mean(x * x, axis=-1, keepdims=True) + 1e-6) * g_ref[...]
        xn_ref[...] = x.astype(bf16)
        if emit_xn:
            refs[-2][...] = xn_ref[...]

    acc = jnp.dot(xn_ref[...], w_ref[...], preferred_element_type=f32)
    if residual:
        acc = acc + res_ref[...]
    o_ref[...] = acc


def _matmul(x, w, gain=None, res=None, emit_xn=False, tm=None, tn=None):
    m, k = x.shape
    n = w.shape[1]
    tm = tm or _pick(m, (1040, 1024, 512, 256, 128))
    tn = tn or _pick(n, (1024, 896, 512, 256, 128))
    norm = gain is not None
    g = (gain if norm else jnp.ones((k,), f32)).reshape(1, k)
    args = [x, g, w]
    in_specs = [pl.BlockSpec((tm, k), lambda i, j: (i, 0)),
                pl.BlockSpec((1, k), lambda i, j: (0, 0)),
                pl.BlockSpec((k, tn), lambda i, j: (0, j))]
    if res is not None:
        args.append(res)
        in_specs.append(pl.BlockSpec((tm, tn), lambda i, j: (i, j)))
    out_specs = [pl.BlockSpec((tm, tn), lambda i, j: (i, j))]
    out_shape = [jax.ShapeDtypeStruct((m, n), f32)]
    if emit_xn:
        out_specs.append(pl.BlockSpec((tm, k), lambda i, j: (i, 0)))
        out_shape.append(jax.ShapeDtypeStruct((m, k), bf16))
    out = pl.pallas_call(
        functools.partial(_mm_kernel, norm=norm, residual=res is not None, emit_xn=emit_xn),
        grid=(m // tm, n // tn),
        in_specs=in_specs,
        out_specs=out_specs,
        out_shape=out_shape,
        scratch_shapes=[pltpu.VMEM((tm, k), bf16)],
        compiler_params=_params("parallel", "arbitrary"),
        name="mm",
    )(*args)
    return out if emit_xn else out[0]


def _pick(n, cands):
    for c in cands:
        if n % c == 0:
            return c
    return n


def _const(shape):
    nd = len(shape)
    return pl.BlockSpec(shape, lambda i, _n=nd: (0,) * _n)


def _cols(width, start, tb):
    assert start % width == 0
    return pl.BlockSpec((tb, width), lambda i, _c=start // width: (i, _c))


def _blockdiag_tri(tb, c):
    r = _iota((tb, tb), 0)
    s = _iota((tb, tb), 1)
    return jnp.where((r // c == s // c) & (s <= r), 1.0, 0.0).astype(f32)


def _segment_ones(n, seg):
    r = _iota((n, n), 0)
    s = _iota((n, n), 1)
    return jnp.where(r // seg == s // seg, 1.0, 0.0).astype(f32)


GLA_TB = 128
GLA_C = 16


def _gla_gate_log(tail, wg2p, bg):
    z = _dot(tail, wg2p, hi=True) + bg
    return -_softplus(-z) * (1.0 / GLA_TAU)


def _gla_out(o, g, gn):
    outs = []
    for h in range(H_A):
        oh = o[:, h * DV_A:(h + 1) * DV_A]
        oh = oh * lax.rsqrt(jnp.mean(oh * oh, axis=-1, keepdims=True) + 1e-6) * gn
        outs.append(oh * _silu(g[:, h * DV_A:(h + 1) * DV_A]))
    return jnp.concatenate(outs, axis=-1)


def _gla_kernel(qk_ref, v_ref, g_ref, tail_ref, wg2_ref, bg_ref, gn_ref, o_ref, sfin_ref, st_ref, w_ref):
    i = pl.program_id(0)

    @pl.when(i == 0)
    def _():
        st_ref[...] = jnp.zeros_like(st_ref)

    tb, c = GLA_TB, GLA_C
    qk = qk_ref[...]
    q = qk[:, :256] * (DK_A ** -0.5)
    k = qk[:, 256:]
    v = v_ref[...]
    la = _gla_gate_log(tail_ref[...], wg2_ref[...], bg_ref[...])
    b = _dot(_blockdiag_tri(tb, c), la, hi=True)
    e_r = _iota((256, 512), 0) // DK_A
    e_c = _iota((256, 512), 1) // DV_A
    expand = jnp.where(e_r == e_c, 1.0, 0.0).astype(bf16)
    s_idx = _iota((c, 256), 0)
    for j in range(tb // c):
        r0 = j * c
        qj, kj, bj, vj = q[r0:r0 + c], k[r0:r0 + c], b[r0:r0 + c], v[r0:r0 + c]
        for t in range(c):
            wt = qj[t:t + 1] * kj * jnp.exp(bj[t:t + 1] - bj)
            w_ref[t * c:(t + 1) * c, :] = jnp.where(s_idx <= t, wt, 0.0)
        att = _dot(w_ref[...], expand)
        o = jnp.sum(att.reshape(c, c, 512) * vj[None], axis=1)
        qe = qj * jnp.exp(bj)
        bl = bj[c - 1:c]
        ke = kj * jnp.exp(bl - bj)
        dl = jnp.exp(bl)
        inter = []
        for h in range(H_A):
            ks = slice(h * DK_A, (h + 1) * DK_A)
            st = st_ref[h]
            inter.append(_dot(qe[:, ks], st, NT))
            st_ref[h] = st * dl[:, ks] + _dot(vj[:, h * DV_A:(h + 1) * DV_A], ke[:, ks], TN)
        o = o + jnp.concatenate(inter, axis=-1)
        o_ref[r0:r0 + c, :] = _gla_out(o, g_ref[r0:r0 + c, :], gn_ref[...])

    @pl.when(i == pl.num_programs(0) - 1)
    def _():
        for h in range(H_A):
            sfin_ref[h] = st_ref[h].T


def _gla_prompt(cols, wg2p, bg, gn, seq=SEQ):
    tb = GLA_TB
    return pl.pallas_call(
        _gla_kernel,
        grid=(seq // tb,),
        in_specs=[_cols(512, C_QKA, tb), _cols(512, C_VA, tb), _cols(512, C_GA, tb), _cols(128, C_TAIL, tb),
                  _const((128, 256)), _const((1, 256)), _const((1, DV_A))],
        out_specs=[pl.BlockSpec((tb, 512), lambda i: (i, 0)), _const((H_A, DK_A, DV_A))],
        out_shape=[jax.ShapeDtypeStruct((seq, 512), f32), jax.ShapeDtypeStruct((H_A, DK_A, DV_A), f32)],
        scratch_shapes=[pltpu.VMEM((H_A, DV_A, DK_A), f32), pltpu.VMEM((GLA_C * GLA_C, 256), f32)],
        compiler_params=_params("arbitrary"),
        name="gla_prompt",
    )(cols, cols, cols, cols, wg2p, bg, gn)


SSD_TB = 128
SSD_C = 64


def _ssd_conv(ext, conv_w, conv_b, rows):
    out = conv_b
    for j in range(CONV_W):
        shifted = pltpu.roll(ext, j, 0) if j else ext
        out = out + shifted[8:8 + rows] * conv_w[CONV_W - 1 - j:CONV_W - j]
    return out


def _ssd_dt(tail, ex, dtb_x):
    return _softplus(_dot(tail, ex, hi=True) + dtb_x)


def _ssd_out(y, z, gn):
    y = y * _silu(z)
    w = GROUP_W // G_B
    outs = []
    for g in range(G_B):
        yg = y[:, g * w:(g + 1) * w]
        outs.append(yg * lax.rsqrt(jnp.mean(yg * yg, axis=-1, keepdims=True) + 1e-6) * gn[:, g * w:(g + 1) * w])
    return jnp.concatenate(outs, axis=-1)


def _ssd_kernel(z_ref, xs_ref, bc_ref, tail_ref, cw_ref, cb_ref, ex_ref, dtb_ref, alog_ref, dskip_ref, gn_ref,
                sel_ref, o_ref, hfin_ref, carry_ref, h_ref, y_ref):
    i = pl.program_id(0)

    @pl.when(i == 0)
    def _():
        carry_ref[...] = jnp.zeros_like(carry_ref)
        h_ref[...] = jnp.zeros_like(h_ref)

    tb, c = SSD_TB, SSD_C
    u = jnp.concatenate([xs_ref[...], bc_ref[...]], axis=-1)
    ext = jnp.concatenate([carry_ref[...], u], axis=0)
    carry_ref[...] = u[tb - 8:tb]
    act = _silu(_ssd_conv(ext, cw_ref[...], cb_ref[...], tb))
    xs = act[:, :GROUP_W]
    dtx = _ssd_dt(tail_ref[...], ex_ref[...], dtb_ref[...])
    la = dtx * (-jnp.exp(alog_ref[...]))
    b = _dot(_blockdiag_tri(tb, c), la, hi=True)
    brow = _dot(sel_ref[...], b, NT, hi=True)
    xd = xs * dtx
    tri = _iota((c, c), 1) <= _iota((c, c), 0)
    for ch in range(tb // c):
        r0 = ch * c
        rows = slice(r0, r0 + c)
        scores = []
        for g in range(G_B):
            bm = act[rows, GROUP_W + g * N_B:GROUP_W + (g + 1) * N_B]
            cm = act[rows, GROUP_W + G_B * N_B + g * N_B:GROUP_W + G_B * N_B + (g + 1) * N_B]
            scores.append((_dot(cm, bm, NT), bm, cm))
        for h in range(H_B):
            hs = slice(h * P_B, (h + 1) * P_B)
            sc, bm, cm = scores[h // (H_B // G_B)]
            bh = b[rows, hs]
            dec = jnp.where(tri, jnp.exp(bh - brow[h:h + 1, r0:r0 + c]), 0.0)
            xdh = xd[rows, hs]
            hst = h_ref[h]
            y = _dot(sc * dec, xdh) + _dot(cm, hst, NT) * jnp.exp(bh)
            bl = bh[c - 1:c]
            h_ref[h] = hst * jnp.exp(bl) + _dot(xdh * jnp.exp(bl - bh), bm, TN)
            y_ref[rows, hs] = y + xs[rows, hs] * dskip_ref[:, hs]
    o_ref[...] = _ssd_out(y_ref[...], z_ref[...], gn_ref[...])

    @pl.when(i == pl.num_programs(0) - 1)
    def _():
        hfin_ref[...] = h_ref[...]


def _ssd_prompt(cols, cw, cb, ex, dtb_x, alog_x, dskip_x, gn, sel, seq=SEQ):
    tb = SSD_TB
    return pl.pallas_call(
        _ssd_kernel,
        grid=(seq // tb,),
        in_specs=[_cols(512, C_Z, tb), _cols(512, C_XS, tb), _cols(256, C_BC, tb), _cols(128, C_TAIL, tb),
                  _const((CONV_W, XBC_W)), _const((1, XBC_W)), _const((128, 512)), _const((1, 512)), _const((1, 512)),
                  _const((1, 512)), _const((1, 512)), _const((8, 512))],
        out_specs=[pl.BlockSpec((tb, 512), lambda i: (i, 0)), _const((H_B, P_B, N_B))],
        out_shape=[jax.ShapeDtypeStruct((seq, 512), f32), jax.ShapeDtypeStruct((H_B, P_B, N_B), f32)],
        scratch_shapes=[pltpu.VMEM((8, XBC_W), f32), pltpu.VMEM((H_B, P_B, N_B), f32), pltpu.VMEM((tb, 512), f32)],
        compiler_params=_params("arbitrary"),
        name="ssd_prompt",
    )(cols, cols, cols, cols, cw, cb, ex, dtb_x, alog_x, dskip_x, gn, sel)


RWKV_TB = 128
RWKV_C = 64


def _rwkv_pre(x, xprev, mu, w0, a0, k_k, k_a, w2a2, g2):
    mixed = x + (xprev - x) * mu
    r = mixed[:, :512]
    kd = mixed[:, 512:1024]
    v = mixed[:, 1024:1536]
    lw = mixed[:, 1536:1664]
    lin = jnp.where(_iota(lw.shape, 1) < 64, jnp.tanh(lw), lw)
    wa = _dot(lin, w2a2, hi=True)
    w = -_softplus(-(w0 + wa[:, :512])) - 0.5
    a = _sigmoid(a0 + wa[:, 512:])
    g = _dot(_sigmoid(mixed[:, 1664:1792]), g2)
    kk = kd * k_k
    ss = _dot(kk * kk, _segment_ones(512, N_D), hi=True)
    kk = kk * lax.rsqrt(jnp.maximum(ss, 1e-24))
    kd = kd * (1.0 + (a - 1.0) * k_a)
    return r, w, kd, v, -kk, kk * a, g


def _rwkv_post(o, r, kd, v, g, r_k, lnx_g, lnx_b):
    seg = _segment_ones(512, N_D)
    mu = _dot(o, seg, hi=True) * (1.0 / N_D)
    d = o - mu
    var = _dot(d * d, seg, hi=True) * (1.0 / N_D)
    o = d * lax.rsqrt(var + LNX_EPS) * lnx_g + lnx_b
    o = o + _dot(r * kd * r_k, seg, hi=True) * v
    return o * g


def _rwkv_kernel(d_ref, mu_ref, w0_ref, a0_ref, kk_ref, ka_ref, rk_ref, lg_ref, lb_ref, w2a2_ref, g2_ref,
                 o_ref, sfin_ref, prev_ref, s_ref, oacc_ref):
    i = pl.program_id(0)

    @pl.when(i == 0)
    def _():
        prev_ref[...] = jnp.zeros_like(prev_ref)
        s_ref[...] = jnp.zeros_like(s_ref)

    tb, c = RWKV_TB, RWKV_C
    x = d_ref[...]
    xprev = jnp.where(_iota(x.shape, 0) == 0, prev_ref[0:1, :], pltpu.roll(x, 1, 0))
    prev_ref[0:1, :] = x[tb - 1:tb]
    r, w, kd, v, alpha, beta, g = _rwkv_pre(x, xprev, mu_ref[...], w0_ref[...], a0_ref[...], kk_ref[...], ka_ref[...],
                                            w2a2_ref[...], g2_ref[...])
    ld = -jnp.exp(w)
    cum = _dot(_blockdiag_tri(tb, c), ld, hi=True)
    at = alpha * jnp.exp(cum - ld)
    rt = r * jnp.exp(cum)
    einv = jnp.exp(-cum)
    kt = kd * einv
    bt = beta * einv
    ri = _iota((c, c), 0)
    ci = _iota((c, c), 1)
    strict, incl = ci < ri, ci <= ri
    eye = jnp.where(ri == ci, 1.0, 0.0).astype(f32)
    pairs = [(ch, h) for ch in range(tb // c) for h in range(H_D)]
    rows_of = lambda ch: slice(ch * c, (ch + 1) * c)
    lanes_of = lambda h: slice(h * N_D, (h + 1) * N_D)
    ar, kb, lk, mkb, tinv, p = {}, {}, {}, {}, {}, {}
    for ch, h in pairs:
        rows, hs = rows_of(ch), lanes_of(h)
        ar[ch, h] = jnp.concatenate([at[rows, hs], rt[rows, hs]], axis=0)
        kb[ch, h] = jnp.concatenate([kt[rows, hs], bt[rows, hs]], axis=0)
    for key in pairs:
        gram = _dot(ar[key], kb[key], NT)
        lk[key] = jnp.where(strict, gram[:c, :c], 0.0)
        p[key] = jnp.where(strict, gram[:c, c:], 0.0)
        mkb[key] = jnp.concatenate([jnp.where(incl, gram[c:, :c], 0.0), jnp.where(incl, gram[c:, c:], 0.0)], axis=1)
        tinv[key] = eye + p[key]
    for _ in range(5):
        for key in pairs:
            p[key] = _dot(p[key], p[key])
        for key in pairs:
            tinv[key] = tinv[key] + _dot(tinv[key], p[key])
    for ch in range(tb // c):
        rows = rows_of(ch)
        cl = cum[ch * c + c - 1:ch * c + c]
        efin = jnp.exp(cl - cum[rows])
        kfin = kd[rows] * efin
        bfin = beta[rows] * efin
        dfin = jnp.exp(cl)
        heads = range(H_D)
        s0 = [s_ref[h] for h in heads]
        ars = [_dot(ar[ch, h], s0[h], NT) for h in heads]
        lkv = [_dot(lk[ch, h], v[rows, lanes_of(h)]) for h in heads]
        u = [_dot(tinv[ch, h], ars[h][:c] + lkv[h]) for h in heads]
        vu = [jnp.concatenate([v[rows, lanes_of(h)], u[h]], axis=0) for h in heads]
        for h in heads:
            hs = lanes_of(h)
            oacc_ref[rows, hs] = ars[h][c:] + _dot(mkb[ch, h], vu[h])
            kbfin = jnp.concatenate([kfin[:, hs], bfin[:, hs]], axis=0)
            s_ref[h] = s0[h] * dfin[:, hs] + _dot(vu[h], kbfin, TN)
    o_ref[...] = _rwkv_post(oacc_ref[...], r, kd, v, g, rk_ref[...], lg_ref[...], lb_ref[...])

    @pl.when(i == pl.num_programs(0) - 1)
    def _():
        sfin_ref[...] = s_ref[...]


def _rwkv_prompt(cols, mu, w0, a0, k_k, k_a, r_k, lnx_g, lnx_b, w2a2, g2, seq=SEQ):
    tb = RWKV_TB
    vec = _const((1, 512))
    return pl.pallas_call(
        _rwkv_kernel,
        grid=(seq // tb,),
        in_specs=[_cols(1792, C_D, tb), _const((1, 1792)), vec, vec, vec, vec, vec, vec, vec,
                  _const((128, 1024)), _const((128, 512))],
        out_specs=[pl.BlockSpec((tb, 512), lambda i: (i, 0)), _const((H_D, N_D, N_D))],
        out_shape=[jax.ShapeDtypeStruct((seq, 512), f32), jax.ShapeDtypeStruct((H_D, N_D, N_D), f32)],
        scratch_shapes=[pltpu.VMEM((8, 1792), f32), pltpu.VMEM((H_D, N_D, N_D), f32), pltpu.VMEM((tb, 512), f32)],
        compiler_params=_params("arbitrary"),
        name="rwkv_prompt",
    )(cols, mu, w0, a0, k_k, k_a, r_k, lnx_g, lnx_b, w2a2, g2)


def _rope_tables(pos):
    half = DK_C // 2
    inv = ROPE_THETA ** (-jnp.arange(half, dtype=f32) / half)
    ang = pos.astype(f32)[:, None] * inv[None, :]
    cos, sin = jnp.cos(ang), jnp.sin(ang)
    return jnp.tile(jnp.concatenate([cos, cos], axis=-1), (1, 2)), jnp.tile(jnp.concatenate([-sin, sin], axis=-1), (1, 2))


def _qk_norm_rope(x, gain, cos, sin):
    ms = _dot(x * x, _segment_ones(512, DK_C), hi=True) * (1.0 / DK_C)
    x = x * lax.rsqrt(ms + 1e-6) * gain
    first = (_iota(x.shape, 1) % DK_C) < (DK_C // 2)
    partner = jnp.where(first, pltpu.roll(x, 512 - DK_C // 2, 1), pltpu.roll(x, DK_C // 2, 1))
    cos = jnp.concatenate([cos] * 4, axis=-1)
    sin = jnp.concatenate([sin] * 4, axis=-1)
    return x * cos + partner * sin


def _diff_prep_kernel(q_ref, k_ref, cos_ref, sin_ref, gq_ref, gk_ref, qb_ref, kf_ref, kb_ref):
    cos, sin = cos_ref[...], sin_ref[...]
    q = _qk_norm_rope(q_ref[...], gq_ref[...], cos, sin)
    k = _qk_norm_rope(k_ref[...], gk_ref[...], cos, sin)
    qb_ref[...] = (q * (DK_C ** -0.5)).astype(bf16)
    kf_ref[...] = k
    kb_ref[...] = k.astype(bf16)


def _diff_prep(cols, cos, sin, gq, gk, row0, rows, tb):
    assert row0 % tb == 0
    r0 = row0 // tb
    colspec = lambda start: pl.BlockSpec((tb, 512), lambda i, _c=start // 512: (i + r0, _c))
    out = pl.BlockSpec((tb, 512), lambda i: (i, 0))
    tab = pl.BlockSpec((tb, 128), lambda i: (i, 0))
    return pl.pallas_call(
        _diff_prep_kernel,
        grid=(rows // tb,),
        in_specs=[colspec(C_QC), colspec(C_KC), tab, tab, _const((1, 512)), _const((1, 512))],
        out_specs=[out, out, out],
        out_shape=[jax.ShapeDtypeStruct((rows, 512), bf16), jax.ShapeDtypeStruct((rows, 512), f32),
                   jax.ShapeDtypeStruct((rows, 512), bf16)],
        compiler_params=_params("parallel"),
        name="diff_prep",
    )(cols, cols, cos, sin, gq, gk)


def _diff_finish(o1, o2, lam, gn, lam_init):
    o = o1 - lam * o2
    return o * lax.rsqrt(jnp.mean(o * o, axis=-1, keepdims=True) + 1e-6) * gn * (1.0 - lam_init)


FLASH_TQ = 1024
FLASH_TK = 512


def _flash_kernel(qi_ref, kj_ref, q_ref, k_ref, v_ref, lam_ref, gn_ref, o_ref, m_ref, l_ref, acc_ref,
                  *, tq, tk, lam_init):
    step_id = pl.program_id(1)
    qi, kj = qi_ref[step_id], kj_ref[step_id]

    @pl.when(kj == 0)
    def _():
        m_ref[...] = jnp.full_like(m_ref, -jnp.inf)
        l_ref[...] = jnp.zeros_like(l_ref)
        acc_ref[...] = jnp.zeros_like(acc_ref)

    def step(masked):
        q = q_ref[...]
        k = k_ref[...]
        v = v_ref[...].astype(bf16)
        lane = _iota(q.shape, 1)
        if masked:
            keep = (kj * tk + _iota((tk, tq), 0)) <= (qi * tq + _iota((tk, tq), 1))
        for m in range(2):
            qm = jnp.where((lane < DK_C) if m == 0 else (lane >= DK_C), q, jnp.zeros_like(q))
            s = _dot(k, qm, NT)
            if masked:
                s = jnp.where(keep, s, -jnp.inf)
            m_old = m_ref[m]
            m_new = jnp.maximum(m_old, jnp.max(s, axis=0, keepdims=True))
            p = jnp.exp(s - m_new)
            corr = jnp.exp(m_old - m_new)
            l_ref[m] = corr * l_ref[m] + jnp.sum(p, axis=0, keepdims=True)
            acc_ref[m] = corr * acc_ref[m] + _dot(v, p, TN)
            m_ref[m] = m_new

    last = kj * tk + tk - 1

    @pl.when(last <= qi * tq)
    def _():
        step(False)

    @pl.when(last > qi * tq)
    def _():
        step(True)

    @pl.when(kj == (qi * tq + tq - 1) // tk)
    def _():
        o1 = (acc_ref[0] / l_ref[0]).T
        o2 = (acc_ref[1] / l_ref[1]).T
        o_ref[...] = _diff_finish(o1, o2, lam_ref[...], gn_ref[...], lam_init)


def _diff_attn_prompt(qb, kb, cols, lam, gn, lam_init, seq=SEQ, tq=512, tk=512):
    pairs = [(i, j) for i in range(seq // tq) for j in range((i * tq + tq - 1) // tk + 1)]
    qi_tab = jnp.asarray([p[0] for p in pairs], jnp.int32)
    kj_tab = jnp.asarray([p[1] for p in pairs], jnp.int32)
    cst = lambda shp: pl.BlockSpec(shp, lambda h, s, qi, kj: (0, 0))
    return pl.pallas_call(
        functools.partial(_flash_kernel, tq=tq, tk=tk, lam_init=lam_init),
        grid_spec=pltpu.PrefetchScalarGridSpec(
            num_scalar_prefetch=2, grid=(H_C, len(pairs)),
            in_specs=[pl.BlockSpec((tq, 128), lambda h, s, qi, kj: (qi[s], h)),
                      pl.BlockSpec((tk, 128), lambda h, s, qi, kj: (kj[s], h)),
                      pl.BlockSpec((tk, DV_C), lambda h, s, qi, kj: (kj[s], C_VC // DV_C + h)),
                      cst((1, 128)), cst((1, DV_C))],
            out_specs=pl.BlockSpec((tq, DV_C), lambda h, s, qi, kj: (qi[s], h)),
            scratch_shapes=[pltpu.VMEM((2, 1, tq), f32), pltpu.VMEM((2, 1, tq), f32), pltpu.VMEM((2, DV_C, tq), f32)]),
        out_shape=jax.ShapeDtypeStruct((seq, 512), f32),
        compiler_params=_params("parallel", "arbitrary"),
        name="diff_attn_prompt",
    )(qi_tab, kj_tab, qb, kb, cols, lam, gn)


PEER_SEL_TB = 128
PEER_TB = 640
PEER_EB = 512
PEER_SUB = 256
_PAIRS = [(a, b) for a in range(TOPK_P) for b in range(TOPK_P) if (a + 1) * (b + 1) <= TOPK_P]
_NPAIR = -(-len(_PAIRS) // 8) * 8


def _top16_ranks(s):
    n_idx = _iota(s.shape, 0)
    rank = jnp.full(s.shape, float(TOPK_P), f32)
    tops = []
    work = s
    for k in range(TOPK_P):
        m = jnp.max(work, axis=0, keepdims=True)
        idx = jnp.min(jnp.where(work == m, n_idx, N_KEYS), axis=0, keepdims=True)
        hit = n_idx == idx
        rank = jnp.where(hit, float(k), rank)
        work = jnp.where(hit, -jnp.inf, work)
        tops.append(m)
    return rank, jnp.concatenate(tops, axis=0)


def _peer_select_kernel(q_ref, keys_ref, pk1_ref, flat_ref, ci_ref, cnt_ref, e2_ref, r2_ref):
    tb = q_ref.shape[0]
    flat = flat_ref[...]
    for h in range(H_P):
        ranks, tops, scores = [], [], []
        for x in range(2):
            hx = 2 * h + x
            s = _dot(keys_ref[hx], q_ref[:, hx * 128:(hx + 1) * 128], NT, hi=True)
            rk, tp = _top16_ranks(s)
            ranks.append(rk), tops.append(tp), scores.append(s)
        t1, t2 = tops
        cand = jnp.concatenate([t1[a:a + 1] + t2[b:b + 1] for a, b in _PAIRS]
                               + [jnp.full((_NPAIR - len(_PAIRS), tb), -jnp.inf, f32)], axis=0)
        work = cand
        sel = jnp.zeros(cand.shape, f32)
        for _ in range(TOPK_P):
            m = jnp.max(work, axis=0, keepdims=True)
            idx = jnp.min(jnp.where(work == m, flat, 4096.0), axis=0, keepdims=True)
            hit = flat == idx
            sel = jnp.where(hit, 1.0, sel)
            work = jnp.where(hit, -jnp.inf, work)
        top = t1[0:1] + t2[0:1]
        z = jnp.sum(sel * jnp.exp(jnp.where(sel > 0, cand - top, 0.0)), axis=0, keepdims=True)
        cnt = _dot(pk1_ref[...], sel)
        cnt_i = jnp.zeros((N_KEYS, tb), f32)
        for k1 in range(TOPK_P):
            cnt_i = cnt_i + jnp.where(ranks[0] == float(k1), cnt[k1:k1 + 1], 0.0)
        ci_ref[0, h] = jnp.exp(scores[0] - t1[0:1]) / z
        cnt_ref[0, h] = cnt_i
        e2_ref[0, h] = jnp.exp(scores[1] - t2[0:1]).astype(bf16)
        r2_ref[0, h] = ranks[1].astype(bf16)


def _peer_select(q, keys):
    t = q.shape[0]
    tb, per = PEER_SEL_TB, PEER_TB // PEER_SEL_TB
    pk1 = np.zeros((TOPK_P, _NPAIR), np.float32)
    flat = np.full((_NPAIR, 1), 8192.0, np.float32)
    for r, (a, b) in enumerate(_PAIRS):
        pk1[a, r] = 1.0
        flat[r, 0] = a * TOPK_P + b
    out = pl.BlockSpec((1, H_P, N_KEYS, tb), lambda i: (i // per, 0, 0, i % per))
    shp = lambda dt: jax.ShapeDtypeStruct((t // PEER_TB, H_P, N_KEYS, PEER_TB), dt)
    return pl.pallas_call(
        _peer_select_kernel,
        grid=(t // tb,),
        in_specs=[pl.BlockSpec((tb, 2048), lambda i: (i, 0)), _const((2 * H_P, N_KEYS, 128)),
                  _const((TOPK_P, _NPAIR)), _const((_NPAIR, 1))],
        out_specs=[out, out, out, out],
        out_shape=[shp(f32), shp(f32), shp(bf16), shp(bf16)],
        compiler_params=_params("parallel"),
        name="peer_select",
    )(q, keys, jnp.asarray(pk1), jnp.asarray(flat))


def _sample_pre_kernel(qk_ref, tail_ref, xs_ref, bc_ref, d_ref, conv_ref, shift_ref,
                       wg2_ref, bg_ref, cw_ref, cb_ref, ex_ref, dtb_ref, alog_ref,
                       mu_ref, w0_ref, a0_ref, kk_ref, ka_ref, w2a2_ref, g2_ref,
                       gq_ref, gk_ref, ga_ref, act_ref, sdec_ref, sxd_ref, rw_ref):
    qk = qk_ref[...]
    la = _gla_gate_log(tail_ref[...], wg2_ref[...], bg_ref[...])
    gq_ref[...] = qk[:, :256] * (DK_A ** -0.5)
    gk_ref[...] = qk[:, 256:]
    ga_ref[...] = jnp.exp(la)
    u = jnp.concatenate([xs_ref[...], bc_ref[...]], axis=-1)
    cw = cw_ref[...]
    conv = cb_ref[...] + u * cw[CONV_W - 1:CONV_W]
    for j in range(CONV_W - 1):
        conv = conv + conv_ref[j] * cw[j:j + 1]
    act = _silu(conv)
    act_ref[...] = act
    dtx = _ssd_dt(tail_ref[...], ex_ref[...], dtb_ref[...])
    sdec_ref[...] = jnp.exp(dtx * (-jnp.exp(alog_ref[...])))
    sxd_ref[...] = act[:, :GROUP_W] * dtx
    r, w, kd, v, alpha, beta, g = _rwkv_pre(d_ref[...], shift_ref[...], mu_ref[...], w0_ref[...], a0_ref[...],
                                            kk_ref[...], ka_ref[...], w2a2_ref[...], g2_ref[...])
    for n, t in enumerate((r, jnp.exp(-jnp.exp(w)), kd, v, alpha, beta, g)):
        rw_ref[n] = t


def _sample_pre(cols, conv_st, shift_st, wts, row0, b):
    tb = b
    assert row0 % tb == 0
    r0 = row0 // tb
    cs = lambda w, start: pl.BlockSpec((tb, w), lambda i, _c=start // w: (r0, _c))
    full = lambda *s: jax.ShapeDtypeStruct(s, f32)
    return pl.pallas_call(
        _sample_pre_kernel,
        grid=(1,),
        in_specs=[cs(512, C_QKA), cs(128, C_TAIL), cs(512, C_XS), cs(256, C_BC), cs(1792, C_D),
                  _const((CONV_W - 1, b, XBC_W)), _const((b, 1792))] + [_const(w.shape) for w in wts],
        out_specs=[_const((b, 256))] * 3 + [_const((b, XBC_W)), _const((b, 512)), _const((b, 512)), _const((7, b, 512))],
        out_shape=[full(b, 256)] * 3 + [full(b, XBC_W), full(b, 512), full(b, 512), full(7, b, 512)],
        compiler_params=_params("arbitrary"),
        name="sample_pre",
    )(cols, cols, cols, cols, cols, conv_st, shift_st, *wts)


def _rows_to_tile(row, heads, width, reps):
    return jnp.concatenate([jnp.broadcast_to(row[:, h * width:(h + 1) * width], (reps, width)) for h in range(heads)], axis=0)


STEP_BPB = 8


def _gla_step_kernel(s_ref, a_ref, k_ref, q_ref, v_ref, sn_ref, o_ref):
    for j in range(s_ref.shape[0]):
        s = a_ref[j] * s_ref[j] + k_ref[j] * _rows_to_tile(v_ref[j], H_A, DV_A, DK_A)
        sn_ref[j] = s
        qs = q_ref[j] * s
        o_ref[j] = jnp.concatenate([jnp.sum(qs[h * DK_A:(h + 1) * DK_A], axis=0, keepdims=True) for h in range(H_A)],
                                   axis=-1)


def _ssd_step_kernel(h_ref, dec_ref, xd_ref, bc_ref, hn_ref, y_ref):
    reps = (H_B // G_B) * P_B
    for j in range(h_ref.shape[0]):
        bc = bc_ref[j]
        hn = dec_ref[j] * h_ref[j] + xd_ref[j] * _rows_to_tile(bc[:, :G_B * N_B], G_B, N_B, reps)
        hn_ref[j] = hn
        y_ref[j] = jnp.sum(hn * _rows_to_tile(bc[:, G_B * N_B:], G_B, N_B, reps), axis=-1, keepdims=True)


def _rwkv_step_kernel(s_ref, rows_ref, v_ref, sn_ref, o_ref):
    for j in range(s_ref.shape[0]):
        tile = lambda n, _j=j: _rows_to_tile(rows_ref[_j, n:n + 1, :], H_D, N_D, N_D)
        s = s_ref[j]
        sa = jnp.sum(s * tile(4), axis=-1, keepdims=True)
        s = s * tile(1) + sa * tile(5) + v_ref[j] * tile(2)
        sn_ref[j] = s
        o_ref[j] = jnp.sum(s * tile(0), axis=-1, keepdims=True)


def _state_step(kernel, name, state, ins, outs):
    b = state.shape[0]
    spec = lambda shp: pl.BlockSpec((STEP_BPB,) + tuple(shp[1:]), lambda i: (i, 0, 0))
    return pl.pallas_call(
        kernel,
        grid=(b // STEP_BPB,),
        in_specs=[spec(state.shape)] + [spec(a.shape) for a in ins],
        out_specs=[spec(state.shape)] + [spec(s) for s in outs],
        out_shape=[jax.ShapeDtypeStruct(state.shape, f32)] + [jax.ShapeDtypeStruct(s, f32) for s in outs],
        compiler_params=_params("parallel"),
        name=name,
    )(state, *ins)


def _rowhead_attend(kx, vx, q4, ind, n_maps, tail_rows=None):
    r = kx.shape[0]
    g = r // 8
    q8 = jnp.concatenate([q4, q4], axis=0)
    qt = jnp.broadcast_to(q8[None], (g, 8, 128)).reshape(r, 128)
    s3 = _dot(kx * qt, ind).reshape(g, 8, 128 * n_maps)
    if tail_rows is not None:
        last = jnp.where(_iota((1, 8, 128 * n_maps), 1) < tail_rows, s3[g - 1:g], -jnp.inf)
        s3 = jnp.concatenate([s3[:g - 1], last], axis=0)
    m8 = jnp.max(s3, axis=0)
    mh = jnp.maximum(m8, pltpu.roll(m8, 4, 0))
    p3 = jnp.exp(s3 - mh[None])
    l8 = jnp.sum(p3, axis=0)
    lh = l8 + pltpu.roll(l8, 4, 0)
    v3 = vx.reshape(g, 8, 128)
    outs = []
    for m in range(n_maps):
        ms = slice(m * 128, (m + 1) * 128)
        pv = jnp.sum(p3[:, :, ms] * v3, axis=0)
        pv = pv + pltpu.roll(pv, 4, 0)
        outs.append(pv / lh[:, ms])
    return outs


def _diff_decode_kernel(pt_ref, q_ref, ks_ref, vs_ref, *rest, lam_init, n_pages):
    del pt_ref
    k_refs, v_refs = rest[:n_pages], rest[n_pages:2 * n_pages]
    ind_ref, lam_ref, gn_ref, o_ref = rest[2 * n_pages:]
    own = lambda ref: jnp.concatenate([ref[0], ref[0]], axis=0)
    kx = jnp.concatenate([r[...] for r in k_refs] + [own(ks_ref)], axis=0)
    vx = jnp.concatenate([r[...] for r in v_refs] + [own(vs_ref)], axis=0)
    o1, o2 = _rowhead_attend(kx, vx, q_ref[0].astype(f32), ind_ref[...], 2, tail_rows=H_C)
    o_ref[0] = _diff_finish(o1, o2, lam_ref[...], gn_ref[...], lam_init)


def _diff_decode(pt_flat, layer, qb, ks, vs, ck, cv, ind, lam, gn, lam_init, n_pages):
    b = qb.shape[0]
    rows = ck.shape[2]
    row = pl.BlockSpec((1, H_C, 128), lambda i, pt: (i, 0, 0))
    page = lambda j: pl.BlockSpec((None, None, rows, 128), lambda i, pt, _j=j: (layer, pt[i * n_pages + _j], 0, 0))
    cst = lambda shp: pl.BlockSpec(shp, lambda i, pt: (0, 0))
    pages = [page(j) for j in range(n_pages)]
    return pl.pallas_call(
        functools.partial(_diff_decode_kernel, lam_init=lam_init, n_pages=n_pages),
        grid_spec=pltpu.PrefetchScalarGridSpec(
            num_scalar_prefetch=1, grid=(b,),
            in_specs=[row, row, row] + pages + pages + [cst((128, 256)), cst((1, 128)), cst((1, DV_C))],
            out_specs=pl.BlockSpec((1, 8, 128), lambda i, pt: (i, 0, 0))),
        out_shape=jax.ShapeDtypeStruct((b, 8, 128), f32),
        compiler_params=_params("parallel"),
        name="diff_decode",
    )(pt_flat, qb, ks, vs, *([ck] * n_pages), *([cv] * n_pages), ind, lam, gn)


def _sample_post_kernel(oa_ref, ga_ref, gn_a_ref, y_ref, act_ref, z_ref, dskip_ref, gn_b_ref, oc_ref,
                        od_ref, rw_ref, rk_ref, lg_ref, lb_ref, o_ref):
    oa = _gla_out(oa_ref[...], ga_ref[...], gn_a_ref[...])
    ob = _ssd_out(y_ref[...] + act_ref[:, :GROUP_W] * dskip_ref[...], z_ref[...], gn_b_ref[...])
    od = _rwkv_post(od_ref[...], rw_ref[0], rw_ref[2], rw_ref[3], rw_ref[6], rk_ref[...], lg_ref[...], lb_ref[...])
    o_ref[...] = jnp.concatenate([oa, ob, oc_ref[...], od], axis=-1)


def _sample_post(cols, oa, y, act, oc, od, rw, gn_a, dskip_x, gn_b, r_k, lnx_g, lnx_b, row0, b):
    r0 = row0 // b
    cs = lambda w, start: pl.BlockSpec((b, w), lambda i, _c=start // w: (r0, _c))
    c512 = _const((b, 512))
    v512 = _const((1, 512))
    return pl.pallas_call(
        _sample_post_kernel,
        grid=(1,),
        in_specs=[c512, cs(512, C_GA), _const((1, DV_A)), c512, _const((b, XBC_W)), cs(512, C_Z), v512, v512, c512,
                  c512, _const((7, b, 512)), v512, v512, v512],
        out_specs=_const((b, D_MODEL)),
        out_shape=jax.ShapeDtypeStruct((b, D_MODEL), f32),
        compiler_params=_params("arbitrary"),
        name="sample_post",
    )(oa, cols, gn_a, y, act, cols, dskip_x, gn_b, oc, od, rw, r_k, lnx_g, lnx_b)


def _head_rms(x, gain, width):
    outs = []
    for h in range(x.shape[1] // width):
        xh = x[:, h * width:(h + 1) * width]
        outs.append(xh * lax.rsqrt(jnp.mean(xh * xh, axis=-1, keepdims=True) + 1e-6) * gain)
    return jnp.concatenate(outs, axis=-1)


def _mem_kv_kernel(m_ref, g_ref, w_ref, gk_ref, k_ref, v_ref):
    m = m_ref[...]
    m = m * lax.rsqrt(jnp.mean(m * m, axis=-1, keepdims=True) + 1e-6) * g_ref[...]
    kv = _dot(m, w_ref[...])
    k_ref[...] = _head_rms(kv[:, :D_MEM], gk_ref[...], DH_M)
    v_ref[...] = kv[:, D_MEM:]


def _mem_kv(mem, g_src, w_kv, g_k):
    shp = jax.ShapeDtypeStruct((N_MEM, D_MEM), f32)
    return pl.pallas_call(
        _mem_kv_kernel, out_shape=[shp, shp],
        compiler_params=pltpu.CompilerParams(vmem_limit_bytes=VMEM_LIMIT), name="mem_kv",
    )(mem, g_src, w_kv, g_k)


def _mem_attn_prompt_kernel(q_ref, gq_ref, k_ref, v_ref, o_ref):
    q = _head_rms(q_ref[...], gq_ref[...], DH_M) * (DH_M ** -0.5)
    k, v = k_ref[...], v_ref[...]
    outs = []
    for h in range(H_M):
        hs = slice(h * DH_M, (h + 1) * DH_M)
        s = _dot(q[:, hs], k[:, hs], NT)
        p = jnp.exp(s - jnp.max(s, axis=-1, keepdims=True))
        outs.append(_dot(p, v[:, hs]) / jnp.sum(p, axis=-1, keepdims=True))
    o_ref[...] = jnp.concatenate(outs, axis=-1)


def _mem_attn_prompt(q, gq, k, v, seq=SEQ, tb=512):
    return pl.pallas_call(
        _mem_attn_prompt_kernel,
        grid=(seq // tb,),
        in_specs=[pl.BlockSpec((tb, D_MEM), lambda i: (i, 0)), _const((1, DH_M)), _const((N_MEM, D_MEM)),
                  _const((N_MEM, D_MEM))],
        out_specs=pl.BlockSpec((tb, D_MEM), lambda i: (i, 0)),
        out_shape=jax.ShapeDtypeStruct((seq, D_MEM), f32),
        compiler_params=_params("parallel"),
        name="mem_attn_prompt",
    )(q, gq, k, v)


def _mem_attn_sample_kernel(q_ref, gq_ref, k_ref, v_ref, ind_ref, o_ref):
    q = q_ref[0]
    q = q * lax.rsqrt(jnp.mean(q * q, axis=-1, keepdims=True) + 1e-6) * gq_ref[...] * (DH_M ** -0.5)
    o_ref[0] = _rowhead_attend(k_ref[...], v_ref[...], q, ind_ref[...], 1)[0]


def _mem_attn_sample(q, gq, ck, cv, ind, layer):
    b = q.shape[0]
    kv = pl.BlockSpec((None, None, N_MEM * H_M, DH_M), lambda i: (layer, i, 0, 0))
    return pl.pallas_call(
        _mem_attn_sample_kernel,
        grid=(b,),
        in_specs=[pl.BlockSpec((1, H_M, DH_M), lambda i: (i, 0, 0)), _const((1, DH_M)), kv, kv, _const((128, 128))],
        out_specs=pl.BlockSpec((1, 8, DH_M), lambda i: (i, 0, 0)),
        out_shape=jax.ShapeDtypeStruct((b, 8, DH_M), f32),
        compiler_params=_params("parallel"),
        name="mem_attn_sample",
    )(q, gq, ck, cv, ind)


def _cast_kernel(x_ref, o_ref):
    o_ref[...] = x_ref[...].astype(bf16)


def _table_bf16(tab, layer, rows=1024):
    n, d = tab.shape[1:]
    return pl.pallas_call(
        _cast_kernel,
        grid=(n // rows,),
        in_specs=[pl.BlockSpec((None, rows, d), lambda i: (layer, i, 0))],
        out_specs=pl.BlockSpec((rows, d), lambda i: (i, 0)),
        out_shape=jax.ShapeDtypeStruct((n, d), bf16),
        compiler_params=_params("parallel"),
        name="table_bf16",
    )(tab)


def _gelu(x):
    return 0.5 * x * (1.0 + jnp.tanh(0.7978845608028654 * (x + 0.044715 * x * x * x)))


def _peer_dense_kernel(xn_ref, u_ref, v_ref, ci_ref, cnt_ref, e2_ref, r2_ref, res_ref, o_ref, w_ref):
    e = pl.program_id(1)
    tb = xn_ref.shape[0]
    n_i = PEER_EB // N_KEYS

    @pl.when(e == 0)
    def _():
        o_ref[...] = res_ref[...]

    n_half = PEER_EB // PEER_SUB
    per = n_i // n_half
    parts = []
    for half in range(n_half):
        for ii in range(half * per, (half + 1) * per):
            w = jnp.zeros((N_KEYS, tb), bf16)
            for h in range(H_P):
                row = pl.ds(e * n_i + ii, 1)
                ci = ci_ref[0, h, row, :].astype(bf16)
                cnt = cnt_ref[0, h, row, :].astype(bf16)
                w = w + jnp.where(r2_ref[0, h] < cnt, e2_ref[0, h] * ci, jnp.zeros((), bf16))
            w_ref[ii * N_KEYS:(ii + 1) * N_KEYS, :] = w
        rows = slice(half * per * N_KEYS, (half + 1) * per * N_KEYS)
        hid = _gelu(_dot(u_ref[rows, :], xn_ref[...], NT))
        parts.append(_dot(hid.astype(bf16) * w_ref[rows, :], v_ref[rows, :], TN))
    o_ref[...] += sum(parts)


def _peer_dense(xn, u, v, sel, res):
    t = xn.shape[0]
    tb, eb = PEER_TB, PEER_EB
    once = pl.Buffered(1)
    selspec = pl.BlockSpec((1, H_P, N_KEYS, tb), lambda i, e: (i, 0, 0, 0), pipeline_mode=once)
    return pl.pallas_call(
        _peer_dense_kernel,
        grid=(t // tb, N_EXPERTS // eb),
        in_specs=[pl.BlockSpec((tb, D_MODEL), lambda i, e: (i, 0), pipeline_mode=once),
                  pl.BlockSpec((eb, D_MODEL), lambda i, e: (e, 0)),
                  pl.BlockSpec((eb, D_MODEL), lambda i, e: (e, 0)), selspec, selspec, selspec, selspec,
                  pl.BlockSpec((tb, D_MODEL), lambda i, e: (i, 0), pipeline_mode=once)],
        out_specs=pl.BlockSpec((tb, D_MODEL), lambda i, e: (i, 0)),
        out_shape=jax.ShapeDtypeStruct((t, D_MODEL), f32),
        scratch_shapes=[pltpu.VMEM((eb, tb), bf16)],
        compiler_params=_params("parallel", "arbitrary"),
        name="peer_dense",
    )(xn, u, v, *sel, res)


def _pad_w_in(w):
    a0, b0, c0, d0 = 0, 1552, 2840, 4376
    seg = lambda s, n: w[:, s:s + n]
    parts = [seg(d0, 1792), seg(b0 + 1024, 256), seg(a0, 512), seg(a0 + 512, 512), seg(a0 + 1024, 512),
             seg(b0, 512), seg(b0 + 512, 512), seg(c0, 512), seg(c0 + 512, 512), seg(c0 + 1024, 512),
             seg(a0 + 1536, 16), seg(b0 + 1280, 8), jnp.zeros((w.shape[0], IN_PAD - 6168), w.dtype)]
    return jnp.concatenate(parts, axis=1).astype(bf16)


def _layer_consts():
    ex = np.zeros((128, 512), np.float32)
    sel = np.zeros((8, 512), np.float32)
    for h in range(8):
        ex[GK_RANK + h, h * 64:(h + 1) * 64] = 1.0
        sel[h, h * 64] = 1.0
    ind2 = np.zeros((128, 256), np.float32)
    ind2[:DK_C, :128] = 1.0
    ind2[DK_C:, 128:] = 1.0
    ind1 = np.ones((128, 128), np.float32)
    return jnp.asarray(ex), jnp.asarray(sel), jnp.asarray(ind2), jnp.asarray(ind1)


def kernel(x_prompt, x_sample, cache_diff_k, cache_diff_v, cache_mem_k, cache_mem_v, state_gla, state_ssm, state_conv, state_rwkv, state_shift, page_table, mem_prompt, norm_mix, w_in, w_out, gla_wg2, gla_bg, gla_gn, conv_w, conv_b, dt_bias, a_log, d_skip, ssm_gn, dq_norm, dk_norm, lam_q, lam_k, diff_gn, shift_mu, w0, w2, a0, a2, g2, k_k, k_a, r_k, lnx_g, lnx_b, norm_mem, norm_memsrc, w_mq, w_mk, w_mv, w_mo, mq_norm, mk_norm, norm_ffn, peer_wq, peer_keys, peer_u, peer_v):
    nb = DEC_BATCH
    n_pages = page_table.shape[1]
    n_pool = cache_diff_k.shape[1]
    x = jnp.concatenate([x_prompt[0], x_sample[:, 0]], axis=0)
    pt_flat = page_table.reshape(-1)
    cos_p, sin_p = _rope_tables(jnp.arange(SEQ, dtype=jnp.int32))
    cos_s, sin_s = _rope_tables(jnp.full((nb,), PAST_LEN, jnp.int32))
    ex, sel8, ind2, ind1 = _layer_consts()
    ck_rows = cache_diff_k.reshape(DEPTH, n_pool, PAGE_SIZE * H_C, 2 * DK_C)
    cv_rows = cache_diff_v.reshape(DEPTH, n_pool, PAGE_SIZE * H_C, DV_C)
    mk_rows = cache_mem_k.reshape(DEPTH, nb, N_MEM * H_M, DH_M)
    mv_rows = cache_mem_v.reshape(DEPTH, nb, N_MEM * H_M, DH_M)
    row = lambda a: a.reshape(1, -1)
    rep64 = lambda a: jnp.repeat(a, 64).reshape(1, 512)
    outs = {n: [] for n in ('kp', 'vp', 'ks', 'vs', 'mk', 'mv', 'gla_p', 'gla_s', 'ssm_p', 'ssm_s', 'conv_p',
                            'conv_s', 'rwkv_p', 'rwkv_s', 'shift_p', 'shift_s')}
    for l in range(DEPTH):
        lam_init = 0.8 - 0.6 * math.exp(-0.3 * l)
        lq, lk = lam_q[l], lam_k[l]
        lam = jnp.exp(jnp.sum(lq[0] * lk[0])) - jnp.exp(jnp.sum(lq[1] * lk[1])) + lam_init
        lam = jnp.full((1, 128), lam, f32)
        wg2p = jnp.zeros((128, 256), f32).at[:GK_RANK].set(gla_wg2[l])
        w2a2 = jnp.zeros((128, 1024), f32).at[:64, :512].set(w2[l]).at[64:, 512:].set(a2[l])
        gq = jnp.tile(dq_norm[l].reshape(128), 4).reshape(1, 512)
        gk = jnp.tile(dk_norm[l].reshape(128), 4).reshape(1, 512)
        dtb_x, alog_x, dskip_x = rep64(dt_bias[l]), rep64(a_log[l]), rep64(d_skip[l])

        cols = _matmul(x, _pad_w_in(w_in[l]), gain=norm_mix[l])

        oa, gla_p = _gla_prompt(cols, wg2p, row(gla_bg[l]), row(gla_gn[l]), seq=SEQ)
        ob, ssm_p = _ssd_prompt(cols, conv_w[l], row(conv_b[l]), ex, dtb_x, alog_x, dskip_x, row(ssm_gn[l]), sel8,
                                seq=SEQ)
        qb, kf, kb = _diff_prep(cols, cos_p, sin_p, gq, gk, 0, SEQ, 512)
        oc = _diff_attn_prompt(qb, kb, cols, lam, row(diff_gn[l]), lam_init, seq=SEQ, tq=FLASH_TQ, tk=FLASH_TK)
        od, rwkv_p = _rwkv_prompt(cols, row(shift_mu[l]), row(w0[l]), row(a0[l]), row(k_k[l]), row(k_a[l]),
                                  row(r_k[l]), row(lnx_g[l]), row(lnx_b[l]), w2a2, g2[l], seq=SEQ)
        mix_p = jnp.concatenate([oa, ob, oc, od], axis=1)

        pre_w = [wg2p, row(gla_bg[l]), conv_w[l], row(conv_b[l]), ex, dtb_x, alog_x,
                 row(shift_mu[l]), row(w0[l]), row(a0[l]), row(k_k[l]), row(k_a[l]), w2a2, g2[l]]
        s_gq, s_gk, s_ga, s_act, s_dec, s_xd, s_rw = _sample_pre(cols, jnp.transpose(state_conv[l], (1, 0, 2)),
                                                                 state_shift[l][:, 0], pre_w, SEQ, nb)
        col = lambda a: a.reshape(nb, -1, 1)
        tail = cols[SEQ:]
        gla_s, oa_s = _state_step(_gla_step_kernel, "gla_step", state_gla[l].reshape(nb, H_A * DK_A, DV_A),
                                  [col(s_ga), col(s_gk), col(s_gq), tail[:, C_VA:C_VA + 512].reshape(nb, 1, 512)],
                                  [(nb, 1, 512)])
        ssm_s, y_s = _state_step(_ssd_step_kernel, "ssd_step", state_ssm[l].reshape(nb, H_B * P_B, N_B),
                                 [col(s_dec), col(s_xd), s_act[:, GROUP_W:].reshape(nb, 1, 256)], [(nb, H_B * P_B, 1)])
        rwkv_s, od_s = _state_step(_rwkv_step_kernel, "rwkv_step", state_rwkv[l].reshape(nb, H_D * N_D, N_D),
                                   [jnp.transpose(s_rw, (1, 0, 2)), col(s_rw[3])], [(nb, H_D * N_D, 1)])
        qb_s, kf_s, _ = _diff_prep(cols, cos_s, sin_s, gq, gk, SEQ, nb, nb)
        vc_s = tail[:, C_VC:C_VC + 512]
        oc_s = _diff_decode(pt_flat, l, qb_s.reshape(nb, H_C, 128), kf_s.reshape(nb, H_C, 128), vc_s.reshape(nb, H_C, 128),
                            ck_rows, cv_rows, ind2, lam, row(diff_gn[l]), lam_init, n_pages)
        mix_s = _sample_post(cols, oa_s.reshape(nb, 512), y_s.reshape(nb, 512), s_act, oc_s[:, :H_C].reshape(nb, 512),
                             od_s.reshape(nb, 512), s_rw, row(gla_gn[l]), dskip_x, row(ssm_gn[l]), row(r_k[l]),
                             row(lnx_g[l]), row(lnx_b[l]), SEQ, nb)

        x = _matmul(jnp.concatenate([mix_p, mix_s], axis=0), w_out[l].astype(bf16), res=x)

        mk_p, mv_p = _mem_kv(mem_prompt[0], row(norm_memsrc[l]),
                             jnp.concatenate([w_mk[l], w_mv[l]], axis=1).astype(bf16), row(mk_norm[l]))
        qm = _matmul(x, w_mq[l].astype(bf16), gain=norm_mem[l])
        om_p = _mem_attn_prompt(qm, row(mq_norm[l]), mk_p, mv_p, seq=SEQ)
        om_s = _mem_attn_sample(qm[SEQ:].reshape(nb, H_M, DH_M), row(mq_norm[l]), mk_rows, mv_rows, ind1, l)
        x = _matmul(jnp.concatenate([om_p, om_s[:, :H_M].reshape(nb, D_MEM)], axis=0), w_mo[l].astype(bf16), res=x)

        qp, xn = _matmul(x, peer_wq[l].astype(bf16), gain=norm_ffn[l], emit_xn=True, tn=512)
        picks = _peer_select(qp, peer_keys[l].reshape(2 * H_P, N_KEYS, D_PK // 2))
        x = _peer_dense(xn, _table_bf16(peer_u, l), _table_bf16(peer_v, l), picks, x)

        outs['kp'].append(kf.reshape(1, SEQ, H_C, 2 * DK_C))
        outs['vp'].append(cols[:SEQ, C_VC:C_VC + 512].reshape(1, SEQ, H_C, DV_C))
        outs['ks'].append(kf_s.reshape(nb, 1, H_C, 2 * DK_C))
        outs['vs'].append(vc_s.reshape(nb, 1, H_C, DV_C))
        outs['mk'].append(mk_p.reshape(1, N_MEM, H_M, DH_M))
        outs['mv'].append(mv_p.reshape(1, N_MEM, H_M, DH_M))
        outs['gla_p'].append(gla_p[None])
        outs['gla_s'].append(gla_s.reshape(nb, H_A, DK_A, DV_A))
        outs['ssm_p'].append(ssm_p[None])
        outs['ssm_s'].append(ssm_s.reshape(nb, H_B, P_B, N_B))
        u_p = jnp.concatenate([cols[SEQ - 3:SEQ, C_XS:C_XS + 512], cols[SEQ - 3:SEQ, C_BC:C_BC + 256]], axis=1)
        u_s = jnp.concatenate([tail[:, C_XS:C_XS + 512], tail[:, C_BC:C_BC + 256]], axis=1)
        outs['conv_p'].append(u_p[None])
        outs['conv_s'].append(jnp.concatenate([state_conv[l][:, 1:], u_s[:, None]], axis=1))
        outs['rwkv_p'].append(rwkv_p[None])
        outs['rwkv_s'].append(rwkv_s.reshape(nb, H_D, N_D, N_D))
        outs['shift_p'].append(cols[SEQ - 1:SEQ, C_D:C_D + 1792][None])
        outs['shift_s'].append(tail[:, C_D:C_D + 1792][:, None])
    st = {n: jnp.stack(v) for n, v in outs.items()}
    return (x[:SEQ][None], x[SEQ:][:, None], st['kp'], st['vp'], st['ks'], st['vs'], st['mk'], st['mv'],
            st['gla_p'], st['gla_s'], st['ssm_p'], st['ssm_s'], st['conv_p'], st['conv_s'],
            st['rwkv_p'], st['rwkv_s'], st['shift_p'], st['shift_s'])
```

```python
import functools
import math

import numpy as np
import jax
import jax.numpy as jnp
from jax import lax
from jax.experimental import pallas as pl
from jax.experimental.pallas import tpu as pltpu

f32 = jnp.float32
bf16 = jnp.bfloat16
HI = lax.Precision.HIGHEST

D_MODEL = 2048
SEQ = 8192
DEPTH = 2
DEC_BATCH = 128
PAST_LEN = 2048
PAGE_SIZE = 128
T_ALL = SEQ + DEC_BATCH

GROUP_W = 512
H_A, DK_A, DV_A, GK_RANK, GLA_TAU = 4, 64, 128, 16, 16.0
H_B, P_B, N_B, G_B, CONV_W, XBC_W = 8, 64, 64, 2, 4, 768
H_C, DK_C, DV_C, ROPE_THETA = 4, 64, 128, 10000.0
H_D, N_D, LNX_EPS = 8, 64, 64e-5
N_MEM, H_M, D_MEM, DH_M = 256, 4, 512, 128
N_KEYS, H_P, TOPK_P, D_PK = 128, 8, 16, 256
N_EXPERTS = N_KEYS * N_KEYS

C_D = 0
C_BC = 1792
C_QKA = 2048
C_VA = 2560
C_GA = 3072
C_Z = 3584
C_XS = 4096
C_QC = 4608
C_KC = 5120
C_VC = 5632
C_TAIL = 6144
IN_PAD = 6272

LANES = 128
VMEM_LIMIT = 56 * 1024 * 1024

NN = ((1,), (0,))
NT = ((1,), (1,))
TN = ((0,), (0,))


def _dot(a, b, dims=NN, hi=False):
    if hi:
        return lax.dot_general(a, b, (dims, ((), ())), precision=HI, preferred_element_type=f32)
    return lax.dot_general(a.astype(bf16), b.astype(bf16), (dims, ((), ())), preferred_element_type=f32)


def _softplus(x):
    return jnp.maximum(x, 0.0) + jnp.log(1.0 + jnp.exp(-jnp.abs(x)))


def _sigmoid(x):
    return 1.0 / (1.0 + jnp.exp(-x))


def _silu(x):
    return x * _sigmoid(x)


def _iota(shape, axis):
    return lax.broadcasted_iota(jnp.int32, shape, axis)


def _params(*sem):
    return pltpu.CompilerParams(dimension_semantics=sem, vmem_limit_bytes=VMEM_LIMIT)


def _mm_kernel(*refs, norm, residual, emit_xn):
    x_ref, g_ref, w_ref = refs[:3]
    res_ref = refs[3] if residual else None
    xn_ref = refs[-1]
    o_ref = refs[-3] if emit_xn else refs[-2]

    @pl.when(pl.program_id(1) == 0)
    def _():
        x = x_ref[...]
        if norm:
            x = x * lax.rsqrt(jnp.mean(x * x, axis=-1, keepdims=True) + 1e-6) * g_ref[...]
        xn_ref[...] = x.astype(bf16)
        if emit_xn:
            refs[-2][...] = xn_ref[...]

    acc = jnp.dot(xn_ref[...], w_ref[...], preferred_element_type=f32)
    if residual:
        acc = acc + res_ref[...]
    o_ref[...] = acc


def _matmul(x, w, gain=None, res=None, emit_xn=False, tm=None, tn=None):
    m, k = x.shape
    n = w.shape[1]
    tm = tm or _pick(m, (1040, 1024, 512, 256, 128))
    tn = tn or _pick(n, (1024, 896, 512, 256, 128))
    norm = gain is not None
    g = (gain if norm else jnp.ones((k,), f32)).reshape(1, k)
    args = [x, g, w]
    in_specs = [pl.BlockSpec((tm, k), lambda i, j: (i, 0)),
                pl.BlockSpec((1, k), lambda i, j: (0, 0)),
                pl.BlockSpec((k, tn), lambda i, j: (0, j))]
    if res is not None:
        args.append(res)
        in_specs.append(pl.BlockSpec((tm, tn), lambda i, j: (i, j)))
    out_specs = [pl.BlockSpec((tm, tn), lambda i, j: (i, j))]
    out_shape = [jax.ShapeDtypeStruct((m, n), f32)]
    if emit_xn:
        out_specs.append(pl.BlockSpec((tm, k), lambda i, j: (i, 0)))
        out_shape.append(jax.ShapeDtypeStruct((m, k), bf16))
    out = pl.pallas_call(
        functools.partial(_mm_kernel, norm=norm, residual=res is not None, emit_xn=emit_xn),
        grid=(m // tm, n // tn),
        in_specs=in_specs,
        out_specs=out_specs,
        out_shape=out_shape,
        scratch_shapes=[pltpu.VMEM((tm, k), bf16)],
        compiler_params=_params("parallel", "arbitrary"),
        name="mm",
    )(*args)
    return out if emit_xn else out[0]


def _pick(n, cands):
    for c in cands:
        if n % c == 0:
            return c
    return n


def _const(shape):
    nd = len(shape)
    return pl.BlockSpec(shape, lambda i, _n=nd: (0,) * _n)


def _cols(width, start, tb):
    assert start % width == 0
    return pl.BlockSpec((tb, width), lambda i, _c=start // width: (i, _c))


def _blockdiag_tri(tb, c):
    r = _iota((tb, tb), 0)
    s = _iota((tb, tb), 1)
    return jnp.where((r // c == s // c) & (s <= r), 1.0, 0.0).astype(f32)


def _segment_ones(n, seg):
    r = _iota((n, n), 0)
    s = _iota((n, n), 1)
    return jnp.where(r // seg == s // seg, 1.0, 0.0).astype(f32)


GLA_TB = 128
GLA_C = 16


def _gla_gate_log(tail, wg2p, bg):
    z = _dot(tail, wg2p, hi=True) + bg
    return -_softplus(-z) * (1.0 / GLA_TAU)


def _gla_out(o, g, gn):
    outs = []
    for h in range(H_A):
        oh = o[:, h * DV_A:(h + 1) * DV_A]
        oh = oh * lax.rsqrt(jnp.mean(oh * oh, axis=-1, keepdims=True) + 1e-6) * gn
        outs.append(oh * _silu(g[:, h * DV_A:(h + 1) * DV_A]))
    return jnp.concatenate(outs, axis=-1)


def _gla_kernel(qk_ref, v_ref, g_ref, tail_ref, wg2_ref, bg_ref, gn_ref, o_ref, sfin_ref, st_ref, w_ref):
    i = pl.program_id(0)

    @pl.when(i == 0)
    def _():
        st_ref[...] = jnp.zeros_like(st_ref)

    tb, c = GLA_TB, GLA_C
    qk = qk_ref[...]
    q = qk[:, :256] * (DK_A ** -0.5)
    k = qk[:, 256:]
    v = v_ref[...]
    la = _gla_gate_log(tail_ref[...], wg2_ref[...], bg_ref[...])
    b = _dot(_blockdiag_tri(tb, c), la, hi=True)
    e_r = _iota((256, 512), 0) // DK_A
    e_c = _iota((256, 512), 1) // DV_A
    expand = jnp.where(e_r == e_c, 1.0, 0.0).astype(bf16)
    s_idx = _iota((c, 256), 0)
    for j in range(tb // c):
        r0 = j * c
        qj, kj, bj, vj = q[r0:r0 + c], k[r0:r0 + c], b[r0:r0 + c], v[r0:r0 + c]
        for t in range(c):
            wt = qj[t:t + 1] * kj * jnp.exp(bj[t:t + 1] - bj)
            w_ref[t * c:(t + 1) * c, :] = jnp.where(s_idx <= t, wt, 0.0)
        att = _dot(w_ref[...], expand)
        o = jnp.sum(att.reshape(c, c, 512) * vj[None], axis=1)
        qe = qj * jnp.exp(bj)
        bl = bj[c - 1:c]
        ke = kj * jnp.exp(bl - bj)
        dl = jnp.exp(bl)
        inter = []
        for h in range(H_A):
            ks = slice(h * DK_A, (h + 1) * DK_A)
            st = st_ref[h]
            inter.append(_dot(qe[:, ks], st, NT))
            st_ref[h] = st * dl[:, ks] + _dot(vj[:, h * DV_A:(h + 1) * DV_A], ke[:, ks], TN)
        o = o + jnp.concatenate(inter, axis=-1)
        o_ref[r0:r0 + c, :] = _gla_out(o, g_ref[r0:r0 + c, :], gn_ref[...])

    @pl.when(i == pl.num_programs(0) - 1)
    def _():
        for h in range(H_A):
            sfin_ref[h] = st_ref[h].T


def _gla_prompt(cols, wg2p, bg, gn, seq=SEQ):
    tb = GLA_TB
    return pl.pallas_call(
        _gla_kernel,
        grid=(seq // tb,),
        in_specs=[_cols(512, C_QKA, tb), _cols(512, C_VA, tb), _cols(512, C_GA, tb), _cols(128, C_TAIL, tb),
                  _const((128, 256)), _const((1, 256)), _const((1, DV_A))],
        out_specs=[pl.BlockSpec((tb, 512), lambda i: (i, 0)), _const((H_A, DK_A, DV_A))],
        out_shape=[jax.ShapeDtypeStruct((seq, 512), f32), jax.ShapeDtypeStruct((H_A, DK_A, DV_A), f32)],
        scratch_shapes=[pltpu.VMEM((H_A, DV_A, DK_A), f32), pltpu.VMEM((GLA_C * GLA_C, 256), f32)],
        compiler_params=_params("arbitrary"),
        name="gla_prompt",
    )(cols, cols, cols, cols, wg2p, bg, gn)


SSD_TB = 128
SSD_C = 64


def _ssd_conv(ext, conv_w, conv_b, rows):
    out = conv_b
    for j in range(CONV_W):
        shifted = pltpu.roll(ext, j, 0) if j else ext
        out = out + shifted[8:8 + rows] * conv_w[CONV_W - 1 - j:CONV_W - j]
    return out


def _ssd_dt(tail, ex, dtb_x):
    return _softplus(_dot(tail, ex, hi=True) + dtb_x)


def _ssd_out(y, z, gn):
    y = y * _silu(z)
    w = GROUP_W // G_B
    outs = []
    for g in range(G_B):
        yg = y[:, g * w:(g + 1) * w]
        outs.append(yg * lax.rsqrt(jnp.mean(yg * yg, axis=-1, keepdims=True) + 1e-6) * gn[:, g * w:(g + 1) * w])
    return jnp.concatenate(outs, axis=-1)


def _ssd_kernel(z_ref, xs_ref, bc_ref, tail_ref, cw_ref, cb_ref, ex_ref, dtb_ref, alog_ref, dskip_ref, gn_ref,
                sel_ref, o_ref, hfin_ref, carry_ref, h_ref, y_ref):
    i = pl.program_id(0)

    @pl.when(i == 0)
    def _():
        carry_ref[...] = jnp.zeros_like(carry_ref)
        h_ref[...] = jnp.zeros_like(h_ref)

    tb, c = SSD_TB, SSD_C
    u = jnp.concatenate([xs_ref[...], bc_ref[...]], axis=-1)
    ext = jnp.concatenate([carry_ref[...], u], axis=0)
    carry_ref[...] = u[tb - 8:tb]
    act = _silu(_ssd_conv(ext, cw_ref[...], cb_ref[...], tb))
    xs = act[:, :GROUP_W]
    dtx = _ssd_dt(tail_ref[...], ex_ref[...], dtb_ref[...])
    la = dtx * (-jnp.exp(alog_ref[...]))
    b = _dot(_blockdiag_tri(tb, c), la, hi=True)
    brow = _dot(sel_ref[...], b, NT, hi=True)
    xd = xs * dtx
    tri = _iota((c, c), 1) <= _iota((c, c), 0)
    for ch in range(tb // c):
        r0 = ch * c
        rows = slice(r0, r0 + c)
        scores = []
        for g in range(G_B):
            bm = act[rows, GROUP_W + g * N_B:GROUP_W + (g + 1) * N_B]
            cm = act[rows, GROUP_W + G_B * N_B + g * N_B:GROUP_W + G_B * N_B + (g + 1) * N_B]
            scores.append((_dot(cm, bm, NT), bm, cm))
        for h in range(H_B):
            hs = slice(h * P_B, (h + 1) * P_B)
            sc, bm, cm = scores[h // (H_B // G_B)]
            bh = b[rows, hs]
            dec = jnp.where(tri, jnp.exp(bh - brow[h:h + 1, r0:r0 + c]), 0.0)
            xdh = xd[rows, hs]
            hst = h_ref[h]
            y = _dot(sc * dec, xdh) + _dot(cm, hst, NT) * jnp.exp(bh)
            bl = bh[c - 1:c]
            h_ref[h] = hst * jnp.exp(bl) + _dot(xdh * jnp.exp(bl - bh), bm, TN)
            y_ref[rows, hs] = y + xs[rows, hs] * dskip_ref[:, hs]
    o_ref[...] = _ssd_out(y_ref[...], z_ref[...], gn_ref[...])

    @pl.when(i == pl.num_programs(0) - 1)
    def _():
        hfin_ref[...] = h_ref[...]


def _ssd_prompt(cols, cw, cb, ex, dtb_x, alog_x, dskip_x, gn, sel, seq=SEQ):
    tb = SSD_TB
    return pl.pallas_call(
        _ssd_kernel,
        grid=(seq // tb,),
        in_specs=[_cols(512, C_Z, tb), _cols(512, C_XS, tb), _cols(256, C_BC, tb), _cols(128, C_TAIL, tb),
                  _const((CONV_W, XBC_W)), _const((1, XBC_W)), _const((128, 512)), _const((1, 512)), _const((1, 512)),
                  _const((1, 512)), _const((1, 512)), _const((8, 512))],
        out_specs=[pl.BlockSpec((tb, 512), lambda i: (i, 0)), _const((H_B, P_B, N_B))],
        out_shape=[jax.ShapeDtypeStruct((seq, 512), f32), jax.ShapeDtypeStruct((H_B, P_B, N_B), f32)],
        scratch_shapes=[pltpu.VMEM((8, XBC_W), f32), pltpu.VMEM((H_B, P_B, N_B), f32), pltpu.VMEM((tb, 512), f32)],
        compiler_params=_params("arbitrary"),
        name="ssd_prompt",
    )(cols, cols, cols, cols, cw, cb, ex, dtb_x, alog_x, dskip_x, gn, sel)


RWKV_TB = 128
RWKV_C = 64


def _rwkv_pre(x, xprev, mu, w0, a0, k_k, k_a, w2a2, g2):
    mixed = x + (xprev - x) * mu
    r = mixed[:, :512]
    kd = mixed[:, 512:1024]
    v = mixed[:, 1024:1536]
    lw = mixed[:, 1536:1664]
    lin = jnp.where(_iota(lw.shape, 1) < 64, jnp.tanh(lw), lw)
    wa = _dot(lin, w2a2, hi=True)
    w = -_softplus(-(w0 + wa[:, :512])) - 0.5
    a = _sigmoid(a0 + wa[:, 512:])
    g = _dot(_sigmoid(mixed[:, 1664:1792]), g2)
    kk = kd * k_k
    ss = _dot(kk * kk, _segment_ones(512, N_D), hi=True)
    kk = kk * lax.rsqrt(jnp.maximum(ss, 1e-24))
    kd = kd * (1.0 + (a - 1.0) * k_a)
    return r, w, kd, v, -kk, kk * a, g


def _rwkv_post(o, r, kd, v, g, r_k, lnx_g, lnx_b):
    seg = _segment_ones(512, N_D)
    mu = _dot(o, seg, hi=True) * (1.0 / N_D)
    d = o - mu
    var = _dot(d * d, seg, hi=True) * (1.0 / N_D)
    o = d * lax.rsqrt(var + LNX_EPS) * lnx_g + lnx_b
    o = o + _dot(r * kd * r_k, seg, hi=True) * v
    return o * g


def _rwkv_kernel(d_ref, mu_ref, w0_ref, a0_ref, kk_ref, ka_ref, rk_ref, lg_ref, lb_ref, w2a2_ref, g2_ref,
                 o_ref, sfin_ref, prev_ref, s_ref, oacc_ref):
    i = pl.program_id(0)

    @pl.when(i == 0)
    def _():
        prev_ref[...] = jnp.zeros_like(prev_ref)
        s_ref[...] = jnp.zeros_like(s_ref)

    tb, c = RWKV_TB, RWKV_C
    x = d_ref[...]
    xprev = jnp.where(_iota(x.shape, 0) == 0, prev_ref[0:1, :], pltpu.roll(x, 1, 0))
    prev_ref[0:1, :] = x[tb - 1:tb]
    r, w, kd, v, alpha, beta, g = _rwkv_pre(x, xprev, mu_ref[...], w0_ref[...], a0_ref[...], kk_ref[...], ka_ref[...],
                                            w2a2_ref[...], g2_ref[...])
    ld = -jnp.exp(w)
    cum = _dot(_blockdiag_tri(tb, c), ld, hi=True)
    at = alpha * jnp.exp(cum - ld)
    rt = r * jnp.exp(cum)
    einv = jnp.exp(-cum)
    kt = kd * einv
    bt = beta * einv
    ri = _iota((c, c), 0)
    ci = _iota((c, c), 1)
    strict, incl = ci < ri, ci <= ri
    eye = jnp.where(ri == ci, 1.0, 0.0).astype(f32)
    pairs = [(ch, h) for ch in range(tb // c) for h in range(H_D)]
    rows_of = lambda ch: slice(ch * c, (ch + 1) * c)
    lanes_of = lambda h: slice(h * N_D, (h + 1) * N_D)
    ar, kb, lk, mkb, tinv, p = {}, {}, {}, {}, {}, {}
    for ch, h in pairs:
        rows, hs = rows_of(ch), lanes_of(h)
        ar[ch, h] = jnp.concatenate([at[rows, hs], rt[rows, hs]], axis=0)
        kb[ch, h] = jnp.concatenate([kt[rows, hs], bt[rows, hs]], axis=0)
    for key in pairs:
        gram = _dot(ar[key], kb[key], NT)
        lk[key] = jnp.where(strict, gram[:c, :c], 0.0)
        p[key] = jnp.where(strict, gram[:c, c:], 0.0)
        mkb[key] = jnp.concatenate([jnp.where(incl, gram[c:, :c], 0.0), jnp.where(incl, gram[c:, c:], 0.0)], axis=1)
        tinv[key] = eye + p[key]
    for _ in range(5):
        for key in pairs:
            p[key] = _dot(p[key], p[key])
        for key in pairs:
            tinv[key] = tinv[key] + _dot(tinv[key], p[key])
    for ch in range(tb // c):
        rows = rows_of(ch)
        cl = cum[ch * c + c - 1:ch * c + c]
        efin = jnp.exp(cl - cum[rows])
        kfin = kd[rows] * efin
        bfin = beta[rows] * efin
        dfin = jnp.exp(cl)
        heads = range(H_D)
        s0 = [s_ref[h] for h in heads]
        ars = [_dot(ar[ch, h], s0[h], NT) for h in heads]
        lkv = [_dot(lk[ch, h], v[rows, lanes_of(h)]) for h in heads]
        u = [_dot(tinv[ch, h], ars[h][:c] + lkv[h]) for h in heads]
        vu = [jnp.concatenate([v[rows, lanes_of(h)], u[h]], axis=0) for h in heads]
        for h in heads:
            hs = lanes_of(h)
            oacc_ref[rows, hs] = ars[h][c:] + _dot(mkb[ch, h], vu[h])
            kbfin = jnp.concatenate([kfin[:, hs], bfin[:, hs]], axis=0)
            s_ref[h] = s0[h] * dfin[:, hs] + _dot(vu[h], kbfin, TN)
    o_ref[...] = _rwkv_post(oacc_ref[...], r, kd, v, g, rk_ref[...], lg_ref[...], lb_ref[...])

    @pl.when(i == pl.num_programs(0) - 1)
    def _():
        sfin_ref[...] = s_ref[...]


def _rwkv_prompt(cols, mu, w0, a0, k_k, k_a, r_k, lnx_g, lnx_b, w2a2, g2, seq=SEQ):
    tb = RWKV_TB
    vec = _const((1, 512))
    return pl.pallas_call(
        _rwkv_kernel,
        grid=(seq // tb,),
        in_specs=[_cols(1792, C_D, tb), _const((1, 1792)), vec, vec, vec, vec, vec, vec, vec,
                  _const((128, 1024)), _const((128, 512))],
        out_specs=[pl.BlockSpec((tb, 512), lambda i: (i, 0)), _const((H_D, N_D, N_D))],
        out_shape=[jax.ShapeDtypeStruct((seq, 512), f32), jax.ShapeDtypeStruct((H_D, N_D, N_D), f32)],
        scratch_shapes=[pltpu.VMEM((8, 1792), f32), pltpu.VMEM((H_D, N_D, N_D), f32), pltpu.VMEM((tb, 512), f32)],
        compiler_params=_params("arbitrary"),
        name="rwkv_prompt",
    )(cols, mu, w0, a0, k_k, k_a, r_k, lnx_g, lnx_b, w2a2, g2)


def _rope_tables(pos):
    half = DK_C // 2
    inv = ROPE_THETA ** (-jnp.arange(half, dtype=f32) / half)
    ang = pos.astype(f32)[:, None] * inv[None, :]
    cos, sin = jnp.cos(ang), jnp.sin(ang)
    return jnp.tile(jnp.concatenate([cos, cos], axis=-1), (1, 2)), jnp.tile(jnp.concatenate([-sin, sin], axis=-1), (1, 2))


def _qk_norm_rope(x, gain, cos, sin):
    ms = _dot(x * x, _segment_ones(512, DK_C), hi=True) * (1.0 / DK_C)
    x = x * lax.rsqrt(ms + 1e-6) * gain
    first = (_iota(x.shape, 1) % DK_C) < (DK_C // 2)
    partner = jnp.where(first, pltpu.roll(x, 512 - DK_C // 2, 1), pltpu.roll(x, DK_C // 2, 1))
    cos = jnp.concatenate([cos] * 4, axis=-1)
    sin = jnp.concatenate([sin] * 4, axis=-1)
    return x * cos + partner * sin


def _diff_prep_kernel(q_ref, k_ref, cos_ref, sin_ref, gq_ref, gk_ref, qb_ref, kf_ref, kb_ref):
    cos, sin = cos_ref[...], sin_ref[...]
    q = _qk_norm_rope(q_ref[...], gq_ref[...], cos, sin)
    k = _qk_norm_rope(k_ref[...], gk_ref[...], cos, sin)
    qb_ref[...] = (q * (DK_C ** -0.5)).astype(bf16)
    kf_ref[...] = k
    kb_ref[...] = k.astype(bf16)


def _diff_prep(cols, cos, sin, gq, gk, row0, rows, tb):
    assert row0 % tb == 0
    r0 = row0 // tb
    colspec = lambda start: pl.BlockSpec((tb, 512), lambda i, _c=start // 512: (i + r0, _c))
    out = pl.BlockSpec((tb, 512), lambda i: (i, 0))
    tab = pl.BlockSpec((tb, 128), lambda i: (i, 0))
    return pl.pallas_call(
        _diff_prep_kernel,
        grid=(rows // tb,),
        in_specs=[colspec(C_QC), colspec(C_KC), tab, tab, _const((1, 512)), _const((1, 512))],
        out_specs=[out, out, out],
        out_shape=[jax.ShapeDtypeStruct((rows, 512), bf16), jax.ShapeDtypeStruct((rows, 512), f32),
                   jax.ShapeDtypeStruct((rows, 512), bf16)],
        compiler_params=_params("parallel"),
        name="diff_prep",
    )(cols, cols, cos, sin, gq, gk)


def _diff_finish(o1, o2, lam, gn, lam_init):
    o = o1 - lam * o2
    return o * lax.rsqrt(jnp.mean(o * o, axis=-1, keepdims=True) + 1e-6) * gn * (1.0 - lam_init)


FLASH_TQ = 1024
FLASH_TK = 512


def _flash_kernel(qi_ref, kj_ref, q_ref, k_ref, v_ref, lam_ref, gn_ref, o_ref, m_ref, l_ref, acc_ref,
                  *, tq, tk, lam_init):
    step_id = pl.program_id(1)
    qi, kj = qi_ref[step_id], kj_ref[step_id]

    @pl.when(kj == 0)
    def _():
        m_ref[...] = jnp.full_like(m_ref, -jnp.inf)
        l_ref[...] = jnp.zeros_like(l_ref)
        acc_ref[...] = jnp.zeros_like(acc_ref)

    def step(masked):
        q = q_ref[...]
        k = k_ref[...]
        v = v_ref[...].astype(bf16)
        lane = _iota(q.shape, 1)
        if masked:
            keep = (kj * tk + _iota((tk, tq), 0)) <= (qi * tq + _iota((tk, tq), 1))
        for m in range(2):
            qm = jnp.where((lane < DK_C) if m == 0 else (lane >= DK_C), q, jnp.zeros_like(q))
            s = _dot(k, qm, NT)
            if masked:
                s = jnp.where(keep, s, -jnp.inf)
            m_old = m_ref[m]
            m_new = jnp.maximum(m_old, jnp.max(s, axis=0, keepdims=True))
            p = jnp.exp(s - m_new)
            corr = jnp.exp(m_old - m_new)
            l_ref[m] = corr * l_ref[m] + jnp.sum(p, axis=0, keepdims=True)
            acc_ref[m] = corr * acc_ref[m] + _dot(v, p, TN)
            m_ref[m] = m_new

    last = kj * tk + tk - 1

    @pl.when(last <= qi * tq)
    def _():
        step(False)

    @pl.when(last > qi * tq)
    def _():
        step(True)

    @pl.when(kj == (qi * tq + tq - 1) // tk)
    def _():
        o1 = (acc_ref[0] / l_ref[0]).T
        o2 = (acc_ref[1] / l_ref[1]).T
        o_ref[...] = _diff_finish(o1, o2, lam_ref[...], gn_ref[...], lam_init)


def _diff_attn_prompt(qb, kb, cols, lam, gn, lam_init, seq=SEQ, tq=512, tk=512):
    pairs = [(i, j) for i in range(seq // tq) for j in range((i * tq + tq - 1) // tk + 1)]
    qi_tab = jnp.asarray([p[0] for p in pairs], jnp.int32)
    kj_tab = jnp.asarray([p[1] for p in pairs], jnp.int32)
    cst = lambda shp: pl.BlockSpec(shp, lambda h, s, qi, kj: (0, 0))
    return pl.pallas_call(
        functools.partial(_flash_kernel, tq=tq, tk=tk, lam_init=lam_init),
        grid_spec=pltpu.PrefetchScalarGridSpec(
            num_scalar_prefetch=2, grid=(H_C, len(pairs)),
            in_specs=[pl.BlockSpec((tq, 128), lambda h, s, qi, kj: (qi[s], h)),
                      pl.BlockSpec((tk, 128), lambda h, s, qi, kj: (kj[s], h)),
                      pl.BlockSpec((tk, DV_C), lambda h, s, qi, kj: (kj[s], C_VC // DV_C + h)),
                      cst((1, 128)), cst((1, DV_C))],
            out_specs=pl.BlockSpec((tq, DV_C), lambda h, s, qi, kj: (qi[s], h)),
            scratch_shapes=[pltpu.VMEM((2, 1, tq), f32), pltpu.VMEM((2, 1, tq), f32), pltpu.VMEM((2, DV_C, tq), f32)]),
        out_shape=jax.ShapeDtypeStruct((seq, 512), f32),
        compiler_params=_params("parallel", "arbitrary"),
        name="diff_attn_prompt",
    )(qi_tab, kj_tab, qb, kb, cols, lam, gn)


PEER_SEL_TB = 128
PEER_TB = 640
PEER_EB = 512
PEER_SUB = 256
_PAIRS = [(a, b) for a in range(TOPK_P) for b in range(TOPK_P) if (a + 1) * (b + 1) <= TOPK_P]
_NPAIR = -(-len(_PAIRS) // 8) * 8


def _top16(s, n_idx, want_rank):
    rank = jnp.full(s.shape, float(TOPK_P), f32) if want_rank else None
    tops, idxs = [], []
    work = s
    for k in range(TOPK_P):
        m = jnp.max(work, axis=0, keepdims=True)
        idx = jnp.min(jnp.where(work == m, n_idx, float(N_KEYS)), axis=0, keepdims=True)
        hit = n_idx == idx
        if want_rank:
            rank = jnp.where(hit, float(k), rank)
        work = jnp.where(hit, -jnp.inf, work)
        tops.append(m)
        idxs.append(idx)
    return rank, jnp.concatenate(tops, axis=0), jnp.concatenate(idxs, axis=0)


def _peer_select_kernel(q_ref, keys_ref, pk1_ref, flat_ref, ci_ref, cnt_ref, e2_ref, r2_ref):
    tb = q_ref.shape[0]
    flat = flat_ref[...]
    n_idx = _iota((N_KEYS, tb), 0).astype(f32)
    for h in range(H_P):
        scores = [_dot(keys_ref[hx], q_ref[:, hx * 128:(hx + 1) * 128], NT, hi=True)
                  for hx in (2 * h, 2 * h + 1)]
        _, t1, idx1 = _top16(scores[0], n_idx, False)
        rank2, t2, _ = _top16(scores[1], n_idx, True)
        cand = jnp.concatenate([t1[a:a + 1] + t2[b:b + 1] for a, b in _PAIRS]
                               + [jnp.full((_NPAIR - len(_PAIRS), tb), -jnp.inf, f32)], axis=0)
        work = cand
        sel = jnp.zeros(cand.shape, f32)
        for _ in range(TOPK_P):
            m = jnp.max(work, axis=0, keepdims=True)
            idx = jnp.min(jnp.where(work == m, flat, 4096.0), axis=0, keepdims=True)
            hit = flat == idx
            sel = jnp.where(hit, 1.0, sel)
            work = jnp.where(hit, -jnp.inf, work)
        top = t1[0:1] + t2[0:1]
        z = jnp.sum(sel * jnp.exp(jnp.where(sel > 0, cand - top, 0.0)), axis=0, keepdims=True)
        cnt = _dot(pk1_ref[...], sel)
        cnt_i = jnp.zeros((N_KEYS, tb), f32)
        for k1 in range(TOPK_P):
            cnt_i = jnp.where(n_idx == idx1[k1:k1 + 1], cnt[k1:k1 + 1], cnt_i)
        ci_ref[0, h] = jnp.exp(scores[0] - t1[0:1]) / z
        cnt_ref[0, h] = cnt_i
        e2_ref[0, h] = jnp.exp(scores[1] - t2[0:1]).astype(bf16)
        r2_ref[0, h] = rank2.astype(bf16)


def _peer_select(q, keys):
    t = q.shape[0]
    tb, per = PEER_SEL_TB, PEER_TB // PEER_SEL_TB
    pk1 = np.zeros((TOPK_P, _NPAIR), np.float32)
    flat = np.full((_NPAIR, 1), 8192.0, np.float32)
    for r, (a, b) in enumerate(_PAIRS):
        pk1[a, r] = 1.0
        flat[r, 0] = a * TOPK_P + b
    out = pl.BlockSpec((1, H_P, N_KEYS, tb), lambda i: (i // per, 0, 0, i % per))
    shp = lambda dt: jax.ShapeDtypeStruct((t // PEER_TB, H_P, N_KEYS, PEER_TB), dt)
    return pl.pallas_call(
        _peer_select_kernel,
        grid=(t // tb,),
        in_specs=[pl.BlockSpec((tb, 2048), lambda i: (i, 0)), _const((2 * H_P, N_KEYS, 128)),
                  _const((TOPK_P, _NPAIR)), _const((_NPAIR, 1))],
        out_specs=[out, out, out, out],
        out_shape=[shp(f32), shp(f32), shp(bf16), shp(bf16)],
        compiler_params=_params("parallel"),
        name="peer_select",
    )(q, keys, jnp.asarray(pk1), jnp.asarray(flat))


def _sample_pre_kernel(qk_ref, tail_ref, xs_ref, bc_ref, d_ref, conv_ref, shift_ref,
                       wg2_ref, bg_ref, cw_ref, cb_ref, ex_ref, dtb_ref, alog_ref,
                       mu_ref, w0_ref, a0_ref, kk_ref, ka_ref, w2a2_ref, g2_ref,
                       gq_ref, gk_ref, ga_ref, act_ref, sdec_ref, sxd_ref, rw_ref):
    qk = qk_ref[...]
    la = _gla_gate_log(tail_ref[...], wg2_ref[...], bg_ref[...])
    gq_ref[...] = qk[:, :256] * (DK_A ** -0.5)
    gk_ref[...] = qk[:, 256:]
    ga_ref[...] = jnp.exp(la)
    u = jnp.concatenate([xs_ref[...], bc_ref[...]], axis=-1)
    cw = cw_ref[...]
    conv = cb_ref[...] + u * cw[CONV_W - 1:CONV_W]
    for j in range(CONV_W - 1):
        conv = conv + conv_ref[j] * cw[j:j + 1]
    act = _silu(conv)
    act_ref[...] = act
    dtx = _ssd_dt(tail_ref[...], ex_ref[...], dtb_ref[...])
    sdec_ref[...] = jnp.exp(dtx * (-jnp.exp(alog_ref[...])))
    sxd_ref[...] = act[:, :GROUP_W] * dtx
    r, w, kd, v, alpha, beta, g = _rwkv_pre(d_ref[...], shift_ref[...], mu_ref[...], w0_ref[...], a0_ref[...],
                                            kk_ref[...], ka_ref[...], w2a2_ref[...], g2_ref[...])
    for n, t in enumerate((r, jnp.exp(-jnp.exp(w)), kd, v, alpha, beta, g)):
        rw_ref[n] = t


def _sample_pre(cols, conv_st, shift_st, wts, row0, b):
    tb = b
    assert row0 % tb == 0
    r0 = row0 // tb
    cs = lambda w, start: pl.BlockSpec((tb, w), lambda i, _c=start // w: (r0, _c))
    full = lambda *s: jax.ShapeDtypeStruct(s, f32)
    return pl.pallas_call(
        _sample_pre_kernel,
        grid=(1,),
        in_specs=[cs(512, C_QKA), cs(128, C_TAIL), cs(512, C_XS), cs(256, C_BC), cs(1792, C_D),
                  _const((CONV_W - 1, b, XBC_W)), _const((b, 1792))] + [_const(w.shape) for w in wts],
        out_specs=[_const((b, 256))] * 3 + [_const((b, XBC_W)), _const((b, 512)), _const((b, 512)), _const((7, b, 512))],
        out_shape=[full(b, 256)] * 3 + [full(b, XBC_W), full(b, 512), full(b, 512), full(7, b, 512)],
        compiler_params=_params("arbitrary"),
        name="sample_pre",
    )(cols, cols, cols, cols, cols, conv_st, shift_st, *wts)


def _rows_to_tile(row, heads, width, reps):
    return jnp.concatenate([jnp.broadcast_to(row[:, h * width:(h + 1) * width], (reps, width)) for h in range(heads)], axis=0)


STEP_BPB = 8


def _gla_step_kernel(s_ref, a_ref, k_ref, q_ref, v_ref, sn_ref, o_ref):
    for j in range(s_ref.shape[0]):
        s = a_ref[j] * s_ref[j] + k_ref[j] * _rows_to_tile(v_ref[j], H_A, DV_A, DK_A)
        sn_ref[j] = s
        qs = q_ref[j] * s
        o_ref[j] = jnp.concatenate([jnp.sum(qs[h * DK_A:(h + 1) * DK_A], axis=0, keepdims=True) for h in range(H_A)],
                                   axis=-1)


def _ssd_step_kernel(h_ref, dec_ref, xd_ref, bc_ref, hn_ref, y_ref):
    reps = (H_B // G_B) * P_B
    for j in range(h_ref.shape[0]):
        bc = bc_ref[j]
        hn = dec_ref[j] * h_ref[j] + xd_ref[j] * _rows_to_tile(bc[:, :G_B * N_B], G_B, N_B, reps)
        hn_ref[j] = hn
        y_ref[j] = jnp.sum(hn * _rows_to_tile(bc[:, G_B * N_B:], G_B, N_B, reps), axis=-1, keepdims=True)


def _rwkv_step_kernel(s_ref, rows_ref, v_ref, sn_ref, o_ref):
    for j in range(s_ref.shape[0]):
        tile = lambda n, _j=j: _rows_to_tile(rows_ref[_j, n:n + 1, :], H_D, N_D, N_D)
        s = s_ref[j]
        sa = jnp.sum(s * tile(4), axis=-1, keepdims=True)
        s = s * tile(1) + sa * tile(5) + v_ref[j] * tile(2)
        sn_ref[j] = s
        o_ref[j] = jnp.sum(s * tile(0), axis=-1, keepdims=True)


def _state_step(kernel, name, state, ins, outs):
    b = state.shape[0]
    spec = lambda shp: pl.BlockSpec((STEP_BPB,) + tuple(shp[1:]), lambda i: (i, 0, 0))
    return pl.pallas_call(
        kernel,
        grid=(b // STEP_BPB,),
        in_specs=[spec(state.shape)] + [spec(a.shape) for a in ins],
        out_specs=[spec(state.shape)] + [spec(s) for s in outs],
        out_shape=[jax.ShapeDtypeStruct(state.shape, f32)] + [jax.ShapeDtypeStruct(s, f32) for s in outs],
        compiler_params=_params("parallel"),
        name=name,
    )(state, *ins)


def _rowhead_attend(kx, vx, q4, ind, n_maps, tail_rows=None):
    r = kx.shape[0]
    g = r // 8
    q8 = jnp.concatenate([q4, q4], axis=0)
    qt = jnp.broadcast_to(q8[None], (g, 8, 128)).reshape(r, 128)
    s3 = _dot(kx * qt, ind).reshape(g, 8, 128 * n_maps)
    if tail_rows is not None:
        last = jnp.where(_iota((1, 8, 128 * n_maps), 1) < tail_rows, s3[g - 1:g], -jnp.inf)
        s3 = jnp.concatenate([s3[:g - 1], last], axis=0)
    m8 = jnp.max(s3, axis=0)
    mh = jnp.maximum(m8, pltpu.roll(m8, 4, 0))
    p3 = jnp.exp(s3 - mh[None])
    l8 = jnp.sum(p3, axis=0)
    lh = l8 + pltpu.roll(l8, 4, 0)
    v3 = vx.reshape(g, 8, 128)
    outs = []
    for m in range(n_maps):
        ms = slice(m * 128, (m + 1) * 128)
        pv = jnp.sum(p3[:, :, ms] * v3, axis=0)
        pv = pv + pltpu.roll(pv, 4, 0)
        outs.append(pv / lh[:, ms])
    return outs


def _diff_decode_kernel(pt_ref, q_ref, ks_ref, vs_ref, *rest, lam_init, n_pages):
    del pt_ref
    k_refs, v_refs = rest[:n_pages], rest[n_pages:2 * n_pages]
    ind_ref, lam_ref, gn_ref, o_ref = rest[2 * n_pages:]
    own = lambda ref: jnp.concatenate([ref[0], ref[0]], axis=0)
    kx = jnp.concatenate([r[...] for r in k_refs] + [own(ks_ref)], axis=0)
    vx = jnp.concatenate([r[...] for r in v_refs] + [own(vs_ref)], axis=0)
    o1, o2 = _rowhead_attend(kx, vx, q_ref[0].astype(f32), ind_ref[...], 2, tail_rows=H_C)
    o_ref[0] = _diff_finish(o1, o2, lam_ref[...], gn_ref[...], lam_init)


def _diff_decode(pt_flat, layer, qb, ks, vs, ck, cv, ind, lam, gn, lam_init, n_pages):
    b = qb.shape[0]
    rows = ck.shape[2]
    row = pl.BlockSpec((1, H_C, 128), lambda i, pt: (i, 0, 0))
    page = lambda j: pl.BlockSpec((None, None, rows, 128), lambda i, pt, _j=j: (layer, pt[i * n_pages + _j], 0, 0))
    cst = lambda shp: pl.BlockSpec(shp, lambda i, pt: (0, 0))
    pages = [page(j) for j in range(n_pages)]
    return pl.pallas_call(
        functools.partial(_diff_decode_kernel, lam_init=lam_init, n_pages=n_pages),
        grid_spec=pltpu.PrefetchScalarGridSpec(
            num_scalar_prefetch=1, grid=(b,),
            in_specs=[row, row, row] + pages + pages + [cst((128, 256)), cst((1, 128)), cst((1, DV_C))],
            out_specs=pl.BlockSpec((1, 8, 128), lambda i, pt: (i, 0, 0))),
        out_shape=jax.ShapeDtypeStruct((b, 8, 128), f32),
        compiler_params=_params("parallel"),
        name="diff_decode",
    )(pt_flat, qb, ks, vs, *([ck] * n_pages), *([cv] * n_pages), ind, lam, gn)


def _sample_post_kernel(oa_ref, ga_ref, gn_a_ref, y_ref, act_ref, z_ref, dskip_ref, gn_b_ref, oc_ref,
                        od_ref, rw_ref, rk_ref, lg_ref, lb_ref, o_ref):
    oa = _gla_out(oa_ref[...], ga_ref[...], gn_a_ref[...])
    ob = _ssd_out(y_ref[...] + act_ref[:, :GROUP_W] * dskip_ref[...], z_ref[...], gn_b_ref[...])
    od = _rwkv_post(od_ref[...], rw_ref[0], rw_ref[2], rw_ref[3], rw_ref[6], rk_ref[...], lg_ref[...], lb_ref[...])
    o_ref[...] = jnp.concatenate([oa, ob, oc_ref[...], od], axis=-1)


def _sample_post(cols, oa, y, act, oc, od, rw, gn_a, dskip_x, gn_b, r_k, lnx_g, lnx_b, row0, b):
    r0 = row0 // b
    cs = lambda w, start: pl.BlockSpec((b, w), lambda i, _c=start // w: (r0, _c))
    c512 = _const((b, 512))
    v512 = _const((1, 512))
    return pl.pallas_call(
        _sample_post_kernel,
        grid=(1,),
        in_specs=[c512, cs(512, C_GA), _const((1, DV_A)), c512, _const((b, XBC_W)), cs(512, C_Z), v512, v512, c512,
                  c512, _const((7, b, 512)), v512, v512, v512],
        out_specs=_const((b, D_MODEL)),
        out_shape=jax.ShapeDtypeStruct((b, D_MODEL), f32),
        compiler_params=_params("arbitrary"),
        name="sample_post",
    )(oa, cols, gn_a, y, act, cols, dskip_x, gn_b, oc, od, rw, r_k, lnx_g, lnx_b)


def _head_rms(x, gain, width):
    outs = []
    for h in range(x.shape[1] // width):
        xh = x[:, h * width:(h + 1) * width]
        outs.append(xh * lax.rsqrt(jnp.mean(xh * xh, axis=-1, keepdims=True) + 1e-6) * gain)
    return jnp.concatenate(outs, axis=-1)


def _mem_kv_kernel(m_ref, g_ref, w_ref, gk_ref, k_ref, v_ref):
    m = m_ref[...]
    m = m * lax.rsqrt(jnp.mean(m * m, axis=-1, keepdims=True) + 1e-6) * g_ref[...]
    kv = _dot(m, w_ref[...])
    k_ref[...] = _head_rms(kv[:, :D_MEM], gk_ref[...], DH_M)
    v_ref[...] = kv[:, D_MEM:]


def _mem_kv(mem, g_src, w_kv, g_k):
    shp = jax.ShapeDtypeStruct((N_MEM, D_MEM), f32)
    return pl.pallas_call(
        _mem_kv_kernel, out_shape=[shp, shp],
        compiler_params=pltpu.CompilerParams(vmem_limit_bytes=VMEM_LIMIT), name="mem_kv",
    )(mem, g_src, w_kv, g_k)


def _mem_attn_prompt_kernel(q_ref, gq_ref, k_ref, v_ref, o_ref):
    q = _head_rms(q_ref[...], gq_ref[...], DH_M) * (DH_M ** -0.5)
    k, v = k_ref[...], v_ref[...]
    outs = []
    for h in range(H_M):
        hs = slice(h * DH_M, (h + 1) * DH_M)
        s = _dot(q[:, hs], k[:, hs], NT)
        p = jnp.exp(s - jnp.max(s, axis=-1, keepdims=True))
        outs.append(_dot(p, v[:, hs]) / jnp.sum(p, axis=-1, keepdims=True))
    o_ref[...] = jnp.concatenate(outs, axis=-1)


def _mem_attn_prompt(q, gq, k, v, seq=SEQ, tb=512):
    return pl.pallas_call(
        _mem_attn_prompt_kernel,
        grid=(seq // tb,),
        in_specs=[pl.BlockSpec((tb, D_MEM), lambda i: (i, 0)), _const((1, DH_M)), _const((N_MEM, D_MEM)),
                  _const((N_MEM, D_MEM))],
        out_specs=pl.BlockSpec((tb, D_MEM), lambda i: (i, 0)),
        out_shape=jax.ShapeDtypeStruct((seq, D_MEM), f32),
        compiler_params=_params("parallel"),
        name="mem_attn_prompt",
    )(q, gq, k, v)


MEM_BPB = 4


def _mem_attn_sample_kernel(q_ref, gq_ref, k_ref, v_ref, ind_ref, o_ref):
    for j in range(q_ref.shape[0]):
        q = q_ref[j]
        q = q * lax.rsqrt(jnp.mean(q * q, axis=-1, keepdims=True) + 1e-6) * gq_ref[...] * (DH_M ** -0.5)
        o_ref[j] = _rowhead_attend(k_ref[j], v_ref[j], q, ind_ref[...], 1)[0]


def _mem_attn_sample(q, gq, ck, cv, ind, layer):
    b = q.shape[0]
    bpb = MEM_BPB
    kv = pl.BlockSpec((None, bpb, N_MEM * H_M, DH_M), lambda i: (layer, i, 0, 0))
    return pl.pallas_call(
        _mem_attn_sample_kernel,
        grid=(b // bpb,),
        in_specs=[pl.BlockSpec((bpb, H_M, DH_M), lambda i: (i, 0, 0)), _const((1, DH_M)), kv, kv, _const((128, 128))],
        out_specs=pl.BlockSpec((bpb, 8, DH_M), lambda i: (i, 0, 0)),
        out_shape=jax.ShapeDtypeStruct((b, 8, DH_M), f32),
        compiler_params=_params("parallel"),
        name="mem_attn_sample",
    )(q, gq, ck, cv, ind)


def _cast_kernel(x_ref, o_ref):
    o_ref[...] = x_ref[...].astype(bf16)


def _table_bf16(tab, layer, rows=1024):
    n, d = tab.shape[1:]
    return pl.pallas_call(
        _cast_kernel,
        grid=(n // rows,),
        in_specs=[pl.BlockSpec((None, rows, d), lambda i: (layer, i, 0))],
        out_specs=pl.BlockSpec((rows, d), lambda i: (i, 0)),
        out_shape=jax.ShapeDtypeStruct((n, d), bf16),
        compiler_params=_params("parallel"),
        name="table_bf16",
    )(tab)


def _gelu(x):
    return 0.5 * x * (1.0 + jnp.tanh(0.7978845608028654 * (x + 0.044715 * x * x * x)))


def _peer_dense_kernel(xn_ref, u_ref, v_ref, ci_ref, cnt_ref, e2_ref, r2_ref, res_ref, o_ref, w_ref):
    e = pl.program_id(1)
    tb = xn_ref.shape[0]
    n_i = PEER_EB // N_KEYS

    @pl.when(e == 0)
    def _():
        o_ref[...] = res_ref[...]

    n_half = PEER_EB // PEER_SUB
    per = n_i // n_half
    parts = []
    for half in range(n_half):
        for ii in range(half * per, (half + 1) * per):
            w = jnp.zeros((N_KEYS, tb), bf16)
            for h in range(H_P):
                row = pl.ds(e * n_i + ii, 1)
                ci = ci_ref[0, h, row, :].astype(bf16)
                cnt = cnt_ref[0, h, row, :].astype(bf16)
                w = w + jnp.where(r2_ref[0, h] < cnt, e2_ref[0, h] * ci, jnp.zeros((), bf16))
            w_ref[ii * N_KEYS:(ii + 1) * N_KEYS, :] = w
        rows = slice(half * per * N_KEYS, (half + 1) * per * N_KEYS)
        hid = _gelu(_dot(u_ref[rows, :], xn_ref[...], NT))
        parts.append(_dot(hid.astype(bf16) * w_ref[rows, :], v_ref[rows, :], TN))
    o_ref[...] += sum(parts)


def _peer_dense(xn, u, v, sel, res):
    t = xn.shape[0]
    tb, eb = PEER_TB, PEER_EB
    once = pl.Buffered(1)
    selspec = pl.BlockSpec((1, H_P, N_KEYS, tb), lambda i, e: (i, 0, 0, 0), pipeline_mode=once)
    return pl.pallas_call(
        _peer_dense_kernel,
        grid=(t // tb, N_EXPERTS // eb),
        in_specs=[pl.BlockSpec((tb, D_MODEL), lambda i, e: (i, 0), pipeline_mode=once),
                  pl.BlockSpec((eb, D_MODEL), lambda i, e: (e, 0)),
                  pl.BlockSpec((eb, D_MODEL), lambda i, e: (e, 0)), selspec, selspec, selspec, selspec,
                  pl.BlockSpec((tb, D_MODEL), lambda i, e: (i, 0), pipeline_mode=once)],
        out_specs=pl.BlockSpec((tb, D_MODEL), lambda i, e: (i, 0)),
        out_shape=jax.ShapeDtypeStruct((t, D_MODEL), f32),
        scratch_shapes=[pltpu.VMEM((eb, tb), bf16)],
        compiler_params=_params("parallel", "arbitrary"),
        name="peer_dense",
    )(xn, u, v, *sel, res)


def _pad_w_in(w):
    a0, b0, c0, d0 = 0, 1552, 2840, 4376
    seg = lambda s, n: w[:, s:s + n]
    parts = [seg(d0, 1792), seg(b0 + 1024, 256), seg(a0, 512), seg(a0 + 512, 512), seg(a0 + 1024, 512),
             seg(b0, 512), seg(b0 + 512, 512), seg(c0, 512), seg(c0 + 512, 512), seg(c0 + 1024, 512),
             seg(a0 + 1536, 16), seg(b0 + 1280, 8), jnp.zeros((w.shape[0], IN_PAD - 6168), w.dtype)]
    return jnp.concatenate(parts, axis=1).astype(bf16)


def _layer_consts():
    ex = np.zeros((128, 512), np.float32)
    sel = np.zeros((8, 512), np.float32)
    for h in range(8):
        ex[GK_RANK + h, h * 64:(h + 1) * 64] = 1.0
        sel[h, h * 64] = 1.0
    ind2 = np.zeros((128, 256), np.float32)
    ind2[:DK_C, :128] = 1.0
    ind2[DK_C:, 128:] = 1.0
    ind1 = np.ones((128, 128), np.float32)
    return jnp.asarray(ex), jnp.asarray(sel), jnp.asarray(ind2), jnp.asarray(ind1)


def kernel(x_prompt, x_sample, cache_diff_k, cache_diff_v, cache_mem_k, cache_mem_v, state_gla, state_ssm, state_conv, state_rwkv, state_shift, page_table, mem_prompt, norm_mix, w_in, w_out, gla_wg2, gla_bg, gla_gn, conv_w, conv_b, dt_bias, a_log, d_skip, ssm_gn, dq_norm, dk_norm, lam_q, lam_k, diff_gn, shift_mu, w0, w2, a0, a2, g2, k_k, k_a, r_k, lnx_g, lnx_b, norm_mem, norm_memsrc, w_mq, w_mk, w_mv, w_mo, mq_norm, mk_norm, norm_ffn, peer_wq, peer_keys, peer_u, peer_v):
    nb = DEC_BATCH
    n_pages = page_table.shape[1]
    n_pool = cache_diff_k.shape[1]
    x = jnp.concatenate([x_prompt[0], x_sample[:, 0]], axis=0)
    pt_flat = page_table.reshape(-1)
    cos_p, sin_p = _rope_tables(jnp.arange(SEQ, dtype=jnp.int32))
    cos_s, sin_s = _rope_tables(jnp.full((nb,), PAST_LEN, jnp.int32))
    ex, sel8, ind2, ind1 = _layer_consts()
    ck_rows = cache_diff_k.reshape(DEPTH, n_pool, PAGE_SIZE * H_C, 2 * DK_C)
    cv_rows = cache_diff_v.reshape(DEPTH, n_pool, PAGE_SIZE * H_C, DV_C)
    mk_rows = cache_mem_k.reshape(DEPTH, nb, N_MEM * H_M, DH_M)
    mv_rows = cache_mem_v.reshape(DEPTH, nb, N_MEM * H_M, DH_M)
    row = lambda a: a.reshape(1, -1)
    rep64 = lambda a: jnp.repeat(a, 64).reshape(1, 512)
    outs = {n: [] for n in ('kp', 'vp', 'ks', 'vs', 'mk', 'mv', 'gla_p', 'gla_s', 'ssm_p', 'ssm_s', 'conv_p',
                            'conv_s', 'rwkv_p', 'rwkv_s', 'shift_p', 'shift_s')}
    for l in range(DEPTH):
        lam_init = 0.8 - 0.6 * math.exp(-0.3 * l)
        lq, lk = lam_q[l], lam_k[l]
        lam = jnp.exp(jnp.sum(lq[0] * lk[0])) - jnp.exp(jnp.sum(lq[1] * lk[1])) + lam_init
        lam = jnp.full((1, 128), lam, f32)
        wg2p = jnp.zeros((128, 256), f32).at[:GK_RANK].set(gla_wg2[l])
        w2a2 = jnp.zeros((128, 1024), f32).at[:64, :512].set(w2[l]).at[64:, 512:].set(a2[l])
        gq = jnp.tile(dq_norm[l].reshape(128), 4).reshape(1, 512)
        gk = jnp.tile(dk_norm[l].reshape(128), 4).reshape(1, 512)
        dtb_x, alog_x, dskip_x = rep64(dt_bias[l]), rep64(a_log[l]), rep64(d_skip[l])

        cols = _matmul(x, _pad_w_in(w_in[l]), gain=norm_mix[l])

        oa, gla_p = _gla_prompt(cols, wg2p, row(gla_bg[l]), row(gla_gn[l]), seq=SEQ)
        ob, ssm_p = _ssd_prompt(cols, conv_w[l], row(conv_b[l]), ex, dtb_x, alog_x, dskip_x, row(ssm_gn[l]), sel8,
                                seq=SEQ)
        qb, kf, kb = _diff_prep(cols, cos_p, sin_p, gq, gk, 0, SEQ, 512)
        oc = _diff_attn_prompt(qb, kb, cols, lam, row(diff_gn[l]), lam_init, seq=SEQ, tq=FLASH_TQ, tk=FLASH_TK)
        od, rwkv_p = _rwkv_prompt(cols, row(shift_mu[l]), row(w0[l]), row(a0[l]), row(k_k[l]), row(k_a[l]),
                                  row(r_k[l]), row(lnx_g[l]), row(lnx_b[l]), w2a2, g2[l], seq=SEQ)
        mix_p = jnp.concatenate([oa, ob, oc, od], axis=1)

        pre_w = [wg2p, row(gla_bg[l]), conv_w[l], row(conv_b[l]), ex, dtb_x, alog_x,
                 row(shift_mu[l]), row(w0[l]), row(a0[l]), row(k_k[l]), row(k_a[l]), w2a2, g2[l]]
        s_gq, s_gk, s_ga, s_act, s_dec, s_xd, s_rw = _sample_pre(cols, jnp.transpose(state_conv[l], (1, 0, 2)),
                                                                 state_shift[l][:, 0], pre_w, SEQ, nb)
        col = lambda a: a.reshape(nb, -1, 1)
        tail = cols[SEQ:]
        gla_s, oa_s = _state_step(_gla_step_kernel, "gla_step", state_gla[l].reshape(nb, H_A * DK_A, DV_A),
                                  [col(s_ga), col(s_gk), col(s_gq), tail[:, C_VA:C_VA + 512].reshape(nb, 1, 512)],
                                  [(nb, 1, 512)])
        ssm_s, y_s = _state_step(_ssd_step_kernel, "ssd_step", state_ssm[l].reshape(nb, H_B * P_B, N_B),
                                 [col(s_dec), col(s_xd), s_act[:, GROUP_W:].reshape(nb, 1, 256)], [(nb, H_B * P_B, 1)])
        rwkv_s, od_s = _state_step(_rwkv_step_kernel, "rwkv_step", state_rwkv[l].reshape(nb, H_D * N_D, N_D),
                                   [jnp.transpose(s_rw, (1, 0, 2)), col(s_rw[3])], [(nb, H_D * N_D, 1)])
        qb_s, kf_s, _ = _diff_prep(cols, cos_s, sin_s, gq, gk, SEQ, nb, nb)
        vc_s = tail[:, C_VC:C_VC + 512]
        oc_s = _diff_decode(pt_flat, l, qb_s.reshape(nb, H_C, 128), kf_s.reshape(nb, H_C, 128), vc_s.reshape(nb, H_C, 128),
                            ck_rows, cv_rows, ind2, lam, row(diff_gn[l]), lam_init, n_pages)
        mix_s = _sample_post(cols, oa_s.reshape(nb, 512), y_s.reshape(nb, 512), s_act, oc_s[:, :H_C].reshape(nb, 512),
                             od_s.reshape(nb, 512), s_rw, row(gla_gn[l]), dskip_x, row(ssm_gn[l]), row(r_k[l]),
                             row(lnx_g[l]), row(lnx_b[l]), SEQ, nb)

        x = _matmul(jnp.concatenate([mix_p, mix_s], axis=0), w_out[l].astype(bf16), res=x)

        mk_p, mv_p = _mem_kv(mem_prompt[0], row(norm_memsrc[l]),
                             jnp.concatenate([w_mk[l], w_mv[l]], axis=1).astype(bf16), row(mk_norm[l]))
        qm = _matmul(x, w_mq[l].astype(bf16), gain=norm_mem[l])
        om_p = _mem_attn_prompt(qm, row(mq_norm[l]), mk_p, mv_p, seq=SEQ)
        om_s = _mem_attn_sample(qm[SEQ:].reshape(nb, H_M, DH_M), row(mq_norm[l]), mk_rows, mv_rows, ind1, l)
        x = _matmul(jnp.concatenate([om_p, om_s[:, :H_M].reshape(nb, D_MEM)], axis=0), w_mo[l].astype(bf16), res=x)

        qp, xn = _matmul(x, peer_wq[l].astype(bf16), gain=norm_ffn[l], emit_xn=True, tn=512)
        picks = _peer_select(qp, peer_keys[l].reshape(2 * H_P, N_KEYS, D_PK // 2))
        x = _peer_dense(xn, _table_bf16(peer_u, l), _table_bf16(peer_v, l), picks, x)

        outs['kp'].append(kf.reshape(1, SEQ, H_C, 2 * DK_C))
        outs['vp'].append(cols[:SEQ, C_VC:C_VC + 512].reshape(1, SEQ, H_C, DV_C))
        outs['ks'].append(kf_s.reshape(nb, 1, H_C, 2 * DK_C))
        outs['vs'].append(vc_s.reshape(nb, 1, H_C, DV_C))
        outs['mk'].append(mk_p.reshape(1, N_MEM, H_M, DH_M))
        outs['mv'].append(mv_p.reshape(1, N_MEM, H_M, DH_M))
        outs['gla_p'].append(gla_p[None])
        outs['gla_s'].append(gla_s.reshape(nb, H_A, DK_A, DV_A))
        outs['ssm_p'].append(ssm_p[None])
        outs['ssm_s'].append(ssm_s.reshape(nb, H_B, P_B, N_B))
        u_p = jnp.concatenate([cols[SEQ - 3:SEQ, C_XS:C_XS + 512], cols[SEQ - 3:SEQ, C_BC:C_BC + 256]], axis=1)
        u_s = jnp.concatenate([tail[:, C_XS:C_XS + 512], tail[:, C_BC:C_BC + 256]], axis=1)
        outs['conv_p'].append(u_p[None])
        outs['conv_s'].append(jnp.concatenate([state_conv[l][:, 1:], u_s[:, None]], axis=1))
        outs['rwkv_p'].append(rwkv_p[None])
        outs['rwkv_s'].append(rwkv_s.reshape(nb, H_D, N_D, N_D))
        outs['shift_p'].append(cols[SEQ - 1:SEQ, C_D:C_D + 1792][None])
        outs['shift_s'].append(tail[:, C_D:C_D + 1792][:, None])
    st = {n: jnp.stack(v) for n, v in outs.items()}
    return (x[:SEQ][None], x[SEQ:][:, None], st['kp'], st['vp'], st['ks'], st['vs'], st['mk'], st['mv'],
            st['gla_p'], st['gla_s'], st['ssm_p'], st['ssm_s'], st['conv_p'], st['conv_s'],
            st['rwkv_p'], st['rwkv_s'], st['shift_p'], st['shift_s'])
```

```python
import functools
import math

import numpy as np
import jax
import jax.numpy as jnp
from jax import lax
from jax.experimental import pallas as pl
from jax.experimental.pallas import tpu as pltpu

f32 = jnp.float32
bf16 = jnp.bfloat16
HI = lax.Precision.HIGHEST

D_MODEL = 2048
SEQ = 8192
DEPTH = 2
DEC_BATCH = 128
PAST_LEN = 2048
PAGE_SIZE = 128
T_ALL = SEQ + DEC_BATCH

GROUP_W = 512
H_A, DK_A, DV_A, GK_RANK, GLA_TAU = 4, 64, 128, 16, 16.0
H_B, P_B, N_B, G_B, CONV_W, XBC_W = 8, 64, 64, 2, 4, 768
H_C, DK_C, DV_C, ROPE_THETA = 4, 64, 128, 10000.0
H_D, N_D, LNX_EPS = 8, 64, 64e-5
N_MEM, H_M, D_MEM, DH_M = 256, 4, 512, 128
N_KEYS, H_P, TOPK_P, D_PK = 128, 8, 16, 256
N_EXPERTS = N_KEYS * N_KEYS

C_D = 0
C_BC = 1792
C_QKA = 2048
C_VA = 2560
C_GA = 3072
C_Z = 3584
C_XS = 4096
C_QC = 4608
C_KC = 5120
C_VC = 5632
C_TAIL = 6144
IN_PAD = 6272

LANES = 128
VMEM_LIMIT = 56 * 1024 * 1024

NN = ((1,), (0,))
NT = ((1,), (1,))
TN = ((0,), (0,))


def _dot(a, b, dims=NN, hi=False):
    if hi:
        return lax.dot_general(a, b, (dims, ((), ())), precision=HI, preferred_element_type=f32)
    return lax.dot_general(a.astype(bf16), b.astype(bf16), (dims, ((), ())), preferred_element_type=f32)


def _softplus(x):
    return jnp.maximum(x, 0.0) + jnp.log(1.0 + jnp.exp(-jnp.abs(x)))


def _sigmoid(x):
    return 1.0 / (1.0 + jnp.exp(-x))


def _silu(x):
    return x * _sigmoid(x)


def _iota(shape, axis):
    return lax.broadcasted_iota(jnp.int32, shape, axis)


def _params(*sem):
    return pltpu.CompilerParams(dimension_semantics=sem, vmem_limit_bytes=VMEM_LIMIT)


def _mm_kernel(*refs, norm, residual, emit_xn):
    x_ref, g_ref, w_ref = refs[:3]
    res_ref = refs[3] if residual else None
    xn_ref = refs[-1]
    o_ref = refs[-3] if emit_xn else refs[-2]

    @pl.when(pl.program_id(1) == 0)
    def _():
        x = x_ref[...]
        if norm:
            x = x * lax.rsqrt(jnp.mean(x * x, axis=-1, keepdims=True) + 1e-6) * g_ref[...]
        xn_ref[...] = x.astype(bf16)
        if emit_xn:
            refs[-2][...] = xn_ref[...]

    acc = jnp.dot(xn_ref[...], w_ref[...], preferred_element_type=f32)
    if residual:
        acc = acc + res_ref[...]
    o_ref[...] = acc


def _matmul(x, w, gain=None, res=None, emit_xn=False, tm=None, tn=None):
    m, k = x.shape
    n = w.shape[1]
    tm = tm or _pick(m, (1040, 1024, 512, 256, 128))
    tn = tn or _pick(n, (1024, 896, 512, 256, 128))
    norm = gain is not None
    g = (gain if norm else jnp.ones((k,), f32)).reshape(1, k)
    args = [x, g, w]
    in_specs = [pl.BlockSpec((tm, k), lambda i, j: (i, 0)),
                pl.BlockSpec((1, k), lambda i, j: (0, 0)),
                pl.BlockSpec((k, tn), lambda i, j: (0, j))]
    if res is not None:
        args.append(res)
        in_specs.append(pl.BlockSpec((tm, tn), lambda i, j: (i, j)))
    out_specs = [pl.BlockSpec((tm, tn), lambda i, j: (i, j))]
    out_shape = [jax.ShapeDtypeStruct((m, n), f32)]
    if emit_xn:
        out_specs.append(pl.BlockSpec((tm, k), lambda i, j: (i, 0)))
        out_shape.append(jax.ShapeDtypeStruct((m, k), bf16))
    out = pl.pallas_call(
        functools.partial(_mm_kernel, norm=norm, residual=res is not None, emit_xn=emit_xn),
        grid=(m // tm, n // tn),
        in_specs=in_specs,
        out_specs=out_specs,
        out_shape=out_shape,
        scratch_shapes=[pltpu.VMEM((tm, k), bf16)],
        compiler_params=_params("parallel", "arbitrary"),
        name="mm",
    )(*args)
    return out if emit_xn else out[0]


def _pick(n, cands):
    for c in cands:
        if n % c == 0:
            return c
    return n


def _const(shape):
    nd = len(shape)
    return pl.BlockSpec(shape, lambda i, _n=nd: (0,) * _n)


def _cols(width, start, tb):
    assert start % width == 0
    return pl.BlockSpec((tb, width), lambda i, _c=start // width: (i, _c))


def _blockdiag_tri(tb, c):
    r = _iota((tb, tb), 0)
    s = _iota((tb, tb), 1)
    return jnp.where((r // c == s // c) & (s <= r), 1.0, 0.0).astype(f32)


def _segment_ones(n, seg):
    r = _iota((n, n), 0)
    s = _iota((n, n), 1)
    return jnp.where(r // seg == s // seg, 1.0, 0.0).astype(f32)


GLA_TB = 128
GLA_C = 16


def _gla_gate_log(tail, wg2p, bg):
    z = _dot(tail, wg2p, hi=True) + bg
    return -_softplus(-z) * (1.0 / GLA_TAU)


def _gla_out(o, g, gn):
    outs = []
    for h in range(H_A):
        oh = o[:, h * DV_A:(h + 1) * DV_A]
        oh = oh * lax.rsqrt(jnp.mean(oh * oh, axis=-1, keepdims=True) + 1e-6) * gn
        outs.append(oh * _silu(g[:, h * DV_A:(h + 1) * DV_A]))
    return jnp.concatenate(outs, axis=-1)


def _gla_kernel(qk_ref, v_ref, g_ref, tail_ref, wg2_ref, bg_ref, gn_ref, o_ref, sfin_ref, st_ref, w_ref):
    i = pl.program_id(0)

    @pl.when(i == 0)
    def _():
        st_ref[...] = jnp.zeros_like(st_ref)

    tb, c = GLA_TB, GLA_C
    qk = qk_ref[...]
    q = qk[:, :256] * (DK_A ** -0.5)
    k = qk[:, 256:]
    v = v_ref[...]
    la = _gla_gate_log(tail_ref[...], wg2_ref[...], bg_ref[...])
    b = _dot(_blockdiag_tri(tb, c), la, hi=True)
    e_r = _iota((256, 512), 0) // DK_A
    e_c = _iota((256, 512), 1) // DV_A
    expand = jnp.where(e_r == e_c, 1.0, 0.0).astype(bf16)
    s_idx = _iota((c, 256), 0)
    for j in range(tb // c):
        r0 = j * c
        qj, kj, bj, vj = q[r0:r0 + c], k[r0:r0 + c], b[r0:r0 + c], v[r0:r0 + c]
        for t in range(c):
            wt = qj[t:t + 1] * kj * jnp.exp(bj[t:t + 1] - bj)
            w_ref[t * c:(t + 1) * c, :] = jnp.where(s_idx <= t, wt, 0.0)
        att = _dot(w_ref[...], expand)
        o = jnp.sum(att.reshape(c, c, 512) * vj[None], axis=1)
        qe = qj * jnp.exp(bj)
        bl = bj[c - 1:c]
        ke = kj * jnp.exp(bl - bj)
        dl = jnp.exp(bl)
        inter = []
        for h in range(H_A):
            ks = slice(h * DK_A, (h + 1) * DK_A)
            st = st_ref[h]
            inter.append(_dot(qe[:, ks], st, NT))
            st_ref[h] = st * dl[:, ks] + _dot(vj[:, h * DV_A:(h + 1) * DV_A], ke[:, ks], TN)
        o = o + jnp.concatenate(inter, axis=-1)
        o_ref[r0:r0 + c, :] = _gla_out(o, g_ref[r0:r0 + c, :], gn_ref[...])

    @pl.when(i == pl.num_programs(0) - 1)
    def _():
        for h in range(H_A):
            sfin_ref[h] = st_ref[h].T


def _gla_prompt(cols, wg2p, bg, gn, seq=SEQ):
    tb = GLA_TB
    return pl.pallas_call(
        _gla_kernel,
        grid=(seq // tb,),
        in_specs=[_cols(512, C_QKA, tb), _cols(512, C_VA, tb), _cols(512, C_GA, tb), _cols(128, C_TAIL, tb),
                  _const((128, 256)), _const((1, 256)), _const((1, DV_A))],
        out_specs=[pl.BlockSpec((tb, 512), lambda i: (i, 0)), _const((H_A, DK_A, DV_A))],
        out_shape=[jax.ShapeDtypeStruct((seq, 512), f32), jax.ShapeDtypeStruct((H_A, DK_A, DV_A), f32)],
        scratch_shapes=[pltpu.VMEM((H_A, DV_A, DK_A), f32), pltpu.VMEM((GLA_C * GLA_C, 256), f32)],
        compiler_params=_params("arbitrary"),
        name="gla_prompt",
    )(cols, cols, cols, cols, wg2p, bg, gn)


SSD_TB = 128
SSD_C = 64


def _ssd_conv(ext, conv_w, conv_b, rows):
    out = conv_b
    for j in range(CONV_W):
        shifted = pltpu.roll(ext, j, 0) if j else ext
        out = out + shifted[8:8 + rows] * conv_w[CONV_W - 1 - j:CONV_W - j]
    return out


def _ssd_dt(tail, ex, dtb_x):
    return _softplus(_dot(tail, ex, hi=True) + dtb_x)


def _ssd_out(y, z, gn):
    y = y * _silu(z)
    w = GROUP_W // G_B
    outs = []
    for g in range(G_B):
        yg = y[:, g * w:(g + 1) * w]
        outs.append(yg * lax.rsqrt(jnp.mean(yg * yg, axis=-1, keepdims=True) + 1e-6) * gn[:, g * w:(g + 1) * w])
    return jnp.concatenate(outs, axis=-1)


def _ssd_kernel(z_ref, xs_ref, bc_ref, tail_ref, cw_ref, cb_ref, ex_ref, dtb_ref, alog_ref, dskip_ref, gn_ref,
                sel_ref, o_ref, hfin_ref, carry_ref, h_ref, y_ref):
    i = pl.program_id(0)

    @pl.when(i == 0)
    def _():
        carry_ref[...] = jnp.zeros_like(carry_ref)
        h_ref[...] = jnp.zeros_like(h_ref)

    tb, c = SSD_TB, SSD_C
    u = jnp.concatenate([xs_ref[...], bc_ref[...]], axis=-1)
    ext = jnp.concatenate([carry_ref[...], u], axis=0)
    carry_ref[...] = u[tb - 8:tb]
    act = _silu(_ssd_conv(ext, cw_ref[...], cb_ref[...], tb))
    xs = act[:, :GROUP_W]
    dtx = _ssd_dt(tail_ref[...], ex_ref[...], dtb_ref[...])
    la = dtx * (-jnp.exp(alog_ref[...]))
    b = _dot(_blockdiag_tri(tb, c), la, hi=True)
    brow = _dot(sel_ref[...], b, NT, hi=True)
    xd = xs * dtx
    tri = _iota((c, c), 1) <= _iota((c, c), 0)
    for ch in range(tb // c):
        r0 = ch * c
        rows = slice(r0, r0 + c)
        scores = []
        for g in range(G_B):
            bm = act[rows, GROUP_W + g * N_B:GROUP_W + (g + 1) * N_B]
            cm = act[rows, GROUP_W + G_B * N_B + g * N_B:GROUP_W + G_B * N_B + (g + 1) * N_B]
            scores.append((_dot(cm, bm, NT), bm, cm))
        for h in range(H_B):
            hs = slice(h * P_B, (h + 1) * P_B)
            sc, bm, cm = scores[h // (H_B // G_B)]
            bh = b[rows, hs]
            dec = jnp.where(tri, jnp.exp(bh - brow[h:h + 1, r0:r0 + c]), 0.0)
            xdh = xd[rows, hs]
            hst = h_ref[h]
            y = _dot(sc * dec, xdh) + _dot(cm, hst, NT) * jnp.exp(bh)
            bl = bh[c - 1:c]
            h_ref[h] = hst * jnp.exp(bl) + _dot(xdh * jnp.exp(bl - bh), bm, TN)
            y_ref[rows, hs] = y + xs[rows, hs] * dskip_ref[:, hs]
    o_ref[...] = _ssd_out(y_ref[...], z_ref[...], gn_ref[...])

    @pl.when(i == pl.num_programs(0) - 1)
    def _():
        hfin_ref[...] = h_ref[...]


def _ssd_prompt(cols, cw, cb, ex, dtb_x, alog_x, dskip_x, gn, sel, seq=SEQ):
    tb = SSD_TB
    return pl.pallas_call(
        _ssd_kernel,
        grid=(seq // tb,),
        in_specs=[_cols(512, C_Z, tb), _cols(512, C_XS, tb), _cols(256, C_BC, tb), _cols(128, C_TAIL, tb),
                  _const((CONV_W, XBC_W)), _const((1, XBC_W)), _const((128, 512)), _const((1, 512)), _const((1, 512)),
                  _const((1, 512)), _const((1, 512)), _const((8, 512))],
        out_specs=[pl.BlockSpec((tb, 512), lambda i: (i, 0)), _const((H_B, P_B, N_B))],
        out_shape=[jax.ShapeDtypeStruct((seq, 512), f32), jax.ShapeDtypeStruct((H_B, P_B, N_B), f32)],
        scratch_shapes=[pltpu.VMEM((8, XBC_W), f32), pltpu.VMEM((H_B, P_B, N_B), f32), pltpu.VMEM((tb, 512), f32)],
        compiler_params=_params("arbitrary"),
        name="ssd_prompt",
    )(cols, cols, cols, cols, cw, cb, ex, dtb_x, alog_x, dskip_x, gn, sel)


RWKV_TB = 128
RWKV_C = 64


def _rwkv_pre(x, xprev, mu, w0, a0, k_k, k_a, w2a2, g2):
    mixed = x + (xprev - x) * mu
    r = mixed[:, :512]
    kd = mixed[:, 512:1024]
    v = mixed[:, 1024:1536]
    lw = mixed[:, 1536:1664]
    lin = jnp.where(_iota(lw.shape, 1) < 64, jnp.tanh(lw), lw)
    wa = _dot(lin, w2a2, hi=True)
    w = -_softplus(-(w0 + wa[:, :512])) - 0.5
    a = _sigmoid(a0 + wa[:, 512:])
    g = _dot(_sigmoid(mixed[:, 1664:1792]), g2)
    kk = kd * k_k
    ss = _dot(kk * kk, _segment_ones(512, N_D), hi=True)
    kk = kk * lax.rsqrt(jnp.maximum(ss, 1e-24))
    kd = kd * (1.0 + (a - 1.0) * k_a)
    return r, w, kd, v, -kk, kk * a, g


def _rwkv_post(o, r, kd, v, g, r_k, lnx_g, lnx_b):
    seg = _segment_ones(512, N_D)
    mu = _dot(o, seg, hi=True) * (1.0 / N_D)
    d = o - mu
    var = _dot(d * d, seg, hi=True) * (1.0 / N_D)
    o = d * lax.rsqrt(var + LNX_EPS) * lnx_g + lnx_b
    o = o + _dot(r * kd * r_k, seg, hi=True) * v
    return o * g


def _rwkv_kernel(d_ref, mu_ref, w0_ref, a0_ref, kk_ref, ka_ref, rk_ref, lg_ref, lb_ref, w2a2_ref, g2_ref,
                 o_ref, sfin_ref, prev_ref, s_ref, oacc_ref):
    i = pl.program_id(0)

    @pl.when(i == 0)
    def _():
        prev_ref[...] = jnp.zeros_like(prev_ref)
        s_ref[...] = jnp.zeros_like(s_ref)

    tb, c = RWKV_TB, RWKV_C
    x = d_ref[...]
    xprev = jnp.where(_iota(x.shape, 0) == 0, prev_ref[0:1, :], pltpu.roll(x, 1, 0))
    prev_ref[0:1, :] = x[tb - 1:tb]
    r, w, kd, v, alpha, beta, g = _rwkv_pre(x, xprev, mu_ref[...], w0_ref[...], a0_ref[...], kk_ref[...], ka_ref[...],
                                            w2a2_ref[...], g2_ref[...])
    ld = -jnp.exp(w)
    cum = _dot(_blockdiag_tri(tb, c), ld, hi=True)
    at = alpha * jnp.exp(cum - ld)
    rt = r * jnp.exp(cum)
    einv = jnp.exp(-cum)
    kt = kd * einv
    bt = beta * einv
    ri = _iota((c, c), 0)
    ci = _iota((c, c), 1)
    strict, incl = ci < ri, ci <= ri
    eye = jnp.where(ri == ci, 1.0, 0.0).astype(f32)
    pairs = [(ch, h) for ch in range(tb // c) for h in range(H_D)]
    rows_of = lambda ch: slice(ch * c, (ch + 1) * c)
    lanes_of = lambda h: slice(h * N_D, (h + 1) * N_D)
    ar, kb, lk, mkb, tinv, p = {}, {}, {}, {}, {}, {}
    for ch, h in pairs:
        rows, hs = rows_of(ch), lanes_of(h)
        ar[ch, h] = jnp.concatenate([at[rows, hs], rt[rows, hs]], axis=0)
        kb[ch, h] = jnp.concatenate([kt[rows, hs], bt[rows, hs]], axis=0)
    for key in pairs:
        gram = _dot(ar[key], kb[key], NT)
        lk[key] = jnp.where(strict, gram[:c, :c], 0.0)
        p[key] = jnp.where(strict, gram[:c, c:], 0.0)
        mkb[key] = jnp.concatenate([jnp.where(incl, gram[c:, :c], 0.0), jnp.where(incl, gram[c:, c:], 0.0)], axis=1)
        tinv[key] = eye + p[key]
    for _ in range(5):
        for key in pairs:
            p[key] = _dot(p[key], p[key])
        for key in pairs:
            tinv[key] = tinv[key] + _dot(tinv[key], p[key])
    for ch in range(tb // c):
        rows = rows_of(ch)
        cl = cum[ch * c + c - 1:ch * c + c]
        efin = jnp.exp(cl - cum[rows])
        kfin = kd[rows] * efin
        bfin = beta[rows] * efin
        dfin = jnp.exp(cl)
        heads = range(H_D)
        s0 = [s_ref[h] for h in heads]
        ars = [_dot(ar[ch, h], s0[h], NT) for h in heads]
        lkv = [_dot(lk[ch, h], v[rows, lanes_of(h)]) for h in heads]
        u = [_dot(tinv[ch, h], ars[h][:c] + lkv[h]) for h in heads]
        vu = [jnp.concatenate([v[rows, lanes_of(h)], u[h]], axis=0) for h in heads]
        for h in heads:
            hs = lanes_of(h)
            oacc_ref[rows, hs] = ars[h][c:] + _dot(mkb[ch, h], vu[h])
            kbfin = jnp.concatenate([kfin[:, hs], bfin[:, hs]], axis=0)
            s_ref[h] = s0[h] * dfin[:, hs] + _dot(vu[h], kbfin, TN)
    o_ref[...] = _rwkv_post(oacc_ref[...], r, kd, v, g, rk_ref[...], lg_ref[...], lb_ref[...])

    @pl.when(i == pl.num_programs(0) - 1)
    def _():
        sfin_ref[...] = s_ref[...]


def _rwkv_prompt(cols, mu, w0, a0, k_k, k_a, r_k, lnx_g, lnx_b, w2a2, g2, seq=SEQ):
    tb = RWKV_TB
    vec = _const((1, 512))
    return pl.pallas_call(
        _rwkv_kernel,
        grid=(seq // tb,),
        in_specs=[_cols(1792, C_D, tb), _const((1, 1792)), vec, vec, vec, vec, vec, vec, vec,
                  _const((128, 1024)), _const((128, 512))],
        out_specs=[pl.BlockSpec((tb, 512), lambda i: (i, 0)), _const((H_D, N_D, N_D))],
        out_shape=[jax.ShapeDtypeStruct((seq, 512), f32), jax.ShapeDtypeStruct((H_D, N_D, N_D), f32)],
        scratch_shapes=[pltpu.VMEM((8, 1792), f32), pltpu.VMEM((H_D, N_D, N_D), f32), pltpu.VMEM((tb, 512), f32)],
        compiler_params=_params("arbitrary"),
        name="rwkv_prompt",
    )(cols, mu, w0, a0, k_k, k_a, r_k, lnx_g, lnx_b, w2a2, g2)


def _rope_tables(pos):
    half = DK_C // 2
    inv = ROPE_THETA ** (-jnp.arange(half, dtype=f32) / half)
    ang = pos.astype(f32)[:, None] * inv[None, :]
    cos, sin = jnp.cos(ang), jnp.sin(ang)
    return jnp.tile(jnp.concatenate([cos, cos], axis=-1), (1, 2)), jnp.tile(jnp.concatenate([-sin, sin], axis=-1), (1, 2))


def _qk_norm_rope(x, gain, cos, sin):
    ms = _dot(x * x, _segment_ones(512, DK_C), hi=True) * (1.0 / DK_C)
    x = x * lax.rsqrt(ms + 1e-6) * gain
    first = (_iota(x.shape, 1) % DK_C) < (DK_C // 2)
    partner = jnp.where(first, pltpu.roll(x, 512 - DK_C // 2, 1), pltpu.roll(x, DK_C // 2, 1))
    cos = jnp.concatenate([cos] * 4, axis=-1)
    sin = jnp.concatenate([sin] * 4, axis=-1)
    return x * cos + partner * sin


def _diff_prep_kernel(q_ref, k_ref, cos_ref, sin_ref, gq_ref, gk_ref, qb_ref, kf_ref, kb_ref):
    cos, sin = cos_ref[...], sin_ref[...]
    q = _qk_norm_rope(q_ref[...], gq_ref[...], cos, sin)
    k = _qk_norm_rope(k_ref[...], gk_ref[...], cos, sin)
    qb_ref[...] = (q * (DK_C ** -0.5)).astype(bf16)
    kf_ref[...] = k
    kb_ref[...] = k.astype(bf16)


def _diff_prep(cols, cos, sin, gq, gk, row0, rows, tb):
    assert row0 % tb == 0
    r0 = row0 // tb
    colspec = lambda start: pl.BlockSpec((tb, 512), lambda i, _c=start // 512: (i + r0, _c))
    out = pl.BlockSpec((tb, 512), lambda i: (i, 0))
    tab = pl.BlockSpec((tb, 128), lambda i: (i, 0))
    return pl.pallas_call(
        _diff_prep_kernel,
        grid=(rows // tb,),
        in_specs=[colspec(C_QC), colspec(C_KC), tab, tab, _const((1, 512)), _const((1, 512))],
        out_specs=[out, out, out],
        out_shape=[jax.ShapeDtypeStruct((rows, 512), bf16), jax.ShapeDtypeStruct((rows, 512), f32),
                   jax.ShapeDtypeStruct((rows, 512), bf16)],
        compiler_params=_params("parallel"),
        name="diff_prep",
    )(cols, cols, cos, sin, gq, gk)


def _diff_finish(o1, o2, lam, gn, lam_init):
    o = o1 - lam * o2
    return o * lax.rsqrt(jnp.mean(o * o, axis=-1, keepdims=True) + 1e-6) * gn * (1.0 - lam_init)


FLASH_TQ = 1024
FLASH_TK = 512


def _flash_kernel(qi_ref, kj_ref, q_ref, k_ref, v_ref, lam_ref, gn_ref, o_ref, m_ref, l_ref, acc_ref,
                  *, tq, tk, lam_init):
    step_id = pl.program_id(1)
    qi, kj = qi_ref[step_id], kj_ref[step_id]

    @pl.when(kj == 0)
    def _():
        m_ref[...] = jnp.full_like(m_ref, -jnp.inf)
        l_ref[...] = jnp.zeros_like(l_ref)
        acc_ref[...] = jnp.zeros_like(acc_ref)

    def step(masked):
        q = q_ref[...]
        k = k_ref[...]
        v = v_ref[...].astype(bf16)
        lane = _iota(q.shape, 1)
        if masked:
            keep = (kj * tk + _iota((tk, tq), 0)) <= (qi * tq + _iota((tk, tq), 1))
        for m in range(2):
            qm = jnp.where((lane < DK_C) if m == 0 else (lane >= DK_C), q, jnp.zeros_like(q))
            s = _dot(k, qm, NT)
            if masked:
                s = jnp.where(keep, s, -jnp.inf)
            m_old = m_ref[m]
            m_new = jnp.maximum(m_old, jnp.max(s, axis=0, keepdims=True))
            p = jnp.exp(s - m_new)
            corr = jnp.exp(m_old - m_new)
            l_ref[m] = corr * l_ref[m] + jnp.sum(p, axis=0, keepdims=True)
            acc_ref[m] = corr * acc_ref[m] + _dot(v, p, TN)
            m_ref[m] = m_new

    last = kj * tk + tk - 1

    @pl.when(last <= qi * tq)
    def _():
        step(False)

    @pl.when(last > qi * tq)
    def _():
        step(True)

    @pl.when(kj == (qi * tq + tq - 1) // tk)
    def _():
        o1 = (acc_ref[0] / l_ref[0]).T
        o2 = (acc_ref[1] / l_ref[1]).T
        o_ref[...] = _diff_finish(o1, o2, lam_ref[...], gn_ref[...], lam_init)


def _diff_attn_prompt(qb, kb, cols, lam, gn, lam_init, seq=SEQ, tq=512, tk=512):
    pairs = [(i, j) for i in range(seq // tq) for j in range((i * tq + tq - 1) // tk + 1)]
    qi_tab = jnp.asarray([p[0] for p in pairs], jnp.int32)
    kj_tab = jnp.asarray([p[1] for p in pairs], jnp.int32)
    cst = lambda shp: pl.BlockSpec(shp, lambda h, s, qi, kj: (0, 0))
    return pl.pallas_call(
        functools.partial(_flash_kernel, tq=tq, tk=tk, lam_init=lam_init),
        grid_spec=pltpu.PrefetchScalarGridSpec(
            num_scalar_prefetch=2, grid=(H_C, len(pairs)),
            in_specs=[pl.BlockSpec((tq, 128), lambda h, s, qi, kj: (qi[s], h)),
                      pl.BlockSpec((tk, 128), lambda h, s, qi, kj: (kj[s], h)),
                      pl.BlockSpec((tk, DV_C), lambda h, s, qi, kj: (kj[s], C_VC // DV_C + h)),
                      cst((1, 128)), cst((1, DV_C))],
            out_specs=pl.BlockSpec((tq, DV_C), lambda h, s, qi, kj: (qi[s], h)),
            scratch_shapes=[pltpu.VMEM((2, 1, tq), f32), pltpu.VMEM((2, 1, tq), f32), pltpu.VMEM((2, DV_C, tq), f32)]),
        out_shape=jax.ShapeDtypeStruct((seq, 512), f32),
        compiler_params=_params("parallel", "arbitrary"),
        name="diff_attn_prompt",
    )(qi_tab, kj_tab, qb, kb, cols, lam, gn)


PEER_SEL_TB = 128
PEER_TB = 640
PEER_EB = 512
PEER_SUB = 256
_PAIRS = [(a, b) for a in range(TOPK_P) for b in range(TOPK_P) if (a + 1) * (b + 1) <= TOPK_P]
_NPAIR = -(-len(_PAIRS) // 8) * 8


def _top16(s, n_idx, want_rank):
    rank = jnp.full(s.shape, float(TOPK_P), f32) if want_rank else None
    tops, idxs = [], []
    work = s
    for k in range(TOPK_P):
        m = jnp.max(work, axis=0, keepdims=True)
        idx = jnp.min(jnp.where(work == m, n_idx, float(N_KEYS)), axis=0, keepdims=True)
        hit = n_idx == idx
        if want_rank:
            rank = jnp.where(hit, float(k), rank)
        work = jnp.where(hit, -jnp.inf, work)
        tops.append(m)
        idxs.append(idx)
    return rank, jnp.concatenate(tops, axis=0), jnp.concatenate(idxs, axis=0)


def _peer_select_kernel(q_ref, keys_ref, pk1_ref, flat_ref, ci_ref, cnt_ref, e2_ref, r2_ref):
    tb = q_ref.shape[0]
    flat = flat_ref[...]
    n_idx = _iota((N_KEYS, tb), 0).astype(f32)
    for h in range(H_P):
        scores = [_dot(keys_ref[hx], q_ref[:, hx * 128:(hx + 1) * 128], NT, hi=True)
                  for hx in (2 * h, 2 * h + 1)]
        _, t1, idx1 = _top16(scores[0], n_idx, False)
        rank2, t2, _ = _top16(scores[1], n_idx, True)
        cand = jnp.concatenate([t1[a:a + 1] + t2[b:b + 1] for a, b in _PAIRS]
                               + [jnp.full((_NPAIR - len(_PAIRS), tb), -jnp.inf, f32)], axis=0)
        work = cand
        sel = jnp.zeros(cand.shape, f32)
        for _ in range(TOPK_P):
            m = jnp.max(work, axis=0, keepdims=True)
            idx = jnp.min(jnp.where(work == m, flat, 4096.0), axis=0, keepdims=True)
            hit = flat == idx
            sel = jnp.where(hit, 1.0, sel)
            work = jnp.where(hit, -jnp.inf, work)
        top = t1[0:1] + t2[0:1]
        z = jnp.sum(sel * jnp.exp(jnp.where(sel > 0, cand - top, 0.0)), axis=0, keepdims=True)
        cnt = _dot(pk1_ref[...], sel)
        cnt_i = jnp.zeros((N_KEYS, tb), f32)
        for k1 in range(TOPK_P):
            cnt_i = jnp.where(n_idx == idx1[k1:k1 + 1], cnt[k1:k1 + 1], cnt_i)
        ci_ref[0, h] = jnp.exp(scores[0] - t1[0:1]) / z
        cnt_ref[0, h] = cnt_i
        e2_ref[0, h] = jnp.exp(scores[1] - t2[0:1]).astype(bf16)
        r2_ref[0, h] = rank2.astype(bf16)


def _peer_select(q, keys):
    t = q.shape[0]
    tb, per = PEER_SEL_TB, PEER_TB // PEER_SEL_TB
    pk1 = np.zeros((TOPK_P, _NPAIR), np.float32)
    flat = np.full((_NPAIR, 1), 8192.0, np.float32)
    for r, (a, b) in enumerate(_PAIRS):
        pk1[a, r] = 1.0
        flat[r, 0] = a * TOPK_P + b
    out = pl.BlockSpec((1, H_P, N_KEYS, tb), lambda i: (i // per, 0, 0, i % per))
    shp = lambda dt: jax.ShapeDtypeStruct((t // PEER_TB, H_P, N_KEYS, PEER_TB), dt)
    return pl.pallas_call(
        _peer_select_kernel,
        grid=(t // tb,),
        in_specs=[pl.BlockSpec((tb, 2048), lambda i: (i, 0)), _const((2 * H_P, N_KEYS, 128)),
                  _const((TOPK_P, _NPAIR)), _const((_NPAIR, 1))],
        out_specs=[out, out, out, out],
        out_shape=[shp(f32), shp(f32), shp(bf16), shp(bf16)],
        compiler_params=_params("parallel"),
        name="peer_select",
    )(q, keys, jnp.asarray(pk1), jnp.asarray(flat))


def _sample_pre_kernel(qk_ref, tail_ref, xs_ref, bc_ref, d_ref, conv_ref, shift_ref,
                       wg2_ref, bg_ref, cw_ref, cb_ref, ex_ref, dtb_ref, alog_ref,
                       mu_ref, w0_ref, a0_ref, kk_ref, ka_ref, w2a2_ref, g2_ref,
                       gq_ref, gk_ref, ga_ref, act_ref, sdec_ref, sxd_ref, rw_ref):
    qk = qk_ref[...]
    la = _gla_gate_log(tail_ref[...], wg2_ref[...], bg_ref[...])
    gq_ref[...] = qk[:, :256] * (DK_A ** -0.5)
    gk_ref[...] = qk[:, 256:]
    ga_ref[...] = jnp.exp(la)
    u = jnp.concatenate([xs_ref[...], bc_ref[...]], axis=-1)
    cw = cw_ref[...]
    conv = cb_ref[...] + u * cw[CONV_W - 1:CONV_W]
    for j in range(CONV_W - 1):
        conv = conv + conv_ref[j] * cw[j:j + 1]
    act = _silu(conv)
    act_ref[...] = act
    dtx = _ssd_dt(tail_ref[...], ex_ref[...], dtb_ref[...])
    sdec_ref[...] = jnp.exp(dtx * (-jnp.exp(alog_ref[...])))
    sxd_ref[...] = act[:, :GROUP_W] * dtx
    r, w, kd, v, alpha, beta, g = _rwkv_pre(d_ref[...], shift_ref[...], mu_ref[...], w0_ref[...], a0_ref[...],
                                            kk_ref[...], ka_ref[...], w2a2_ref[...], g2_ref[...])
    for n, t in enumerate((r, jnp.exp(-jnp.exp(w)), kd, v, alpha, beta, g)):
        rw_ref[n] = t


def _sample_pre(cols, conv_st, shift_st, wts, row0, b):
    tb = b
    assert row0 % tb == 0
    r0 = row0 // tb
    cs = lambda w, start: pl.BlockSpec((tb, w), lambda i, _c=start // w: (r0, _c))
    full = lambda *s: jax.ShapeDtypeStruct(s, f32)
    return pl.pallas_call(
        _sample_pre_kernel,
        grid=(1,),
        in_specs=[cs(512, C_QKA), cs(128, C_TAIL), cs(512, C_XS), cs(256, C_BC), cs(1792, C_D),
                  _const((CONV_W - 1, b, XBC_W)), _const((b, 1792))] + [_const(w.shape) for w in wts],
        out_specs=[_const((b, 256))] * 3 + [_const((b, XBC_W)), _const((b, 512)), _const((b, 512)), _const((7, b, 512))],
        out_shape=[full(b, 256)] * 3 + [full(b, XBC_W), full(b, 512), full(b, 512), full(7, b, 512)],
        compiler_params=_params("arbitrary"),
        name="sample_pre",
    )(cols, cols, cols, cols, cols, conv_st, shift_st, *wts)


def _rows_to_tile(row, heads, width, reps):
    return jnp.concatenate([jnp.broadcast_to(row[:, h * width:(h + 1) * width], (reps, width)) for h in range(heads)], axis=0)


STEP_BPB = 8


def _gla_step_kernel(s_ref, a_ref, k_ref, q_ref, v_ref, sn_ref, o_ref):
    a_t, k_t, q_t = a_ref[...].T, k_ref[...].T, q_ref[...].T
    for j in range(s_ref.shape[0]):
        s = a_t[:, j:j + 1] * s_ref[j] + k_t[:, j:j + 1] * _rows_to_tile(v_ref[j:j + 1, :], H_A, DV_A, DK_A)
        sn_ref[j] = s
        qs = q_t[:, j:j + 1] * s
        o_ref[j:j + 1, :] = jnp.concatenate(
            [jnp.sum(qs[h * DK_A:(h + 1) * DK_A], axis=0, keepdims=True) for h in range(H_A)], axis=-1)


def _ssd_step_kernel(h_ref, dec_ref, xd_ref, bc_ref, hn_ref, y_ref):
    reps = (H_B // G_B) * P_B
    dec_t, xd_t = dec_ref[...].T, xd_ref[...].T
    ys = []
    for j in range(h_ref.shape[0]):
        bc = bc_ref[j:j + 1, :]
        hn = dec_t[:, j:j + 1] * h_ref[j] + xd_t[:, j:j + 1] * _rows_to_tile(bc[:, :G_B * N_B], G_B, N_B, reps)
        hn_ref[j] = hn
        ys.append(jnp.sum(hn * _rows_to_tile(bc[:, G_B * N_B:], G_B, N_B, reps), axis=-1, keepdims=True))
    y_ref[...] = jnp.concatenate(ys, axis=-1).T


def _rwkv_step_kernel(s_ref, rows_ref, sn_ref, o_ref):
    v_t = rows_ref[3].T
    os_ = []
    for j in range(s_ref.shape[0]):
        tile = lambda n, _j=j: _rows_to_tile(rows_ref[n, _j:_j + 1, :], H_D, N_D, N_D)
        s = s_ref[j]
        sa = jnp.sum(s * tile(4), axis=-1, keepdims=True)
        s = s * tile(1) + sa * tile(5) + v_t[:, j:j + 1] * tile(2)
        sn_ref[j] = s
        os_.append(jnp.sum(s * tile(0), axis=-1, keepdims=True))
    o_ref[...] = jnp.concatenate(os_, axis=-1).T


def _state_step(kernel, name, states, layer, ins, out_width):
    _, b, r, c = states.shape
    bpb = STEP_BPB

    def spec(a):
        if a.ndim == 2:
            return pl.BlockSpec((bpb, a.shape[1]), lambda i: (i, 0))
        return pl.BlockSpec((a.shape[0], bpb, a.shape[2]), lambda i: (0, i, 0))

    return pl.pallas_call(
        kernel,
        grid=(b // bpb,),
        in_specs=[pl.BlockSpec((None, bpb, r, c), lambda i: (layer, i, 0, 0))] + [spec(a) for a in ins],
        out_specs=[pl.BlockSpec((bpb, r, c), lambda i: (i, 0, 0)), pl.BlockSpec((bpb, out_width), lambda i: (i, 0))],
        out_shape=[jax.ShapeDtypeStruct((b, r, c), f32), jax.ShapeDtypeStruct((b, out_width), f32)],
        compiler_params=_params("parallel"),
        name=name,
    )(states, *ins)


def _rowhead_attend(kx, vx, q4, ind, n_maps, tail_rows=None):
    r = kx.shape[0]
    g = r // 8
    q8 = jnp.concatenate([q4, q4], axis=0)
    qt = jnp.broadcast_to(q8[None], (g, 8, 128)).reshape(r, 128)
    s3 = _dot(kx * qt, ind).reshape(g, 8, 128 * n_maps)
    if tail_rows is not None:
        last = jnp.where(_iota((1, 8, 128 * n_maps), 1) < tail_rows, s3[g - 1:g], -jnp.inf)
        s3 = jnp.concatenate([s3[:g - 1], last], axis=0)
    m8 = jnp.max(s3, axis=0)
    mh = jnp.maximum(m8, pltpu.roll(m8, 4, 0))
    p3 = jnp.exp(s3 - mh[None])
    l8 = jnp.sum(p3, axis=0)
    lh = l8 + pltpu.roll(l8, 4, 0)
    v3 = vx.reshape(g, 8, 128)
    outs = []
    for m in range(n_maps):
        ms = slice(m * 128, (m + 1) * 128)
        pv = jnp.sum(p3[:, :, ms] * v3, axis=0)
        pv = pv + pltpu.roll(pv, 4, 0)
        outs.append(pv / lh[:, ms])
    return outs


def _diff_decode_kernel(pt_ref, q_ref, ks_ref, vs_ref, *rest, lam_init, n_pages):
    del pt_ref
    k_refs, v_refs = rest[:n_pages], rest[n_pages:2 * n_pages]
    ind_ref, lam_ref, gn_ref, o_ref = rest[2 * n_pages:]
    own = lambda ref: jnp.concatenate([ref[0], ref[0]], axis=0)
    kx = jnp.concatenate([r[...] for r in k_refs] + [own(ks_ref)], axis=0)
    vx = jnp.concatenate([r[...] for r in v_refs] + [own(vs_ref)], axis=0)
    o1, o2 = _rowhead_attend(kx, vx, q_ref[0].astype(f32), ind_ref[...], 2, tail_rows=H_C)
    o_ref[0] = _diff_finish(o1, o2, lam_ref[...], gn_ref[...], lam_init)


def _diff_decode(pt_flat, layer, qb, ks, vs, ck, cv, ind, lam, gn, lam_init, n_pages):
    b = qb.shape[0]
    rows = ck.shape[2]
    row = pl.BlockSpec((1, H_C, 128), lambda i, pt: (i, 0, 0))
    page = lambda j: pl.BlockSpec((None, None, rows, 128), lambda i, pt, _j=j: (layer, pt[i * n_pages + _j], 0, 0))
    cst = lambda shp: pl.BlockSpec(shp, lambda i, pt: (0, 0))
    pages = [page(j) for j in range(n_pages)]
    return pl.pallas_call(
        functools.partial(_diff_decode_kernel, lam_init=lam_init, n_pages=n_pages),
        grid_spec=pltpu.PrefetchScalarGridSpec(
            num_scalar_prefetch=1, grid=(b,),
            in_specs=[row, row, row] + pages + pages + [cst((128, 256)), cst((1, 128)), cst((1, DV_C))],
            out_specs=pl.BlockSpec((1, 8, 128), lambda i, pt: (i, 0, 0))),
        out_shape=jax.ShapeDtypeStruct((b, 8, 128), f32),
        compiler_params=_params("parallel"),
        name="diff_decode",
    )(pt_flat, qb, ks, vs, *([ck] * n_pages), *([cv] * n_pages), ind, lam, gn)


def _sample_post_kernel(oa_ref, ga_ref, gn_a_ref, y_ref, act_ref, z_ref, dskip_ref, gn_b_ref, oc_ref,
                        od_ref, rw_ref, rk_ref, lg_ref, lb_ref, o_ref):
    oa = _gla_out(oa_ref[...], ga_ref[...], gn_a_ref[...])
    ob = _ssd_out(y_ref[...] + act_ref[:, :GROUP_W] * dskip_ref[...], z_ref[...], gn_b_ref[...])
    od = _rwkv_post(od_ref[...], rw_ref[0], rw_ref[2], rw_ref[3], rw_ref[6], rk_ref[...], lg_ref[...], lb_ref[...])
    o_ref[...] = jnp.concatenate([oa, ob, oc_ref[...], od], axis=-1)


def _sample_post(cols, oa, y, act, oc, od, rw, gn_a, dskip_x, gn_b, r_k, lnx_g, lnx_b, row0, b):
    r0 = row0 // b
    cs = lambda w, start: pl.BlockSpec((b, w), lambda i, _c=start // w: (r0, _c))
    c512 = _const((b, 512))
    v512 = _const((1, 512))
    return pl.pallas_call(
        _sample_post_kernel,
        grid=(1,),
        in_specs=[c512, cs(512, C_GA), _const((1, DV_A)), c512, _const((b, XBC_W)), cs(512, C_Z), v512, v512, c512,
                  c512, _const((7, b, 512)), v512, v512, v512],
        out_specs=_const((b, D_MODEL)),
        out_shape=jax.ShapeDtypeStruct((b, D_MODEL), f32),
        compiler_params=_params("arbitrary"),
        name="sample_post",
    )(oa, cols, gn_a, y, act, cols, dskip_x, gn_b, oc, od, rw, r_k, lnx_g, lnx_b)


def _head_rms(x, gain, width):
    outs = []
    for h in range(x.shape[1] // width):
        xh = x[:, h * width:(h + 1) * width]
        outs.append(xh * lax.rsqrt(jnp.mean(xh * xh, axis=-1, keepdims=True) + 1e-6) * gain)
    return jnp.concatenate(outs, axis=-1)


def _mem_kv_kernel(m_ref, g_ref, w_ref, gk_ref, k_ref, v_ref):
    m = m_ref[...]
    m = m * lax.rsqrt(jnp.mean(m * m, axis=-1, keepdims=True) + 1e-6) * g_ref[...]
    kv = _dot(m, w_ref[...])
    k_ref[...] = _head_rms(kv[:, :D_MEM], gk_ref[...], DH_M)
    v_ref[...] = kv[:, D_MEM:]


def _mem_kv(mem, g_src, w_kv, g_k):
    shp = jax.ShapeDtypeStruct((N_MEM, D_MEM), f32)
    return pl.pallas_call(
        _mem_kv_kernel, out_shape=[shp, shp],
        compiler_params=pltpu.CompilerParams(vmem_limit_bytes=VMEM_LIMIT), name="mem_kv",
    )(mem, g_src, w_kv, g_k)


def _mem_attn_prompt_kernel(q_ref, gq_ref, k_ref, v_ref, o_ref):
    q = _head_rms(q_ref[...], gq_ref[...], DH_M) * (DH_M ** -0.5)
    k, v = k_ref[...], v_ref[...]
    outs = []
    for h in range(H_M):
        hs = slice(h * DH_M, (h + 1) * DH_M)
        s = _dot(q[:, hs], k[:, hs], NT)
        p = jnp.exp(s - jnp.max(s, axis=-1, keepdims=True))
        outs.append(_dot(p, v[:, hs]) / jnp.sum(p, axis=-1, keepdims=True))
    o_ref[...] = jnp.concatenate(outs, axis=-1)


def _mem_attn_prompt(q, gq, k, v, seq=SEQ, tb=512):
    return pl.pallas_call(
        _mem_attn_prompt_kernel,
        grid=(seq // tb,),
        in_specs=[pl.BlockSpec((tb, D_MEM), lambda i: (i, 0)), _const((1, DH_M)), _const((N_MEM, D_MEM)),
                  _const((N_MEM, D_MEM))],
        out_specs=pl.BlockSpec((tb, D_MEM), lambda i: (i, 0)),
        out_shape=jax.ShapeDtypeStruct((seq, D_MEM), f32),
        compiler_params=_params("parallel"),
        name="mem_attn_prompt",
    )(q, gq, k, v)


MEM_BPB = 4


def _mem_attn_sample_kernel(q_ref, gq_ref, k_ref, v_ref, ind_ref, o_ref):
    for j in range(q_ref.shape[0]):
        q = q_ref[j]
        q = q * lax.rsqrt(jnp.mean(q * q, axis=-1, keepdims=True) + 1e-6) * gq_ref[...] * (DH_M ** -0.5)
        o_ref[j] = _rowhead_attend(k_ref[j], v_ref[j], q, ind_ref[...], 1)[0]


def _mem_attn_sample(q, gq, ck, cv, ind, layer):
    b = q.shape[0]
    bpb = MEM_BPB
    kv = pl.BlockSpec((None, bpb, N_MEM * H_M, DH_M), lambda i: (layer, i, 0, 0))
    return pl.pallas_call(
        _mem_attn_sample_kernel,
        grid=(b // bpb,),
        in_specs=[pl.BlockSpec((bpb, H_M, DH_M), lambda i: (i, 0, 0)), _const((1, DH_M)), kv, kv, _const((128, 128))],
        out_specs=pl.BlockSpec((bpb, 8, DH_M), lambda i: (i, 0, 0)),
        out_shape=jax.ShapeDtypeStruct((b, 8, DH_M), f32),
        compiler_params=_params("parallel"),
        name="mem_attn_sample",
    )(q, gq, ck, cv, ind)


def _cast_kernel(x_ref, o_ref):
    o_ref[...] = x_ref[...].astype(bf16)


def _table_bf16(tab, layer, rows=1024):
    n, d = tab.shape[1:]
    return pl.pallas_call(
        _cast_kernel,
        grid=(n // rows,),
        in_specs=[pl.BlockSpec((None, rows, d), lambda i: (layer, i, 0))],
        out_specs=pl.BlockSpec((rows, d), lambda i: (i, 0)),
        out_shape=jax.ShapeDtypeStruct((n, d), bf16),
        compiler_params=_params("parallel"),
        name="table_bf16",
    )(tab)


def _gelu(x):
    return 0.5 * x * (1.0 + jnp.tanh(0.7978845608028654 * (x + 0.044715 * x * x * x)))


def _peer_dense_kernel(xn_ref, u_ref, v_ref, ci_ref, cnt_ref, e2_ref, r2_ref, res_ref, o_ref, w_ref):
    e = pl.program_id(1)
    tb = xn_ref.shape[0]
    n_i = PEER_EB // N_KEYS

    @pl.when(e == 0)
    def _():
        o_ref[...] = res_ref[...]

    n_half = PEER_EB // PEER_SUB
    per = n_i // n_half
    parts = []
    for half in range(n_half):
        for ii in range(half * per, (half + 1) * per):
            w = jnp.zeros((N_KEYS, tb), bf16)
            for h in range(H_P):
                row = pl.ds(e * n_i + ii, 1)
                ci = ci_ref[0, h, row, :].astype(bf16)
                cnt = cnt_ref[0, h, row, :].astype(bf16)
                w = w + jnp.where(r2_ref[0, h] < cnt, e2_ref[0, h] * ci, jnp.zeros((), bf16))
            w_ref[ii * N_KEYS:(ii + 1) * N_KEYS, :] = w
        rows = slice(half * per * N_KEYS, (half + 1) * per * N_KEYS)
        hid = _gelu(_dot(u_ref[rows, :], xn_ref[...], NT))
        parts.append(_dot(hid.astype(bf16) * w_ref[rows, :], v_ref[rows, :], TN))
    o_ref[...] += sum(parts)


def _peer_dense(xn, u, v, sel, res):
    t = xn.shape[0]
    tb, eb = PEER_TB, PEER_EB
    once = pl.Buffered(1)
    selspec = pl.BlockSpec((1, H_P, N_KEYS, tb), lambda i, e: (i, 0, 0, 0), pipeline_mode=once)
    return pl.pallas_call(
        _peer_dense_kernel,
        grid=(t // tb, N_EXPERTS // eb),
        in_specs=[pl.BlockSpec((tb, D_MODEL), lambda i, e: (i, 0), pipeline_mode=once),
                  pl.BlockSpec((eb, D_MODEL), lambda i, e: (e, 0)),
                  pl.BlockSpec((eb, D_MODEL), lambda i, e: (e, 0)), selspec, selspec, selspec, selspec,
                  pl.BlockSpec((tb, D_MODEL), lambda i, e: (i, 0), pipeline_mode=once)],
        out_specs=pl.BlockSpec((tb, D_MODEL), lambda i, e: (i, 0)),
        out_shape=jax.ShapeDtypeStruct((t, D_MODEL), f32),
        scratch_shapes=[pltpu.VMEM((eb, tb), bf16)],
        compiler_params=_params("parallel", "arbitrary"),
        name="peer_dense",
    )(xn, u, v, *sel, res)


def _pad_w_in(w):
    a0, b0, c0, d0 = 0, 1552, 2840, 4376
    seg = lambda s, n: w[:, s:s + n]
    parts = [seg(d0, 1792), seg(b0 + 1024, 256), seg(a0, 512), seg(a0 + 512, 512), seg(a0 + 1024, 512),
             seg(b0, 512), seg(b0 + 512, 512), seg(c0, 512), seg(c0 + 512, 512), seg(c0 + 1024, 512),
             seg(a0 + 1536, 16), seg(b0 + 1280, 8), jnp.zeros((w.shape[0], IN_PAD - 6168), w.dtype)]
    return jnp.concatenate(parts, axis=1).astype(bf16)


def _layer_consts():
    ex = np.zeros((128, 512), np.float32)
    sel = np.zeros((8, 512), np.float32)
    for h in range(8):
        ex[GK_RANK + h, h * 64:(h + 1) * 64] = 1.0
        sel[h, h * 64] = 1.0
    ind2 = np.zeros((128, 256), np.float32)
    ind2[:DK_C, :128] = 1.0
    ind2[DK_C:, 128:] = 1.0
    ind1 = np.ones((128, 128), np.float32)
    return jnp.asarray(ex), jnp.asarray(sel), jnp.asarray(ind2), jnp.asarray(ind1)


def kernel(x_prompt, x_sample, cache_diff_k, cache_diff_v, cache_mem_k, cache_mem_v, state_gla, state_ssm, state_conv, state_rwkv, state_shift, page_table, mem_prompt, norm_mix, w_in, w_out, gla_wg2, gla_bg, gla_gn, conv_w, conv_b, dt_bias, a_log, d_skip, ssm_gn, dq_norm, dk_norm, lam_q, lam_k, diff_gn, shift_mu, w0, w2, a0, a2, g2, k_k, k_a, r_k, lnx_g, lnx_b, norm_mem, norm_memsrc, w_mq, w_mk, w_mv, w_mo, mq_norm, mk_norm, norm_ffn, peer_wq, peer_keys, peer_u, peer_v):
    nb = DEC_BATCH
    n_pages = page_table.shape[1]
    n_pool = cache_diff_k.shape[1]
    x = jnp.concatenate([x_prompt[0], x_sample[:, 0]], axis=0)
    pt_flat = page_table.reshape(-1)
    cos_p, sin_p = _rope_tables(jnp.arange(SEQ, dtype=jnp.int32))
    cos_s, sin_s = _rope_tables(jnp.full((nb,), PAST_LEN, jnp.int32))
    ex, sel8, ind2, ind1 = _layer_consts()
    ck_rows = cache_diff_k.reshape(DEPTH, n_pool, PAGE_SIZE * H_C, 2 * DK_C)
    cv_rows = cache_diff_v.reshape(DEPTH, n_pool, PAGE_SIZE * H_C, DV_C)
    mk_rows = cache_mem_k.reshape(DEPTH, nb, N_MEM * H_M, DH_M)
    mv_rows = cache_mem_v.reshape(DEPTH, nb, N_MEM * H_M, DH_M)
    gla_rows = state_gla.reshape(DEPTH, nb, H_A * DK_A, DV_A)
    ssm_rows = state_ssm.reshape(DEPTH, nb, H_B * P_B, N_B)
    rwkv_rows = state_rwkv.reshape(DEPTH, nb, H_D * N_D, N_D)
    row = lambda a: a.reshape(1, -1)
    rep64 = lambda a: jnp.repeat(a, 64).reshape(1, 512)
    outs = {n: [] for n in ('kp', 'vp', 'ks', 'vs', 'mk', 'mv', 'gla_p', 'gla_s', 'ssm_p', 'ssm_s', 'conv_p',
                            'conv_s', 'rwkv_p', 'rwkv_s', 'shift_p', 'shift_s')}
    for l in range(DEPTH):
        lam_init = 0.8 - 0.6 * math.exp(-0.3 * l)
        lq, lk = lam_q[l], lam_k[l]
        lam = jnp.exp(jnp.sum(lq[0] * lk[0])) - jnp.exp(jnp.sum(lq[1] * lk[1])) + lam_init
        lam = jnp.full((1, 128), lam, f32)
        wg2p = jnp.zeros((128, 256), f32).at[:GK_RANK].set(gla_wg2[l])
        w2a2 = jnp.zeros((128, 1024), f32).at[:64, :512].set(w2[l]).at[64:, 512:].set(a2[l])
        gq = jnp.tile(dq_norm[l].reshape(128), 4).reshape(1, 512)
        gk = jnp.tile(dk_norm[l].reshape(128), 4).reshape(1, 512)
        dtb_x, alog_x, dskip_x = rep64(dt_bias[l]), rep64(a_log[l]), rep64(d_skip[l])

        cols = _matmul(x, _pad_w_in(w_in[l]), gain=norm_mix[l])

        oa, gla_p = _gla_prompt(cols, wg2p, row(gla_bg[l]), row(gla_gn[l]), seq=SEQ)
        ob, ssm_p = _ssd_prompt(cols, conv_w[l], row(conv_b[l]), ex, dtb_x, alog_x, dskip_x, row(ssm_gn[l]), sel8,
                                seq=SEQ)
        qb, kf, kb = _diff_prep(cols, cos_p, sin_p, gq, gk, 0, SEQ, 512)
        oc = _diff_attn_prompt(qb, kb, cols, lam, row(diff_gn[l]), lam_init, seq=SEQ, tq=FLASH_TQ, tk=FLASH_TK)
        od, rwkv_p = _rwkv_prompt(cols, row(shift_mu[l]), row(w0[l]), row(a0[l]), row(k_k[l]), row(k_a[l]),
                                  row(r_k[l]), row(lnx_g[l]), row(lnx_b[l]), w2a2, g2[l], seq=SEQ)
        mix_p = jnp.concatenate([oa, ob, oc, od], axis=1)

        pre_w = [wg2p, row(gla_bg[l]), conv_w[l], row(conv_b[l]), ex, dtb_x, alog_x,
                 row(shift_mu[l]), row(w0[l]), row(a0[l]), row(k_k[l]), row(k_a[l]), w2a2, g2[l]]
        s_gq, s_gk, s_ga, s_act, s_dec, s_xd, s_rw = _sample_pre(cols, jnp.transpose(state_conv[l], (1, 0, 2)),
                                                                 state_shift[l][:, 0], pre_w, SEQ, nb)
        tail = cols[SEQ:]
        gla_s, oa_s = _state_step(_gla_step_kernel, "gla_step", gla_rows, l,
                                  [s_ga, s_gk, s_gq, tail[:, C_VA:C_VA + 512]], 512)
        ssm_s, y_s = _state_step(_ssd_step_kernel, "ssd_step", ssm_rows, l, [s_dec, s_xd, s_act[:, GROUP_W:]], 512)
        rwkv_s, od_s = _state_step(_rwkv_step_kernel, "rwkv_step", rwkv_rows, l, [s_rw], 512)
        qb_s, kf_s, _ = _diff_prep(cols, cos_s, sin_s, gq, gk, SEQ, nb, nb)
        vc_s = tail[:, C_VC:C_VC + 512]
        oc_s = _diff_decode(pt_flat, l, qb_s.reshape(nb, H_C, 128), kf_s.reshape(nb, H_C, 128), vc_s.reshape(nb, H_C, 128),
                            ck_rows, cv_rows, ind2, lam, row(diff_gn[l]), lam_init, n_pages)
        mix_s = _sample_post(cols, oa_s, y_s, s_act, oc_s[:, :H_C].reshape(nb, 512), od_s, s_rw, row(gla_gn[l]), dskip_x, row(ssm_gn[l]), row(r_k[l]),
                             row(lnx_g[l]), row(lnx_b[l]), SEQ, nb)

        x = _matmul(jnp.concatenate([mix_p, mix_s], axis=0), w_out[l].astype(bf16), res=x)

        mk_p, mv_p = _mem_kv(mem_prompt[0], row(norm_memsrc[l]),
                             jnp.concatenate([w_mk[l], w_mv[l]], axis=1).astype(bf16), row(mk_norm[l]))
        qm = _matmul(x, w_mq[l].astype(bf16), gain=norm_mem[l])
        om_p = _mem_attn_prompt(qm, row(mq_norm[l]), mk_p, mv_p, seq=SEQ)
        om_s = _mem_attn_sample(qm[SEQ:].reshape(nb, H_M, DH_M), row(mq_norm[l]), mk_rows, mv_rows, ind1, l)
        x = _matmul(jnp.concatenate([om_p, om_s[:, :H_M].reshape(nb, D_MEM)], axis=0), w_mo[l].astype(bf16), res=x)

        qp, xn = _matmul(x, peer_wq[l].astype(bf16), gain=norm_ffn[l], emit_xn=True, tn=512)
        picks = _peer_select(qp, peer_keys[l].reshape(2 * H_P, N_KEYS, D_PK // 2))
        x = _peer_dense(xn, _table_bf16(peer_u, l), _table_bf16(peer_v, l), picks, x)

        outs['kp'].append(kf.reshape(1, SEQ, H_C, 2 * DK_C))
        outs['vp'].append(cols[:SEQ, C_VC:C_VC + 512].reshape(1, SEQ, H_C, DV_C))
        outs['ks'].append(kf_s.reshape(nb, 1, H_C, 2 * DK_C))
        outs['vs'].append(vc_s.reshape(nb, 1, H_C, DV_C))
        outs['mk'].append(mk_p.reshape(1, N_MEM, H_M, DH_M))
        outs['mv'].append(mv_p.reshape(1, N_MEM, H_M, DH_M))
        outs['gla_p'].append(gla_p[None])
        outs['gla_s'].append(gla_s.reshape(nb, H_A, DK_A, DV_A))
        outs['ssm_p'].append(ssm_p[None])
        outs['ssm_s'].append(ssm_s.reshape(nb, H_B, P_B, N_B))
        u_p = jnp.concatenate([cols[SEQ - 3:SEQ, C_XS:C_XS + 512], cols[SEQ - 3:SEQ, C_BC:C_BC + 256]], axis=1)
        u_s = jnp.concatenate([tail[:, C_XS:C_XS + 512], tail[:, C_BC:C_BC + 256]], axis=1)
        outs['conv_p'].append(u_p[None])
        outs['conv_s'].append(jnp.concatenate([state_conv[l][:, 1:], u_s[:, None]], axis=1))
        outs['rwkv_p'].append(rwkv_p[None])
        outs['rwkv_s'].append(rwkv_s.reshape(nb, H_D, N_D, N_D))
        outs['shift_p'].append(cols[SEQ - 1:SEQ, C_D:C_D + 1792][None])
        outs['shift_s'].append(tail[:, C_D:C_D + 1792][:, None])
    st = {n: jnp.stack(v) for n, v in outs.items()}
    return (x[:SEQ][None], x[SEQ:][:, None], st['kp'], st['vp'], st['ks'], st['vs'], st['mk'], st['mv'],
            st['gla_p'], st['gla_s'], st['ssm_p'], st['ssm_s'], st['conv_p'], st['conv_s'],
            st['rwkv_p'], st['rwkv_s'], st['shift_p'], st['shift_s'])
```

```python
import functools
import math

import numpy as np
import jax
import jax.numpy as jnp
from jax import lax
from jax.experimental import pallas as pl
from jax.experimental.pallas import tpu as pltpu

f32 = jnp.float32
bf16 = jnp.bfloat16
HI = lax.Precision.HIGHEST

D_MODEL = 2048
SEQ = 8192
DEPTH = 2
DEC_BATCH = 128
PAST_LEN = 2048
PAGE_SIZE = 128
T_ALL = SEQ + DEC_BATCH

GROUP_W = 512
H_A, DK_A, DV_A, GK_RANK, GLA_TAU = 4, 64, 128, 16, 16.0
H_B, P_B, N_B, G_B, CONV_W, XBC_W = 8, 64, 64, 2, 4, 768
H_C, DK_C, DV_C, ROPE_THETA = 4, 64, 128, 10000.0
H_D, N_D, LNX_EPS = 8, 64, 64e-5
N_MEM, H_M, D_MEM, DH_M = 256, 4, 512, 128
N_KEYS, H_P, TOPK_P, D_PK = 128, 8, 16, 256
N_EXPERTS = N_KEYS * N_KEYS

C_D = 0
C_BC = 1792
C_QKA = 2048
C_VA = 2560
C_GA = 3072
C_Z = 3584
C_XS = 4096
C_QC = 4608
C_KC = 5120
C_VC = 5632
C_TAIL = 6144
IN_PAD = 6272

LANES = 128
VMEM_LIMIT = 56 * 1024 * 1024

NN = ((1,), (0,))
NT = ((1,), (1,))
TN = ((0,), (0,))


def _dot(a, b, dims=NN, hi=False):
    if hi:
        return lax.dot_general(a, b, (dims, ((), ())), precision=HI, preferred_element_type=f32)
    return lax.dot_general(a.astype(bf16), b.astype(bf16), (dims, ((), ())), preferred_element_type=f32)


def _softplus(x):
    return jnp.maximum(x, 0.0) + jnp.log(1.0 + jnp.exp(-jnp.abs(x)))


def _sigmoid(x):
    return 1.0 / (1.0 + jnp.exp(-x))


def _silu(x):
    return x * _sigmoid(x)


def _iota(shape, axis):
    return lax.broadcasted_iota(jnp.int32, shape, axis)


def _params(*sem):
    return pltpu.CompilerParams(dimension_semantics=sem, vmem_limit_bytes=VMEM_LIMIT)


def _mm_kernel(*refs, norm, residual, emit_xn):
    x_ref, g_ref, w_ref = refs[:3]
    res_ref = refs[3] if residual else None
    xn_ref = refs[-1]
    o_ref = refs[-3] if emit_xn else refs[-2]

    @pl.when(pl.program_id(1) == 0)
    def _():
        x = x_ref[...]
        if norm:
            x = x * lax.rsqrt(jnp.mean(x * x, axis=-1, keepdims=True) + 1e-6) * g_ref[...]
        xn_ref[...] = x.astype(bf16)
        if emit_xn:
            refs[-2][...] = xn_ref[...]

    acc = jnp.dot(xn_ref[...], w_ref[...], preferred_element_type=f32)
    if residual:
        acc = acc + res_ref[...]
    o_ref[...] = acc


def _matmul(x, w, gain=None, res=None, emit_xn=False, tm=None, tn=None):
    m, k = x.shape
    n = w.shape[1]
    tm = tm or _pick(m, (1040, 1024, 512, 256, 128))
    tn = tn or _pick(n, (1024, 896, 512, 256, 128))
    norm = gain is not None
    g = (gain if norm else jnp.ones((k,), f32)).reshape(1, k)
    args = [x, g, w]
    in_specs = [pl.BlockSpec((tm, k), lambda i, j: (i, 0)),
                pl.BlockSpec((1, k), lambda i, j: (0, 0)),
                pl.BlockSpec((k, tn), lambda i, j: (0, j))]
    if res is not None:
        args.append(res)
        in_specs.append(pl.BlockSpec((tm, tn), lambda i, j: (i, j)))
    out_specs = [pl.BlockSpec((tm, tn), lambda i, j: (i, j))]
    out_shape = [jax.ShapeDtypeStruct((m, n), f32)]
    if emit_xn:
        out_specs.append(pl.BlockSpec((tm, k), lambda i, j: (i, 0)))
        out_shape.append(jax.ShapeDtypeStruct((m, k), bf16))
    out = pl.pallas_call(
        functools.partial(_mm_kernel, norm=norm, residual=res is not None, emit_xn=emit_xn),
        grid=(m // tm, n // tn),
        in_specs=in_specs,
        out_specs=out_specs,
        out_shape=out_shape,
        scratch_shapes=[pltpu.VMEM((tm, k), bf16)],
        compiler_params=_params("parallel", "arbitrary"),
        name="mm",
    )(*args)
    return out if emit_xn else out[0]


def _pick(n, cands):
    for c in cands:
        if n % c == 0:
            return c
    return n


def _const(shape):
    nd = len(shape)
    return pl.BlockSpec(shape, lambda i, _n=nd: (0,) * _n)


def _cols(width, start, tb):
    assert start % width == 0
    return pl.BlockSpec((tb, width), lambda i, _c=start // width: (i, _c))


def _blockdiag_tri(tb, c):
    r = _iota((tb, tb), 0)
    s = _iota((tb, tb), 1)
    return jnp.where((r // c == s // c) & (s <= r), 1.0, 0.0).astype(f32)


def _segment_ones(n, seg):
    r = _iota((n, n), 0)
    s = _iota((n, n), 1)
    return jnp.where(r // seg == s // seg, 1.0, 0.0).astype(f32)


GLA_TB = 256
GLA_C = 16


def _gla_gate_log(tail, wg2p, bg):
    z = _dot(tail, wg2p, hi=True) + bg
    return -_softplus(-z) * (1.0 / GLA_TAU)


def _gla_out(o, g, gn):
    outs = []
    for h in range(H_A):
        oh = o[:, h * DV_A:(h + 1) * DV_A]
        oh = oh * lax.rsqrt(jnp.mean(oh * oh, axis=-1, keepdims=True) + 1e-6) * gn
        outs.append(oh * _silu(g[:, h * DV_A:(h + 1) * DV_A]))
    return jnp.concatenate(outs, axis=-1)


def _gla_kernel(qk_ref, v_ref, g_ref, tail_ref, wg2_ref, bg_ref, gn_ref, o_ref, sfin_ref, st_ref, w_ref):
    i = pl.program_id(0)

    @pl.when(i == 0)
    def _():
        st_ref[...] = jnp.zeros_like(st_ref)

    tb, c = GLA_TB, GLA_C
    qk = qk_ref[...]
    q = qk[:, :256] * (DK_A ** -0.5)
    k = qk[:, 256:]
    v = v_ref[...]
    la = _gla_gate_log(tail_ref[...], wg2_ref[...], bg_ref[...])
    b = _dot(_blockdiag_tri(tb, c), la, hi=True)
    e_r = _iota((256, 512), 0) // DK_A
    e_c = _iota((256, 512), 1) // DV_A
    expand = jnp.where(e_r == e_c, 1.0, 0.0).astype(bf16)
    s_idx = _iota((c, 256), 0)
    for j in range(tb // c):
        r0 = j * c
        qj, kj, bj, vj = q[r0:r0 + c], k[r0:r0 + c], b[r0:r0 + c], v[r0:r0 + c]
        for t in range(c):
            wt = qj[t:t + 1] * kj * jnp.exp(bj[t:t + 1] - bj)
            w_ref[t * c:(t + 1) * c, :] = jnp.where(s_idx <= t, wt, 0.0)
        att = _dot(w_ref[...], expand)
        o = jnp.sum(att.reshape(c, c, 512) * vj[None], axis=1)
        qe = qj * jnp.exp(bj)
        bl = bj[c - 1:c]
        ke = kj * jnp.exp(bl - bj)
        dl = jnp.exp(bl)
        inter = []
        for h in range(H_A):
            ks = slice(h * DK_A, (h + 1) * DK_A)
            st = st_ref[h]
            inter.append(_dot(qe[:, ks], st, NT))
            st_ref[h] = st * dl[:, ks] + _dot(vj[:, h * DV_A:(h + 1) * DV_A], ke[:, ks], TN)
        o = o + jnp.concatenate(inter, axis=-1)
        o_ref[r0:r0 + c, :] = _gla_out(o, g_ref[r0:r0 + c, :], gn_ref[...])

    @pl.when(i == pl.num_programs(0) - 1)
    def _():
        for h in range(H_A):
            sfin_ref[h] = st_ref[h].T


def _gla_prompt(cols, wg2p, bg, gn, seq=SEQ):
    tb = GLA_TB
    return pl.pallas_call(
        _gla_kernel,
        grid=(seq // tb,),
        in_specs=[_cols(512, C_QKA, tb), _cols(512, C_VA, tb), _cols(512, C_GA, tb), _cols(128, C_TAIL, tb),
                  _const((128, 256)), _const((1, 256)), _const((1, DV_A))],
        out_specs=[pl.BlockSpec((tb, 512), lambda i: (i, 0)), _const((H_A, DK_A, DV_A))],
        out_shape=[jax.ShapeDtypeStruct((seq, 512), f32), jax.ShapeDtypeStruct((H_A, DK_A, DV_A), f32)],
        scratch_shapes=[pltpu.VMEM((H_A, DV_A, DK_A), f32), pltpu.VMEM((GLA_C * GLA_C, 256), f32)],
        compiler_params=_params("arbitrary"),
        name="gla_prompt",
    )(cols, cols, cols, cols, wg2p, bg, gn)


SSD_TB = 256
SSD_C = 64


def _ssd_conv(ext, conv_w, conv_b, rows):
    out = conv_b
    for j in range(CONV_W):
        shifted = pltpu.roll(ext, j, 0) if j else ext
        out = out + shifted[8:8 + rows] * conv_w[CONV_W - 1 - j:CONV_W - j]
    return out


def _ssd_dt(tail, ex, dtb_x):
    return _softplus(_dot(tail, ex, hi=True) + dtb_x)


def _ssd_out(y, z, gn):
    y = y * _silu(z)
    w = GROUP_W // G_B
    outs = []
    for g in range(G_B):
        yg = y[:, g * w:(g + 1) * w]
        outs.append(yg * lax.rsqrt(jnp.mean(yg * yg, axis=-1, keepdims=True) + 1e-6) * gn[:, g * w:(g + 1) * w])
    return jnp.concatenate(outs, axis=-1)


def _ssd_kernel(z_ref, xs_ref, bc_ref, tail_ref, cw_ref, cb_ref, ex_ref, dtb_ref, alog_ref, dskip_ref, gn_ref,
                sel_ref, o_ref, hfin_ref, carry_ref, h_ref, y_ref):
    i = pl.program_id(0)

    @pl.when(i == 0)
    def _():
        carry_ref[...] = jnp.zeros_like(carry_ref)
        h_ref[...] = jnp.zeros_like(h_ref)

    tb, c = SSD_TB, SSD_C
    u = jnp.concatenate([xs_ref[...], bc_ref[...]], axis=-1)
    ext = jnp.concatenate([carry_ref[...], u], axis=0)
    carry_ref[...] = u[tb - 8:tb]
    act = _silu(_ssd_conv(ext, cw_ref[...], cb_ref[...], tb))
    xs = act[:, :GROUP_W]
    dtx = _ssd_dt(tail_ref[...], ex_ref[...], dtb_ref[...])
    la = dtx * (-jnp.exp(alog_ref[...]))
    b = _dot(_blockdiag_tri(tb, c), la, hi=True)
    brow = _dot(sel_ref[...], b, NT, hi=True)
    xd = xs * dtx
    tri = _iota((c, c), 1) <= _iota((c, c), 0)
    for ch in range(tb // c):
        r0 = ch * c
        rows = slice(r0, r0 + c)
        scores = []
        for g in range(G_B):
            bm = act[rows, GROUP_W + g * N_B:GROUP_W + (g + 1) * N_B]
            cm = act[rows, GROUP_W + G_B * N_B + g * N_B:GROUP_W + G_B * N_B + (g + 1) * N_B]
            scores.append((_dot(cm, bm, NT), bm, cm))
        for h in range(H_B):
            hs = slice(h * P_B, (h + 1) * P_B)
            sc, bm, cm = scores[h // (H_B // G_B)]
            bh = b[rows, hs]
            dec = jnp.where(tri, jnp.exp(bh - brow[h:h + 1, r0:r0 + c]), 0.0)
            xdh = xd[rows, hs]
            hst = h_ref[h]
            y = _dot(sc * dec, xdh) + _dot(cm, hst, NT) * jnp.exp(bh)
            bl = bh[c - 1:c]
            h_ref[h] = hst * jnp.exp(bl) + _dot(xdh * jnp.exp(bl - bh), bm, TN)
            y_ref[rows, hs] = y + xs[rows, hs] * dskip_ref[:, hs]
    o_ref[...] = _ssd_out(y_ref[...], z_ref[...], gn_ref[...])

    @pl.when(i == pl.num_programs(0) - 1)
    def _():
        hfin_ref[...] = h_ref[...]


def _ssd_prompt(cols, cw, cb, ex, dtb_x, alog_x, dskip_x, gn, sel, seq=SEQ):
    tb = SSD_TB
    return pl.pallas_call(
        _ssd_kernel,
        grid=(seq // tb,),
        in_specs=[_cols(512, C_Z, tb), _cols(512, C_XS, tb), _cols(256, C_BC, tb), _cols(128, C_TAIL, tb),
                  _const((CONV_W, XBC_W)), _const((1, XBC_W)), _const((128, 512)), _const((1, 512)), _const((1, 512)),
                  _const((1, 512)), _const((1, 512)), _const((8, 512))],
        out_specs=[pl.BlockSpec((tb, 512), lambda i: (i, 0)), _const((H_B, P_B, N_B))],
        out_shape=[jax.ShapeDtypeStruct((seq, 512), f32), jax.ShapeDtypeStruct((H_B, P_B, N_B), f32)],
        scratch_shapes=[pltpu.VMEM((8, XBC_W), f32), pltpu.VMEM((H_B, P_B, N_B), f32), pltpu.VMEM((tb, 512), f32)],
        compiler_params=_params("arbitrary"),
        name="ssd_prompt",
    )(cols, cols, cols, cols, cw, cb, ex, dtb_x, alog_x, dskip_x, gn, sel)


RWKV_TB = 256
RWKV_C = 64


def _rwkv_pre(x, xprev, mu, w0, a0, k_k, k_a, w2a2, g2):
    mixed = x + (xprev - x) * mu
    r = mixed[:, :512]
    kd = mixed[:, 512:1024]
    v = mixed[:, 1024:1536]
    lw = mixed[:, 1536:1664]
    lin = jnp.where(_iota(lw.shape, 1) < 64, jnp.tanh(lw), lw)
    wa = _dot(lin, w2a2, hi=True)
    w = -_softplus(-(w0 + wa[:, :512])) - 0.5
    a = _sigmoid(a0 + wa[:, 512:])
    g = _dot(_sigmoid(mixed[:, 1664:1792]), g2)
    kk = kd * k_k
    ss = _dot(kk * kk, _segment_ones(512, N_D), hi=True)
    kk = kk * lax.rsqrt(jnp.maximum(ss, 1e-24))
    kd = kd * (1.0 + (a - 1.0) * k_a)
    return r, w, kd, v, -kk, kk * a, g


def _rwkv_post(o, r, kd, v, g, r_k, lnx_g, lnx_b):
    seg = _segment_ones(512, N_D)
    mu = _dot(o, seg, hi=True) * (1.0 / N_D)
    d = o - mu
    var = _dot(d * d, seg, hi=True) * (1.0 / N_D)
    o = d * lax.rsqrt(var + LNX_EPS) * lnx_g + lnx_b
    o = o + _dot(r * kd * r_k, seg, hi=True) * v
    return o * g


def _rwkv_kernel(d_ref, mu_ref, w0_ref, a0_ref, kk_ref, ka_ref, rk_ref, lg_ref, lb_ref, w2a2_ref, g2_ref,
                 o_ref, sfin_ref, prev_ref, s_ref, oacc_ref):
    i = pl.program_id(0)

    @pl.when(i == 0)
    def _():
        prev_ref[...] = jnp.zeros_like(prev_ref)
        s_ref[...] = jnp.zeros_like(s_ref)

    tb, c = RWKV_TB, RWKV_C
    x = d_ref[...]
    xprev = jnp.where(_iota(x.shape, 0) == 0, prev_ref[0:1, :], pltpu.roll(x, 1, 0))
    prev_ref[0:1, :] = x[tb - 1:tb]
    r, w, kd, v, alpha, beta, g = _rwkv_pre(x, xprev, mu_ref[...], w0_ref[...], a0_ref[...], kk_ref[...], ka_ref[...],
                                            w2a2_ref[...], g2_ref[...])
    ld = -jnp.exp(w)
    cum = _dot(_blockdiag_tri(tb, c), ld, hi=True)
    at = alpha * jnp.exp(cum - ld)
    rt = r * jnp.exp(cum)
    einv = jnp.exp(-cum)
    kt = kd * einv
    bt = beta * einv
    ri = _iota((c, c), 0)
    ci = _iota((c, c), 1)
    strict, incl = ci < ri, ci <= ri
    eye = jnp.where(ri == ci, 1.0, 0.0).astype(f32)
    pairs = [(ch, h) for ch in range(tb // c) for h in range(H_D)]
    rows_of = lambda ch: slice(ch * c, (ch + 1) * c)
    lanes_of = lambda h: slice(h * N_D, (h + 1) * N_D)
    ar, kb, lk, mkb, tinv, p = {}, {}, {}, {}, {}, {}
    for ch, h in pairs:
        rows, hs = rows_of(ch), lanes_of(h)
        ar[ch, h] = jnp.concatenate([at[rows, hs], rt[rows, hs]], axis=0)
        kb[ch, h] = jnp.concatenate([kt[rows, hs], bt[rows, hs]], axis=0)
    for key in pairs:
        gram = _dot(ar[key], kb[key], NT)
        lk[key] = jnp.where(strict, gram[:c, :c], 0.0)
        p[key] = jnp.where(strict, gram[:c, c:], 0.0)
        mkb[key] = jnp.concatenate([jnp.where(incl, gram[c:, :c], 0.0), jnp.where(incl, gram[c:, c:], 0.0)], axis=1)
        tinv[key] = eye + p[key]
    for _ in range(5):
        for key in pairs:
            p[key] = _dot(p[key], p[key])
        for key in pairs:
            tinv[key] = tinv[key] + _dot(tinv[key], p[key])
    for ch in range(tb // c):
        rows = rows_of(ch)
        cl = cum[ch * c + c - 1:ch * c + c]
        efin = jnp.exp(cl - cum[rows])
        kfin = kd[rows] * efin
        bfin = beta[rows] * efin
        dfin = jnp.exp(cl)
        heads = range(H_D)
        s0 = [s_ref[h] for h in heads]
        ars = [_dot(ar[ch, h], s0[h], NT) for h in heads]
        lkv = [_dot(lk[ch, h], v[rows, lanes_of(h)]) for h in heads]
        u = [_dot(tinv[ch, h], ars[h][:c] + lkv[h]) for h in heads]
        vu = [jnp.concatenate([v[rows, lanes_of(h)], u[h]], axis=0) for h in heads]
        for h in heads:
            hs = lanes_of(h)
            oacc_ref[rows, hs] = ars[h][c:] + _dot(mkb[ch, h], vu[h])
            kbfin = jnp.concatenate([kfin[:, hs], bfin[:, hs]], axis=0)
            s_ref[h] = s0[h] * dfin[:, hs] + _dot(vu[h], kbfin, TN)
    o_ref[...] = _rwkv_post(oacc_ref[...], r, kd, v, g, rk_ref[...], lg_ref[...], lb_ref[...])

    @pl.when(i == pl.num_programs(0) - 1)
    def _():
        sfin_ref[...] = s_ref[...]


def _rwkv_prompt(cols, mu, w0, a0, k_k, k_a, r_k, lnx_g, lnx_b, w2a2, g2, seq=SEQ):
    tb = RWKV_TB
    vec = _const((1, 512))
    return pl.pallas_call(
        _rwkv_kernel,
        grid=(seq // tb,),
        in_specs=[_cols(1792, C_D, tb), _const((1, 1792)), vec, vec, vec, vec, vec, vec, vec,
                  _const((128, 1024)), _const((128, 512))],
        out_specs=[pl.BlockSpec((tb, 512), lambda i: (i, 0)), _const((H_D, N_D, N_D))],
        out_shape=[jax.ShapeDtypeStruct((seq, 512), f32), jax.ShapeDtypeStruct((H_D, N_D, N_D), f32)],
        scratch_shapes=[pltpu.VMEM((8, 1792), f32), pltpu.VMEM((H_D, N_D, N_D), f32), pltpu.VMEM((tb, 512), f32)],
        compiler_params=_params("arbitrary"),
        name="rwkv_prompt",
    )(cols, mu, w0, a0, k_k, k_a, r_k, lnx_g, lnx_b, w2a2, g2)


def _rope_tables(pos):
    half = DK_C // 2
    inv = ROPE_THETA ** (-jnp.arange(half, dtype=f32) / half)
    ang = pos.astype(f32)[:, None] * inv[None, :]
    cos, sin = jnp.cos(ang), jnp.sin(ang)
    return jnp.tile(jnp.concatenate([cos, cos], axis=-1), (1, 2)), jnp.tile(jnp.concatenate([-sin, sin], axis=-1), (1, 2))


def _qk_norm_rope(x, gain, cos, sin):
    ms = _dot(x * x, _segment_ones(512, DK_C), hi=True) * (1.0 / DK_C)
    x = x * lax.rsqrt(ms + 1e-6) * gain
    first = (_iota(x.shape, 1) % DK_C) < (DK_C // 2)
    partner = jnp.where(first, pltpu.roll(x, 512 - DK_C // 2, 1), pltpu.roll(x, DK_C // 2, 1))
    cos = jnp.concatenate([cos] * 4, axis=-1)
    sin = jnp.concatenate([sin] * 4, axis=-1)
    return x * cos + partner * sin


def _diff_prep_kernel(q_ref, k_ref, cos_ref, sin_ref, gq_ref, gk_ref, qb_ref, kf_ref, kb_ref):
    cos, sin = cos_ref[...], sin_ref[...]
    q = _qk_norm_rope(q_ref[...], gq_ref[...], cos, sin)
    k = _qk_norm_rope(k_ref[...], gk_ref[...], cos, sin)
    qb_ref[...] = (q * (DK_C ** -0.5)).astype(bf16)
    kf_ref[...] = k
    kb_ref[...] = k.astype(bf16)


def _diff_prep(cols, cos, sin, gq, gk, row0, rows, tb):
    assert row0 % tb == 0
    r0 = row0 // tb
    colspec = lambda start: pl.BlockSpec((tb, 512), lambda i, _c=start // 512: (i + r0, _c))
    out = pl.BlockSpec((tb, 512), lambda i: (i, 0))
    tab = pl.BlockSpec((tb, 128), lambda i: (i, 0))
    return pl.pallas_call(
        _diff_prep_kernel,
        grid=(rows // tb,),
        in_specs=[colspec(C_QC), colspec(C_KC), tab, tab, _const((1, 512)), _const((1, 512))],
        out_specs=[out, out, out],
        out_shape=[jax.ShapeDtypeStruct((rows, 512), bf16), jax.ShapeDtypeStruct((rows, 512), f32),
                   jax.ShapeDtypeStruct((rows, 512), bf16)],
        compiler_params=_params("parallel"),
        name="diff_prep",
    )(cols, cols, cos, sin, gq, gk)


def _diff_finish(o1, o2, lam, gn, lam_init):
    o = o1 - lam * o2
    return o * lax.rsqrt(jnp.mean(o * o, axis=-1, keepdims=True) + 1e-6) * gn * (1.0 - lam_init)


FLASH_TQ = 1024
FLASH_TK = 1024


def _flash_kernel(qi_ref, kj_ref, q_ref, k_ref, v_ref, lam_ref, gn_ref, o_ref, m_ref, l_ref, acc_ref,
                  *, tq, tk, lam_init):
    step_id = pl.program_id(1)
    qi, kj = qi_ref[step_id], kj_ref[step_id]

    @pl.when(kj == 0)
    def _():
        m_ref[...] = jnp.full_like(m_ref, -jnp.inf)
        l_ref[...] = jnp.zeros_like(l_ref)
        acc_ref[...] = jnp.zeros_like(acc_ref)

    def step(masked):
        q = q_ref[...]
        k = k_ref[...]
        v = v_ref[...].astype(bf16)
        lane = _iota(q.shape, 1)
        if masked:
            keep = (kj * tk + _iota((tk, tq), 0)) <= (qi * tq + _iota((tk, tq), 1))
        for m in range(2):
            qm = jnp.where((lane < DK_C) if m == 0 else (lane >= DK_C), q, jnp.zeros_like(q))
            s = _dot(k, qm, NT)
            if masked:
                s = jnp.where(keep, s, -jnp.inf)
            m_old = m_ref[m]
            m_new = jnp.maximum(m_old, jnp.max(s, axis=0, keepdims=True))
            p = jnp.exp(s - m_new)
            corr = jnp.exp(m_old - m_new)
            l_ref[m] = corr * l_ref[m] + jnp.sum(p, axis=0, keepdims=True)
            acc_ref[m] = corr * acc_ref[m] + _dot(v, p, TN)
            m_ref[m] = m_new

    last = kj * tk + tk - 1

    @pl.when(last <= qi * tq)
    def _():
        step(False)

    @pl.when(last > qi * tq)
    def _():
        step(True)

    @pl.when(kj == (qi * tq + tq - 1) // tk)
    def _():
        o1 = (acc_ref[0] / l_ref[0]).T
        o2 = (acc_ref[1] / l_ref[1]).T
        o_ref[...] = _diff_finish(o1, o2, lam_ref[...], gn_ref[...], lam_init)


def _diff_attn_prompt(qb, kb, cols, lam, gn, lam_init, seq=SEQ, tq=512, tk=512):
    pairs = [(i, j) for i in range(seq // tq) for j in range((i * tq + tq - 1) // tk + 1)]
    qi_tab = jnp.asarray([p[0] for p in pairs], jnp.int32)
    kj_tab = jnp.asarray([p[1] for p in pairs], jnp.int32)
    cst = lambda shp: pl.BlockSpec(shp, lambda h, s, qi, kj: (0, 0))
    return pl.pallas_call(
        functools.partial(_flash_kernel, tq=tq, tk=tk, lam_init=lam_init),
        grid_spec=pltpu.PrefetchScalarGridSpec(
            num_scalar_prefetch=2, grid=(H_C, len(pairs)),
            in_specs=[pl.BlockSpec((tq, 128), lambda h, s, qi, kj: (qi[s], h)),
                      pl.BlockSpec((tk, 128), lambda h, s, qi, kj: (kj[s], h)),
                      pl.BlockSpec((tk, DV_C), lambda h, s, qi, kj: (kj[s], C_VC // DV_C + h)),
                      cst((1, 128)), cst((1, DV_C))],
            out_specs=pl.BlockSpec((tq, DV_C), lambda h, s, qi, kj: (qi[s], h)),
            scratch_shapes=[pltpu.VMEM((2, 1, tq), f32), pltpu.VMEM((2, 1, tq), f32), pltpu.VMEM((2, DV_C, tq), f32)]),
        out_shape=jax.ShapeDtypeStruct((seq, 512), f32),
        compiler_params=_params("parallel", "arbitrary"),
        name="diff_attn_prompt",
    )(qi_tab, kj_tab, qb, kb, cols, lam, gn)


PEER_SEL_TB = 128
PEER_TB = 640
PEER_EB = 512
PEER_SUB = 256
_PAIRS = [(a, b) for a in range(TOPK_P) for b in range(TOPK_P) if (a + 1) * (b + 1) <= TOPK_P]
_NPAIR = -(-len(_PAIRS) // 8) * 8


def _top16(s, n_idx, want_rank):
    rank = jnp.full(s.shape, float(TOPK_P), f32) if want_rank else None
    tops, idxs = [], []
    work = s
    for k in range(TOPK_P):
        m = jnp.max(work, axis=0, keepdims=True)
        idx = jnp.min(jnp.where(work == m, n_idx, float(N_KEYS)), axis=0, keepdims=True)
        hit = n_idx == idx
        if want_rank:
            rank = jnp.where(hit, float(k), rank)
        work = jnp.where(hit, -jnp.inf, work)
        tops.append(m)
        idxs.append(idx)
    return rank, jnp.concatenate(tops, axis=0), jnp.concatenate(idxs, axis=0)


def _peer_select_kernel(q_ref, keys_ref, pk1_ref, flat_ref, ci_ref, cnt_ref, e2_ref, r2_ref):
    tb = q_ref.shape[0]
    flat = flat_ref[...]
    n_idx = _iota((N_KEYS, tb), 0).astype(f32)
    for h in range(H_P):
        scores = [_dot(keys_ref[hx], q_ref[:, hx * 128:(hx + 1) * 128], NT, hi=True)
                  for hx in (2 * h, 2 * h + 1)]
        _, t1, idx1 = _top16(scores[0], n_idx, False)
        rank2, t2, _ = _top16(scores[1], n_idx, True)
        cand = jnp.concatenate([t1[a:a + 1] + t2[b:b + 1] for a, b in _PAIRS]
                               + [jnp.full((_NPAIR - len(_PAIRS), tb), -jnp.inf, f32)], axis=0)
        work = cand
        sel = jnp.zeros(cand.shape, f32)
        for _ in range(TOPK_P):
            m = jnp.max(work, axis=0, keepdims=True)
            idx = jnp.min(jnp.where(work == m, flat, 4096.0), axis=0, keepdims=True)
            hit = flat == idx
            sel = jnp.where(hit, 1.0, sel)
            work = jnp.where(hit, -jnp.inf, work)
        top = t1[0:1] + t2[0:1]
        z = jnp.sum(sel * jnp.exp(jnp.where(sel > 0, cand - top, 0.0)), axis=0, keepdims=True)
        cnt = _dot(pk1_ref[...], sel)
        cnt_i = jnp.zeros((N_KEYS, tb), f32)
        for k1 in range(TOPK_P):
            cnt_i = jnp.where(n_idx == idx1[k1:k1 + 1], cnt[k1:k1 + 1], cnt_i)
        ci_ref[0, h] = jnp.exp(scores[0] - t1[0:1]) / z
        cnt_ref[0, h] = cnt_i
        e2_ref[0, h] = jnp.exp(scores[1] - t2[0:1]).astype(bf16)
        r2_ref[0, h] = rank2.astype(bf16)


def _peer_select(q, keys):
    t = q.shape[0]
    tb, per = PEER_SEL_TB, PEER_TB // PEER_SEL_TB
    pk1 = np.zeros((TOPK_P, _NPAIR), np.float32)
    flat = np.full((_NPAIR, 1), 8192.0, np.float32)
    for r, (a, b) in enumerate(_PAIRS):
        pk1[a, r] = 1.0
        flat[r, 0] = a * TOPK_P + b
    out = pl.BlockSpec((1, H_P, N_KEYS, tb), lambda i: (i // per, 0, 0, i % per))
    shp = lambda dt: jax.ShapeDtypeStruct((t // PEER_TB, H_P, N_KEYS, PEER_TB), dt)
    return pl.pallas_call(
        _peer_select_kernel,
        grid=(t // tb,),
        in_specs=[pl.BlockSpec((tb, 2048), lambda i: (i, 0)), _const((2 * H_P, N_KEYS, 128)),
                  _const((TOPK_P, _NPAIR)), _const((_NPAIR, 1))],
        out_specs=[out, out, out, out],
        out_shape=[shp(f32), shp(f32), shp(bf16), shp(bf16)],
        compiler_params=_params("parallel"),
        name="peer_select",
    )(q, keys, jnp.asarray(pk1), jnp.asarray(flat))


def _sample_pre_kernel(qk_ref, tail_ref, xs_ref, bc_ref, d_ref, conv_ref, shift_ref,
                       wg2_ref, bg_ref, cw_ref, cb_ref, ex_ref, dtb_ref, alog_ref,
                       mu_ref, w0_ref, a0_ref, kk_ref, ka_ref, w2a2_ref, g2_ref,
                       gq_ref, gk_ref, ga_ref, act_ref, sdec_ref, sxd_ref, rw_ref):
    qk = qk_ref[...]
    la = _gla_gate_log(tail_ref[...], wg2_ref[...], bg_ref[...])
    gq_ref[...] = qk[:, :256] * (DK_A ** -0.5)
    gk_ref[...] = qk[:, 256:]
    ga_ref[...] = jnp.exp(la)
    u = jnp.concatenate([xs_ref[...], bc_ref[...]], axis=-1)
    cw = cw_ref[...]
    conv = cb_ref[...] + u * cw[CONV_W - 1:CONV_W]
    for j in range(CONV_W - 1):
        conv = conv + conv_ref[j] * cw[j:j + 1]
    act = _silu(conv)
    act_ref[...] = act
    dtx = _ssd_dt(tail_ref[...], ex_ref[...], dtb_ref[...])
    sdec_ref[...] = jnp.exp(dtx * (-jnp.exp(alog_ref[...])))
    sxd_ref[...] = act[:, :GROUP_W] * dtx
    r, w, kd, v, alpha, beta, g = _rwkv_pre(d_ref[...], shift_ref[...], mu_ref[...], w0_ref[...], a0_ref[...],
                                            kk_ref[...], ka_ref[...], w2a2_ref[...], g2_ref[...])
    for n, t in enumerate((r, jnp.exp(-jnp.exp(w)), kd, v, alpha, beta, g)):
        rw_ref[n] = t


def _sample_pre(cols, conv_st, shift_st, wts, row0, b):
    tb = b
    assert row0 % tb == 0
    r0 = row0 // tb
    cs = lambda w, start: pl.BlockSpec((tb, w), lambda i, _c=start // w: (r0, _c))
    full = lambda *s: jax.ShapeDtypeStruct(s, f32)
    return pl.pallas_call(
        _sample_pre_kernel,
        grid=(1,),
        in_specs=[cs(512, C_QKA), cs(128, C_TAIL), cs(512, C_XS), cs(256, C_BC), cs(1792, C_D),
                  _const((CONV_W - 1, b, XBC_W)), _const((b, 1792))] + [_const(w.shape) for w in wts],
        out_specs=[_const((b, 256))] * 3 + [_const((b, XBC_W)), _const((b, 512)), _const((b, 512)), _const((7, b, 512))],
        out_shape=[full(b, 256)] * 3 + [full(b, XBC_W), full(b, 512), full(b, 512), full(7, b, 512)],
        compiler_params=_params("arbitrary"),
        name="sample_pre",
    )(cols, cols, cols, cols, cols, conv_st, shift_st, *wts)


def _rows_to_tile(row, heads, width, reps):
    return jnp.concatenate([jnp.broadcast_to(row[:, h * width:(h + 1) * width], (reps, width)) for h in range(heads)], axis=0)


STEP_BPB = 8


def _gla_step_kernel(s_ref, a_ref, k_ref, q_ref, v_ref, sn_ref, o_ref):
    a_t, k_t, q_t = a_ref[...].T, k_ref[...].T, q_ref[...].T
    for j in range(s_ref.shape[0]):
        s = a_t[:, j:j + 1] * s_ref[j] + k_t[:, j:j + 1] * _rows_to_tile(v_ref[j:j + 1, :], H_A, DV_A, DK_A)
        sn_ref[j] = s
        qs = q_t[:, j:j + 1] * s
        o_ref[j:j + 1, :] = jnp.concatenate(
            [jnp.sum(qs[h * DK_A:(h + 1) * DK_A], axis=0, keepdims=True) for h in range(H_A)], axis=-1)


def _ssd_step_kernel(h_ref, dec_ref, xd_ref, bc_ref, hn_ref, y_ref):
    reps = (H_B // G_B) * P_B
    dec_t, xd_t = dec_ref[...].T, xd_ref[...].T
    ys = []
    for j in range(h_ref.shape[0]):
        bc = bc_ref[j:j + 1, :]
        hn = dec_t[:, j:j + 1] * h_ref[j] + xd_t[:, j:j + 1] * _rows_to_tile(bc[:, :G_B * N_B], G_B, N_B, reps)
        hn_ref[j] = hn
        ys.append(jnp.sum(hn * _rows_to_tile(bc[:, G_B * N_B:], G_B, N_B, reps), axis=-1, keepdims=True))
    y_ref[...] = jnp.concatenate(ys, axis=-1).T


def _rwkv_step_kernel(s_ref, rows_ref, sn_ref, o_ref):
    v_t = rows_ref[3].T
    os_ = []
    for j in range(s_ref.shape[0]):
        tile = lambda n, _j=j: _rows_to_tile(rows_ref[n, _j:_j + 1, :], H_D, N_D, N_D)
        s = s_ref[j]
        sa = jnp.sum(s * tile(4), axis=-1, keepdims=True)
        s = s * tile(1) + sa * tile(5) + v_t[:, j:j + 1] * tile(2)
        sn_ref[j] = s
        os_.append(jnp.sum(s * tile(0), axis=-1, keepdims=True))
    o_ref[...] = jnp.concatenate(os_, axis=-1).T


def _state_step(kernel, name, states, layer, ins, out_width):
    _, b, r, c = states.shape
    bpb = STEP_BPB

    def spec(a):
        if a.ndim == 2:
            return pl.BlockSpec((bpb, a.shape[1]), lambda i: (i, 0))
        return pl.BlockSpec((a.shape[0], bpb, a.shape[2]), lambda i: (0, i, 0))

    return pl.pallas_call(
        kernel,
        grid=(b // bpb,),
        in_specs=[pl.BlockSpec((None, bpb, r, c), lambda i: (layer, i, 0, 0))] + [spec(a) for a in ins],
        out_specs=[pl.BlockSpec((bpb, r, c), lambda i: (i, 0, 0)), pl.BlockSpec((bpb, out_width), lambda i: (i, 0))],
        out_shape=[jax.ShapeDtypeStruct((b, r, c), f32), jax.ShapeDtypeStruct((b, out_width), f32)],
        compiler_params=_params("parallel"),
        name=name,
    )(states, *ins)


def _rowhead_attend(kx, vx, q4, ind, n_maps, tail_rows=None):
    r = kx.shape[0]
    g = r // 8
    q8 = jnp.concatenate([q4, q4], axis=0)
    qt = jnp.broadcast_to(q8[None], (g, 8, 128)).reshape(r, 128)
    s3 = _dot(kx * qt, ind).reshape(g, 8, 128 * n_maps)
    if tail_rows is not None:
        last = jnp.where(_iota((1, 8, 128 * n_maps), 1) < tail_rows, s3[g - 1:g], -jnp.inf)
        s3 = jnp.concatenate([s3[:g - 1], last], axis=0)
    m8 = jnp.max(s3, axis=0)
    mh = jnp.maximum(m8, pltpu.roll(m8, 4, 0))
    p3 = jnp.exp(s3 - mh[None])
    l8 = jnp.sum(p3, axis=0)
    lh = l8 + pltpu.roll(l8, 4, 0)
    v3 = vx.reshape(g, 8, 128)
    outs = []
    for m in range(n_maps):
        ms = slice(m * 128, (m + 1) * 128)
        pv = jnp.sum(p3[:, :, ms] * v3, axis=0)
        pv = pv + pltpu.roll(pv, 4, 0)
        outs.append(pv / lh[:, ms])
    return outs


def _diff_decode_kernel(pt_ref, q_ref, ks_ref, vs_ref, *rest, lam_init, n_pages):
    del pt_ref
    k_refs, v_refs = rest[:n_pages], rest[n_pages:2 * n_pages]
    ind_ref, lam_ref, gn_ref, o_ref = rest[2 * n_pages:]
    own = lambda ref: jnp.concatenate([ref[0], ref[0]], axis=0)
    kx = jnp.concatenate([r[...] for r in k_refs] + [own(ks_ref)], axis=0)
    vx = jnp.concatenate([r[...] for r in v_refs] + [own(vs_ref)], axis=0)
    o1, o2 = _rowhead_attend(kx, vx, q_ref[0].astype(f32), ind_ref[...], 2, tail_rows=H_C)
    o_ref[0] = _diff_finish(o1, o2, lam_ref[...], gn_ref[...], lam_init)


def _diff_decode(pt_flat, layer, qb, ks, vs, ck, cv, ind, lam, gn, lam_init, n_pages):
    b = qb.shape[0]
    rows = ck.shape[2]
    row = pl.BlockSpec((1, H_C, 128), lambda i, pt: (i, 0, 0))
    page = lambda j: pl.BlockSpec((None, None, rows, 128), lambda i, pt, _j=j: (layer, pt[i * n_pages + _j], 0, 0))
    cst = lambda shp: pl.BlockSpec(shp, lambda i, pt: (0, 0))
    pages = [page(j) for j in range(n_pages)]
    return pl.pallas_call(
        functools.partial(_diff_decode_kernel, lam_init=lam_init, n_pages=n_pages),
        grid_spec=pltpu.PrefetchScalarGridSpec(
            num_scalar_prefetch=1, grid=(b,),
            in_specs=[row, row, row] + pages + pages + [cst((128, 256)), cst((1, 128)), cst((1, DV_C))],
            out_specs=pl.BlockSpec((1, 8, 128), lambda i, pt: (i, 0, 0))),
        out_shape=jax.ShapeDtypeStruct((b, 8, 128), f32),
        compiler_params=_params("parallel"),
        name="diff_decode",
    )(pt_flat, qb, ks, vs, *([ck] * n_pages), *([cv] * n_pages), ind, lam, gn)


def _sample_post_kernel(oa_ref, ga_ref, gn_a_ref, y_ref, act_ref, z_ref, dskip_ref, gn_b_ref, oc_ref,
                        od_ref, rw_ref, rk_ref, lg_ref, lb_ref, o_ref):
    oa = _gla_out(oa_ref[...], ga_ref[...], gn_a_ref[...])
    ob = _ssd_out(y_ref[...] + act_ref[:, :GROUP_W] * dskip_ref[...], z_ref[...], gn_b_ref[...])
    od = _rwkv_post(od_ref[...], rw_ref[0], rw_ref[2], rw_ref[3], rw_ref[6], rk_ref[...], lg_ref[...], lb_ref[...])
    o_ref[...] = jnp.concatenate([oa, ob, oc_ref[...], od], axis=-1)


def _sample_post(cols, oa, y, act, oc, od, rw, gn_a, dskip_x, gn_b, r_k, lnx_g, lnx_b, row0, b):
    r0 = row0 // b
    cs = lambda w, start: pl.BlockSpec((b, w), lambda i, _c=start // w: (r0, _c))
    c512 = _const((b, 512))
    v512 = _const((1, 512))
    return pl.pallas_call(
        _sample_post_kernel,
        grid=(1,),
        in_specs=[c512, cs(512, C_GA), _const((1, DV_A)), c512, _const((b, XBC_W)), cs(512, C_Z), v512, v512, c512,
                  c512, _const((7, b, 512)), v512, v512, v512],
        out_specs=_const((b, D_MODEL)),
        out_shape=jax.ShapeDtypeStruct((b, D_MODEL), f32),
        compiler_params=_params("arbitrary"),
        name="sample_post",
    )(oa, cols, gn_a, y, act, cols, dskip_x, gn_b, oc, od, rw, r_k, lnx_g, lnx_b)


def _head_rms(x, gain, width):
    outs = []
    for h in range(x.shape[1] // width):
        xh = x[:, h * width:(h + 1) * width]
        outs.append(xh * lax.rsqrt(jnp.mean(xh * xh, axis=-1, keepdims=True) + 1e-6) * gain)
    return jnp.concatenate(outs, axis=-1)


def _mem_kv_kernel(m_ref, g_ref, w_ref, gk_ref, k_ref, v_ref):
    m = m_ref[...]
    m = m * lax.rsqrt(jnp.mean(m * m, axis=-1, keepdims=True) + 1e-6) * g_ref[...]
    kv = _dot(m, w_ref[...])
    k_ref[...] = _head_rms(kv[:, :D_MEM], gk_ref[...], DH_M)
    v_ref[...] = kv[:, D_MEM:]


def _mem_kv(mem, g_src, w_kv, g_k):
    shp = jax.ShapeDtypeStruct((N_MEM, D_MEM), f32)
    return pl.pallas_call(
        _mem_kv_kernel, out_shape=[shp, shp],
        compiler_params=pltpu.CompilerParams(vmem_limit_bytes=VMEM_LIMIT), name="mem_kv",
    )(mem, g_src, w_kv, g_k)


def _mem_attn_prompt_kernel(q_ref, gq_ref, k_ref, v_ref, o_ref):
    q = _head_rms(q_ref[...], gq_ref[...], DH_M) * (DH_M ** -0.5)
    k, v = k_ref[...], v_ref[...]
    outs = []
    for h in range(H_M):
        hs = slice(h * DH_M, (h + 1) * DH_M)
        s = _dot(q[:, hs], k[:, hs], NT)
        p = jnp.exp(s - jnp.max(s, axis=-1, keepdims=True))
        outs.append(_dot(p, v[:, hs]) / jnp.sum(p, axis=-1, keepdims=True))
    o_ref[...] = jnp.concatenate(outs, axis=-1)


def _mem_attn_prompt(q, gq, k, v, seq=SEQ, tb=512):
    return pl.pallas_call(
        _mem_attn_prompt_kernel,
        grid=(seq // tb,),
        in_specs=[pl.BlockSpec((tb, D_MEM), lambda i: (i, 0)), _const((1, DH_M)), _const((N_MEM, D_MEM)),
                  _const((N_MEM, D_MEM))],
        out_specs=pl.BlockSpec((tb, D_MEM), lambda i: (i, 0)),
        out_shape=jax.ShapeDtypeStruct((seq, D_MEM), f32),
        compiler_params=_params("parallel"),
        name="mem_attn_prompt",
    )(q, gq, k, v)


MEM_BPB = 4


def _mem_attn_sample_kernel(q_ref, gq_ref, k_ref, v_ref, ind_ref, o_ref):
    for j in range(q_ref.shape[0]):
        q = q_ref[j]
        q = q * lax.rsqrt(jnp.mean(q * q, axis=-1, keepdims=True) + 1e-6) * gq_ref[...] * (DH_M ** -0.5)
        o_ref[j] = _rowhead_attend(k_ref[j], v_ref[j], q, ind_ref[...], 1)[0]


def _mem_attn_sample(q, gq, ck, cv, ind, layer):
    b = q.shape[0]
    bpb = MEM_BPB
    kv = pl.BlockSpec((None, bpb, N_MEM * H_M, DH_M), lambda i: (layer, i, 0, 0))
    return pl.pallas_call(
        _mem_attn_sample_kernel,
        grid=(b // bpb,),
        in_specs=[pl.BlockSpec((bpb, H_M, DH_M), lambda i: (i, 0, 0)), _const((1, DH_M)), kv, kv, _const((128, 128))],
        out_specs=pl.BlockSpec((bpb, 8, DH_M), lambda i: (i, 0, 0)),
        out_shape=jax.ShapeDtypeStruct((b, 8, DH_M), f32),
        compiler_params=_params("parallel"),
        name="mem_attn_sample",
    )(q, gq, ck, cv, ind)


def _cast_kernel(x_ref, o_ref):
    o_ref[...] = x_ref[...].astype(bf16)


def _table_bf16(tab, layer, rows=1024):
    n, d = tab.shape[1:]
    return pl.pallas_call(
        _cast_kernel,
        grid=(n // rows,),
        in_specs=[pl.BlockSpec((None, rows, d), lambda i: (layer, i, 0))],
        out_specs=pl.BlockSpec((rows, d), lambda i: (i, 0)),
        out_shape=jax.ShapeDtypeStruct((n, d), bf16),
        compiler_params=_params("parallel"),
        name="table_bf16",
    )(tab)


def _gelu(x):
    return 0.5 * x * (1.0 + jnp.tanh(0.7978845608028654 * (x + 0.044715 * x * x * x)))


def _peer_dense_kernel(xn_ref, u_ref, v_ref, ci_ref, cnt_ref, e2_ref, r2_ref, res_ref, o_ref, w_ref):
    e = pl.program_id(1)
    tb = xn_ref.shape[0]
    n_i = PEER_EB // N_KEYS

    @pl.when(e == 0)
    def _():
        o_ref[...] = res_ref[...]

    n_half = PEER_EB // PEER_SUB
    per = n_i // n_half
    parts = []
    for half in range(n_half):
        for ii in range(half * per, (half + 1) * per):
            w = jnp.zeros((N_KEYS, tb), bf16)
            for h in range(H_P):
                row = pl.ds(e * n_i + ii, 1)
                ci = ci_ref[0, h, row, :].astype(bf16)
                cnt = cnt_ref[0, h, row, :].astype(bf16)
                w = w + jnp.where(r2_ref[0, h] < cnt, e2_ref[0, h] * ci, jnp.zeros((), bf16))
            w_ref[ii * N_KEYS:(ii + 1) * N_KEYS, :] = w
        rows = slice(half * per * N_KEYS, (half + 1) * per * N_KEYS)
        hid = _gelu(_dot(u_ref[rows, :], xn_ref[...], NT))
        parts.append(_dot(hid.astype(bf16) * w_ref[rows, :], v_ref[rows, :], TN))
    o_ref[...] += sum(parts)


def _peer_dense(xn, u, v, sel, res):
    t = xn.shape[0]
    tb, eb = PEER_TB, PEER_EB
    once = pl.Buffered(1)
    selspec = pl.BlockSpec((1, H_P, N_KEYS, tb), lambda i, e: (i, 0, 0, 0), pipeline_mode=once)
    return pl.pallas_call(
        _peer_dense_kernel,
        grid=(t // tb, N_EXPERTS // eb),
        in_specs=[pl.BlockSpec((tb, D_MODEL), lambda i, e: (i, 0), pipeline_mode=once),
                  pl.BlockSpec((eb, D_MODEL), lambda i, e: (e, 0)),
                  pl.BlockSpec((eb, D_MODEL), lambda i, e: (e, 0)), selspec, selspec, selspec, selspec,
                  pl.BlockSpec((tb, D_MODEL), lambda i, e: (i, 0), pipeline_mode=once)],
        out_specs=pl.BlockSpec((tb, D_MODEL), lambda i, e: (i, 0)),
        out_shape=jax.ShapeDtypeStruct((t, D_MODEL), f32),
        scratch_shapes=[pltpu.VMEM((eb, tb), bf16)],
        compiler_params=_params("parallel", "arbitrary"),
        name="peer_dense",
    )(xn, u, v, *sel, res)


def _pad_w_in(w):
    a0, b0, c0, d0 = 0, 1552, 2840, 4376
    seg = lambda s, n: w[:, s:s + n]
    parts = [seg(d0, 1792), seg(b0 + 1024, 256), seg(a0, 512), seg(a0 + 512, 512), seg(a0 + 1024, 512),
             seg(b0, 512), seg(b0 + 512, 512), seg(c0, 512), seg(c0 + 512, 512), seg(c0 + 1024, 512),
             seg(a0 + 1536, 16), seg(b0 + 1280, 8), jnp.zeros((w.shape[0], IN_PAD - 6168), w.dtype)]
    return jnp.concatenate(parts, axis=1).astype(bf16)


def _layer_consts():
    ex = np.zeros((128, 512), np.float32)
    sel = np.zeros((8, 512), np.float32)
    for h in range(8):
        ex[GK_RANK + h, h * 64:(h + 1) * 64] = 1.0
        sel[h, h * 64] = 1.0
    ind2 = np.zeros((128, 256), np.float32)
    ind2[:DK_C, :128] = 1.0
    ind2[DK_C:, 128:] = 1.0
    ind1 = np.ones((128, 128), np.float32)
    return jnp.asarray(ex), jnp.asarray(sel), jnp.asarray(ind2), jnp.asarray(ind1)


def kernel(x_prompt, x_sample, cache_diff_k, cache_diff_v, cache_mem_k, cache_mem_v, state_gla, state_ssm, state_conv, state_rwkv, state_shift, page_table, mem_prompt, norm_mix, w_in, w_out, gla_wg2, gla_bg, gla_gn, conv_w, conv_b, dt_bias, a_log, d_skip, ssm_gn, dq_norm, dk_norm, lam_q, lam_k, diff_gn, shift_mu, w0, w2, a0, a2, g2, k_k, k_a, r_k, lnx_g, lnx_b, norm_mem, norm_memsrc, w_mq, w_mk, w_mv, w_mo, mq_norm, mk_norm, norm_ffn, peer_wq, peer_keys, peer_u, peer_v):
    nb = DEC_BATCH
    n_pages = page_table.shape[1]
    n_pool = cache_diff_k.shape[1]
    x = jnp.concatenate([x_prompt[0], x_sample[:, 0]], axis=0)
    pt_flat = page_table.reshape(-1)
    cos_p, sin_p = _rope_tables(jnp.arange(SEQ, dtype=jnp.int32))
    cos_s, sin_s = _rope_tables(jnp.full((nb,), PAST_LEN, jnp.int32))
    ex, sel8, ind2, ind1 = _layer_consts()
    ck_rows = cache_diff_k.reshape(DEPTH, n_pool, PAGE_SIZE * H_C, 2 * DK_C)
    cv_rows = cache_diff_v.reshape(DEPTH, n_pool, PAGE_SIZE * H_C, DV_C)
    mk_rows = cache_mem_k.reshape(DEPTH, nb, N_MEM * H_M, DH_M)
    mv_rows = cache_mem_v.reshape(DEPTH, nb, N_MEM * H_M, DH_M)
    gla_rows = state_gla.reshape(DEPTH, nb, H_A * DK_A, DV_A)
    ssm_rows = state_ssm.reshape(DEPTH, nb, H_B * P_B, N_B)
    rwkv_rows = state_rwkv.reshape(DEPTH, nb, H_D * N_D, N_D)
    row = lambda a: a.reshape(1, -1)
    rep64 = lambda a: jnp.repeat(a, 64).reshape(1, 512)
    outs = {n: [] for n in ('kp', 'vp', 'ks', 'vs', 'mk', 'mv', 'gla_p', 'gla_s', 'ssm_p', 'ssm_s', 'conv_p',
                            'conv_s', 'rwkv_p', 'rwkv_s', 'shift_p', 'shift_s')}
    for l in range(DEPTH):
        lam_init = 0.8 - 0.6 * math.exp(-0.3 * l)
        lq, lk = lam_q[l], lam_k[l]
        lam = jnp.exp(jnp.sum(lq[0] * lk[0])) - jnp.exp(jnp.sum(lq[1] * lk[1])) + lam_init
        lam = jnp.full((1, 128), lam, f32)
        wg2p = jnp.zeros((128, 256), f32).at[:GK_RANK].set(gla_wg2[l])
        w2a2 = jnp.zeros((128, 1024), f32).at[:64, :512].set(w2[l]).at[64:, 512:].set(a2[l])
        gq = jnp.tile(dq_norm[l].reshape(128), 4).reshape(1, 512)
        gk = jnp.tile(dk_norm[l].reshape(128), 4).reshape(1, 512)
        dtb_x, alog_x, dskip_x = rep64(dt_bias[l]), rep64(a_log[l]), rep64(d_skip[l])

        cols = _matmul(x, _pad_w_in(w_in[l]), gain=norm_mix[l])

        oa, gla_p = _gla_prompt(cols, wg2p, row(gla_bg[l]), row(gla_gn[l]), seq=SEQ)
        ob, ssm_p = _ssd_prompt(cols, conv_w[l], row(conv_b[l]), ex, dtb_x, alog_x, dskip_x, row(ssm_gn[l]), sel8,
                                seq=SEQ)
        qb, kf, kb = _diff_prep(cols, cos_p, sin_p, gq, gk, 0, SEQ, 512)
        oc = _diff_attn_prompt(qb, kb, cols, lam, row(diff_gn[l]), lam_init, seq=SEQ, tq=FLASH_TQ, tk=FLASH_TK)
        od, rwkv_p = _rwkv_prompt(cols, row(shift_mu[l]), row(w0[l]), row(a0[l]), row(k_k[l]), row(k_a[l]),
                                  row(r_k[l]), row(lnx_g[l]), row(lnx_b[l]), w2a2, g2[l], seq=SEQ)
        mix_p = jnp.concatenate([oa, ob, oc, od], axis=1)

        pre_w = [wg2p, row(gla_bg[l]), conv_w[l], row(conv_b[l]), ex, dtb_x, alog_x,
                 row(shift_mu[l]), row(w0[l]), row(a0[l]), row(k_k[l]), row(k_a[l]), w2a2, g2[l]]
        s_gq, s_gk, s_ga, s_act, s_dec, s_xd, s_rw = _sample_pre(cols, jnp.transpose(state_conv[l], (1, 0, 2)),
                                                                 state_shift[l][:, 0], pre_w, SEQ, nb)
        tail = cols[SEQ:]
        gla_s, oa_s = _state_step(_gla_step_kernel, "gla_step", gla_rows, l,
                                  [s_ga, s_gk, s_gq, tail[:, C_VA:C_VA + 512]], 512)
        ssm_s, y_s = _state_step(_ssd_step_kernel, "ssd_step", ssm_rows, l, [s_dec, s_xd, s_act[:, GROUP_W:]], 512)
        rwkv_s, od_s = _state_step(_rwkv_step_kernel, "rwkv_step", rwkv_rows, l, [s_rw], 512)
        qb_s, kf_s, _ = _diff_prep(cols, cos_s, sin_s, gq, gk, SEQ, nb, nb)
        vc_s = tail[:, C_VC:C_VC + 512]
        oc_s = _diff_decode(pt_flat, l, qb_s.reshape(nb, H_C, 128), kf_s.reshape(nb, H_C, 128), vc_s.reshape(nb, H_C, 128),
                            ck_rows, cv_rows, ind2, lam, row(diff_gn[l]), lam_init, n_pages)
        mix_s = _sample_post(cols, oa_s, y_s, s_act, oc_s[:, :H_C].reshape(nb, 512), od_s, s_rw, row(gla_gn[l]), dskip_x, row(ssm_gn[l]), row(r_k[l]),
                             row(lnx_g[l]), row(lnx_b[l]), SEQ, nb)

        x = _matmul(jnp.concatenate([mix_p, mix_s], axis=0), w_out[l].astype(bf16), res=x)

        mk_p, mv_p = _mem_kv(mem_prompt[0], row(norm_memsrc[l]),
                             jnp.concatenate([w_mk[l], w_mv[l]], axis=1).astype(bf16), row(mk_norm[l]))
        qm = _matmul(x, w_mq[l].astype(bf16), gain=norm_mem[l])
        om_p = _mem_attn_prompt(qm, row(mq_norm[l]), mk_p, mv_p, seq=SEQ)
        om_s = _mem_attn_sample(qm[SEQ:].reshape(nb, H_M, DH_M), row(mq_norm[l]), mk_rows, mv_rows, ind1, l)
        x = _matmul(jnp.concatenate([om_p, om_s[:, :H_M].reshape(nb, D_MEM)], axis=0), w_mo[l].astype(bf16), res=x)

        qp, xn = _matmul(x, peer_wq[l].astype(bf16), gain=norm_ffn[l], emit_xn=True, tn=512)
        picks = _peer_select(qp, peer_keys[l].reshape(2 * H_P, N_KEYS, D_PK // 2))
        x = _peer_dense(xn, _table_bf16(peer_u, l), _table_bf16(peer_v, l), picks, x)

        outs['kp'].append(kf.reshape(1, SEQ, H_C, 2 * DK_C))
        outs['vp'].append(cols[:SEQ, C_VC:C_VC + 512].reshape(1, SEQ, H_C, DV_C))
        outs['ks'].append(kf_s.reshape(nb, 1, H_C, 2 * DK_C))
        outs['vs'].append(vc_s.reshape(nb, 1, H_C, DV_C))
        outs['mk'].append(mk_p.reshape(1, N_MEM, H_M, DH_M))
        outs['mv'].append(mv_p.reshape(1, N_MEM, H_M, DH_M))
        outs['gla_p'].append(gla_p[None])
        outs['gla_s'].append(gla_s.reshape(nb, H_A, DK_A, DV_A))
        outs['ssm_p'].append(ssm_p[None])
        outs['ssm_s'].append(ssm_s.reshape(nb, H_B, P_B, N_B))
        u_p = jnp.concatenate([cols[SEQ - 3:SEQ, C_XS:C_XS + 512], cols[SEQ - 3:SEQ, C_BC:C_BC + 256]], axis=1)
        u_s = jnp.concatenate([tail[:, C_XS:C_XS + 512], tail[:, C_BC:C_BC + 256]], axis=1)
        outs['conv_p'].append(u_p[None])
        outs['conv_s'].append(jnp.concatenate([state_conv[l][:, 1:], u_s[:, None]], axis=1))
        outs['rwkv_p'].append(rwkv_p[None])
        outs['rwkv_s'].append(rwkv_s.reshape(nb, H_D, N_D, N_D))
        outs['shift_p'].append(cols[SEQ - 1:SEQ, C_D:C_D + 1792][None])
        outs['shift_s'].append(tail[:, C_D:C_D + 1792][:, None])
    st = {n: jnp.stack(v) for n, v in outs.items()}
    return (x[:SEQ][None], x[SEQ:][:, None], st['kp'], st['vp'], st['ks'], st['vs'], st['mk'], st['mv'],
            st['gla_p'], st['gla_s'], st['ssm_p'], st['ssm_s'], st['conv_p'], st['conv_s'],
            st['rwkv_p'], st['rwkv_s'], st['shift_p'], st['shift_s'])
```

```python
import functools
import math

import numpy as np
import jax
import jax.numpy as jnp
from jax import lax
from jax.experimental import pallas as pl
from jax.experimental.pallas import tpu as pltpu

f32 = jnp.float32
bf16 = jnp.bfloat16
HI = lax.Precision.HIGHEST

D_MODEL = 2048
SEQ = 8192
DEPTH = 2
DEC_BATCH = 128
PAST_LEN = 2048
PAGE_SIZE = 128
T_ALL = SEQ + DEC_BATCH

GROUP_W = 512
H_A, DK_A, DV_A, GK_RANK, GLA_TAU = 4, 64, 128, 16, 16.0
H_B, P_B, N_B, G_B, CONV_W, XBC_W = 8, 64, 64, 2, 4, 768
H_C, DK_C, DV_C, ROPE_THETA = 4, 64, 128, 10000.0
H_D, N_D, LNX_EPS = 8, 64, 64e-5
N_MEM, H_M, D_MEM, DH_M = 256, 4, 512, 128
N_KEYS, H_P, TOPK_P, D_PK = 128, 8, 16, 256
N_EXPERTS = N_KEYS * N_KEYS

C_D = 0
C_BC = 1792
C_QKA = 2048
C_VA = 2560
C_GA = 3072
C_Z = 3584
C_XS = 4096
C_QC = 4608
C_KC = 5120
C_VC = 5632
C_TAIL = 6144
IN_PAD = 6272

LANES = 128
VMEM_LIMIT = 56 * 1024 * 1024

NN = ((1,), (0,))
NT = ((1,), (1,))
TN = ((0,), (0,))


def _dot(a, b, dims=NN, hi=False):
    if hi:
        return lax.dot_general(a, b, (dims, ((), ())), precision=HI, preferred_element_type=f32)
    return lax.dot_general(a.astype(bf16), b.astype(bf16), (dims, ((), ())), preferred_element_type=f32)


def _softplus(x):
    return jnp.maximum(x, 0.0) + jnp.log(1.0 + jnp.exp(-jnp.abs(x)))


def _sigmoid(x):
    return 1.0 / (1.0 + jnp.exp(-x))


def _silu(x):
    return x * _sigmoid(x)


def _iota(shape, axis):
    return lax.broadcasted_iota(jnp.int32, shape, axis)


def _params(*sem):
    return pltpu.CompilerParams(dimension_semantics=sem, vmem_limit_bytes=VMEM_LIMIT)


def _mm_kernel(*refs, norm, residual, emit_xn):
    x_ref, g_ref, w_ref = refs[:3]
    res_ref = refs[3] if residual else None
    xn_ref = refs[-1]
    o_ref = refs[-3] if emit_xn else refs[-2]

    @pl.when(pl.program_id(1) == 0)
    def _():
        x = x_ref[...]
        if norm:
            x = x * lax.rsqrt(jnp.mean(x * x, axis=-1, keepdims=True) + 1e-6) * g_ref[...]
        xn_ref[...] = x.astype(bf16)
        if emit_xn:
            refs[-2][...] = xn_ref[...]

    acc = jnp.dot(xn_ref[...], w_ref[...], preferred_element_type=f32)
    if residual:
        acc = acc + res_ref[...]
    o_ref[...] = acc


def _matmul(x, w, gain=None, res=None, emit_xn=False, tm=None, tn=None):
    m, k = x.shape
    n = w.shape[1]
    tm = tm or _pick(m, (1040, 1024, 512, 256, 128))
    tn = tn or _pick(n, (1024, 896, 512, 256, 128))
    norm = gain is not None
    g = (gain if norm else jnp.ones((k,), f32)).reshape(1, k)
    args = [x, g, w]
    in_specs = [pl.BlockSpec((tm, k), lambda i, j: (i, 0)),
                pl.BlockSpec((1, k), lambda i, j: (0, 0)),
                pl.BlockSpec((k, tn), lambda i, j: (0, j))]
    if res is not None:
        args.append(res)
        in_specs.append(pl.BlockSpec((tm, tn), lambda i, j: (i, j)))
    out_specs = [pl.BlockSpec((tm, tn), lambda i, j: (i, j))]
    out_shape = [jax.ShapeDtypeStruct((m, n), f32)]
    if emit_xn:
        out_specs.append(pl.BlockSpec((tm, k), lambda i, j: (i, 0)))
        out_shape.append(jax.ShapeDtypeStruct((m, k), bf16))
    out = pl.pallas_call(
        functools.partial(_mm_kernel, norm=norm, residual=res is not None, emit_xn=emit_xn),
        grid=(m // tm, n // tn),
        in_specs=in_specs,
        out_specs=out_specs,
        out_shape=out_shape,
        scratch_shapes=[pltpu.VMEM((tm, k), bf16)],
        compiler_params=_params("parallel", "arbitrary"),
        name="mm",
    )(*args)
    return out if emit_xn else out[0]


def _pick(n, cands):
    for c in cands:
        if n % c == 0:
            return c
    return n


def _const(shape):
    nd = len(shape)
    return pl.BlockSpec(shape, lambda i, _n=nd: (0,) * _n)


def _cols(width, start, tb):
    assert start % width == 0
    return pl.BlockSpec((tb, width), lambda i, _c=start // width: (i, _c))


def _blockdiag_tri(tb, c):
    r = _iota((tb, tb), 0)
    s = _iota((tb, tb), 1)
    return jnp.where((r // c == s // c) & (s <= r), 1.0, 0.0).astype(f32)


def _segment_ones(n, seg):
    r = _iota((n, n), 0)
    s = _iota((n, n), 1)
    return jnp.where(r // seg == s // seg, 1.0, 0.0).astype(f32)


GLA_TB = 256
GLA_C = 16


def _gla_gate_log(tail, wg2p, bg):
    z = _dot(tail, wg2p, hi=True) + bg
    return -_softplus(-z) * (1.0 / GLA_TAU)


def _gla_out(o, g, gn):
    outs = []
    for h in range(H_A):
        oh = o[:, h * DV_A:(h + 1) * DV_A]
        oh = oh * lax.rsqrt(jnp.mean(oh * oh, axis=-1, keepdims=True) + 1e-6) * gn
        outs.append(oh * _silu(g[:, h * DV_A:(h + 1) * DV_A]))
    return jnp.concatenate(outs, axis=-1)


def _gla_kernel(qk_ref, v_ref, g_ref, tail_ref, wg2_ref, bg_ref, gn_ref, o_ref, sfin_ref, st_ref, w_ref):
    i = pl.program_id(0)

    @pl.when(i == 0)
    def _():
        st_ref[...] = jnp.zeros_like(st_ref)

    tb, c = GLA_TB, GLA_C
    qk = qk_ref[...]
    q = qk[:, :256] * (DK_A ** -0.5)
    k = qk[:, 256:]
    v = v_ref[...]
    la = _gla_gate_log(tail_ref[...], wg2_ref[...], bg_ref[...])
    b = _dot(_blockdiag_tri(tb, c), la, hi=True)
    e_r = _iota((256, 512), 0) // DK_A
    e_c = _iota((256, 512), 1) // DV_A
    expand = jnp.where(e_r == e_c, 1.0, 0.0).astype(bf16)
    s_idx = _iota((c, 256), 0)
    for j in range(tb // c):
        r0 = j * c
        qj, kj, bj, vj = q[r0:r0 + c], k[r0:r0 + c], b[r0:r0 + c], v[r0:r0 + c]
        for t in range(c):
            wt = qj[t:t + 1] * kj * jnp.exp(bj[t:t + 1] - bj)
            w_ref[t * c:(t + 1) * c, :] = jnp.where(s_idx <= t, wt, 0.0)
        att = _dot(w_ref[...], expand)
        o = jnp.sum(att.reshape(c, c, 512) * vj[None], axis=1)
        qe = qj * jnp.exp(bj)
        bl = bj[c - 1:c]
        ke = kj * jnp.exp(bl - bj)
        dl = jnp.exp(bl)
        inter = []
        for h in range(H_A):
            ks = slice(h * DK_A, (h + 1) * DK_A)
            st = st_ref[h]
            inter.append(_dot(qe[:, ks], st, NT))
            st_ref[h] = st * dl[:, ks] + _dot(vj[:, h * DV_A:(h + 1) * DV_A], ke[:, ks], TN)
        o = o + jnp.concatenate(inter, axis=-1)
        o_ref[r0:r0 + c, :] = _gla_out(o, g_ref[r0:r0 + c, :], gn_ref[...])

    @pl.when(i == pl.num_programs(0) - 1)
    def _():
        for h in range(H_A):
            sfin_ref[h] = st_ref[h].T


def _gla_prompt(cols, wg2p, bg, gn, seq=SEQ):
    tb = GLA_TB
    return pl.pallas_call(
        _gla_kernel,
        grid=(seq // tb,),
        in_specs=[_cols(512, C_QKA, tb), _cols(512, C_VA, tb), _cols(512, C_GA, tb), _cols(128, C_TAIL, tb),
                  _const((128, 256)), _const((1, 256)), _const((1, DV_A))],
        out_specs=[pl.BlockSpec((tb, 512), lambda i: (i, 0)), _const((H_A, DK_A, DV_A))],
        out_shape=[jax.ShapeDtypeStruct((seq, 512), f32), jax.ShapeDtypeStruct((H_A, DK_A, DV_A), f32)],
        scratch_shapes=[pltpu.VMEM((H_A, DV_A, DK_A), f32), pltpu.VMEM((GLA_C * GLA_C, 256), f32)],
        compiler_params=_params("arbitrary"),
        name="gla_prompt",
    )(cols, cols, cols, cols, wg2p, bg, gn)


SSD_TB = 512
SSD_C = 64


def _ssd_conv(ext, conv_w, conv_b, rows):
    out = conv_b
    for j in range(CONV_W):
        shifted = pltpu.roll(ext, j, 0) if j else ext
        out = out + shifted[8:8 + rows] * conv_w[CONV_W - 1 - j:CONV_W - j]
    return out


def _ssd_dt(tail, ex, dtb_x):
    return _softplus(_dot(tail, ex, hi=True) + dtb_x)


def _ssd_out(y, z, gn):
    y = y * _silu(z)
    w = GROUP_W // G_B
    outs = []
    for g in range(G_B):
        yg = y[:, g * w:(g + 1) * w]
        outs.append(yg * lax.rsqrt(jnp.mean(yg * yg, axis=-1, keepdims=True) + 1e-6) * gn[:, g * w:(g + 1) * w])
    return jnp.concatenate(outs, axis=-1)


def _ssd_kernel(z_ref, xs_ref, bc_ref, tail_ref, cw_ref, cb_ref, ex_ref, dtb_ref, alog_ref, dskip_ref, gn_ref,
                sel_ref, o_ref, hfin_ref, carry_ref, h_ref, y_ref):
    i = pl.program_id(0)

    @pl.when(i == 0)
    def _():
        carry_ref[...] = jnp.zeros_like(carry_ref)
        h_ref[...] = jnp.zeros_like(h_ref)

    tb, c = SSD_TB, SSD_C
    u = jnp.concatenate([xs_ref[...], bc_ref[...]], axis=-1)
    ext = jnp.concatenate([carry_ref[...], u], axis=0)
    carry_ref[...] = u[tb - 8:tb]
    act = _silu(_ssd_conv(ext, cw_ref[...], cb_ref[...], tb))
    xs = act[:, :GROUP_W]
    dtx = _ssd_dt(tail_ref[...], ex_ref[...], dtb_ref[...])
    la = dtx * (-jnp.exp(alog_ref[...]))
    b = _dot(_blockdiag_tri(tb, c), la, hi=True)
    brow = _dot(sel_ref[...], b, NT, hi=True)
    xd = xs * dtx
    tri = _iota((c, c), 1) <= _iota((c, c), 0)
    for ch in range(tb // c):
        r0 = ch * c
        rows = slice(r0, r0 + c)
        scores = []
        for g in range(G_B):
            bm = act[rows, GROUP_W + g * N_B:GROUP_W + (g + 1) * N_B]
            cm = act[rows, GROUP_W + G_B * N_B + g * N_B:GROUP_W + G_B * N_B + (g + 1) * N_B]
            scores.append((_dot(cm, bm, NT), bm, cm))
        for h in range(H_B):
            hs = slice(h * P_B, (h + 1) * P_B)
            sc, bm, cm = scores[h // (H_B // G_B)]
            bh = b[rows, hs]
            dec = jnp.where(tri, jnp.exp(bh - brow[h:h + 1, r0:r0 + c]), 0.0)
            xdh = xd[rows, hs]
            hst = h_ref[h]
            y = _dot(sc * dec, xdh) + _dot(cm, hst, NT) * jnp.exp(bh)
            bl = bh[c - 1:c]
            h_ref[h] = hst * jnp.exp(bl) + _dot(xdh * jnp.exp(bl - bh), bm, TN)
            y_ref[rows, hs] = y + xs[rows, hs] * dskip_ref[:, hs]
    o_ref[...] = _ssd_out(y_ref[...], z_ref[...], gn_ref[...])

    @pl.when(i == pl.num_programs(0) - 1)
    def _():
        hfin_ref[...] = h_ref[...]


def _ssd_prompt(cols, cw, cb, ex, dtb_x, alog_x, dskip_x, gn, sel, seq=SEQ):
    tb = SSD_TB
    return pl.pallas_call(
        _ssd_kernel,
        grid=(seq // tb,),
        in_specs=[_cols(512, C_Z, tb), _cols(512, C_XS, tb), _cols(256, C_BC, tb), _cols(128, C_TAIL, tb),
                  _const((CONV_W, XBC_W)), _const((1, XBC_W)), _const((128, 512)), _const((1, 512)), _const((1, 512)),
                  _const((1, 512)), _const((1, 512)), _const((8, 512))],
        out_specs=[pl.BlockSpec((tb, 512), lambda i: (i, 0)), _const((H_B, P_B, N_B))],
        out_shape=[jax.ShapeDtypeStruct((seq, 512), f32), jax.ShapeDtypeStruct((H_B, P_B, N_B), f32)],
        scratch_shapes=[pltpu.VMEM((8, XBC_W), f32), pltpu.VMEM((H_B, P_B, N_B), f32), pltpu.VMEM((tb, 512), f32)],
        compiler_params=_params("arbitrary"),
        name="ssd_prompt",
    )(cols, cols, cols, cols, cw, cb, ex, dtb_x, alog_x, dskip_x, gn, sel)


RWKV_TB = 512
RWKV_C = 64


def _rwkv_pre(x, xprev, mu, w0, a0, k_k, k_a, w2a2, g2):
    mixed = x + (xprev - x) * mu
    r = mixed[:, :512]
    kd = mixed[:, 512:1024]
    v = mixed[:, 1024:1536]
    lw = mixed[:, 1536:1664]
    lin = jnp.where(_iota(lw.shape, 1) < 64, jnp.tanh(lw), lw)
    wa = _dot(lin, w2a2, hi=True)
    w = -_softplus(-(w0 + wa[:, :512])) - 0.5
    a = _sigmoid(a0 + wa[:, 512:])
    g = _dot(_sigmoid(mixed[:, 1664:1792]), g2)
    kk = kd * k_k
    ss = _dot(kk * kk, _segment_ones(512, N_D), hi=True)
    kk = kk * lax.rsqrt(jnp.maximum(ss, 1e-24))
    kd = kd * (1.0 + (a - 1.0) * k_a)
    return r, w, kd, v, -kk, kk * a, g


def _rwkv_post(o, r, kd, v, g, r_k, lnx_g, lnx_b):
    seg = _segment_ones(512, N_D)
    mu = _dot(o, seg, hi=True) * (1.0 / N_D)
    d = o - mu
    var = _dot(d * d, seg, hi=True) * (1.0 / N_D)
    o = d * lax.rsqrt(var + LNX_EPS) * lnx_g + lnx_b
    o = o + _dot(r * kd * r_k, seg, hi=True) * v
    return o * g


def _rwkv_kernel(d_ref, mu_ref, w0_ref, a0_ref, kk_ref, ka_ref, rk_ref, lg_ref, lb_ref, w2a2_ref, g2_ref,
                 o_ref, sfin_ref, prev_ref, s_ref, oacc_ref):
    i = pl.program_id(0)

    @pl.when(i == 0)
    def _():
        prev_ref[...] = jnp.zeros_like(prev_ref)
        s_ref[...] = jnp.zeros_like(s_ref)

    tb, c = RWKV_TB, RWKV_C
    x = d_ref[...]
    xprev = jnp.where(_iota(x.shape, 0) == 0, prev_ref[0:1, :], pltpu.roll(x, 1, 0))
    prev_ref[0:1, :] = x[tb - 1:tb]
    r, w, kd, v, alpha, beta, g = _rwkv_pre(x, xprev, mu_ref[...], w0_ref[...], a0_ref[...], kk_ref[...], ka_ref[...],
                                            w2a2_ref[...], g2_ref[...])
    ld = -jnp.exp(w)
    cum = _dot(_blockdiag_tri(tb, c), ld, hi=True)
    at = alpha * jnp.exp(cum - ld)
    rt = r * jnp.exp(cum)
    einv = jnp.exp(-cum)
    kt = kd * einv
    bt = beta * einv
    ri = _iota((c, c), 0)
    ci = _iota((c, c), 1)
    strict, incl = ci < ri, ci <= ri
    eye = jnp.where(ri == ci, 1.0, 0.0).astype(f32)
    pairs = [(ch, h) for ch in range(tb // c) for h in range(H_D)]
    rows_of = lambda ch: slice(ch * c, (ch + 1) * c)
    lanes_of = lambda h: slice(h * N_D, (h + 1) * N_D)
    ar, kb, lk, mkb, tinv, p = {}, {}, {}, {}, {}, {}
    for ch, h in pairs:
        rows, hs = rows_of(ch), lanes_of(h)
        ar[ch, h] = jnp.concatenate([at[rows, hs], rt[rows, hs]], axis=0)
        kb[ch, h] = jnp.concatenate([kt[rows, hs], bt[rows, hs]], axis=0)
    for key in pairs:
        gram = _dot(ar[key], kb[key], NT)
        lk[key] = jnp.where(strict, gram[:c, :c], 0.0)
        p[key] = jnp.where(strict, gram[:c, c:], 0.0)
        mkb[key] = jnp.concatenate([jnp.where(incl, gram[c:, :c], 0.0), jnp.where(incl, gram[c:, c:], 0.0)], axis=1)
        tinv[key] = eye + p[key]
    for _ in range(5):
        for key in pairs:
            p[key] = _dot(p[key], p[key])
        for key in pairs:
            tinv[key] = tinv[key] + _dot(tinv[key], p[key])
    for ch in range(tb // c):
        rows = rows_of(ch)
        cl = cum[ch * c + c - 1:ch * c + c]
        efin = jnp.exp(cl - cum[rows])
        kfin = kd[rows] * efin
        bfin = beta[rows] * efin
        dfin = jnp.exp(cl)
        heads = range(H_D)
        s0 = [s_ref[h] for h in heads]
        ars = [_dot(ar[ch, h], s0[h], NT) for h in heads]
        lkv = [_dot(lk[ch, h], v[rows, lanes_of(h)]) for h in heads]
        u = [_dot(tinv[ch, h], ars[h][:c] + lkv[h]) for h in heads]
        vu = [jnp.concatenate([v[rows, lanes_of(h)], u[h]], axis=0) for h in heads]
        for h in heads:
            hs = lanes_of(h)
            oacc_ref[rows, hs] = ars[h][c:] + _dot(mkb[ch, h], vu[h])
            kbfin = jnp.concatenate([kfin[:, hs], bfin[:, hs]], axis=0)
            s_ref[h] = s0[h] * dfin[:, hs] + _dot(vu[h], kbfin, TN)
    o_ref[...] = _rwkv_post(oacc_ref[...], r, kd, v, g, rk_ref[...], lg_ref[...], lb_ref[...])

    @pl.when(i == pl.num_programs(0) - 1)
    def _():
        sfin_ref[...] = s_ref[...]


def _rwkv_prompt(cols, mu, w0, a0, k_k, k_a, r_k, lnx_g, lnx_b, w2a2, g2, seq=SEQ):
    tb = RWKV_TB
    vec = _const((1, 512))
    return pl.pallas_call(
        _rwkv_kernel,
        grid=(seq // tb,),
        in_specs=[_cols(1792, C_D, tb), _const((1, 1792)), vec, vec, vec, vec, vec, vec, vec,
                  _const((128, 1024)), _const((128, 512))],
        out_specs=[pl.BlockSpec((tb, 512), lambda i: (i, 0)), _const((H_D, N_D, N_D))],
        out_shape=[jax.ShapeDtypeStruct((seq, 512), f32), jax.ShapeDtypeStruct((H_D, N_D, N_D), f32)],
        scratch_shapes=[pltpu.VMEM((8, 1792), f32), pltpu.VMEM((H_D, N_D, N_D), f32), pltpu.VMEM((tb, 512), f32)],
        compiler_params=_params("arbitrary"),
        name="rwkv_prompt",
    )(cols, mu, w0, a0, k_k, k_a, r_k, lnx_g, lnx_b, w2a2, g2)


def _rope_tables(pos):
    half = DK_C // 2
    inv = ROPE_THETA ** (-jnp.arange(half, dtype=f32) / half)
    ang = pos.astype(f32)[:, None] * inv[None, :]
    cos, sin = jnp.cos(ang), jnp.sin(ang)
    return jnp.tile(jnp.concatenate([cos, cos], axis=-1), (1, 2)), jnp.tile(jnp.concatenate([-sin, sin], axis=-1), (1, 2))


def _qk_norm_rope(x, gain, cos, sin):
    ms = _dot(x * x, _segment_ones(512, DK_C), hi=True) * (1.0 / DK_C)
    x = x * lax.rsqrt(ms + 1e-6) * gain
    first = (_iota(x.shape, 1) % DK_C) < (DK_C // 2)
    partner = jnp.where(first, pltpu.roll(x, 512 - DK_C // 2, 1), pltpu.roll(x, DK_C // 2, 1))
    cos = jnp.concatenate([cos] * 4, axis=-1)
    sin = jnp.concatenate([sin] * 4, axis=-1)
    return x * cos + partner * sin


def _diff_prep_kernel(q_ref, k_ref, cos_ref, sin_ref, gq_ref, gk_ref, qb_ref, kf_ref, kb_ref):
    cos, sin = cos_ref[...], sin_ref[...]
    q = _qk_norm_rope(q_ref[...], gq_ref[...], cos, sin)
    k = _qk_norm_rope(k_ref[...], gk_ref[...], cos, sin)
    qb_ref[...] = (q * (DK_C ** -0.5)).astype(bf16)
    kf_ref[...] = k
    kb_ref[...] = k.astype(bf16)


def _diff_prep(cols, cos, sin, gq, gk, row0, rows, tb):
    assert row0 % tb == 0
    r0 = row0 // tb
    colspec = lambda start: pl.BlockSpec((tb, 512), lambda i, _c=start // 512: (i + r0, _c))
    out = pl.BlockSpec((tb, 512), lambda i: (i, 0))
    tab = pl.BlockSpec((tb, 128), lambda i: (i, 0))
    return pl.pallas_call(
        _diff_prep_kernel,
        grid=(rows // tb,),
        in_specs=[colspec(C_QC), colspec(C_KC), tab, tab, _const((1, 512)), _const((1, 512))],
        out_specs=[out, out, out],
        out_shape=[jax.ShapeDtypeStruct((rows, 512), bf16), jax.ShapeDtypeStruct((rows, 512), f32),
                   jax.ShapeDtypeStruct((rows, 512), bf16)],
        compiler_params=_params("parallel"),
        name="diff_prep",
    )(cols, cols, cos, sin, gq, gk)


def _diff_finish(o1, o2, lam, gn, lam_init):
    o = o1 - lam * o2
    return o * lax.rsqrt(jnp.mean(o * o, axis=-1, keepdims=True) + 1e-6) * gn * (1.0 - lam_init)


FLASH_TQ = 1024
FLASH_TK = 1024


def _flash_kernel(qi_ref, kj_ref, q_ref, k_ref, v_ref, lam_ref, gn_ref, o_ref, m_ref, l_ref, acc_ref,
                  *, tq, tk, lam_init):
    step_id = pl.program_id(1)
    qi, kj = qi_ref[step_id], kj_ref[step_id]

    @pl.when(kj == 0)
    def _():
        m_ref[...] = jnp.full_like(m_ref, -jnp.inf)
        l_ref[...] = jnp.zeros_like(l_ref)
        acc_ref[...] = jnp.zeros_like(acc_ref)

    def step(masked):
        q = q_ref[...]
        k = k_ref[...]
        v = v_ref[...].astype(bf16)
        lane = _iota(q.shape, 1)
        if masked:
            keep = (kj * tk + _iota((tk, tq), 0)) <= (qi * tq + _iota((tk, tq), 1))
        for m in range(2):
            qm = jnp.where((lane < DK_C) if m == 0 else (lane >= DK_C), q, jnp.zeros_like(q))
            s = _dot(k, qm, NT)
            if masked:
                s = jnp.where(keep, s, -jnp.inf)
            m_old = m_ref[m]
            m_new = jnp.maximum(m_old, jnp.max(s, axis=0, keepdims=True))
            p = jnp.exp(s - m_new)
            corr = jnp.exp(m_old - m_new)
            l_ref[m] = corr * l_ref[m] + jnp.sum(p, axis=0, keepdims=True)
            acc_ref[m] = corr * acc_ref[m] + _dot(v, p, TN)
            m_ref[m] = m_new

    last = kj * tk + tk - 1

    @pl.when(last <= qi * tq)
    def _():
        step(False)

    @pl.when(last > qi * tq)
    def _():
        step(True)

    @pl.when(kj == (qi * tq + tq - 1) // tk)
    def _():
        o1 = (acc_ref[0] / l_ref[0]).T
        o2 = (acc_ref[1] / l_ref[1]).T
        o_ref[...] = _diff_finish(o1, o2, lam_ref[...], gn_ref[...], lam_init)


def _diff_attn_prompt(qb, kb, cols, lam, gn, lam_init, seq=SEQ, tq=512, tk=512):
    pairs = [(i, j) for i in range(seq // tq) for j in range((i * tq + tq - 1) // tk + 1)]
    qi_tab = jnp.asarray([p[0] for p in pairs], jnp.int32)
    kj_tab = jnp.asarray([p[1] for p in pairs], jnp.int32)
    cst = lambda shp: pl.BlockSpec(shp, lambda h, s, qi, kj: (0, 0))
    return pl.pallas_call(
        functools.partial(_flash_kernel, tq=tq, tk=tk, lam_init=lam_init),
        grid_spec=pltpu.PrefetchScalarGridSpec(
            num_scalar_prefetch=2, grid=(H_C, len(pairs)),
            in_specs=[pl.BlockSpec((tq, 128), lambda h, s, qi, kj: (qi[s], h)),
                      pl.BlockSpec((tk, 128), lambda h, s, qi, kj: (kj[s], h)),
                      pl.BlockSpec((tk, DV_C), lambda h, s, qi, kj: (kj[s], C_VC // DV_C + h)),
                      cst((1, 128)), cst((1, DV_C))],
            out_specs=pl.BlockSpec((tq, DV_C), lambda h, s, qi, kj: (qi[s], h)),
            scratch_shapes=[pltpu.VMEM((2, 1, tq), f32), pltpu.VMEM((2, 1, tq), f32), pltpu.VMEM((2, DV_C, tq), f32)]),
        out_shape=jax.ShapeDtypeStruct((seq, 512), f32),
        compiler_params=_params("parallel", "arbitrary"),
        name="diff_attn_prompt",
    )(qi_tab, kj_tab, qb, kb, cols, lam, gn)


PEER_SEL_TB = 128
PEER_TB = 640
PEER_EB = 512
PEER_SUB = 256
_PAIRS = [(a, b) for a in range(TOPK_P) for b in range(TOPK_P) if (a + 1) * (b + 1) <= TOPK_P]
_NPAIR = -(-len(_PAIRS) // 8) * 8


def _top16(s, n_idx, want_rank):
    rank = jnp.full(s.shape, float(TOPK_P), f32) if want_rank else None
    tops, idxs = [], []
    work = s
    for k in range(TOPK_P):
        m = jnp.max(work, axis=0, keepdims=True)
        idx = jnp.min(jnp.where(work == m, n_idx, float(N_KEYS)), axis=0, keepdims=True)
        hit = n_idx == idx
        if want_rank:
            rank = jnp.where(hit, float(k), rank)
        work = jnp.where(hit, -jnp.inf, work)
        tops.append(m)
        idxs.append(idx)
    return rank, jnp.concatenate(tops, axis=0), jnp.concatenate(idxs, axis=0)


def _peer_select_kernel(q_ref, keys_ref, pk1_ref, flat_ref, ci_ref, cnt_ref, e2_ref, r2_ref):
    tb = q_ref.shape[0]
    flat = flat_ref[...]
    n_idx = _iota((N_KEYS, tb), 0).astype(f32)
    for h in range(H_P):
        scores = [_dot(keys_ref[hx], q_ref[:, hx * 128:(hx + 1) * 128], NT, hi=True)
                  for hx in (2 * h, 2 * h + 1)]
        _, t1, idx1 = _top16(scores[0], n_idx, False)
        rank2, t2, _ = _top16(scores[1], n_idx, True)
        cand = jnp.concatenate([t1[a:a + 1] + t2[b:b + 1] for a, b in _PAIRS]
                               + [jnp.full((_NPAIR - len(_PAIRS), tb), -jnp.inf, f32)], axis=0)
        work = cand
        sel = jnp.zeros(cand.shape, f32)
        for _ in range(TOPK_P):
            m = jnp.max(work, axis=0, keepdims=True)
            idx = jnp.min(jnp.where(work == m, flat, 4096.0), axis=0, keepdims=True)
            hit = flat == idx
            sel = jnp.where(hit, 1.0, sel)
            work = jnp.where(hit, -jnp.inf, work)
        top = t1[0:1] + t2[0:1]
        z = jnp.sum(sel * jnp.exp(jnp.where(sel > 0, cand - top, 0.0)), axis=0, keepdims=True)
        cnt = _dot(pk1_ref[...], sel)
        cnt_i = jnp.zeros((N_KEYS, tb), f32)
        for k1 in range(TOPK_P):
            cnt_i = jnp.where(n_idx == idx1[k1:k1 + 1], cnt[k1:k1 + 1], cnt_i)
        ci_ref[0, h] = jnp.exp(scores[0] - t1[0:1]) / z
        cnt_ref[0, h] = cnt_i
        e2_ref[0, h] = jnp.exp(scores[1] - t2[0:1]).astype(bf16)
        r2_ref[0, h] = rank2.astype(bf16)


def _peer_select(q, keys):
    t = q.shape[0]
    tb, per = PEER_SEL_TB, PEER_TB // PEER_SEL_TB
    pk1 = np.zeros((TOPK_P, _NPAIR), np.float32)
    flat = np.full((_NPAIR, 1), 8192.0, np.float32)
    for r, (a, b) in enumerate(_PAIRS):
        pk1[a, r] = 1.0
        flat[r, 0] = a * TOPK_P + b
    out = pl.BlockSpec((1, H_P, N_KEYS, tb), lambda i: (i // per, 0, 0, i % per))
    shp = lambda dt: jax.ShapeDtypeStruct((t // PEER_TB, H_P, N_KEYS, PEER_TB), dt)
    return pl.pallas_call(
        _peer_select_kernel,
        grid=(t // tb,),
        in_specs=[pl.BlockSpec((tb, 2048), lambda i: (i, 0)), _const((2 * H_P, N_KEYS, 128)),
                  _const((TOPK_P, _NPAIR)), _const((_NPAIR, 1))],
        out_specs=[out, out, out, out],
        out_shape=[shp(f32), shp(f32), shp(bf16), shp(bf16)],
        compiler_params=_params("parallel"),
        name="peer_select",
    )(q, keys, jnp.asarray(pk1), jnp.asarray(flat))


def _sample_pre_kernel(qk_ref, tail_ref, xs_ref, bc_ref, d_ref, conv_ref, shift_ref,
                       wg2_ref, bg_ref, cw_ref, cb_ref, ex_ref, dtb_ref, alog_ref,
                       mu_ref, w0_ref, a0_ref, kk_ref, ka_ref, w2a2_ref, g2_ref,
                       gq_ref, gk_ref, ga_ref, act_ref, sdec_ref, sxd_ref, rw_ref):
    qk = qk_ref[...]
    la = _gla_gate_log(tail_ref[...], wg2_ref[...], bg_ref[...])
    gq_ref[...] = qk[:, :256] * (DK_A ** -0.5)
    gk_ref[...] = qk[:, 256:]
    ga_ref[...] = jnp.exp(la)
    u = jnp.concatenate([xs_ref[...], bc_ref[...]], axis=-1)
    cw = cw_ref[...]
    conv = cb_ref[...] + u * cw[CONV_W - 1:CONV_W]
    for j in range(CONV_W - 1):
        conv = conv + conv_ref[j] * cw[j:j + 1]
    act = _silu(conv)
    act_ref[...] = act
    dtx = _ssd_dt(tail_ref[...], ex_ref[...], dtb_ref[...])
    sdec_ref[...] = jnp.exp(dtx * (-jnp.exp(alog_ref[...])))
    sxd_ref[...] = act[:, :GROUP_W] * dtx
    r, w, kd, v, alpha, beta, g = _rwkv_pre(d_ref[...], shift_ref[...], mu_ref[...], w0_ref[...], a0_ref[...],
                                            kk_ref[...], ka_ref[...], w2a2_ref[...], g2_ref[...])
    for n, t in enumerate((r, jnp.exp(-jnp.exp(w)), kd, v, alpha, beta, g)):
        rw_ref[n] = t


def _sample_pre(cols, conv_st, shift_st, wts, row0, b):
    tb = b
    assert row0 % tb == 0
    r0 = row0 // tb
    cs = lambda w, start: pl.BlockSpec((tb, w), lambda i, _c=start // w: (r0, _c))
    full = lambda *s: jax.ShapeDtypeStruct(s, f32)
    return pl.pallas_call(
        _sample_pre_kernel,
        grid=(1,),
        in_specs=[cs(512, C_QKA), cs(128, C_TAIL), cs(512, C_XS), cs(256, C_BC), cs(1792, C_D),
                  _const((CONV_W - 1, b, XBC_W)), _const((b, 1792))] + [_const(w.shape) for w in wts],
        out_specs=[_const((b, 256))] * 3 + [_const((b, XBC_W)), _const((b, 512)), _const((b, 512)), _const((7, b, 512))],
        out_shape=[full(b, 256)] * 3 + [full(b, XBC_W), full(b, 512), full(b, 512), full(7, b, 512)],
        compiler_params=_params("arbitrary"),
        name="sample_pre",
    )(cols, cols, cols, cols, cols, conv_st, shift_st, *wts)


def _rows_to_tile(row, heads, width, reps):
    return jnp.concatenate([jnp.broadcast_to(row[:, h * width:(h + 1) * width], (reps, width)) for h in range(heads)], axis=0)


STEP_BPB = 8


def _gla_step_kernel(s_ref, a_ref, k_ref, q_ref, v_ref, sn_ref, o_ref):
    a_t, k_t, q_t = a_ref[...].T, k_ref[...].T, q_ref[...].T
    for j in range(s_ref.shape[0]):
        s = a_t[:, j:j + 1] * s_ref[j] + k_t[:, j:j + 1] * _rows_to_tile(v_ref[j:j + 1, :], H_A, DV_A, DK_A)
        sn_ref[j] = s
        qs = q_t[:, j:j + 1] * s
        o_ref[j:j + 1, :] = jnp.concatenate(
            [jnp.sum(qs[h * DK_A:(h + 1) * DK_A], axis=0, keepdims=True) for h in range(H_A)], axis=-1)


def _ssd_step_kernel(h_ref, dec_ref, xd_ref, bc_ref, hn_ref, y_ref):
    reps = (H_B // G_B) * P_B
    dec_t, xd_t = dec_ref[...].T, xd_ref[...].T
    ys = []
    for j in range(h_ref.shape[0]):
        bc = bc_ref[j:j + 1, :]
        hn = dec_t[:, j:j + 1] * h_ref[j] + xd_t[:, j:j + 1] * _rows_to_tile(bc[:, :G_B * N_B], G_B, N_B, reps)
        hn_ref[j] = hn
        ys.append(jnp.sum(hn * _rows_to_tile(bc[:, G_B * N_B:], G_B, N_B, reps), axis=-1, keepdims=True))
    y_ref[...] = jnp.concatenate(ys, axis=-1).T


def _rwkv_step_kernel(s_ref, rows_ref, sn_ref, o_ref):
    v_t = rows_ref[3].T
    os_ = []
    for j in range(s_ref.shape[0]):
        tile = lambda n, _j=j: _rows_to_tile(rows_ref[n, _j:_j + 1, :], H_D, N_D, N_D)
        s = s_ref[j]
        sa = jnp.sum(s * tile(4), axis=-1, keepdims=True)
        s = s * tile(1) + sa * tile(5) + v_t[:, j:j + 1] * tile(2)
        sn_ref[j] = s
        os_.append(jnp.sum(s * tile(0), axis=-1, keepdims=True))
    o_ref[...] = jnp.concatenate(os_, axis=-1).T


def _state_step(kernel, name, states, layer, ins, out_width):
    _, b, r, c = states.shape
    bpb = STEP_BPB

    def spec(a):
        if a.ndim == 2:
            return pl.BlockSpec((bpb, a.shape[1]), lambda i: (i, 0))
        return pl.BlockSpec((a.shape[0], bpb, a.shape[2]), lambda i: (0, i, 0))

    return pl.pallas_call(
        kernel,
        grid=(b // bpb,),
        in_specs=[pl.BlockSpec((None, bpb, r, c), lambda i: (layer, i, 0, 0))] + [spec(a) for a in ins],
        out_specs=[pl.BlockSpec((bpb, r, c), lambda i: (i, 0, 0)), pl.BlockSpec((bpb, out_width), lambda i: (i, 0))],
        out_shape=[jax.ShapeDtypeStruct((b, r, c), f32), jax.ShapeDtypeStruct((b, out_width), f32)],
        compiler_params=_params("parallel"),
        name=name,
    )(states, *ins)


def _rowhead_attend(kx, vx, q4, ind, n_maps, tail_rows=None):
    r = kx.shape[0]
    g = r // 8
    q8 = jnp.concatenate([q4, q4], axis=0)
    qt = jnp.broadcast_to(q8[None], (g, 8, 128)).reshape(r, 128)
    s3 = _dot(kx * qt, ind).reshape(g, 8, 128 * n_maps)
    if tail_rows is not None:
        last = jnp.where(_iota((1, 8, 128 * n_maps), 1) < tail_rows, s3[g - 1:g], -jnp.inf)
        s3 = jnp.concatenate([s3[:g - 1], last], axis=0)
    m8 = jnp.max(s3, axis=0)
    mh = jnp.maximum(m8, pltpu.roll(m8, 4, 0))
    p3 = jnp.exp(s3 - mh[None])
    l8 = jnp.sum(p3, axis=0)
    lh = l8 + pltpu.roll(l8, 4, 0)
    v3 = vx.reshape(g, 8, 128)
    outs = []
    for m in range(n_maps):
        ms = slice(m * 128, (m + 1) * 128)
        pv = jnp.sum(p3[:, :, ms] * v3, axis=0)
        pv = pv + pltpu.roll(pv, 4, 0)
        outs.append(pv / lh[:, ms])
    return outs


def _diff_decode_kernel(pt_ref, q_ref, ks_ref, vs_ref, *rest, lam_init, n_pages):
    del pt_ref
    k_refs, v_refs = rest[:n_pages], rest[n_pages:2 * n_pages]
    ind_ref, lam_ref, gn_ref, o_ref = rest[2 * n_pages:]
    own = lambda ref: jnp.concatenate([ref[0], ref[0]], axis=0)
    kx = jnp.concatenate([r[...] for r in k_refs] + [own(ks_ref)], axis=0)
    vx = jnp.concatenate([r[...] for r in v_refs] + [own(vs_ref)], axis=0)
    o1, o2 = _rowhead_attend(kx, vx, q_ref[0].astype(f32), ind_ref[...], 2, tail_rows=H_C)
    o_ref[0] = _diff_finish(o1, o2, lam_ref[...], gn_ref[...], lam_init)


def _diff_decode(pt_flat, layer, qb, ks, vs, ck, cv, ind, lam, gn, lam_init, n_pages):
    b = qb.shape[0]
    rows = ck.shape[2]
    row = pl.BlockSpec((1, H_C, 128), lambda i, pt: (i, 0, 0))
    page = lambda j: pl.BlockSpec((None, None, rows, 128), lambda i, pt, _j=j: (layer, pt[i * n_pages + _j], 0, 0))
    cst = lambda shp: pl.BlockSpec(shp, lambda i, pt: (0, 0))
    pages = [page(j) for j in range(n_pages)]
    return pl.pallas_call(
        functools.partial(_diff_decode_kernel, lam_init=lam_init, n_pages=n_pages),
        grid_spec=pltpu.PrefetchScalarGridSpec(
            num_scalar_prefetch=1, grid=(b,),
            in_specs=[row, row, row] + pages + pages + [cst((128, 256)), cst((1, 128)), cst((1, DV_C))],
            out_specs=pl.BlockSpec((1, 8, 128), lambda i, pt: (i, 0, 0))),
        out_shape=jax.ShapeDtypeStruct((b, 8, 128), f32),
        compiler_params=_params("parallel"),
        name="diff_decode",
    )(pt_flat, qb, ks, vs, *([ck] * n_pages), *([cv] * n_pages), ind, lam, gn)


def _sample_post_kernel(oa_ref, ga_ref, gn_a_ref, y_ref, act_ref, z_ref, dskip_ref, gn_b_ref, oc_ref,
                        od_ref, rw_ref, rk_ref, lg_ref, lb_ref, o_ref):
    oa = _gla_out(oa_ref[...], ga_ref[...], gn_a_ref[...])
    ob = _ssd_out(y_ref[...] + act_ref[:, :GROUP_W] * dskip_ref[...], z_ref[...], gn_b_ref[...])
    od = _rwkv_post(od_ref[...], rw_ref[0], rw_ref[2], rw_ref[3], rw_ref[6], rk_ref[...], lg_ref[...], lb_ref[...])
    o_ref[...] = jnp.concatenate([oa, ob, oc_ref[...], od], axis=-1)


def _sample_post(cols, oa, y, act, oc, od, rw, gn_a, dskip_x, gn_b, r_k, lnx_g, lnx_b, row0, b):
    r0 = row0 // b
    cs = lambda w, start: pl.BlockSpec((b, w), lambda i, _c=start // w: (r0, _c))
    c512 = _const((b, 512))
    v512 = _const((1, 512))
    return pl.pallas_call(
        _sample_post_kernel,
        grid=(1,),
        in_specs=[c512, cs(512, C_GA), _const((1, DV_A)), c512, _const((b, XBC_W)), cs(512, C_Z), v512, v512, c512,
                  c512, _const((7, b, 512)), v512, v512, v512],
        out_specs=_const((b, D_MODEL)),
        out_shape=jax.ShapeDtypeStruct((b, D_MODEL), f32),
        compiler_params=_params("arbitrary"),
        name="sample_post",
    )(oa, cols, gn_a, y, act, cols, dskip_x, gn_b, oc, od, rw, r_k, lnx_g, lnx_b)


def _head_rms(x, gain, width):
    outs = []
    for h in range(x.shape[1] // width):
        xh = x[:, h * width:(h + 1) * width]
        outs.append(xh * lax.rsqrt(jnp.mean(xh * xh, axis=-1, keepdims=True) + 1e-6) * gain)
    return jnp.concatenate(outs, axis=-1)


def _mem_kv_kernel(m_ref, g_ref, w_ref, gk_ref, k_ref, v_ref):
    m = m_ref[...]
    m = m * lax.rsqrt(jnp.mean(m * m, axis=-1, keepdims=True) + 1e-6) * g_ref[...]
    kv = _dot(m, w_ref[...])
    k_ref[...] = _head_rms(kv[:, :D_MEM], gk_ref[...], DH_M)
    v_ref[...] = kv[:, D_MEM:]


def _mem_kv(mem, g_src, w_kv, g_k):
    shp = jax.ShapeDtypeStruct((N_MEM, D_MEM), f32)
    return pl.pallas_call(
        _mem_kv_kernel, out_shape=[shp, shp],
        compiler_params=pltpu.CompilerParams(vmem_limit_bytes=VMEM_LIMIT), name="mem_kv",
    )(mem, g_src, w_kv, g_k)


def _mem_attn_prompt_kernel(q_ref, gq_ref, k_ref, v_ref, o_ref):
    q = _head_rms(q_ref[...], gq_ref[...], DH_M) * (DH_M ** -0.5)
    k, v = k_ref[...], v_ref[...]
    outs = []
    for h in range(H_M):
        hs = slice(h * DH_M, (h + 1) * DH_M)
        s = _dot(q[:, hs], k[:, hs], NT)
        p = jnp.exp(s - jnp.max(s, axis=-1, keepdims=True))
        outs.append(_dot(p, v[:, hs]) / jnp.sum(p, axis=-1, keepdims=True))
    o_ref[...] = jnp.concatenate(outs, axis=-1)


def _mem_attn_prompt(q, gq, k, v, seq=SEQ, tb=512):
    return pl.pallas_call(
        _mem_attn_prompt_kernel,
        grid=(seq // tb,),
        in_specs=[pl.BlockSpec((tb, D_MEM), lambda i: (i, 0)), _const((1, DH_M)), _const((N_MEM, D_MEM)),
                  _const((N_MEM, D_MEM))],
        out_specs=pl.BlockSpec((tb, D_MEM), lambda i: (i, 0)),
        out_shape=jax.ShapeDtypeStruct((seq, D_MEM), f32),
        compiler_params=_params("parallel"),
        name="mem_attn_prompt",
    )(q, gq, k, v)


MEM_BPB = 4


def _mem_attn_sample_kernel(q_ref, gq_ref, k_ref, v_ref, ind_ref, o_ref):
    for j in range(q_ref.shape[0]):
        q = q_ref[j]
        q = q * lax.rsqrt(jnp.mean(q * q, axis=-1, keepdims=True) + 1e-6) * gq_ref[...] * (DH_M ** -0.5)
        o_ref[j] = _rowhead_attend(k_ref[j], v_ref[j], q, ind_ref[...], 1)[0]


def _mem_attn_sample(q, gq, ck, cv, ind, layer):
    b = q.shape[0]
    bpb = MEM_BPB
    kv = pl.BlockSpec((None, bpb, N_MEM * H_M, DH_M), lambda i: (layer, i, 0, 0))
    return pl.pallas_call(
        _mem_attn_sample_kernel,
        grid=(b // bpb,),
        in_specs=[pl.BlockSpec((bpb, H_M, DH_M), lambda i: (i, 0, 0)), _const((1, DH_M)), kv, kv, _const((128, 128))],
        out_specs=pl.BlockSpec((bpb, 8, DH_M), lambda i: (i, 0, 0)),
        out_shape=jax.ShapeDtypeStruct((b, 8, DH_M), f32),
        compiler_params=_params("parallel"),
        name="mem_attn_sample",
    )(q, gq, ck, cv, ind)


def _cast_kernel(x_ref, o_ref):
    o_ref[...] = x_ref[...].astype(bf16)


def _table_bf16(tab, layer, rows=1024):
    n, d = tab.shape[1:]
    return pl.pallas_call(
        _cast_kernel,
        grid=(n // rows,),
        in_specs=[pl.BlockSpec((None, rows, d), lambda i: (layer, i, 0))],
        out_specs=pl.BlockSpec((rows, d), lambda i: (i, 0)),
        out_shape=jax.ShapeDtypeStruct((n, d), bf16),
        compiler_params=_params("parallel"),
        name="table_bf16",
    )(tab)


def _gelu(x):
    return 0.5 * x * (1.0 + jnp.tanh(0.7978845608028654 * (x + 0.044715 * x * x * x)))


def _peer_dense_kernel(xn_ref, u_ref, v_ref, ci_ref, cnt_ref, e2_ref, r2_ref, res_ref, o_ref, w_ref):
    e = pl.program_id(1)
    tb = xn_ref.shape[0]
    n_i = PEER_EB // N_KEYS

    @pl.when(e == 0)
    def _():
        o_ref[...] = res_ref[...]

    n_half = PEER_EB // PEER_SUB
    per = n_i // n_half
    parts = []
    for half in range(n_half):
        for ii in range(half * per, (half + 1) * per):
            w = jnp.zeros((N_KEYS, tb), bf16)
            for h in range(H_P):
                row = pl.ds(e * n_i + ii, 1)
                ci = ci_ref[0, h, row, :].astype(bf16)
                cnt = cnt_ref[0, h, row, :].astype(bf16)
                w = w + jnp.where(r2_ref[0, h] < cnt, e2_ref[0, h] * ci, jnp.zeros((), bf16))
            w_ref[ii * N_KEYS:(ii + 1) * N_KEYS, :] = w
        rows = slice(half * per * N_KEYS, (half + 1) * per * N_KEYS)
        hid = _gelu(_dot(u_ref[rows, :], xn_ref[...], NT))
        parts.append(_dot(hid.astype(bf16) * w_ref[rows, :], v_ref[rows, :], TN))
    o_ref[...] += sum(parts)


def _peer_dense(xn, u, v, sel, res):
    t = xn.shape[0]
    tb, eb = PEER_TB, PEER_EB
    once = pl.Buffered(1)
    selspec = pl.BlockSpec((1, H_P, N_KEYS, tb), lambda i, e: (i, 0, 0, 0), pipeline_mode=once)
    return pl.pallas_call(
        _peer_dense_kernel,
        grid=(t // tb, N_EXPERTS // eb),
        in_specs=[pl.BlockSpec((tb, D_MODEL), lambda i, e: (i, 0), pipeline_mode=once),
                  pl.BlockSpec((eb, D_MODEL), lambda i, e: (e, 0)),
                  pl.BlockSpec((eb, D_MODEL), lambda i, e: (e, 0)), selspec, selspec, selspec, selspec,
                  pl.BlockSpec((tb, D_MODEL), lambda i, e: (i, 0), pipeline_mode=once)],
        out_specs=pl.BlockSpec((tb, D_MODEL), lambda i, e: (i, 0)),
        out_shape=jax.ShapeDtypeStruct((t, D_MODEL), f32),
        scratch_shapes=[pltpu.VMEM((eb, tb), bf16)],
        compiler_params=_params("parallel", "arbitrary"),
        name="peer_dense",
    )(xn, u, v, *sel, res)


def _pad_w_in(w):
    a0, b0, c0, d0 = 0, 1552, 2840, 4376
    seg = lambda s, n: w[:, s:s + n]
    parts = [seg(d0, 1792), seg(b0 + 1024, 256), seg(a0, 512), seg(a0 + 512, 512), seg(a0 + 1024, 512),
             seg(b0, 512), seg(b0 + 512, 512), seg(c0, 512), seg(c0 + 512, 512), seg(c0 + 1024, 512),
             seg(a0 + 1536, 16), seg(b0 + 1280, 8), jnp.zeros((w.shape[0], IN_PAD - 6168), w.dtype)]
    return jnp.concatenate(parts, axis=1).astype(bf16)


def _layer_consts():
    ex = np.zeros((128, 512), np.float32)
    sel = np.zeros((8, 512), np.float32)
    for h in range(8):
        ex[GK_RANK + h, h * 64:(h + 1) * 64] = 1.0
        sel[h, h * 64] = 1.0
    ind2 = np.zeros((128, 256), np.float32)
    ind2[:DK_C, :128] = 1.0
    ind2[DK_C:, 128:] = 1.0
    ind1 = np.ones((128, 128), np.float32)
    return jnp.asarray(ex), jnp.asarray(sel), jnp.asarray(ind2), jnp.asarray(ind1)


def kernel(x_prompt, x_sample, cache_diff_k, cache_diff_v, cache_mem_k, cache_mem_v, state_gla, state_ssm, state_conv, state_rwkv, state_shift, page_table, mem_prompt, norm_mix, w_in, w_out, gla_wg2, gla_bg, gla_gn, conv_w, conv_b, dt_bias, a_log, d_skip, ssm_gn, dq_norm, dk_norm, lam_q, lam_k, diff_gn, shift_mu, w0, w2, a0, a2, g2, k_k, k_a, r_k, lnx_g, lnx_b, norm_mem, norm_memsrc, w_mq, w_mk, w_mv, w_mo, mq_norm, mk_norm, norm_ffn, peer_wq, peer_keys, peer_u, peer_v):
    nb = DEC_BATCH
    n_pages = page_table.shape[1]
    n_pool = cache_diff_k.shape[1]
    x = jnp.concatenate([x_prompt[0], x_sample[:, 0]], axis=0)
    pt_flat = page_table.reshape(-1)
    cos_p, sin_p = _rope_tables(jnp.arange(SEQ, dtype=jnp.int32))
    cos_s, sin_s = _rope_tables(jnp.full((nb,), PAST_LEN, jnp.int32))
    ex, sel8, ind2, ind1 = _layer_consts()
    ck_rows = cache_diff_k.reshape(DEPTH, n_pool, PAGE_SIZE * H_C, 2 * DK_C)
    cv_rows = cache_diff_v.reshape(DEPTH, n_pool, PAGE_SIZE * H_C, DV_C)
    mk_rows = cache_mem_k.reshape(DEPTH, nb, N_MEM * H_M, DH_M)
    mv_rows = cache_mem_v.reshape(DEPTH, nb, N_MEM * H_M, DH_M)
    gla_rows = state_gla.reshape(DEPTH, nb, H_A * DK_A, DV_A)
    ssm_rows = state_ssm.reshape(DEPTH, nb, H_B * P_B, N_B)
    rwkv_rows = state_rwkv.reshape(DEPTH, nb, H_D * N_D, N_D)
    row = lambda a: a.reshape(1, -1)
    rep64 = lambda a: jnp.repeat(a, 64).reshape(1, 512)
    outs = {n: [] for n in ('kp', 'vp', 'ks', 'vs', 'mk', 'mv', 'gla_p', 'gla_s', 'ssm_p', 'ssm_s', 'conv_p',
                            'conv_s', 'rwkv_p', 'rwkv_s', 'shift_p', 'shift_s')}
    for l in range(DEPTH):
        lam_init = 0.8 - 0.6 * math.exp(-0.3 * l)
        lq, lk = lam_q[l], lam_k[l]
        lam = jnp.exp(jnp.sum(lq[0] * lk[0])) - jnp.exp(jnp.sum(lq[1] * lk[1])) + lam_init
        lam = jnp.full((1, 128), lam, f32)
        wg2p = jnp.zeros((128, 256), f32).at[:GK_RANK].set(gla_wg2[l])
        w2a2 = jnp.zeros((128, 1024), f32).at[:64, :512].set(w2[l]).at[64:, 512:].set(a2[l])
        gq = jnp.tile(dq_norm[l].reshape(128), 4).reshape(1, 512)
        gk = jnp.tile(dk_norm[l].reshape(128), 4).reshape(1, 512)
        dtb_x, alog_x, dskip_x = rep64(dt_bias[l]), rep64(a_log[l]), rep64(d_skip[l])

        cols = _matmul(x, _pad_w_in(w_in[l]), gain=norm_mix[l])

        oa, gla_p = _gla_prompt(cols, wg2p, row(gla_bg[l]), row(gla_gn[l]), seq=SEQ)
        ob, ssm_p = _ssd_prompt(cols, conv_w[l], row(conv_b[l]), ex, dtb_x, alog_x, dskip_x, row(ssm_gn[l]), sel8,
                                seq=SEQ)
        qb, kf, kb = _diff_prep(cols, cos_p, sin_p, gq, gk, 0, SEQ, 512)
        oc = _diff_attn_prompt(qb, kb, cols, lam, row(diff_gn[l]), lam_init, seq=SEQ, tq=FLASH_TQ, tk=FLASH_TK)
        od, rwkv_p = _rwkv_prompt(cols, row(shift_mu[l]), row(w0[l]), row(a0[l]), row(k_k[l]), row(k_a[l]),
                                  row(r_k[l]), row(lnx_g[l]), row(lnx_b[l]), w2a2, g2[l], seq=SEQ)
        mix_p = jnp.concatenate([oa, ob, oc, od], axis=1)

        pre_w = [wg2p, row(gla_bg[l]), conv_w[l], row(conv_b[l]), ex, dtb_x, alog_x,
                 row(shift_mu[l]), row(w0[l]), row(a0[l]), row(k_k[l]), row(k_a[l]), w2a2, g2[l]]
        s_gq, s_gk, s_ga, s_act, s_dec, s_xd, s_rw = _sample_pre(cols, jnp.transpose(state_conv[l], (1, 0, 2)),
                                                                 state_shift[l][:, 0], pre_w, SEQ, nb)
        tail = cols[SEQ:]
        gla_s, oa_s = _state_step(_gla_step_kernel, "gla_step", gla_rows, l,
                                  [s_ga, s_gk, s_gq, tail[:, C_VA:C_VA + 512]], 512)
        ssm_s, y_s = _state_step(_ssd_step_kernel, "ssd_step", ssm_rows, l, [s_dec, s_xd, s_act[:, GROUP_W:]], 512)
        rwkv_s, od_s = _state_step(_rwkv_step_kernel, "rwkv_step", rwkv_rows, l, [s_rw], 512)
        qb_s, kf_s, _ = _diff_prep(cols, cos_s, sin_s, gq, gk, SEQ, nb, nb)
        vc_s = tail[:, C_VC:C_VC + 512]
        oc_s = _diff_decode(pt_flat, l, qb_s.reshape(nb, H_C, 128), kf_s.reshape(nb, H_C, 128), vc_s.reshape(nb, H_C, 128),
                            ck_rows, cv_rows, ind2, lam, row(diff_gn[l]), lam_init, n_pages)
        mix_s = _sample_post(cols, oa_s, y_s, s_act, oc_s[:, :H_C].reshape(nb, 512), od_s, s_rw, row(gla_gn[l]), dskip_x, row(ssm_gn[l]), row(r_k[l]),
                             row(lnx_g[l]), row(lnx_b[l]), SEQ, nb)

        x = _matmul(jnp.concatenate([mix_p, mix_s], axis=0), w_out[l].astype(bf16), res=x)

        mk_p, mv_p = _mem_kv(mem_prompt[0], row(norm_memsrc[l]),
                             jnp.concatenate([w_mk[l], w_mv[l]], axis=1).astype(bf16), row(mk_norm[l]))
        qm = _matmul(x, w_mq[l].astype(bf16), gain=norm_mem[l])
        om_p = _mem_attn_prompt(qm, row(mq_norm[l]), mk_p, mv_p, seq=SEQ)
        om_s = _mem_attn_sample(qm[SEQ:].reshape(nb, H_M, DH_M), row(mq_norm[l]), mk_rows, mv_rows, ind1, l)
        x = _matmul(jnp.concatenate([om_p, om_s[:, :H_M].reshape(nb, D_MEM)], axis=0), w_mo[l].astype(bf16), res=x)

        qp, xn = _matmul(x, peer_wq[l].astype(bf16), gain=norm_ffn[l], emit_xn=True, tn=512)
        picks = _peer_select(qp, peer_keys[l].reshape(2 * H_P, N_KEYS, D_PK // 2))
        x = _peer_dense(xn, _table_bf16(peer_u, l), _table_bf16(peer_v, l), picks, x)

        outs['kp'].append(kf.reshape(1, SEQ, H_C, 2 * DK_C))
        outs['vp'].append(cols[:SEQ, C_VC:C_VC + 512].reshape(1, SEQ, H_C, DV_C))
        outs['ks'].append(kf_s.reshape(nb, 1, H_C, 2 * DK_C))
        outs['vs'].append(vc_s.reshape(nb, 1, H_C, DV_C))
        outs['mk'].append(mk_p.reshape(1, N_MEM, H_M, DH_M))
        outs['mv'].append(mv_p.reshape(1, N_MEM, H_M, DH_M))
        outs['gla_p'].append(gla_p[None])
        outs['gla_s'].append(gla_s.reshape(nb, H_A, DK_A, DV_A))
        outs['ssm_p'].append(ssm_p[None])
        outs['ssm_s'].append(ssm_s.reshape(nb, H_B, P_B, N_B))
        u_p = jnp.concatenate([cols[SEQ - 3:SEQ, C_XS:C_XS + 512], cols[SEQ - 3:SEQ, C_BC:C_BC + 256]], axis=1)
        u_s = jnp.concatenate([tail[:, C_XS:C_XS + 512], tail[:, C_BC:C_BC + 256]], axis=1)
        outs['conv_p'].append(u_p[None])
        outs['conv_s'].append(jnp.concatenate([state_conv[l][:, 1:], u_s[:, None]], axis=1))
        outs['rwkv_p'].append(rwkv_p[None])
        outs['rwkv_s'].append(rwkv_s.reshape(nb, H_D, N_D, N_D))
        outs['shift_p'].append(cols[SEQ - 1:SEQ, C_D:C_D + 1792][None])
        outs['shift_s'].append(tail[:, C_D:C_D + 1792][:, None])
    st = {n: jnp.stack(v) for n, v in outs.items()}
    return (x[:SEQ][None], x[SEQ:][:, None], st['kp'], st['vp'], st['ks'], st['vs'], st['mk'], st['mv'],
            st['gla_p'], st['gla_s'], st['ssm_p'], st['ssm_s'], st['conv_p'], st['conv_s'],
            st['rwkv_p'], st['rwkv_s'], st['shift_p'], st['shift_s'])
```

```python
import functools
import math

import numpy as np
import jax
import jax.numpy as jnp
from jax import lax
from jax.experimental import pallas as pl
from jax.experimental.pallas import tpu as pltpu

f32 = jnp.float32
bf16 = jnp.bfloat16
HI = lax.Precision.HIGHEST

D_MODEL = 2048
SEQ = 8192
DEPTH = 2
DEC_BATCH = 128
PAST_LEN = 2048
PAGE_SIZE = 128
T_ALL = SEQ + DEC_BATCH

GROUP_W = 512
H_A, DK_A, DV_A, GK_RANK, GLA_TAU = 4, 64, 128, 16, 16.0
H_B, P_B, N_B, G_B, CONV_W, XBC_W = 8, 64, 64, 2, 4, 768
H_C, DK_C, DV_C, ROPE_THETA = 4, 64, 128, 10000.0
H_D, N_D, LNX_EPS = 8, 64, 64e-5
N_MEM, H_M, D_MEM, DH_M = 256, 4, 512, 128
N_KEYS, H_P, TOPK_P, D_PK = 128, 8, 16, 256
N_EXPERTS = N_KEYS * N_KEYS

C_D = 0
C_BC = 1792
C_QKA = 2048
C_VA = 2560
C_GA = 3072
C_Z = 3584
C_XS = 4096
C_QC = 4608
C_KC = 5120
C_VC = 5632
C_TAIL = 6144
IN_PAD = 6272

LANES = 128
VMEM_LIMIT = 56 * 1024 * 1024

NN = ((1,), (0,))
NT = ((1,), (1,))
TN = ((0,), (0,))


def _dot(a, b, dims=NN, hi=False):
    if hi:
        return lax.dot_general(a, b, (dims, ((), ())), precision=HI, preferred_element_type=f32)
    return lax.dot_general(a.astype(bf16), b.astype(bf16), (dims, ((), ())), preferred_element_type=f32)


def _softplus(x):
    return jnp.maximum(x, 0.0) + jnp.log(1.0 + jnp.exp(-jnp.abs(x)))


def _sigmoid(x):
    return 1.0 / (1.0 + jnp.exp(-x))


def _silu(x):
    return x * _sigmoid(x)


def _iota(shape, axis):
    return lax.broadcasted_iota(jnp.int32, shape, axis)


def _params(*sem):
    return pltpu.CompilerParams(dimension_semantics=sem, vmem_limit_bytes=VMEM_LIMIT)


def _mm_kernel(*refs, norm, residual, emit_xn):
    x_ref, g_ref, w_ref = refs[:3]
    res_ref = refs[3] if residual else None
    xn_ref = refs[-1]
    o_ref = refs[-3] if emit_xn else refs[-2]

    @pl.when(pl.program_id(1) == 0)
    def _():
        x = x_ref[...]
        if norm:
            x = x * lax.rsqrt(jnp.mean(x * x, axis=-1, keepdims=True) + 1e-6) * g_ref[...]
        xn_ref[...] = x.astype(bf16)
        if emit_xn:
            refs[-2][...] = xn_ref[...]

    acc = jnp.dot(xn_ref[...], w_ref[...], preferred_element_type=f32)
    if residual:
        acc = acc + res_ref[...]
    o_ref[...] = acc


def _matmul(x, w, gain=None, res=None, emit_xn=False, tm=None, tn=None):
    m, k = x.shape
    n = w.shape[1]
    tm = tm or _pick(m, (1040, 1024, 512, 256, 128))
    tn = tn or _pick(n, (1024, 896, 512, 256, 128))
    norm = gain is not None
    g = (gain if norm else jnp.ones((k,), f32)).reshape(1, k)
    args = [x, g, w]
    in_specs = [pl.BlockSpec((tm, k), lambda i, j: (i, 0)),
                pl.BlockSpec((1, k), lambda i, j: (0, 0)),
                pl.BlockSpec((k, tn), lambda i, j: (0, j))]
    if res is not None:
        args.append(res)
        in_specs.append(pl.BlockSpec((tm, tn), lambda i, j: (i, j)))
    out_specs = [pl.BlockSpec((tm, tn), lambda i, j: (i, j))]
    out_shape = [jax.ShapeDtypeStruct((m, n), f32)]
    if emit_xn:
        out_specs.append(pl.BlockSpec((tm, k), lambda i, j: (i, 0)))
        out_shape.append(jax.ShapeDtypeStruct((m, k), bf16))
    out = pl.pallas_call(
        functools.partial(_mm_kernel, norm=norm, residual=res is not None, emit_xn=emit_xn),
        grid=(m // tm, n // tn),
        in_specs=in_specs,
        out_specs=out_specs,
        out_shape=out_shape,
        scratch_shapes=[pltpu.VMEM((tm, k), bf16)],
        compiler_params=_params("parallel", "arbitrary"),
        name="mm",
    )(*args)
    return out if emit_xn else out[0]


def _pick(n, cands):
    for c in cands:
        if n % c == 0:
            return c
    return n


def _const(shape):
    nd = len(shape)
    return pl.BlockSpec(shape, lambda i, _n=nd: (0,) * _n)


def _cols(width, start, tb):
    assert start % width == 0
    return pl.BlockSpec((tb, width), lambda i, _c=start // width: (i, _c))


def _blockdiag_tri(tb, c):
    r = _iota((tb, tb), 0)
    s = _iota((tb, tb), 1)
    return jnp.where((r // c == s // c) & (s <= r), 1.0, 0.0).astype(f32)


def _segment_ones(n, seg):
    r = _iota((n, n), 0)
    s = _iota((n, n), 1)
    return jnp.where(r // seg == s // seg, 1.0, 0.0).astype(f32)


GLA_TB = 256
GLA_C = 16


def _gla_gate_log(tail, wg2p, bg):
    z = _dot(tail, wg2p, hi=True) + bg
    return -_softplus(-z) * (1.0 / GLA_TAU)


def _gla_out(o, g, gn):
    outs = []
    for h in range(H_A):
        oh = o[:, h * DV_A:(h + 1) * DV_A]
        oh = oh * lax.rsqrt(jnp.mean(oh * oh, axis=-1, keepdims=True) + 1e-6) * gn
        outs.append(oh * _silu(g[:, h * DV_A:(h + 1) * DV_A]))
    return jnp.concatenate(outs, axis=-1)


def _gla_kernel(qk_ref, v_ref, g_ref, tail_ref, wg2_ref, bg_ref, gn_ref, o_ref, sfin_ref, st_ref, w_ref):
    i = pl.program_id(0)

    @pl.when(i == 0)
    def _():
        st_ref[...] = jnp.zeros_like(st_ref)

    tb, c = GLA_TB, GLA_C
    qk = qk_ref[...]
    q = qk[:, :256] * (DK_A ** -0.5)
    k = qk[:, 256:]
    v = v_ref[...]
    la = _gla_gate_log(tail_ref[...], wg2_ref[...], bg_ref[...])
    b = _dot(_blockdiag_tri(tb, c), la, hi=True)
    e_r = _iota((256, 512), 0) // DK_A
    e_c = _iota((256, 512), 1) // DV_A
    expand = jnp.where(e_r == e_c, 1.0, 0.0).astype(bf16)
    s_idx = _iota((c, 256), 0)
    for j in range(tb // c):
        r0 = j * c
        qj, kj, bj, vj = q[r0:r0 + c], k[r0:r0 + c], b[r0:r0 + c], v[r0:r0 + c]
        for t in range(c):
            wt = qj[t:t + 1] * kj * jnp.exp(bj[t:t + 1] - bj)
            w_ref[t * c:(t + 1) * c, :] = jnp.where(s_idx <= t, wt, 0.0)
        att = _dot(w_ref[...], expand)
        o = jnp.sum(att.reshape(c, c, 512) * vj[None], axis=1)
        qe = qj * jnp.exp(bj)
        bl = bj[c - 1:c]
        ke = kj * jnp.exp(bl - bj)
        dl = jnp.exp(bl)
        inter = []
        for h in range(H_A):
            ks = slice(h * DK_A, (h + 1) * DK_A)
            st = st_ref[h]
            inter.append(_dot(qe[:, ks], st, NT))
            st_ref[h] = st * dl[:, ks] + _dot(vj[:, h * DV_A:(h + 1) * DV_A], ke[:, ks], TN)
        o = o + jnp.concatenate(inter, axis=-1)
        o_ref[r0:r0 + c, :] = _gla_out(o, g_ref[r0:r0 + c, :], gn_ref[...])

    @pl.when(i == pl.num_programs(0) - 1)
    def _():
        for h in range(H_A):
            sfin_ref[h] = st_ref[h].T


def _gla_prompt(cols, wg2p, bg, gn, seq=SEQ):
    tb = GLA_TB
    return pl.pallas_call(
        _gla_kernel,
        grid=(seq // tb,),
        in_specs=[_cols(512, C_QKA, tb), _cols(512, C_VA, tb), _cols(512, C_GA, tb), _cols(128, C_TAIL, tb),
                  _const((128, 256)), _const((1, 256)), _const((1, DV_A))],
        out_specs=[pl.BlockSpec((tb, 512), lambda i: (i, 0)), _const((H_A, DK_A, DV_A))],
        out_shape=[jax.ShapeDtypeStruct((seq, 512), f32), jax.ShapeDtypeStruct((H_A, DK_A, DV_A), f32)],
        scratch_shapes=[pltpu.VMEM((H_A, DV_A, DK_A), f32), pltpu.VMEM((GLA_C * GLA_C, 256), f32)],
        compiler_params=_params("arbitrary"),
        name="gla_prompt",
    )(cols, cols, cols, cols, wg2p, bg, gn)


SSD_TB = 512
SSD_C = 64


def _ssd_conv(ext, conv_w, conv_b, rows):
    out = conv_b
    for j in range(CONV_W):
        shifted = pltpu.roll(ext, j, 0) if j else ext
        out = out + shifted[8:8 + rows] * conv_w[CONV_W - 1 - j:CONV_W - j]
    return out


def _ssd_dt(tail, ex, dtb_x):
    return _softplus(_dot(tail, ex, hi=True) + dtb_x)


def _ssd_out(y, z, gn):
    y = y * _silu(z)
    w = GROUP_W // G_B
    outs = []
    for g in range(G_B):
        yg = y[:, g * w:(g + 1) * w]
        outs.append(yg * lax.rsqrt(jnp.mean(yg * yg, axis=-1, keepdims=True) + 1e-6) * gn[:, g * w:(g + 1) * w])
    return jnp.concatenate(outs, axis=-1)


def _ssd_kernel(z_ref, xs_ref, bc_ref, tail_ref, cw_ref, cb_ref, ex_ref, dtb_ref, alog_ref, dskip_ref, gn_ref,
                sel_ref, o_ref, hfin_ref, carry_ref, h_ref, y_ref):
    i = pl.program_id(0)

    @pl.when(i == 0)
    def _():
        carry_ref[...] = jnp.zeros_like(carry_ref)
        h_ref[...] = jnp.zeros_like(h_ref)

    tb, c = SSD_TB, SSD_C
    u = jnp.concatenate([xs_ref[...], bc_ref[...]], axis=-1)
    ext = jnp.concatenate([carry_ref[...], u], axis=0)
    carry_ref[...] = u[tb - 8:tb]
    act = _silu(_ssd_conv(ext, cw_ref[...], cb_ref[...], tb))
    xs = act[:, :GROUP_W]
    dtx = _ssd_dt(tail_ref[...], ex_ref[...], dtb_ref[...])
    la = dtx * (-jnp.exp(alog_ref[...]))
    b = _dot(_blockdiag_tri(tb, c), la, hi=True)
    brow = _dot(sel_ref[...], b, NT, hi=True)
    xd = xs * dtx
    tri = _iota((c, c), 1) <= _iota((c, c), 0)
    for ch in range(tb // c):
        r0 = ch * c
        rows = slice(r0, r0 + c)
        scores = []
        for g in range(G_B):
            bm = act[rows, GROUP_W + g * N_B:GROUP_W + (g + 1) * N_B]
            cm = act[rows, GROUP_W + G_B * N_B + g * N_B:GROUP_W + G_B * N_B + (g + 1) * N_B]
            scores.append((_dot(cm, bm, NT), bm, cm))
        for h in range(H_B):
            hs = slice(h * P_B, (h + 1) * P_B)
            sc, bm, cm = scores[h // (H_B // G_B)]
            bh = b[rows, hs]
            dec = jnp.where(tri, jnp.exp(bh - brow[h:h + 1, r0:r0 + c]), 0.0)
            xdh = xd[rows, hs]
            hst = h_ref[h]
            y = _dot(sc * dec, xdh) + _dot(cm, hst, NT) * jnp.exp(bh)
            bl = bh[c - 1:c]
            h_ref[h] = hst * jnp.exp(bl) + _dot(xdh * jnp.exp(bl - bh), bm, TN)
            y_ref[rows, hs] = y + xs[rows, hs] * dskip_ref[:, hs]
    o_ref[...] = _ssd_out(y_ref[...], z_ref[...], gn_ref[...])

    @pl.when(i == pl.num_programs(0) - 1)
    def _():
        hfin_ref[...] = h_ref[...]


def _ssd_prompt(cols, cw, cb, ex, dtb_x, alog_x, dskip_x, gn, sel, seq=SEQ):
    tb = SSD_TB
    return pl.pallas_call(
        _ssd_kernel,
        grid=(seq // tb,),
        in_specs=[_cols(512, C_Z, tb), _cols(512, C_XS, tb), _cols(256, C_BC, tb), _cols(128, C_TAIL, tb),
                  _const((CONV_W, XBC_W)), _const((1, XBC_W)), _const((128, 512)), _const((1, 512)), _const((1, 512)),
                  _const((1, 512)), _const((1, 512)), _const((8, 512))],
        out_specs=[pl.BlockSpec((tb, 512), lambda i: (i, 0)), _const((H_B, P_B, N_B))],
        out_shape=[jax.ShapeDtypeStruct((seq, 512), f32), jax.ShapeDtypeStruct((H_B, P_B, N_B), f32)],
        scratch_shapes=[pltpu.VMEM((8, XBC_W), f32), pltpu.VMEM((H_B, P_B, N_B), f32), pltpu.VMEM((tb, 512), f32)],
        compiler_params=_params("arbitrary"),
        name="ssd_prompt",
    )(cols, cols, cols, cols, cw, cb, ex, dtb_x, alog_x, dskip_x, gn, sel)


RWKV_TB = 512
RWKV_C = 64


def _rwkv_pre(x, xprev, mu, w0, a0, k_k, k_a, w2a2, g2):
    mixed = x + (xprev - x) * mu
    r = mixed[:, :512]
    kd = mixed[:, 512:1024]
    v = mixed[:, 1024:1536]
    lw = mixed[:, 1536:1664]
    lin = jnp.where(_iota(lw.shape, 1) < 64, jnp.tanh(lw), lw)
    wa = _dot(lin, w2a2, hi=True)
    w = -_softplus(-(w0 + wa[:, :512])) - 0.5
    a = _sigmoid(a0 + wa[:, 512:])
    g = _dot(_sigmoid(mixed[:, 1664:1792]), g2)
    kk = kd * k_k
    ss = _dot(kk * kk, _segment_ones(512, N_D), hi=True)
    kk = kk * lax.rsqrt(jnp.maximum(ss, 1e-24))
    kd = kd * (1.0 + (a - 1.0) * k_a)
    return r, w, kd, v, -kk, kk * a, g


def _rwkv_post(o, r, kd, v, g, r_k, lnx_g, lnx_b):
    seg = _segment_ones(512, N_D)
    mu = _dot(o, seg, hi=True) * (1.0 / N_D)
    d = o - mu
    var = _dot(d * d, seg, hi=True) * (1.0 / N_D)
    o = d * lax.rsqrt(var + LNX_EPS) * lnx_g + lnx_b
    o = o + _dot(r * kd * r_k, seg, hi=True) * v
    return o * g


def _rwkv_kernel(d_ref, mu_ref, w0_ref, a0_ref, kk_ref, ka_ref, rk_ref, lg_ref, lb_ref, w2a2_ref, g2_ref,
                 o_ref, sfin_ref, prev_ref, s_ref, oacc_ref):
    i = pl.program_id(0)

    @pl.when(i == 0)
    def _():
        prev_ref[...] = jnp.zeros_like(prev_ref)
        s_ref[...] = jnp.zeros_like(s_ref)

    tb, c = RWKV_TB, RWKV_C
    x = d_ref[...]
    xprev = jnp.where(_iota(x.shape, 0) == 0, prev_ref[0:1, :], pltpu.roll(x, 1, 0))
    prev_ref[0:1, :] = x[tb - 1:tb]
    r, w, kd, v, alpha, beta, g = _rwkv_pre(x, xprev, mu_ref[...], w0_ref[...], a0_ref[...], kk_ref[...], ka_ref[...],
                                            w2a2_ref[...], g2_ref[...])
    ld = -jnp.exp(w)
    cum = _dot(_blockdiag_tri(tb, c), ld, hi=True)
    at = alpha * jnp.exp(cum - ld)
    rt = r * jnp.exp(cum)
    einv = jnp.exp(-cum)
    kt = kd * einv
    bt = beta * einv
    ri = _iota((c, c), 0)
    ci = _iota((c, c), 1)
    strict, incl = ci < ri, ci <= ri
    eye = jnp.where(ri == ci, 1.0, 0.0).astype(f32)
    pairs = [(ch, h) for ch in range(tb // c) for h in range(H_D)]
    rows_of = lambda ch: slice(ch * c, (ch + 1) * c)
    lanes_of = lambda h: slice(h * N_D, (h + 1) * N_D)
    ar, kb, lk, mkb, tinv, p = {}, {}, {}, {}, {}, {}
    for ch, h in pairs:
        rows, hs = rows_of(ch), lanes_of(h)
        ar[ch, h] = jnp.concatenate([at[rows, hs], rt[rows, hs]], axis=0)
        kb[ch, h] = jnp.concatenate([kt[rows, hs], bt[rows, hs]], axis=0)
    for key in pairs:
        gram = _dot(ar[key], kb[key], NT)
        lk[key] = jnp.where(strict, gram[:c, :c], 0.0)
        p[key] = jnp.where(strict, gram[:c, c:], 0.0)
        mkb[key] = jnp.concatenate([jnp.where(incl, gram[c:, :c], 0.0), jnp.where(incl, gram[c:, c:], 0.0)], axis=1)
        tinv[key] = eye + p[key]
    for _ in range(5):
        for key in pairs:
            p[key] = _dot(p[key], p[key])
        for key in pairs:
            tinv[key] = tinv[key] + _dot(tinv[key], p[key])
    for ch in range(tb // c):
        rows = rows_of(ch)
        cl = cum[ch * c + c - 1:ch * c + c]
        efin = jnp.exp(cl - cum[rows])
        kfin = kd[rows] * efin
        bfin = beta[rows] * efin
        dfin = jnp.exp(cl)
        heads = range(H_D)
        s0 = [s_ref[h] for h in heads]
        ars = [_dot(ar[ch, h], s0[h], NT) for h in heads]
        lkv = [_dot(lk[ch, h], v[rows, lanes_of(h)]) for h in heads]
        u = [_dot(tinv[ch, h], ars[h][:c] + lkv[h]) for h in heads]
        vu = [jnp.concatenate([v[rows, lanes_of(h)], u[h]], axis=0) for h in heads]
        for h in heads:
            hs = lanes_of(h)
            oacc_ref[rows, hs] = ars[h][c:] + _dot(mkb[ch, h], vu[h])
            kbfin = jnp.concatenate([kfin[:, hs], bfin[:, hs]], axis=0)
            s_ref[h] = s0[h] * dfin[:, hs] + _dot(vu[h], kbfin, TN)
    o_ref[...] = _rwkv_post(oacc_ref[...], r, kd, v, g, rk_ref[...], lg_ref[...], lb_ref[...])

    @pl.when(i == pl.num_programs(0) - 1)
    def _():
        sfin_ref[...] = s_ref[...]


def _rwkv_prompt(cols, mu, w0, a0, k_k, k_a, r_k, lnx_g, lnx_b, w2a2, g2, seq=SEQ):
    tb = RWKV_TB
    vec = _const((1, 512))
    return pl.pallas_call(
        _rwkv_kernel,
        grid=(seq // tb,),
        in_specs=[_cols(1792, C_D, tb), _const((1, 1792)), vec, vec, vec, vec, vec, vec, vec,
                  _const((128, 1024)), _const((128, 512))],
        out_specs=[pl.BlockSpec((tb, 512), lambda i: (i, 0)), _const((H_D, N_D, N_D))],
        out_shape=[jax.ShapeDtypeStruct((seq, 512), f32), jax.ShapeDtypeStruct((H_D, N_D, N_D), f32)],
        scratch_shapes=[pltpu.VMEM((8, 1792), f32), pltpu.VMEM((H_D, N_D, N_D), f32), pltpu.VMEM((tb, 512), f32)],
        compiler_params=_params("arbitrary"),
        name="rwkv_prompt",
    )(cols, mu, w0, a0, k_k, k_a, r_k, lnx_g, lnx_b, w2a2, g2)


def _rope_tables(pos):
    half = DK_C // 2
    inv = ROPE_THETA ** (-jnp.arange(half, dtype=f32) / half)
    ang = pos.astype(f32)[:, None] * inv[None, :]
    cos, sin = jnp.cos(ang), jnp.sin(ang)
    return jnp.tile(jnp.concatenate([cos, cos], axis=-1), (1, 2)), jnp.tile(jnp.concatenate([-sin, sin], axis=-1), (1, 2))


def _qk_norm_rope(x, gain, cos, sin):
    ms = _dot(x * x, _segment_ones(512, DK_C), hi=True) * (1.0 / DK_C)
    x = x * lax.rsqrt(ms + 1e-6) * gain
    first = (_iota(x.shape, 1) % DK_C) < (DK_C // 2)
    partner = jnp.where(first, pltpu.roll(x, 512 - DK_C // 2, 1), pltpu.roll(x, DK_C // 2, 1))
    cos = jnp.concatenate([cos] * 4, axis=-1)
    sin = jnp.concatenate([sin] * 4, axis=-1)
    return x * cos + partner * sin


def _diff_prep_kernel(q_ref, k_ref, v_ref, cos_ref, sin_ref, gq_ref, gk_ref, qb_ref, kf_ref, kb_ref, vf_ref):
    cos, sin = cos_ref[...], sin_ref[...]
    q = _qk_norm_rope(q_ref[...], gq_ref[...], cos, sin)
    k = _qk_norm_rope(k_ref[...], gk_ref[...], cos, sin)
    qb_ref[...] = (q * (DK_C ** -0.5)).astype(bf16)
    v = v_ref[...]
    for h in range(H_C):
        kf_ref[:, h, :] = k[:, h * 128:(h + 1) * 128]
        vf_ref[:, h, :] = v[:, h * DV_C:(h + 1) * DV_C]
    kb_ref[...] = k.astype(bf16)


def _diff_prep(cols, cos, sin, gq, gk, row0, rows, tb):
    assert row0 % tb == 0
    r0 = row0 // tb
    colspec = lambda start: pl.BlockSpec((tb, 512), lambda i, _c=start // 512: (i + r0, _c))
    out = pl.BlockSpec((tb, 512), lambda i: (i, 0))
    tab = pl.BlockSpec((tb, 128), lambda i: (i, 0))
    out4 = pl.BlockSpec((tb, H_C, 128), lambda i: (i, 0, 0))
    return pl.pallas_call(
        _diff_prep_kernel,
        grid=(rows // tb,),
        in_specs=[colspec(C_QC), colspec(C_KC), colspec(C_VC), tab, tab, _const((1, 512)), _const((1, 512))],
        out_specs=[out, out4, out, out4],
        out_shape=[jax.ShapeDtypeStruct((rows, 512), bf16), jax.ShapeDtypeStruct((rows, H_C, 128), f32),
                   jax.ShapeDtypeStruct((rows, 512), bf16), jax.ShapeDtypeStruct((rows, H_C, 128), f32)],
        compiler_params=_params("parallel"),
        name="diff_prep",
    )(cols, cols, cols, cos, sin, gq, gk)


def _diff_finish(o1, o2, lam, gn, lam_init):
    o = o1 - lam * o2
    return o * lax.rsqrt(jnp.mean(o * o, axis=-1, keepdims=True) + 1e-6) * gn * (1.0 - lam_init)


FLASH_TQ = 1024
FLASH_TK = 1024


def _flash_kernel(qi_ref, kj_ref, q_ref, k_ref, v_ref, lam_ref, gn_ref, o_ref, m_ref, l_ref, acc_ref,
                  *, tq, tk, lam_init):
    step_id = pl.program_id(1)
    qi, kj = qi_ref[step_id], kj_ref[step_id]

    @pl.when(kj == 0)
    def _():
        m_ref[...] = jnp.full_like(m_ref, -jnp.inf)
        l_ref[...] = jnp.zeros_like(l_ref)
        acc_ref[...] = jnp.zeros_like(acc_ref)

    def step(masked):
        q = q_ref[...]
        k = k_ref[...]
        v = v_ref[...].astype(bf16)
        lane = _iota(q.shape, 1)
        if masked:
            keep = (kj * tk + _iota((tk, tq), 0)) <= (qi * tq + _iota((tk, tq), 1))
        for m in range(2):
            qm = jnp.where((lane < DK_C) if m == 0 else (lane >= DK_C), q, jnp.zeros_like(q))
            s = _dot(k, qm, NT)
            if masked:
                s = jnp.where(keep, s, -jnp.inf)
            m_old = m_ref[m]
            m_new = jnp.maximum(m_old, jnp.max(s, axis=0, keepdims=True))
            p = jnp.exp(s - m_new)
            corr = jnp.exp(m_old - m_new)
            l_ref[m] = corr * l_ref[m] + jnp.sum(p, axis=0, keepdims=True)
            acc_ref[m] = corr * acc_ref[m] + _dot(v, p, TN)
            m_ref[m] = m_new

    last = kj * tk + tk - 1

    @pl.when(last <= qi * tq)
    def _():
        step(False)

    @pl.when(last > qi * tq)
    def _():
        step(True)

    @pl.when(kj == (qi * tq + tq - 1) // tk)
    def _():
        o1 = (acc_ref[0] / l_ref[0]).T
        o2 = (acc_ref[1] / l_ref[1]).T
        o_ref[...] = _diff_finish(o1, o2, lam_ref[...], gn_ref[...], lam_init)


def _diff_attn_prompt(qb, kb, cols, lam, gn, lam_init, seq=SEQ, tq=512, tk=512):
    pairs = [(i, j) for i in range(seq // tq) for j in range((i * tq + tq - 1) // tk + 1)]
    qi_tab = jnp.asarray([p[0] for p in pairs], jnp.int32)
    kj_tab = jnp.asarray([p[1] for p in pairs], jnp.int32)
    cst = lambda shp: pl.BlockSpec(shp, lambda h, s, qi, kj: (0, 0))
    return pl.pallas_call(
        functools.partial(_flash_kernel, tq=tq, tk=tk, lam_init=lam_init),
        grid_spec=pltpu.PrefetchScalarGridSpec(
            num_scalar_prefetch=2, grid=(H_C, len(pairs)),
            in_specs=[pl.BlockSpec((tq, 128), lambda h, s, qi, kj: (qi[s], h)),
                      pl.BlockSpec((tk, 128), lambda h, s, qi, kj: (kj[s], h)),
                      pl.BlockSpec((tk, DV_C), lambda h, s, qi, kj: (kj[s], C_VC // DV_C + h)),
                      cst((1, 128)), cst((1, DV_C))],
            out_specs=pl.BlockSpec((tq, DV_C), lambda h, s, qi, kj: (qi[s], h)),
            scratch_shapes=[pltpu.VMEM((2, 1, tq), f32), pltpu.VMEM((2, 1, tq), f32), pltpu.VMEM((2, DV_C, tq), f32)]),
        out_shape=jax.ShapeDtypeStruct((seq, 512), f32),
        compiler_params=_params("parallel", "arbitrary"),
        name="diff_attn_prompt",
    )(qi_tab, kj_tab, qb, kb, cols, lam, gn)


PEER_SEL_TB = 128
PEER_TB = 640
PEER_EB = 512
PEER_SUB = 256
_PAIRS = [(a, b) for a in range(TOPK_P) for b in range(TOPK_P) if (a + 1) * (b + 1) <= TOPK_P]
_NPAIR = -(-len(_PAIRS) // 8) * 8


def _top16(s, n_idx, want_rank):
    rank = jnp.full(s.shape, float(TOPK_P), f32) if want_rank else None
    tops, idxs = [], []
    work = s
    for k in range(TOPK_P):
        m = jnp.max(work, axis=0, keepdims=True)
        idx = jnp.min(jnp.where(work == m, n_idx, float(N_KEYS)), axis=0, keepdims=True)
        hit = n_idx == idx
        if want_rank:
            rank = jnp.where(hit, float(k), rank)
        work = jnp.where(hit, -jnp.inf, work)
        tops.append(m)
        idxs.append(idx)
    return rank, jnp.concatenate(tops, axis=0), jnp.concatenate(idxs, axis=0)


def _peer_select_kernel(q_ref, keys_ref, pk1_ref, flat_ref, ci_ref, cnt_ref, e2_ref, r2_ref):
    tb = q_ref.shape[0]
    flat = flat_ref[...]
    n_idx = _iota((N_KEYS, tb), 0).astype(f32)
    for h in range(H_P):
        scores = [_dot(keys_ref[hx], q_ref[:, hx * 128:(hx + 1) * 128], NT, hi=True)
                  for hx in (2 * h, 2 * h + 1)]
        _, t1, idx1 = _top16(scores[0], n_idx, False)
        rank2, t2, _ = _top16(scores[1], n_idx, True)
        cand = jnp.concatenate([t1[a:a + 1] + t2[b:b + 1] for a, b in _PAIRS]
                               + [jnp.full((_NPAIR - len(_PAIRS), tb), -jnp.inf, f32)], axis=0)
        work = cand
        sel = jnp.zeros(cand.shape, f32)
        for _ in range(TOPK_P):
            m = jnp.max(work, axis=0, keepdims=True)
            idx = jnp.min(jnp.where(work == m, flat, 4096.0), axis=0, keepdims=True)
            hit = flat == idx
            sel = jnp.where(hit, 1.0, sel)
            work = jnp.where(hit, -jnp.inf, work)
        top = t1[0:1] + t2[0:1]
        z = jnp.sum(sel * jnp.exp(jnp.where(sel > 0, cand - top, 0.0)), axis=0, keepdims=True)
        cnt = _dot(pk1_ref[...], sel)
        cnt_i = jnp.zeros((N_KEYS, tb), f32)
        for k1 in range(TOPK_P):
            cnt_i = jnp.where(n_idx == idx1[k1:k1 + 1], cnt[k1:k1 + 1], cnt_i)
        ci_ref[0, h] = jnp.exp(scores[0] - t1[0:1]) / z
        cnt_ref[0, h] = cnt_i
        e2_ref[0, h] = jnp.exp(scores[1] - t2[0:1]).astype(bf16)
        r2_ref[0, h] = rank2.astype(bf16)


def _peer_select(q, keys):
    t = q.shape[0]
    tb, per = PEER_SEL_TB, PEER_TB // PEER_SEL_TB
    pk1 = np.zeros((TOPK_P, _NPAIR), np.float32)
    flat = np.full((_NPAIR, 1), 8192.0, np.float32)
    for r, (a, b) in enumerate(_PAIRS):
        pk1[a, r] = 1.0
        flat[r, 0] = a * TOPK_P + b
    out = pl.BlockSpec((1, H_P, N_KEYS, tb), lambda i: (i // per, 0, 0, i % per))
    shp = lambda dt: jax.ShapeDtypeStruct((t // PEER_TB, H_P, N_KEYS, PEER_TB), dt)
    return pl.pallas_call(
        _peer_select_kernel,
        grid=(t // tb,),
        in_specs=[pl.BlockSpec((tb, 2048), lambda i: (i, 0)), _const((2 * H_P, N_KEYS, 128)),
                  _const((TOPK_P, _NPAIR)), _const((_NPAIR, 1))],
        out_specs=[out, out, out, out],
        out_shape=[shp(f32), shp(f32), shp(bf16), shp(bf16)],
        compiler_params=_params("parallel"),
        name="peer_select",
    )(q, keys, jnp.asarray(pk1), jnp.asarray(flat))


def _sample_pre_kernel(qk_ref, tail_ref, xs_ref, bc_ref, d_ref, conv_ref, shift_ref,
                       wg2_ref, bg_ref, cw_ref, cb_ref, ex_ref, dtb_ref, alog_ref,
                       mu_ref, w0_ref, a0_ref, kk_ref, ka_ref, w2a2_ref, g2_ref,
                       gq_ref, gk_ref, ga_ref, act_ref, sdec_ref, sxd_ref, rw_ref):
    qk = qk_ref[...]
    la = _gla_gate_log(tail_ref[...], wg2_ref[...], bg_ref[...])
    gq_ref[...] = qk[:, :256] * (DK_A ** -0.5)
    gk_ref[...] = qk[:, 256:]
    ga_ref[...] = jnp.exp(la)
    u = jnp.concatenate([xs_ref[...], bc_ref[...]], axis=-1)
    cw = cw_ref[...]
    conv = cb_ref[...] + u * cw[CONV_W - 1:CONV_W]
    for j in range(CONV_W - 1):
        conv = conv + conv_ref[j] * cw[j:j + 1]
    act = _silu(conv)
    act_ref[...] = act
    dtx = _ssd_dt(tail_ref[...], ex_ref[...], dtb_ref[...])
    sdec_ref[...] = jnp.exp(dtx * (-jnp.exp(alog_ref[...])))
    sxd_ref[...] = act[:, :GROUP_W] * dtx
    r, w, kd, v, alpha, beta, g = _rwkv_pre(d_ref[...], shift_ref[...], mu_ref[...], w0_ref[...], a0_ref[...],
                                            kk_ref[...], ka_ref[...], w2a2_ref[...], g2_ref[...])
    for n, t in enumerate((r, jnp.exp(-jnp.exp(w)), kd, v, alpha, beta, g)):
        rw_ref[n] = t


def _sample_pre(cols, conv_st, shift_st, wts, row0, b):
    tb = b
    assert row0 % tb == 0
    r0 = row0 // tb
    cs = lambda w, start: pl.BlockSpec((tb, w), lambda i, _c=start // w: (r0, _c))
    full = lambda *s: jax.ShapeDtypeStruct(s, f32)
    return pl.pallas_call(
        _sample_pre_kernel,
        grid=(1,),
        in_specs=[cs(512, C_QKA), cs(128, C_TAIL), cs(512, C_XS), cs(256, C_BC), cs(1792, C_D),
                  _const((CONV_W - 1, b, XBC_W)), _const((b, 1792))] + [_const(w.shape) for w in wts],
        out_specs=[_const((b, 256))] * 3 + [_const((b, XBC_W)), _const((b, 512)), _const((b, 512)), _const((7, b, 512))],
        out_shape=[full(b, 256)] * 3 + [full(b, XBC_W), full(b, 512), full(b, 512), full(7, b, 512)],
        compiler_params=_params("arbitrary"),
        name="sample_pre",
    )(cols, cols, cols, cols, cols, conv_st, shift_st, *wts)


def _rows_to_tile(row, heads, width, reps):
    return jnp.concatenate([jnp.broadcast_to(row[:, h * width:(h + 1) * width], (reps, width)) for h in range(heads)], axis=0)


STEP_BPB = 8


def _gla_step_kernel(s_ref, a_ref, k_ref, q_ref, v_ref, sn_ref, o_ref):
    a_t, k_t, q_t = a_ref[...].T, k_ref[...].T, q_ref[...].T
    for j in range(s_ref.shape[0]):
        s = a_t[:, j:j + 1] * s_ref[j] + k_t[:, j:j + 1] * _rows_to_tile(v_ref[j:j + 1, :], H_A, DV_A, DK_A)
        sn_ref[j] = s
        qs = q_t[:, j:j + 1] * s
        o_ref[j:j + 1, :] = jnp.concatenate(
            [jnp.sum(qs[h * DK_A:(h + 1) * DK_A], axis=0, keepdims=True) for h in range(H_A)], axis=-1)


def _ssd_step_kernel(h_ref, dec_ref, xd_ref, bc_ref, hn_ref, y_ref):
    reps = (H_B // G_B) * P_B
    dec_t, xd_t = dec_ref[...].T, xd_ref[...].T
    ys = []
    for j in range(h_ref.shape[0]):
        bc = bc_ref[j:j + 1, :]
        hn = dec_t[:, j:j + 1] * h_ref[j] + xd_t[:, j:j + 1] * _rows_to_tile(bc[:, :G_B * N_B], G_B, N_B, reps)
        hn_ref[j] = hn
        ys.append(jnp.sum(hn * _rows_to_tile(bc[:, G_B * N_B:], G_B, N_B, reps), axis=-1, keepdims=True))
    y_ref[...] = jnp.concatenate(ys, axis=-1).T


def _rwkv_step_kernel(s_ref, rows_ref, sn_ref, o_ref):
    v_t = rows_ref[3].T
    os_ = []
    for j in range(s_ref.shape[0]):
        tile = lambda n, _j=j: _rows_to_tile(rows_ref[n, _j:_j + 1, :], H_D, N_D, N_D)
        s = s_ref[j]
        sa = jnp.sum(s * tile(4), axis=-1, keepdims=True)
        s = s * tile(1) + sa * tile(5) + v_t[:, j:j + 1] * tile(2)
        sn_ref[j] = s
        os_.append(jnp.sum(s * tile(0), axis=-1, keepdims=True))
    o_ref[...] = jnp.concatenate(os_, axis=-1).T


def _state_step(kernel, name, states, layer, ins, out_width):
    _, b, r, c = states.shape
    bpb = STEP_BPB

    def spec(a):
        if a.ndim == 2:
            return pl.BlockSpec((bpb, a.shape[1]), lambda i: (i, 0))
        return pl.BlockSpec((a.shape[0], bpb, a.shape[2]), lambda i: (0, i, 0))

    return pl.pallas_call(
        kernel,
        grid=(b // bpb,),
        in_specs=[pl.BlockSpec((None, bpb, r, c), lambda i: (layer, i, 0, 0))] + [spec(a) for a in ins],
        out_specs=[pl.BlockSpec((bpb, r, c), lambda i: (i, 0, 0)), pl.BlockSpec((bpb, out_width), lambda i: (i, 0))],
        out_shape=[jax.ShapeDtypeStruct((b, r, c), f32), jax.ShapeDtypeStruct((b, out_width), f32)],
        compiler_params=_params("parallel"),
        name=name,
    )(states, *ins)


def _rowhead_attend(kx, vx, q4, ind, n_maps, tail_rows=None):
    r = kx.shape[0]
    g = r // 8
    q8 = jnp.concatenate([q4, q4], axis=0)
    qt = jnp.broadcast_to(q8[None], (g, 8, 128)).reshape(r, 128)
    s3 = _dot(kx * qt, ind).reshape(g, 8, 128 * n_maps)
    if tail_rows is not None:
        last = jnp.where(_iota((1, 8, 128 * n_maps), 1) < tail_rows, s3[g - 1:g], -jnp.inf)
        s3 = jnp.concatenate([s3[:g - 1], last], axis=0)
    m8 = jnp.max(s3, axis=0)
    mh = jnp.maximum(m8, pltpu.roll(m8, 4, 0))
    p3 = jnp.exp(s3 - mh[None])
    l8 = jnp.sum(p3, axis=0)
    lh = l8 + pltpu.roll(l8, 4, 0)
    v3 = vx.reshape(g, 8, 128)
    outs = []
    for m in range(n_maps):
        ms = slice(m * 128, (m + 1) * 128)
        pv = jnp.sum(p3[:, :, ms] * v3, axis=0)
        pv = pv + pltpu.roll(pv, 4, 0)
        outs.append(pv / lh[:, ms])
    return outs


def _diff_decode_kernel(pt_ref, q_ref, ks_ref, vs_ref, *rest, lam_init, n_pages):
    del pt_ref
    k_refs, v_refs = rest[:n_pages], rest[n_pages:2 * n_pages]
    ind_ref, lam_ref, gn_ref, o_ref = rest[2 * n_pages:]
    own = lambda ref: jnp.concatenate([ref[0], ref[0]], axis=0)
    kx = jnp.concatenate([r[...] for r in k_refs] + [own(ks_ref)], axis=0)
    vx = jnp.concatenate([r[...] for r in v_refs] + [own(vs_ref)], axis=0)
    o1, o2 = _rowhead_attend(kx, vx, q_ref[0].astype(f32), ind_ref[...], 2, tail_rows=H_C)
    o_ref[0] = _diff_finish(o1, o2, lam_ref[...], gn_ref[...], lam_init)


def _diff_decode(pt_flat, layer, qb, ks, vs, ck, cv, ind, lam, gn, lam_init, n_pages):
    b = qb.shape[0]
    rows = ck.shape[2]
    row = pl.BlockSpec((1, H_C, 128), lambda i, pt: (i, 0, 0))
    page = lambda j: pl.BlockSpec((None, None, rows, 128), lambda i, pt, _j=j: (layer, pt[i * n_pages + _j], 0, 0))
    cst = lambda shp: pl.BlockSpec(shp, lambda i, pt: (0, 0))
    pages = [page(j) for j in range(n_pages)]
    return pl.pallas_call(
        functools.partial(_diff_decode_kernel, lam_init=lam_init, n_pages=n_pages),
        grid_spec=pltpu.PrefetchScalarGridSpec(
            num_scalar_prefetch=1, grid=(b,),
            in_specs=[row, row, row] + pages + pages + [cst((128, 256)), cst((1, 128)), cst((1, DV_C))],
            out_specs=pl.BlockSpec((1, 8, 128), lambda i, pt: (i, 0, 0))),
        out_shape=jax.ShapeDtypeStruct((b, 8, 128), f32),
        compiler_params=_params("parallel"),
        name="diff_decode",
    )(pt_flat, qb, ks, vs, *([ck] * n_pages), *([cv] * n_pages), ind, lam, gn)


def _sample_post_kernel(oa_ref, ga_ref, gn_a_ref, y_ref, act_ref, z_ref, dskip_ref, gn_b_ref, oc_ref,
                        od_ref, rw_ref, rk_ref, lg_ref, lb_ref, o_ref):
    oa = _gla_out(oa_ref[...], ga_ref[...], gn_a_ref[...])
    ob = _ssd_out(y_ref[...] + act_ref[:, :GROUP_W] * dskip_ref[...], z_ref[...], gn_b_ref[...])
    od = _rwkv_post(od_ref[...], rw_ref[0], rw_ref[2], rw_ref[3], rw_ref[6], rk_ref[...], lg_ref[...], lb_ref[...])
    o_ref[...] = jnp.concatenate([oa, ob, oc_ref[...], od], axis=-1)


def _sample_post(cols, oa, y, act, oc, od, rw, gn_a, dskip_x, gn_b, r_k, lnx_g, lnx_b, row0, b):
    r0 = row0 // b
    cs = lambda w, start: pl.BlockSpec((b, w), lambda i, _c=start // w: (r0, _c))
    c512 = _const((b, 512))
    v512 = _const((1, 512))
    return pl.pallas_call(
        _sample_post_kernel,
        grid=(1,),
        in_specs=[c512, cs(512, C_GA), _const((1, DV_A)), c512, _const((b, XBC_W)), cs(512, C_Z), v512, v512, c512,
                  c512, _const((7, b, 512)), v512, v512, v512],
        out_specs=_const((b, D_MODEL)),
        out_shape=jax.ShapeDtypeStruct((b, D_MODEL), f32),
        compiler_params=_params("arbitrary"),
        name="sample_post",
    )(oa, cols, gn_a, y, act, cols, dskip_x, gn_b, oc, od, rw, r_k, lnx_g, lnx_b)


def _head_rms(x, gain, width):
    outs = []
    for h in range(x.shape[1] // width):
        xh = x[:, h * width:(h + 1) * width]
        outs.append(xh * lax.rsqrt(jnp.mean(xh * xh, axis=-1, keepdims=True) + 1e-6) * gain)
    return jnp.concatenate(outs, axis=-1)


def _mem_kv_kernel(m_ref, g_ref, w_ref, gk_ref, k_ref, v_ref):
    m = m_ref[...]
    m = m * lax.rsqrt(jnp.mean(m * m, axis=-1, keepdims=True) + 1e-6) * g_ref[...]
    kv = _dot(m, w_ref[...])
    k_ref[...] = _head_rms(kv[:, :D_MEM], gk_ref[...], DH_M)
    v_ref[...] = kv[:, D_MEM:]


def _mem_kv(mem, g_src, w_kv, g_k):
    shp = jax.ShapeDtypeStruct((N_MEM, D_MEM), f32)
    return pl.pallas_call(
        _mem_kv_kernel, out_shape=[shp, shp],
        compiler_params=pltpu.CompilerParams(vmem_limit_bytes=VMEM_LIMIT), name="mem_kv",
    )(mem, g_src, w_kv, g_k)


def _mem_attn_prompt_kernel(q_ref, gq_ref, k_ref, v_ref, o_ref):
    q = _head_rms(q_ref[...], gq_ref[...], DH_M) * (DH_M ** -0.5)
    k, v = k_ref[...], v_ref[...]
    outs = []
    for h in range(H_M):
        hs = slice(h * DH_M, (h + 1) * DH_M)
        s = _dot(q[:, hs], k[:, hs], NT)
        p = jnp.exp(s - jnp.max(s, axis=-1, keepdims=True))
        outs.append(_dot(p, v[:, hs]) / jnp.sum(p, axis=-1, keepdims=True))
    o_ref[...] = jnp.concatenate(outs, axis=-1)


def _mem_attn_prompt(q, gq, k, v, seq=SEQ, tb=512):
    return pl.pallas_call(
        _mem_attn_prompt_kernel,
        grid=(seq // tb,),
        in_specs=[pl.BlockSpec((tb, D_MEM), lambda i: (i, 0)), _const((1, DH_M)), _const((N_MEM, D_MEM)),
                  _const((N_MEM, D_MEM))],
        out_specs=pl.BlockSpec((tb, D_MEM), lambda i: (i, 0)),
        out_shape=jax.ShapeDtypeStruct((seq, D_MEM), f32),
        compiler_params=_params("parallel"),
        name="mem_attn_prompt",
    )(q, gq, k, v)


MEM_BPB = 4


def _mem_attn_sample_kernel(q_ref, gq_ref, k_ref, v_ref, ind_ref, o_ref):
    for j in range(q_ref.shape[0]):
        q = q_ref[j]
        q = q * lax.rsqrt(jnp.mean(q * q, axis=-1, keepdims=True) + 1e-6) * gq_ref[...] * (DH_M ** -0.5)
        o_ref[j] = _rowhead_attend(k_ref[j], v_ref[j], q, ind_ref[...], 1)[0]


def _mem_attn_sample(q, gq, ck, cv, ind, layer):
    b = q.shape[0]
    bpb = MEM_BPB
    kv = pl.BlockSpec((None, bpb, N_MEM * H_M, DH_M), lambda i: (layer, i, 0, 0))
    return pl.pallas_call(
        _mem_attn_sample_kernel,
        grid=(b // bpb,),
        in_specs=[pl.BlockSpec((bpb, H_M, DH_M), lambda i: (i, 0, 0)), _const((1, DH_M)), kv, kv, _const((128, 128))],
        out_specs=pl.BlockSpec((bpb, 8, DH_M), lambda i: (i, 0, 0)),
        out_shape=jax.ShapeDtypeStruct((b, 8, DH_M), f32),
        compiler_params=_params("parallel"),
        name="mem_attn_sample",
    )(q, gq, ck, cv, ind)


def _cast_kernel(x_ref, o_ref):
    o_ref[...] = x_ref[...].astype(bf16)


def _table_bf16(tab, layer, rows=1024):
    n, d = tab.shape[1:]
    return pl.pallas_call(
        _cast_kernel,
        grid=(n // rows,),
        in_specs=[pl.BlockSpec((None, rows, d), lambda i: (layer, i, 0))],
        out_specs=pl.BlockSpec((rows, d), lambda i: (i, 0)),
        out_shape=jax.ShapeDtypeStruct((n, d), bf16),
        compiler_params=_params("parallel"),
        name="table_bf16",
    )(tab)


def _gelu(x):
    return 0.5 * x * (1.0 + jnp.tanh(0.7978845608028654 * (x + 0.044715 * x * x * x)))


def _peer_dense_kernel(xn_ref, u_ref, v_ref, ci_ref, cnt_ref, e2_ref, r2_ref, res_ref, o_ref, w_ref):
    e = pl.program_id(1)
    tb = xn_ref.shape[0]
    n_i = PEER_EB // N_KEYS

    @pl.when(e == 0)
    def _():
        o_ref[...] = res_ref[...]

    n_half = PEER_EB // PEER_SUB
    per = n_i // n_half
    parts = []
    for half in range(n_half):
        for ii in range(half * per, (half + 1) * per):
            w = jnp.zeros((N_KEYS, tb), bf16)
            for h in range(H_P):
                row = pl.ds(e * n_i + ii, 1)
                ci = ci_ref[0, h, row, :].astype(bf16)
                cnt = cnt_ref[0, h, row, :].astype(bf16)
                w = w + jnp.where(r2_ref[0, h] < cnt, e2_ref[0, h] * ci, jnp.zeros((), bf16))
            w_ref[ii * N_KEYS:(ii + 1) * N_KEYS, :] = w
        rows = slice(half * per * N_KEYS, (half + 1) * per * N_KEYS)
        hid = _gelu(_dot(u_ref[rows, :], xn_ref[...], NT))
        parts.append(_dot(hid.astype(bf16) * w_ref[rows, :], v_ref[rows, :], TN))
    o_ref[...] += sum(parts)


def _peer_dense(xn, u, v, sel, res):
    t = xn.shape[0]
    tb, eb = PEER_TB, PEER_EB
    once = pl.Buffered(1)
    selspec = pl.BlockSpec((1, H_P, N_KEYS, tb), lambda i, e: (i, 0, 0, 0), pipeline_mode=once)
    return pl.pallas_call(
        _peer_dense_kernel,
        grid=(t // tb, N_EXPERTS // eb),
        in_specs=[pl.BlockSpec((tb, D_MODEL), lambda i, e: (i, 0), pipeline_mode=once),
                  pl.BlockSpec((eb, D_MODEL), lambda i, e: (e, 0)),
                  pl.BlockSpec((eb, D_MODEL), lambda i, e: (e, 0)), selspec, selspec, selspec, selspec,
                  pl.BlockSpec((tb, D_MODEL), lambda i, e: (i, 0), pipeline_mode=once)],
        out_specs=pl.BlockSpec((tb, D_MODEL), lambda i, e: (i, 0)),
        out_shape=jax.ShapeDtypeStruct((t, D_MODEL), f32),
        scratch_shapes=[pltpu.VMEM((eb, tb), bf16)],
        compiler_params=_params("parallel", "arbitrary"),
        name="peer_dense",
    )(xn, u, v, *sel, res)


def _pad_w_in(w):
    a0, b0, c0, d0 = 0, 1552, 2840, 4376
    seg = lambda s, n: w[:, s:s + n]
    parts = [seg(d0, 1792), seg(b0 + 1024, 256), seg(a0, 512), seg(a0 + 512, 512), seg(a0 + 1024, 512),
             seg(b0, 512), seg(b0 + 512, 512), seg(c0, 512), seg(c0 + 512, 512), seg(c0 + 1024, 512),
             seg(a0 + 1536, 16), seg(b0 + 1280, 8), jnp.zeros((w.shape[0], IN_PAD - 6168), w.dtype)]
    return jnp.concatenate(parts, axis=1).astype(bf16)


def _layer_consts():
    ex = np.zeros((128, 512), np.float32)
    sel = np.zeros((8, 512), np.float32)
    for h in range(8):
        ex[GK_RANK + h, h * 64:(h + 1) * 64] = 1.0
        sel[h, h * 64] = 1.0
    ind2 = np.zeros((128, 256), np.float32)
    ind2[:DK_C, :128] = 1.0
    ind2[DK_C:, 128:] = 1.0
    ind1 = np.ones((128, 128), np.float32)
    return jnp.asarray(ex), jnp.asarray(sel), jnp.asarray(ind2), jnp.asarray(ind1)


def kernel(x_prompt, x_sample, cache_diff_k, cache_diff_v, cache_mem_k, cache_mem_v, state_gla, state_ssm, state_conv, state_rwkv, state_shift, page_table, mem_prompt, norm_mix, w_in, w_out, gla_wg2, gla_bg, gla_gn, conv_w, conv_b, dt_bias, a_log, d_skip, ssm_gn, dq_norm, dk_norm, lam_q, lam_k, diff_gn, shift_mu, w0, w2, a0, a2, g2, k_k, k_a, r_k, lnx_g, lnx_b, norm_mem, norm_memsrc, w_mq, w_mk, w_mv, w_mo, mq_norm, mk_norm, norm_ffn, peer_wq, peer_keys, peer_u, peer_v):
    nb = DEC_BATCH
    n_pages = page_table.shape[1]
    n_pool = cache_diff_k.shape[1]
    x = jnp.concatenate([x_prompt[0], x_sample[:, 0]], axis=0)
    pt_flat = page_table.reshape(-1)
    cos_p, sin_p = _rope_tables(jnp.arange(SEQ, dtype=jnp.int32))
    cos_s, sin_s = _rope_tables(jnp.full((nb,), PAST_LEN, jnp.int32))
    ex, sel8, ind2, ind1 = _layer_consts()
    ck_rows = cache_diff_k.reshape(DEPTH, n_pool, PAGE_SIZE * H_C, 2 * DK_C)
    cv_rows = cache_diff_v.reshape(DEPTH, n_pool, PAGE_SIZE * H_C, DV_C)
    mk_rows = cache_mem_k.reshape(DEPTH, nb, N_MEM * H_M, DH_M)
    mv_rows = cache_mem_v.reshape(DEPTH, nb, N_MEM * H_M, DH_M)
    gla_rows = state_gla.reshape(DEPTH, nb, H_A * DK_A, DV_A)
    ssm_rows = state_ssm.reshape(DEPTH, nb, H_B * P_B, N_B)
    rwkv_rows = state_rwkv.reshape(DEPTH, nb, H_D * N_D, N_D)
    row = lambda a: a.reshape(1, -1)
    rep64 = lambda a: jnp.repeat(a, 64).reshape(1, 512)
    outs = {n: [] for n in ('kp', 'vp', 'ks', 'vs', 'mk', 'mv', 'gla_p', 'gla_s', 'ssm_p', 'ssm_s', 'conv_p',
                            'conv_s', 'rwkv_p', 'rwkv_s', 'shift_p', 'shift_s')}
    for l in range(DEPTH):
        lam_init = 0.8 - 0.6 * math.exp(-0.3 * l)
        lq, lk = lam_q[l], lam_k[l]
        lam = jnp.exp(jnp.sum(lq[0] * lk[0])) - jnp.exp(jnp.sum(lq[1] * lk[1])) + lam_init
        lam = jnp.full((1, 128), lam, f32)
        wg2p = jnp.zeros((128, 256), f32).at[:GK_RANK].set(gla_wg2[l])
        w2a2 = jnp.zeros((128, 1024), f32).at[:64, :512].set(w2[l]).at[64:, 512:].set(a2[l])
        gq = jnp.tile(dq_norm[l].reshape(128), 4).reshape(1, 512)
        gk = jnp.tile(dk_norm[l].reshape(128), 4).reshape(1, 512)
        dtb_x, alog_x, dskip_x = rep64(dt_bias[l]), rep64(a_log[l]), rep64(d_skip[l])

        cols = _matmul(x, _pad_w_in(w_in[l]), gain=norm_mix[l])

        oa, gla_p = _gla_prompt(cols, wg2p, row(gla_bg[l]), row(gla_gn[l]), seq=SEQ)
        ob, ssm_p = _ssd_prompt(cols, conv_w[l], row(conv_b[l]), ex, dtb_x, alog_x, dskip_x, row(ssm_gn[l]), sel8,
                                seq=SEQ)
        qb, kf, kb, vf = _diff_prep(cols, cos_p, sin_p, gq, gk, 0, SEQ, 512)
        oc = _diff_attn_prompt(qb, kb, cols, lam, row(diff_gn[l]), lam_init, seq=SEQ, tq=FLASH_TQ, tk=FLASH_TK)
        od, rwkv_p = _rwkv_prompt(cols, row(shift_mu[l]), row(w0[l]), row(a0[l]), row(k_k[l]), row(k_a[l]),
                                  row(r_k[l]), row(lnx_g[l]), row(lnx_b[l]), w2a2, g2[l], seq=SEQ)
        mix_p = jnp.concatenate([oa, ob, oc, od], axis=1)

        pre_w = [wg2p, row(gla_bg[l]), conv_w[l], row(conv_b[l]), ex, dtb_x, alog_x,
                 row(shift_mu[l]), row(w0[l]), row(a0[l]), row(k_k[l]), row(k_a[l]), w2a2, g2[l]]
        s_gq, s_gk, s_ga, s_act, s_dec, s_xd, s_rw = _sample_pre(cols, jnp.transpose(state_conv[l], (1, 0, 2)),
                                                                 state_shift[l][:, 0], pre_w, SEQ, nb)
        tail = cols[SEQ:]
        gla_s, oa_s = _state_step(_gla_step_kernel, "gla_step", gla_rows, l,
                                  [s_ga, s_gk, s_gq, tail[:, C_VA:C_VA + 512]], 512)
        ssm_s, y_s = _state_step(_ssd_step_kernel, "ssd_step", ssm_rows, l, [s_dec, s_xd, s_act[:, GROUP_W:]], 512)
        rwkv_s, od_s = _state_step(_rwkv_step_kernel, "rwkv_step", rwkv_rows, l, [s_rw], 512)
        qb_s, kf_s, _, vf_s = _diff_prep(cols, cos_s, sin_s, gq, gk, SEQ, nb, nb)
        oc_s = _diff_decode(pt_flat, l, qb_s.reshape(nb, H_C, 128), kf_s, vf_s,
                            ck_rows, cv_rows, ind2, lam, row(diff_gn[l]), lam_init, n_pages)
        mix_s = _sample_post(cols, oa_s, y_s, s_act, oc_s[:, :H_C].reshape(nb, 512), od_s, s_rw, row(gla_gn[l]), dskip_x, row(ssm_gn[l]), row(r_k[l]),
                             row(lnx_g[l]), row(lnx_b[l]), SEQ, nb)

        x = _matmul(jnp.concatenate([mix_p, mix_s], axis=0), w_out[l].astype(bf16), res=x)

        mk_p, mv_p = _mem_kv(mem_prompt[0], row(norm_memsrc[l]),
                             jnp.concatenate([w_mk[l], w_mv[l]], axis=1).astype(bf16), row(mk_norm[l]))
        qm = _matmul(x, w_mq[l].astype(bf16), gain=norm_mem[l])
        om_p = _mem_attn_prompt(qm, row(mq_norm[l]), mk_p, mv_p, seq=SEQ)
        om_s = _mem_attn_sample(qm[SEQ:].reshape(nb, H_M, DH_M), row(mq_norm[l]), mk_rows, mv_rows, ind1, l)
        x = _matmul(jnp.concatenate([om_p, om_s[:, :H_M].reshape(nb, D_MEM)], axis=0), w_mo[l].astype(bf16), res=x)

        qp, xn = _matmul(x, peer_wq[l].astype(bf16), gain=norm_ffn[l], emit_xn=True, tn=512)
        picks = _peer_select(qp, peer_keys[l].reshape(2 * H_P, N_KEYS, D_PK // 2))
        x = _peer_dense(xn, _table_bf16(peer_u, l), _table_bf16(peer_v, l), picks, x)

        outs['kp'].append(kf[None])
        outs['vp'].append(vf[None])
        outs['ks'].append(kf_s[:, None])
        outs['vs'].append(vf_s[:, None])
        outs['mk'].append(mk_p.reshape(1, N_MEM, H_M, DH_M))
        outs['mv'].append(mv_p.reshape(1, N_MEM, H_M, DH_M))
        outs['gla_p'].append(gla_p[None])
        outs['gla_s'].append(gla_s.reshape(nb, H_A, DK_A, DV_A))
        outs['ssm_p'].append(ssm_p[None])
        outs['ssm_s'].append(ssm_s.reshape(nb, H_B, P_B, N_B))
        u_p = jnp.concatenate([cols[SEQ - 3:SEQ, C_XS:C_XS + 512], cols[SEQ - 3:SEQ, C_BC:C_BC + 256]], axis=1)
        u_s = jnp.concatenate([tail[:, C_XS:C_XS + 512], tail[:, C_BC:C_BC + 256]], axis=1)
        outs['conv_p'].append(u_p[None])
        outs['conv_s'].append(jnp.concatenate([state_conv[l][:, 1:], u_s[:, None]], axis=1))
        outs['rwkv_p'].append(rwkv_p[None])
        outs['rwkv_s'].append(rwkv_s.reshape(nb, H_D, N_D, N_D))
        outs['shift_p'].append(cols[SEQ - 1:SEQ, C_D:C_D + 1792][None])
        outs['shift_s'].append(tail[:, C_D:C_D + 1792][:, None])
    st = {n: jnp.stack(v) for n, v in outs.items()}
    return (x[:SEQ][None], x[SEQ:][:, None], st['kp'], st['vp'], st['ks'], st['vs'], st['mk'], st['mv'],
            st['gla_p'], st['gla_s'], st['ssm_p'], st['ssm_s'], st['conv_p'], st['conv_s'],
            st['rwkv_p'], st['rwkv_s'], st['shift_p'], st['shift_s'])
```
